```python
import jax, jax.numpy as jnp
from jax import lax
import numpy as np

D_MODEL = 1024
BATCH = 4
SEQ = 4096
DEPTH = 4

N_MIXERS = 2
NORM_EPS = 1e-6
N_MOD = 6
LRU_WIDTH = D_MODEL
LRU_BLOCKS = 4
LRU_BLOCK_W = LRU_WIDTH // LRU_BLOCKS
CONV_WIDTH = 4
LRU_C = 8.0
ATTN_HEADS = 8
HEAD_DIM = D_MODEL // ATTN_HEADS
MOBA_BLOCK = 256
MOBA_TOPK = 3
MOBA_ROW_BLOCK = 128
N_EXPERTS = 32
TOP_K = 4
D_EXPERT = D_MODEL
SWIGLU_LIMIT = 7.0
SWIGLU_ALPHA = 1.702
MOE_ROW_BLOCK = 256

kernel_name = 'hybrid_rglru_moba_moe_adaln'


def rms_norm(x, g):
    xf = x.astype(jnp.float32)
    y = xf * lax.rsqrt(jnp.mean(xf * xf, axis=-1, keepdims=True) + NORM_EPS)
    return (y * g.astype(jnp.float32)).astype(x.dtype)


def group_rows(group_ids, n_groups, row_block):
    n = group_ids.shape[0]
    order = jnp.argsort(group_ids).astype(jnp.int32)
    g_sorted = group_ids[order]
    counts = jnp.bincount(group_ids, length=n_groups).astype(jnp.int32)
    padded = (counts + row_block - 1) // row_block * row_block
    pad_end = jnp.cumsum(padded)
    pad_start = pad_end - padded
    grp_start = jnp.cumsum(counts) - counts
    dest = pad_start[g_sorted] + jnp.arange(n, dtype=jnp.int32) - grp_start[g_sorted]
    n_row_blocks = -(-n // row_block) + n_groups
    row_item = jnp.full((n_row_blocks * row_block,), n, jnp.int32).at[dest].set(order)
    starts = jnp.arange(n_row_blocks, dtype=jnp.int32) * row_block
    block_group = jnp.minimum(jnp.searchsorted(pad_end, starts, side='right'), n_groups - 1)
    return row_item.reshape(n_row_blocks, row_block), block_group.astype(jnp.int32)


def causal_depthwise_conv(x, w, b):
    y = lax.conv_general_dilated(x, w[:, None, :], window_strides=(1,),
                                 padding=[(CONV_WIDTH - 1, 0)],
                                 dimension_numbers=('NWC', 'WIO', 'NWC'),
                                 feature_group_count=x.shape[-1])
    return y + b


def _lru_combine(left, right):
    a_l, h_l = left
    a_r, h_r = right
    return a_l * a_r, a_r * h_l + h_r


def rglru_mixer(h, w_in, conv_w, conv_b, w_a, b_a, w_x, b_x, lam, w_out):
    B, S, _ = h.shape
    gate_branch, rec_branch = jnp.split(h @ w_in, 2, axis=-1)
    xc = causal_depthwise_conv(rec_branch, conv_w, conv_b)
    xg = xc.reshape(B, S, LRU_BLOCKS, LRU_BLOCK_W)
    r = jax.nn.sigmoid((jnp.einsum('bsgi,gij->bsgj', xg, w_a).reshape(B, S, LRU_WIDTH) + b_a).astype(jnp.float32))
    i = jax.nn.sigmoid((jnp.einsum('bsgi,gij->bsgj', xg, w_x).reshape(B, S, LRU_WIDTH) + b_x).astype(jnp.float32))
    log_a = -LRU_C * r * jax.nn.softplus(-lam.astype(jnp.float32))
    a = jnp.exp(log_a)
    u = jnp.sqrt(-jnp.expm1(2.0 * log_a)) * (i * xc.astype(jnp.float32))
    _, hs = lax.associative_scan(_lru_combine, (a, u), axis=1)
    y = jax.nn.gelu(gate_branch, approximate=True) * hs.astype(h.dtype)
    return y @ w_out


def moba_mixer(h, w_qkv, q_g, k_g, w_out):
    B, S, D = h.shape
    BH = B * ATTN_HEADS
    f32 = jnp.float32
    qkv = (h @ w_qkv).reshape(B, S, 3, ATTN_HEADS, HEAD_DIM).transpose(2, 0, 3, 1, 4)
    q = rms_norm(qkv[0], q_g) * (HEAD_DIM ** -0.5)
    k = rms_norm(qkv[1], k_g)
    v = qkv[2]
    nb = -(-S // MOBA_BLOCK)
    sp = nb * MOBA_BLOCK
    topk = min(MOBA_TOPK, nb)
    padw = ((0, 0), (0, 0), (0, sp - S), (0, 0))
    qb = jnp.pad(q, padw).reshape(B, ATTN_HEADS, nb, MOBA_BLOCK, HEAD_DIM)
    kb = jnp.pad(k, padw).reshape(B, ATTN_HEADS, nb, MOBA_BLOCK, HEAD_DIM)
    vb = jnp.pad(v, padw).reshape(B, ATTN_HEADS, nb, MOBA_BLOCK, HEAD_DIM)

    q_blk = jnp.arange(S) // MOBA_BLOCK
    k_mean = jnp.mean(kb.astype(f32), axis=3)
    gate = jnp.einsum('bhsd,bhnd->bhsn', q.astype(f32), k_mean)
    fully_past = jnp.arange(nb)[None, :] < q_blk[:, None]
    gate = jnp.where(fully_past, gate, -jnp.inf)
    _, sel = lax.top_k(gate, topk)
    sel_valid = sel < q_blk[:, None]

    s_own = jnp.einsum('bhnqd,bhnkd->bhnqk', qb, kb).astype(f32)
    causal = jnp.tril(jnp.ones((MOBA_BLOCK, MOBA_BLOCK), dtype=bool))
    s_own = jnp.where(causal, s_own, -jnp.inf)
    m_own = jnp.max(s_own, axis=-1)
    p_own = jnp.exp(s_own - m_own[..., None])
    l_own = jnp.sum(p_own, axis=-1)
    o_own = jnp.einsum('bhnqk,bhnkd->bhnqd', p_own, vb.astype(f32))
    m_own = m_own.reshape(B, ATTN_HEADS, sp)[:, :, :S]
    l_own = l_own.reshape(B, ATTN_HEADS, sp)[:, :, :S]
    o_own = o_own.reshape(B, ATTN_HEADS, sp, HEAD_DIM)[:, :, :S]

    n_pairs = BH * S * topk
    n_groups = BH * nb
    pair_query = jnp.arange(n_pairs, dtype=jnp.int32) // topk
    group = (pair_query // S) * nb + sel.reshape(-1).astype(jnp.int32)
    group = jnp.where(sel_valid.reshape(-1), group, n_groups)
    row_item, block_group = group_rows(group, n_groups + 1, MOBA_ROW_BLOCK)
    q_flat = q.reshape(BH * S, HEAD_DIM)
    k_flat = kb.reshape(n_groups, MOBA_BLOCK, HEAD_DIM)
    v_flat = vb.reshape(n_groups, MOBA_BLOCK, HEAD_DIM)

    def attend_block(args):
        items, g = args
        q_rows = q_flat[jnp.minimum(items, n_pairs - 1) // topk].astype(f32)
        g = jnp.minimum(g, n_groups - 1)
        s = q_rows @ k_flat[g].astype(f32).T
        m = jnp.max(s, axis=-1)
        p = jnp.exp(s - m[:, None])
        return m, jnp.sum(p, axis=-1), p @ v_flat[g].astype(f32)

    m_rows, l_rows, o_rows = lax.map(attend_block, (row_item, block_group))
    items = row_item.reshape(-1)
    m_pair = jnp.full((n_pairs,), -jnp.inf, f32).at[items].set(m_rows.reshape(-1), mode='drop')
    l_pair = jnp.zeros((n_pairs,), f32).at[items].set(l_rows.reshape(-1), mode='drop')
    o_pair = jnp.zeros((n_pairs, HEAD_DIM), f32).at[items].set(o_rows.reshape(-1, HEAD_DIM), mode='drop')
    m_pair = jnp.where(sel_valid, m_pair.reshape(B, ATTN_HEADS, S, topk), -jnp.inf)
    l_pair = l_pair.reshape(B, ATTN_HEADS, S, topk)
    o_pair = o_pair.reshape(B, ATTN_HEADS, S, topk, HEAD_DIM)

    m_all = jnp.maximum(m_own, jnp.max(m_pair, axis=-1))
    w_own = jnp.exp(m_own - m_all)
    w_pair = jnp.exp(m_pair - m_all[..., None])
    num = o_own * w_own[..., None] + jnp.einsum('bhsk,bhskd->bhsd', w_pair, o_pair)
    den = l_own * w_own + jnp.sum(w_pair * l_pair, axis=-1)
    out = (num / den[..., None]).astype(h.dtype)
    out = out.transpose(0, 2, 1, 3).reshape(B, S, D)
    return out @ w_out


def moe_ffn(h, router_w, router_b, w_gu, b_gu, w_down, b_down):
    B, S, D = h.shape
    T = B * S
    n = T * TOP_K
    xt = h.reshape(T, D)
    logits = (xt @ router_w).astype(jnp.float32) + router_b.astype(jnp.float32)
    top_logits, top_e = lax.top_k(logits, TOP_K)
    top_w = jax.nn.softmax(top_logits, axis=-1).reshape(-1)
    row_item, block_expert = group_rows(top_e.reshape(-1).astype(jnp.int32), N_EXPERTS, MOE_ROW_BLOCK)

    def expert_block(args):
        items, e = args
        it = jnp.minimum(items, n - 1)
        xb = xt[it // TOP_K]
        gu = xb @ w_gu[e] + b_gu[e]
        gate_v = jnp.minimum(gu[:, :D_EXPERT], SWIGLU_LIMIT)
        up_v = jnp.clip(gu[:, D_EXPERT:], -SWIGLU_LIMIT, SWIGLU_LIMIT)
        act = gate_v * jax.nn.sigmoid(SWIGLU_ALPHA * gate_v) * (up_v + 1.0)
        y = act @ w_down[e] + b_down[e]
        return y * top_w[it][:, None].astype(y.dtype)

    y_rows = lax.map(expert_block, (row_item, block_expert))
    y = jnp.zeros((T, D), h.dtype).at[row_item.reshape(-1)].add(y_rows.reshape(-1, D), mode='drop')
    return y.reshape(B, S, D)


def setup_inputs(seed: int = 0) -> dict:
    key = jax.random.key(seed)
    ks = jax.random.split(key, 24)
    n_lru = (DEPTH + N_MIXERS - 1) // N_MIXERS
    n_attn = DEPTH // N_MIXERS
    D = D_MODEL
    nrm = lambda k, shape, s: jax.random.normal(k, shape, jnp.float32) * s
    u = jax.random.uniform(ks[11], (n_lru, LRU_WIDTH), jnp.float32, minval=0.9, maxval=0.999)
    sg = u ** (1.0 / LRU_C)
    lam = jnp.log(sg) - jnp.log1p(-sg)
    return {
        'x': nrm(ks[0], (BATCH, SEQ, D), 1.0),
        'c': nrm(ks[1], (BATCH, D), 1.0),
        'norm_mix_g': 1.0 + nrm(ks[2], (DEPTH, D), 0.02),
        'norm_ffn_g': 1.0 + nrm(ks[3], (DEPTH, D), 0.02),
        'w_mod': nrm(ks[4], (DEPTH, D, N_MOD * D), 0.5 * D ** -0.5),
        'b_mod': nrm(ks[5], (DEPTH, N_MOD * D), 0.02),
        'lru_w_in': nrm(ks[6], (n_lru, D, 2 * LRU_WIDTH), D ** -0.5),
        'lru_conv_w': nrm(ks[7], (n_lru, CONV_WIDTH, LRU_WIDTH), CONV_WIDTH ** -0.5),
        'lru_conv_b': nrm(ks[8], (n_lru, LRU_WIDTH), 0.02),
        'lru_w_a': nrm(ks[9], (n_lru, LRU_BLOCKS, LRU_BLOCK_W, LRU_BLOCK_W), LRU_BLOCK_W ** -0.5),
        'lru_b_a': nrm(ks[10], (n_lru, LRU_WIDTH), 0.02),
        'lru_w_x': nrm(ks[12], (n_lru, LRU_BLOCKS, LRU_BLOCK_W, LRU_BLOCK_W), LRU_BLOCK_W ** -0.5),
        'lru_b_x': nrm(ks[13], (n_lru, LRU_WIDTH), 0.02),
        'lru_lambda': lam,
        'lru_w_out': nrm(ks[14], (n_lru, LRU_WIDTH, D), LRU_WIDTH ** -0.5),
        'attn_w_qkv': nrm(ks[15], (n_attn, D, 3 * D), D ** -0.5),
        'attn_q_norm_g': 1.0 + nrm(ks[16], (n_attn, HEAD_DIM), 0.02),
        'attn_k_norm_g': 1.0 + nrm(ks[17], (n_attn, HEAD_DIM), 0.02),
        'attn_w_out': nrm(ks[18], (n_attn, D, D), D ** -0.5),
        'router_w': nrm(ks[19], (DEPTH, D, N_EXPERTS), D ** -0.5),
        'router_b': nrm(ks[20], (DEPTH, N_EXPERTS), 0.01),
        'expert_w_gu': nrm(ks[21], (DEPTH, N_EXPERTS, D, 2 * D_EXPERT), D ** -0.5),
        'expert_b_gu': nrm(ks[22], (DEPTH, N_EXPERTS, 2 * D_EXPERT), 0.01),
        'expert_w_down': nrm(ks[23], (DEPTH, N_EXPERTS, D_EXPERT, D), D_EXPERT ** -0.5),
        'expert_b_down': nrm(jax.random.fold_in(key, 99), (DEPTH, N_EXPERTS, D), 0.01),
    }


def reference(x, c, norm_mix_g, norm_ffn_g, w_mod, b_mod, lru_w_in, lru_conv_w, lru_conv_b,
              lru_w_a, lru_b_a, lru_w_x, lru_b_x, lru_lambda, lru_w_out, attn_w_qkv,
              attn_q_norm_g, attn_k_norm_g, attn_w_out, router_w, router_b, expert_w_gu,
              expert_b_gu, expert_w_down, expert_b_down):
    cond = jax.nn.silu(c)
    for i in range(DEPTH):
        mod = cond @ w_mod[i] + b_mod[i]
        sh1, sc1, g1, sh2, sc2, g2 = jnp.split(mod[:, None, :], N_MOD, axis=-1)
        h = rms_norm(x, norm_mix_g[i]) * (1.0 + sc1) + sh1
        j = i // N_MIXERS
        if i % N_MIXERS == 0:
            mixed = rglru_mixer(h, lru_w_in[j], lru_conv_w[j], lru_conv_b[j], lru_w_a[j], lru_b_a[j],
                                lru_w_x[j], lru_b_x[j], lru_lambda[j], lru_w_out[j])
        else:
            mixed = moba_mixer(h, attn_w_qkv[j], attn_q_norm_g[j], attn_k_norm_g[j], attn_w_out[j])
        x = x + g1 * mixed
        h = rms_norm(x, norm_ffn_g[i]) * (1.0 + sc2) + sh2
        x = x + g2 * moe_ffn(h, router_w[i], router_b[i], expert_w_gu[i], expert_b_gu[i],
                             expert_w_down[i], expert_b_down[i])
    return x
```

```python
import functools

import jax
import jax.numpy as jnp
from jax import lax
from jax.experimental import pallas as pl
from jax.experimental.pallas import tpu as pltpu

NORM_EPS = 1e-6
N_MOD = 6
LRU_BLOCKS = 4
CONV_WIDTH = 4
LRU_C = 8.0
ATTN_HEADS = 8
MOBA_BLOCK = 256
MOBA_TOPK = 3
N_EXPERTS = 32
TOP_K = 4
SWIGLU_LIMIT = 7.0
SWIGLU_ALPHA = 1.702

LANES = 128
SUBLANES = 8
MASK_BIAS = -1e30
VMEM_LIMIT = 52 * 1024 * 1024

F32 = jnp.float32
BF16 = jnp.bfloat16
HIGHEST = lax.Precision.HIGHEST


def _cparams(sem):
    return pltpu.CompilerParams(dimension_semantics=sem, vmem_limit_bytes=VMEM_LIMIT)


def _sigmoid(z):
    return 1.0 / (1.0 + jnp.exp(-z))


def _modulated_norm(x, g, sc, sh):
    ms = jnp.mean(x * x, axis=-1, keepdims=True)
    return x * lax.rsqrt(ms + NORM_EPS) * g * (1.0 + sc) + sh


def _mod_kernel(c_ref, w_ref, b_ref, o_ref):
    c = c_ref[...]
    cond = c * _sigmoid(c)
    o_ref[...] = jnp.dot(cond, w_ref[...], preferred_element_type=F32,
                         precision=HIGHEST) + b_ref[...]


def _modulation(c, w_mod, b_mod):
    depth, d, _ = w_mod.shape
    b = c.shape[0]
    rows = -(-b // SUBLANES) * SUBLANES
    c_pad = jnp.pad(c, ((0, rows - b), (0, 0)))
    out = pl.pallas_call(
        _mod_kernel,
        grid=(depth, N_MOD),
        in_specs=[
            pl.BlockSpec((rows, d), lambda i, j: (0, 0)),
            pl.BlockSpec((None, d, d), lambda i, j: (i, 0, j)),
            pl.BlockSpec((None, None, 1, d), lambda i, j: (i, j, 0, 0)),
        ],
        out_specs=pl.BlockSpec((None, None, rows, d), lambda i, j: (i, j, 0, 0)),
        out_shape=jax.ShapeDtypeStruct((depth, N_MOD, rows, d), F32),
        compiler_params=_cparams(("arbitrary", "arbitrary")),
    )(c_pad, w_mod, b_mod.reshape(depth, N_MOD, 1, d))
    return out[:, :, :b].reshape(depth, N_MOD, b, 1, d)


def _mod_spec(j, d):
    return pl.BlockSpec((None, None, 1, d), lambda b, s: (j, b, 0, 0))


def _rglru_kernel(x_ref, sh_ref, sc_ref, g1_ref, ng_ref, win_ref, cw_ref, cb_ref,
                  wa_ref, ba_ref, wx_ref, bx_ref, lam_ref, wout_ref, o_ref,
                  ext_ref, a_ref, u_ref, h_ref):
    ts, d = x_ref.shape
    bw = d // LRU_BLOCKS

    @pl.when(pl.program_id(1) == 0)
    def _():
        ext_ref[0:SUBLANES, :] = jnp.zeros((SUBLANES, d), F32)
        h_ref[...] = jnp.zeros_like(h_ref)

    x = x_ref[...]
    h = _modulated_norm(x, ng_ref[...], sc_ref[...], sh_ref[...])
    gr = jnp.dot(h.astype(BF16), win_ref[...], preferred_element_type=F32)
    gate_branch = gr[:, :d]
    rec = gr[:, d:]

    ext_ref[SUBLANES:, :] = rec
    xc = cb_ref[...] + cw_ref[CONV_WIDTH - 1:CONV_WIDTH, :] * rec
    for k in range(CONV_WIDTH - 1):
        off = SUBLANES - (CONV_WIDTH - 1) + k
        xc = xc + cw_ref[k:k + 1, :] * ext_ref[off:off + ts, :]
    ext_ref[0:SUBLANES, :] = rec[ts - SUBLANES:, :]

    xcb = xc.astype(BF16)
    ra = jnp.concatenate(
        [jnp.dot(xcb[:, g * bw:(g + 1) * bw], wa_ref[g], preferred_element_type=F32)
         for g in range(LRU_BLOCKS)], axis=1) + ba_ref[...]
    rx = jnp.concatenate(
        [jnp.dot(xcb[:, g * bw:(g + 1) * bw], wx_ref[g], preferred_element_type=F32)
         for g in range(LRU_BLOCKS)], axis=1) + bx_ref[...]
    r = _sigmoid(ra)
    ig = _sigmoid(rx)
    z = -lam_ref[...]
    softplus = jnp.maximum(z, 0.0) + jnp.log(1.0 + jnp.exp(-jnp.abs(z)))
    a = jnp.exp(-LRU_C * r * softplus)
    u = jnp.sqrt(1.0 - a * a) * (ig * xc)

    row = lax.broadcasted_iota(jnp.int32, (ts, d), 0) % SUBLANES
    k = 1
    while k < SUBLANES:
        a_prev = pltpu.roll(a, k, 0)
        u_prev = pltpu.roll(u, k, 0)
        m = row >= k
        u = jnp.where(m, a * u_prev + u, u)
        a = jnp.where(m, a * a_prev, a)
        k *= 2
    a_ref[...] = a
    u_ref[...] = u

    def group(j, hc):
        r0 = pl.multiple_of(j * SUBLANES, SUBLANES)
        hs = u_ref[pl.ds(r0, SUBLANES), :] + a_ref[pl.ds(r0, SUBLANES), :] * hc
        u_ref[pl.ds(r0, SUBLANES), :] = hs
        return hs[SUBLANES - 1:SUBLANES, :]

    h_ref[0:1, :] = lax.fori_loop(0, ts // SUBLANES, group, h_ref[0:1, :])

    y = jax.nn.gelu(gate_branch, approximate=True) * u_ref[...]
    out = jnp.dot(y.astype(BF16), wout_ref[...], preferred_element_type=F32)
    o_ref[...] = x + g1_ref[...] * out


def _rglru_layer(x, mod, ng, w_in, conv_w, conv_b, w_a, b_a, w_x, b_x, lam, w_out, ts=256):
    b, s, d = x.shape
    full = lambda shape: pl.BlockSpec(shape, lambda bi, si: (0,) * len(shape))
    row = lambda v: v.reshape(1, d)
    return pl.pallas_call(
        _rglru_kernel,
        grid=(b, s // ts),
        in_specs=[
            pl.BlockSpec((None, ts, d), lambda bi, si: (bi, si, 0)),
            _mod_spec(0, d), _mod_spec(1, d), _mod_spec(2, d),
            full((1, d)), full((d, 2 * d)), full((CONV_WIDTH, d)), full((1, d)),
            full(w_a.shape), full((1, d)), full(w_x.shape), full((1, d)), full((1, d)),
            full((d, d)),
        ],
        out_specs=pl.BlockSpec((None, ts, d), lambda bi, si: (bi, si, 0)),
        out_shape=jax.ShapeDtypeStruct(x.shape, F32),
        scratch_shapes=[
            pltpu.VMEM((ts + SUBLANES, d), F32),
            pltpu.VMEM((ts, d), F32),
            pltpu.VMEM((ts, d), F32),
            pltpu.VMEM((SUBLANES, d), F32),
        ],
        compiler_params=_cparams(("arbitrary", "arbitrary")),
    )(x, mod, mod, mod, row(ng), w_in.astype(BF16), conv_w, row(conv_b),
      w_a.astype(BF16), row(b_a), w_x.astype(BF16), row(b_x), row(lam), w_out.astype(BF16))


def _qkv_kernel(x_ref, sh_ref, sc_ref, ng_ref, w_ref, qg_ref, kg_ref,
                q_ref, kext_ref, v_ref, km_ref):
    ts, d = x_ref.shape
    nh, _, hd = q_ref.shape
    blk = pl.program_id(1)
    h = _modulated_norm(x_ref[...], ng_ref[...], sc_ref[...], sh_ref[...])
    qkv = jnp.dot(h.astype(BF16), w_ref[...], preferred_element_type=F32)
    lane = lax.broadcasted_iota(jnp.int32, (ts, hd), 1)
    onehot = jnp.where(lane == blk, 1.0, 0.0).astype(BF16)
    for hh in range(nh):
        qh = qkv[:, hh * hd:(hh + 1) * hd]
        qn = qh * lax.rsqrt(jnp.mean(qh * qh, axis=-1, keepdims=True) + NORM_EPS)
        q_ref[hh] = (qn * qg_ref[...] * (hd ** -0.5)).astype(BF16)
        kh = qkv[:, d + hh * hd:d + (hh + 1) * hd]
        kn = kh * lax.rsqrt(jnp.mean(kh * kh, axis=-1, keepdims=True) + NORM_EPS) * kg_ref[...]
        kext_ref[hh, :, 0:hd] = kn.astype(BF16)
        kext_ref[hh, :, hd:2 * hd] = onehot
        km_ref[:, hh * hd:(hh + 1) * hd] = jnp.mean(kn, axis=0, keepdims=True)
        v_ref[hh] = qkv[:, 2 * d + hh * hd:2 * d + (hh + 1) * hd].astype(BF16)


def _attn_kernel(q_ref, kext_ref, v_ref, km_ref, o_ref):
    tq, hd = q_ref.shape
    i = pl.program_id(2)
    q = q_ref[...]
    nt = (((1,), (1,)), ((), ()))

    gate = lax.dot_general(q.astype(F32), km_ref[...], nt, precision=HIGHEST,
                           preferred_element_type=F32)
    lane = lax.broadcasted_iota(jnp.int32, (tq, LANES), 1)
    lane_f = lane.astype(F32)
    past = lane < i
    g = jnp.where(past, gate, -jnp.inf)
    sel = lane == i
    for _ in range(MOBA_TOPK):
        m = jnp.max(g, axis=-1, keepdims=True)
        idx = jnp.min(jnp.where(g == m, lane_f, float(LANES)), axis=-1, keepdims=True)
        hit = lane_f == idx
        sel = sel | (hit & past)
        g = jnp.where(hit, -jnp.inf, g)
    bias = jnp.where(sel, 0.0, MASK_BIAS).astype(BF16)
    q_ext = jnp.concatenate([q, bias], axis=1)

    r0 = pl.multiple_of(i * tq, tq)
    s = lax.dot_general(q_ext, kext_ref[pl.ds(r0, tq), :], nt, preferred_element_type=F32)
    rr = lax.broadcasted_iota(jnp.int32, (tq, tq), 0)
    cc = lax.broadcasted_iota(jnp.int32, (tq, tq), 1)
    s = jnp.where(cc <= rr, s, MASK_BIAS)
    m0 = jnp.max(s, axis=-1, keepdims=True)
    p = jnp.exp(s - m0)
    l0 = jnp.sum(p, axis=-1, keepdims=True)
    acc0 = jnp.dot(p.astype(BF16), v_ref[pl.ds(r0, tq), :], preferred_element_type=F32)

    def past_block(j, carry):
        m, l, acc = carry
        c0 = pl.multiple_of(j * tq, tq)
        sj = lax.dot_general(q_ext, kext_ref[pl.ds(c0, tq), :], nt, preferred_element_type=F32)
        m_new = jnp.maximum(m, jnp.max(sj, axis=-1, keepdims=True))
        alpha = jnp.exp(m - m_new)
        pj = jnp.exp(sj - m_new)
        l = alpha * l + jnp.sum(pj, axis=-1, keepdims=True)
        acc = alpha * acc + jnp.dot(pj.astype(BF16), v_ref[pl.ds(c0, tq), :],
                                    preferred_element_type=F32)
        return m_new, l, acc

    _, l, acc = lax.fori_loop(0, i, past_block, (m0, l0, acc0))
    o_ref[...] = (acc / l).astype(BF16)


def _attn_out_kernel(o_ref, x_ref, g1_ref, w_ref, y_ref):
    nh = o_ref.shape[0]
    o = jnp.concatenate([o_ref[hh] for hh in range(nh)], axis=1)
    out = jnp.dot(o, w_ref[...], preferred_element_type=F32)
    y_ref[...] = x_ref[...] + g1_ref[...] * out


def _moba_layer(x, mod, ng, w_qkv, q_g, k_g, w_out, ts_out=512):
    b, s, d = x.shape
    nh = ATTN_HEADS
    hd = d // nh
    tq = MOBA_BLOCK
    nb = s // tq
    assert s % tq == 0 and nb <= LANES and hd == LANES
    full = lambda shape: pl.BlockSpec(shape, lambda bi, si: (0,) * len(shape))

    q, kext, v, km = pl.pallas_call(
        _qkv_kernel,
        grid=(b, nb),
        in_specs=[
            pl.BlockSpec((None, tq, d), lambda bi, si: (bi, si, 0)),
            _mod_spec(0, d), _mod_spec(1, d),
            full((1, d)), full((d, 3 * d)), full((1, hd)), full((1, hd)),
        ],
        out_specs=[
            pl.BlockSpec((None, nh, tq, hd), lambda bi, si: (bi, 0, si, 0)),
            pl.BlockSpec((None, nh, tq, 2 * hd), lambda bi, si: (bi, 0, si, 0)),
            pl.BlockSpec((None, nh, tq, hd), lambda bi, si: (bi, 0, si, 0)),
            pl.BlockSpec((None, None, 1, d), lambda bi, si: (bi, si, 0, 0)),
        ],
        out_shape=[
            jax.ShapeDtypeStruct((b, nh, s, hd), BF16),
            jax.ShapeDtypeStruct((b, nh, s, 2 * hd), BF16),
            jax.ShapeDtypeStruct((b, nh, s, hd), BF16),
            jax.ShapeDtypeStruct((b, nb, 1, d), F32),
        ],
        compiler_params=_cparams(("arbitrary", "arbitrary")),
    )(x, mod, mod, ng.reshape(1, d), w_qkv.astype(BF16), q_g.reshape(1, hd), k_g.reshape(1, hd))

    km = km.reshape(b, nb, nh, hd).transpose(0, 2, 1, 3)
    km = jnp.pad(km, ((0, 0), (0, 0), (0, LANES - nb), (0, 0)))

    o = pl.pallas_call(
        _attn_kernel,
        grid=(b, nh, nb),
        in_specs=[
            pl.BlockSpec((None, None, tq, hd), lambda bi, hi, qi: (bi, hi, qi, 0)),
            pl.BlockSpec((None, None, s, 2 * hd), lambda bi, hi, qi: (bi, hi, 0, 0)),
            pl.BlockSpec((None, None, s, hd), lambda bi, hi, qi: (bi, hi, 0, 0)),
            pl.BlockSpec((None, None, LANES, hd), lambda bi, hi, qi: (bi, hi, 0, 0)),
        ],
        out_specs=pl.BlockSpec((None, None, tq, hd), lambda bi, hi, qi: (bi, hi, qi, 0)),
        out_shape=jax.ShapeDtypeStruct((b, nh, s, hd), BF16),
        compiler_params=_cparams(("arbitrary", "arbitrary", "arbitrary")),
    )(q, kext, v, km)

    return pl.pallas_call(
        _attn_out_kernel,
        grid=(b, s // ts_out),
        in_specs=[
            pl.BlockSpec((None, nh, ts_out, hd), lambda bi, si: (bi, 0, si, 0)),
            pl.BlockSpec((None, ts_out, d), lambda bi, si: (bi, si, 0)),
            _mod_spec(2, d),
            full((d, d)),
        ],
        out_specs=pl.BlockSpec((None, ts_out, d), lambda bi, si: (bi, si, 0)),
        out_shape=jax.ShapeDtypeStruct(x.shape, F32),
        compiler_params=_cparams(("arbitrary", "arbitrary")),
    )(o, x, mod, w_out.astype(BF16))


def _route_kernel(x_ref, sh_ref, sc_ref, ng_ref, rw_ref, rb_ref,
                  h_ref, e_ref, w_ref, r_ref, cnt_ref, run_ref):
    tm = x_ref.shape[0]

    @pl.when(pl.program_id(0) == 0)
    def _():
        run_ref[...] = jnp.zeros_like(run_ref)

    h = _modulated_norm(x_ref[...], ng_ref[...], sc_ref[...], sh_ref[...])
    h_ref[...] = h
    logits = jnp.dot(h, rw_ref[...], preferred_element_type=F32, precision=HIGHEST) + rb_ref[...]
    lane = lax.broadcasted_iota(jnp.int32, (tm, LANES), 1)
    lane_f = lane.astype(F32)
    lg = jnp.where(lane < N_EXPERTS, logits, -jnp.inf)
    tops, hits = [], []
    for _ in range(TOP_K):
        m = jnp.max(lg, axis=-1, keepdims=True)
        idx = jnp.min(jnp.where(lg == m, lane_f, float(LANES)), axis=-1, keepdims=True)
        hit = lane_f == idx
        tops.append((m, idx))
        hits.append(hit)
        lg = jnp.where(hit, -jnp.inf, lg)
    exps = [jnp.exp(m - tops[0][0]) for m, _ in tops]
    denom = exps[0]
    for ex in exps[1:]:
        denom = denom + ex

    member = jnp.zeros((tm, LANES), F32)
    for hit in hits:
        member = member + jnp.where(hit, 1.0, 0.0)
    rr = lax.broadcasted_iota(jnp.int32, (tm, tm), 0)
    cc = lax.broadcasted_iota(jnp.int32, (tm, tm), 1)
    earlier = jnp.where(cc < rr, 1.0, 0.0).astype(BF16)
    before = jnp.dot(earlier, member.astype(BF16), preferred_element_type=F32) + run_ref[...]

    e_out = jnp.zeros((tm, LANES), F32)
    w_out = jnp.zeros((tm, LANES), F32)
    r_out = jnp.zeros((tm, LANES), F32)
    for k in range(TOP_K):
        rank = jnp.sum(jnp.where(hits[k], before, 0.0), axis=-1, keepdims=True)
        e_out = jnp.where(lane == k, tops[k][1], e_out)
        w_out = jnp.where(lane == k, exps[k] / denom, w_out)
        r_out = jnp.where(lane == k, rank, r_out)
    e_ref[...] = e_out.astype(jnp.int32)
    w_ref[...] = w_out
    r_ref[...] = r_out.astype(jnp.int32)
    run = run_ref[...] + jnp.sum(member, axis=0, keepdims=True)
    run_ref[...] = run
    cnt_ref[...] = run.astype(jnp.int32)


def _dispatch_kernel(dest_ref, h_hbm, xs_in_hbm, xs_hbm, sem):
    del xs_in_hbm
    n = dest_ref.shape[1]
    tm = n // TOP_K
    base = pl.program_id(0) * tm

    def issue(t, _):
        for k in range(TOP_K):
            pltpu.make_async_copy(h_hbm.at[pl.ds(base + t, 1), :],
                                  xs_hbm.at[pl.ds(dest_ref[0, t * TOP_K + k], 1), :],
                                  sem).start()
        return 0

    lax.fori_loop(0, tm, issue, 0)
    pltpu.make_async_copy(h_hbm.at[pl.ds(0, n), :], xs_hbm.at[pl.ds(0, n), :], sem).wait()


def _expert_kernel(be_ref, first_ref, nu_ref, xs_ref, wgu_ref, bgu_ref, wd_ref, bd_ref,
                   o_ref, wgu_bf, wd_bf):
    i = pl.program_id(0)
    de = wd_ref.shape[0]

    @pl.when(i < nu_ref[0])
    def _():
        @pl.when(first_ref[i] == 1)
        def _():
            wgu_bf[...] = wgu_ref[...].astype(BF16)
            wd_bf[...] = wd_ref[...].astype(BF16)

        gu = jnp.dot(xs_ref[...].astype(BF16), wgu_bf[...], preferred_element_type=F32) + bgu_ref[...]
        gate = jnp.minimum(gu[:, :de], SWIGLU_LIMIT)
        up = jnp.clip(gu[:, de:], -SWIGLU_LIMIT, SWIGLU_LIMIT)
        act = gate * _sigmoid(SWIGLU_ALPHA * gate) * (up + 1.0)
        o_ref[...] = jnp.dot(act.astype(BF16), wd_bf[...], preferred_element_type=F32) + bd_ref[...]

    @pl.when(i >= nu_ref[0])
    def _():
        o_ref[...] = jnp.zeros_like(o_ref)


def _combine_kernel(dest_ref, x_ref, w_ref, g2_ref, y_hbm, o_ref, buf, sem):
    tm = x_ref.shape[0]

    def issue(r, _):
        pltpu.make_async_copy(y_hbm.at[pl.ds(dest_ref[0, r], 1), :], buf.at[pl.ds(r, 1), :],
                              sem).start()
        return 0

    lax.fori_loop(0, tm, issue, 0)
    pltpu.make_async_copy(y_hbm.at[pl.ds(0, tm), :], buf, sem).wait()
    o_ref[...] = x_ref[...] + g2_ref[...] * (w_ref[...] * buf[...])


def _tok_mod_spec(j, tm, s, d):
    return pl.BlockSpec((None, None, 1, d), lambda i: (j, (i * tm) // s, 0, 0))


def _moe_route(xt, s, mod, ng, router_w, router_b, tm=256):
    t, d = xt.shape
    rw = jnp.pad(router_w, ((0, 0), (0, LANES - N_EXPERTS)))
    rb = jnp.pad(router_b, (0, LANES - N_EXPERTS)).reshape(1, LANES)
    return pl.pallas_call(
        _route_kernel,
        grid=(t // tm,),
        in_specs=[
            pl.BlockSpec((tm, d), lambda i: (i, 0)),
            _tok_mod_spec(3, tm, s, d), _tok_mod_spec(4, tm, s, d),
            pl.BlockSpec((1, d), lambda i: (0, 0)),
            pl.BlockSpec((d, LANES), lambda i: (0, 0)),
            pl.BlockSpec((1, LANES), lambda i: (0, 0)),
        ],
        out_specs=[
            pl.BlockSpec((tm, d), lambda i: (i, 0)),
            pl.BlockSpec((tm, LANES), lambda i: (i, 0)),
            pl.BlockSpec((tm, LANES), lambda i: (i, 0)),
            pl.BlockSpec((tm, LANES), lambda i: (i, 0)),
            pl.BlockSpec((1, LANES), lambda i: (0, 0)),
        ],
        out_shape=[
            jax.ShapeDtypeStruct((t, d), F32),
            jax.ShapeDtypeStruct((t, LANES), jnp.int32),
            jax.ShapeDtypeStruct((t, LANES), F32),
            jax.ShapeDtypeStruct((t, LANES), jnp.int32),
            jax.ShapeDtypeStruct((1, LANES), jnp.int32),
        ],
        scratch_shapes=[pltpu.VMEM((1, LANES), F32)],
        compiler_params=_cparams(("arbitrary",)),
    )(xt, mod, mod, ng.reshape(1, d), rw, rb)


def _moe_plan(top_e, rank, counts, tm):
    t = top_e.shape[0]
    n_e = N_EXPERTS
    counts = counts[0, :n_e]
    padded = (counts + tm - 1) // tm * tm
    pad_end = jnp.cumsum(padded)
    pad_start = pad_end - padded
    dest = pad_start[top_e[:, :TOP_K]] + rank[:, :TOP_K]
    n_tiles = (t * TOP_K) // tm + n_e
    tile_start = jnp.arange(n_tiles, dtype=jnp.int32) * tm
    n_used = (pad_end[-1] // tm).astype(jnp.int32)
    tile_expert = jnp.minimum(jnp.searchsorted(pad_end, tile_start, side='right'), n_e - 1)
    tile_expert = jnp.where(jnp.arange(n_tiles) < n_used, tile_expert,
                            tile_expert[jnp.maximum(n_used - 1, 0)]).astype(jnp.int32)
    first = jnp.concatenate([jnp.ones((1,), jnp.int32),
                             (tile_expert[1:] != tile_expert[:-1]).astype(jnp.int32)])
    return dest.astype(jnp.int32), tile_expert, first, n_used.reshape(1), n_tiles


def _moe_dispatch(h2, dest, n_rows, tm=256):
    t, d = h2.shape
    return pl.pallas_call(
        _dispatch_kernel,
        grid=(t // tm,),
        in_specs=[
            pl.BlockSpec((None, 1, tm * TOP_K), lambda i: (i, 0, 0), memory_space=pltpu.SMEM),
            pl.BlockSpec(memory_space=pl.ANY),
            pl.BlockSpec(memory_space=pl.ANY),
        ],
        out_specs=pl.BlockSpec(memory_space=pl.ANY),
        out_shape=jax.ShapeDtypeStruct((n_rows, d), F32),
        scratch_shapes=[pltpu.SemaphoreType.DMA],
        input_output_aliases={2: 0},
        compiler_params=_cparams(("arbitrary",)),
    )(dest.reshape(t // tm, 1, tm * TOP_K), h2, jnp.zeros((n_rows, d), F32))


def _moe_experts(xs, tile_expert, first, n_used, w_gu, b_gu, w_down, b_down, layer, tm):
    n_rows, d = xs.shape
    n_e = N_EXPERTS
    de = w_down.shape[2]
    tile_idx = lambda i, be, fi, nu: (jnp.minimum(i, nu[0] - 1), 0)
    w_idx = lambda i, be, fi, nu: (layer, be[i], 0, 0)
    return pl.pallas_call(
        _expert_kernel,
        grid_spec=pltpu.PrefetchScalarGridSpec(
            num_scalar_prefetch=3,
            grid=(n_rows // tm,),
            in_specs=[
                pl.BlockSpec((tm, d), tile_idx),
                pl.BlockSpec((None, None, d, 2 * de), w_idx),
                pl.BlockSpec((None, None, 1, 2 * de), w_idx),
                pl.BlockSpec((None, None, de, d), w_idx),
                pl.BlockSpec((None, None, 1, d), w_idx),
            ],
            out_specs=pl.BlockSpec((tm, d), lambda i, be, fi, nu: (i, 0)),
            scratch_shapes=[pltpu.VMEM((d, 2 * de), BF16), pltpu.VMEM((de, d), BF16)],
        ),
        out_shape=jax.ShapeDtypeStruct((n_rows, d), F32),
        compiler_params=_cparams(("arbitrary",)),
    )(tile_expert, first, n_used, xs, w_gu, b_gu.reshape(b_gu.shape[0], n_e, 1, 2 * de),
      w_down, b_down.reshape(b_down.shape[0], n_e, 1, d))


def _moe_combine(xt, s, mod, dest_item, w_item, y_rows, tm=256):
    t, d = xt.shape
    return pl.pallas_call(
        _combine_kernel,
        grid=(t // tm,),
        in_specs=[
            pl.BlockSpec((None, 1, tm), lambda i: (i, 0, 0), memory_space=pltpu.SMEM),
            pl.BlockSpec((tm, d), lambda i: (i, 0)),
            pl.BlockSpec((tm, 1), lambda i: (i, 0)),
            _tok_mod_spec(5, tm, s, d),
            pl.BlockSpec(memory_space=pl.ANY),
        ],
        out_specs=pl.BlockSpec((tm, d), lambda i: (i, 0)),
        out_shape=jax.ShapeDtypeStruct((t, d), F32),
        scratch_shapes=[pltpu.VMEM((tm, d), F32), pltpu.SemaphoreType.DMA],
        compiler_params=_cparams(("arbitrary",)),
    )(dest_item.reshape(t // tm, 1, tm), xt, w_item, mod, y_rows)


def _moe_layer(x, mod, ng, router_w, router_b, w_gu, b_gu, w_down, b_down, layer, tm_exp=256):
    b, s, d = x.shape
    t = b * s
    xt = x.reshape(t, d)
    h2, top_e, top_w, rank, counts = _moe_route(xt, s, mod, ng, router_w, router_b)
    dest, tile_expert, first, n_used, n_tiles = _moe_plan(top_e, rank, counts, tm_exp)
    xs = _moe_dispatch(h2, dest, n_tiles * tm_exp)
    y_rows = _moe_experts(xs, tile_expert, first, n_used, w_gu, b_gu, w_down, b_down, layer, tm_exp)
    dest_item = dest.reshape(-1)[:t]
    w_item = top_w[:, :TOP_K].reshape(-1)[:t].reshape(t, 1)
    return _moe_combine(xt, s, mod, dest_item, w_item, y_rows).reshape(b, s, d)


def kernel(x, c, norm_mix_g, norm_ffn_g, w_mod, b_mod, lru_w_in, lru_conv_w, lru_conv_b, lru_w_a, lru_b_a, lru_w_x, lru_b_x, lru_lambda, lru_w_out, attn_w_qkv, attn_q_norm_g, attn_k_norm_g, attn_w_out, router_w, router_b, expert_w_gu, expert_b_gu, expert_w_down, expert_b_down):
    depth = w_mod.shape[0]
    mods = _modulation(c, w_mod, b_mod)
    for i in range(depth):
        mod = mods[i]
        j = i // 2
        if i % 2 == 0:
            x = _rglru_layer(x, mod, norm_mix_g[i], lru_w_in[j], lru_conv_w[j], lru_conv_b[j],
                             lru_w_a[j], lru_b_a[j], lru_w_x[j], lru_b_x[j], lru_lambda[j],
                             lru_w_out[j])
        else:
            x = _moba_layer(x, mod, norm_mix_g[i], attn_w_qkv[j], attn_q_norm_g[j],
                            attn_k_norm_g[j], attn_w_out[j])
        x = _moe_layer(x, mod, norm_ffn_g[i], router_w[i], router_b[i], expert_w_gu,
                       expert_b_gu, expert_w_down, expert_b_down, i)
    return x
```

```python
import functools

import jax
import jax.numpy as jnp
from jax import lax
from jax.experimental import pallas as pl
from jax.experimental.pallas import tpu as pltpu

NORM_EPS = 1e-6
N_MOD = 6
LRU_BLOCKS = 4
CONV_WIDTH = 4
LRU_C = 8.0
ATTN_HEADS = 8
MOBA_BLOCK = 256
MOBA_TOPK = 3
N_EXPERTS = 32
TOP_K = 4
SWIGLU_LIMIT = 7.0
SWIGLU_ALPHA = 1.702

LANES = 128
SUBLANES = 8
MASK_BIAS = -1e30
VMEM_LIMIT = 52 * 1024 * 1024

F32 = jnp.float32
BF16 = jnp.bfloat16
HIGHEST = lax.Precision.HIGHEST


def _cparams(sem):
    return pltpu.CompilerParams(dimension_semantics=sem, vmem_limit_bytes=VMEM_LIMIT)


def _sigmoid(z):
    return 1.0 / (1.0 + jnp.exp(-z))


def _modulated_norm(x, g, sc, sh):
    ms = jnp.mean(x * x, axis=-1, keepdims=True)
    return x * lax.rsqrt(ms + NORM_EPS) * g * (1.0 + sc) + sh


def _mod_kernel(c_ref, w_ref, b_ref, o_ref):
    c = c_ref[...]
    cond = c * _sigmoid(c)
    o_ref[...] = jnp.dot(cond, w_ref[...], preferred_element_type=F32,
                         precision=HIGHEST) + b_ref[...]


def _modulation(c, w_mod, b_mod):
    depth, d, _ = w_mod.shape
    b = c.shape[0]
    rows = -(-b // SUBLANES) * SUBLANES
    c_pad = jnp.pad(c, ((0, rows - b), (0, 0)))
    out = pl.pallas_call(
        _mod_kernel,
        grid=(depth, N_MOD),
        in_specs=[
            pl.BlockSpec((rows, d), lambda i, j: (0, 0)),
            pl.BlockSpec((None, d, d), lambda i, j: (i, 0, j)),
            pl.BlockSpec((None, None, 1, d), lambda i, j: (i, j, 0, 0)),
        ],
        out_specs=pl.BlockSpec((None, None, rows, d), lambda i, j: (i, j, 0, 0)),
        out_shape=jax.ShapeDtypeStruct((depth, N_MOD, rows, d), F32),
        compiler_params=_cparams(("arbitrary", "arbitrary")),
    )(c_pad, w_mod, b_mod.reshape(depth, N_MOD, 1, d))
    return out[:, :, :b].reshape(depth, N_MOD, b, 1, d)


def _mod_spec(j, d):
    return pl.BlockSpec((None, None, 1, d), lambda b, s: (j, b, 0, 0))


def _rglru_kernel(x_ref, sh_ref, sc_ref, g1_ref, ng_ref, win_ref, cw_ref, cb_ref,
                  wa_ref, ba_ref, wx_ref, bx_ref, lam_ref, wout_ref, o_ref,
                  ext_ref, a_ref, u_ref, h_ref):
    ts, d = x_ref.shape
    bw = d // LRU_BLOCKS

    @pl.when(pl.program_id(1) == 0)
    def _():
        ext_ref[0:SUBLANES, :] = jnp.zeros((SUBLANES, d), F32)
        h_ref[...] = jnp.zeros_like(h_ref)

    x = x_ref[...]
    h = _modulated_norm(x, ng_ref[...], sc_ref[...], sh_ref[...])
    gr = jnp.dot(h.astype(BF16), win_ref[...], preferred_element_type=F32)
    gate_branch = gr[:, :d]
    rec = gr[:, d:]

    ext_ref[SUBLANES:, :] = rec
    xc = cb_ref[...] + cw_ref[CONV_WIDTH - 1:CONV_WIDTH, :] * rec
    for k in range(CONV_WIDTH - 1):
        off = SUBLANES - (CONV_WIDTH - 1) + k
        xc = xc + cw_ref[k:k + 1, :] * ext_ref[off:off + ts, :]
    ext_ref[0:SUBLANES, :] = rec[ts - SUBLANES:, :]

    xcb = xc.astype(BF16)
    ra = jnp.concatenate(
        [jnp.dot(xcb[:, g * bw:(g + 1) * bw], wa_ref[g], preferred_element_type=F32)
         for g in range(LRU_BLOCKS)], axis=1) + ba_ref[...]
    rx = jnp.concatenate(
        [jnp.dot(xcb[:, g * bw:(g + 1) * bw], wx_ref[g], preferred_element_type=F32)
         for g in range(LRU_BLOCKS)], axis=1) + bx_ref[...]
    r = _sigmoid(ra)
    ig = _sigmoid(rx)
    z = -lam_ref[...]
    softplus = jnp.maximum(z, 0.0) + jnp.log(1.0 + jnp.exp(-jnp.abs(z)))
    a = jnp.exp(-LRU_C * r * softplus)
    u = jnp.sqrt(1.0 - a * a) * (ig * xc)

    row = lax.broadcasted_iota(jnp.int32, (ts, d), 0) % SUBLANES
    k = 1
    while k < SUBLANES:
        a_prev = pltpu.roll(a, k, 0)
        u_prev = pltpu.roll(u, k, 0)
        m = row >= k
        u = jnp.where(m, a * u_prev + u, u)
        a = jnp.where(m, a * a_prev, a)
        k *= 2
    a_ref[...] = a
    u_ref[...] = u

    def group(j, hc):
        r0 = pl.multiple_of(j * SUBLANES, SUBLANES)
        hs = u_ref[pl.ds(r0, SUBLANES), :] + a_ref[pl.ds(r0, SUBLANES), :] * hc
        u_ref[pl.ds(r0, SUBLANES), :] = hs
        return hs[SUBLANES - 1:SUBLANES, :]

    h_ref[0:1, :] = lax.fori_loop(0, ts // SUBLANES, group, h_ref[0:1, :])

    y = jax.nn.gelu(gate_branch, approximate=True) * u_ref[...]
    out = jnp.dot(y.astype(BF16), wout_ref[...], preferred_element_type=F32)
    o_ref[...] = x + g1_ref[...] * out


def _rglru_layer(x, mod, ng, w_in, conv_w, conv_b, w_a, b_a, w_x, b_x, lam, w_out, ts=256):
    b, s, d = x.shape
    full = lambda shape: pl.BlockSpec(shape, lambda bi, si: (0,) * len(shape))
    row = lambda v: v.reshape(1, d)
    return pl.pallas_call(
        _rglru_kernel,
        grid=(b, s // ts),
        in_specs=[
            pl.BlockSpec((None, ts, d), lambda bi, si: (bi, si, 0)),
            _mod_spec(0, d), _mod_spec(1, d), _mod_spec(2, d),
            full((1, d)), full((d, 2 * d)), full((CONV_WIDTH, d)), full((1, d)),
            full(w_a.shape), full((1, d)), full(w_x.shape), full((1, d)), full((1, d)),
            full((d, d)),
        ],
        out_specs=pl.BlockSpec((None, ts, d), lambda bi, si: (bi, si, 0)),
        out_shape=jax.ShapeDtypeStruct(x.shape, F32),
        scratch_shapes=[
            pltpu.VMEM((ts + SUBLANES, d), F32),
            pltpu.VMEM((ts, d), F32),
            pltpu.VMEM((ts, d), F32),
            pltpu.VMEM((SUBLANES, d), F32),
        ],
        compiler_params=_cparams(("arbitrary", "arbitrary")),
    )(x, mod, mod, mod, row(ng), w_in.astype(BF16), conv_w, row(conv_b),
      w_a.astype(BF16), row(b_a), w_x.astype(BF16), row(b_x), row(lam), w_out.astype(BF16))


def _qkv_kernel(x_ref, sh_ref, sc_ref, ng_ref, w_ref, qg_ref, kg_ref,
                q_ref, kext_ref, v_ref, km_ref):
    ts, d = x_ref.shape
    nh, _, hd = q_ref.shape
    blk = pl.program_id(1)
    h = _modulated_norm(x_ref[...], ng_ref[...], sc_ref[...], sh_ref[...])
    qkv = jnp.dot(h.astype(BF16), w_ref[...], preferred_element_type=F32)
    lane = lax.broadcasted_iota(jnp.int32, (ts, hd), 1)
    onehot = jnp.where(lane == blk, 1.0, 0.0).astype(BF16)
    for hh in range(nh):
        qh = qkv[:, hh * hd:(hh + 1) * hd]
        qn = qh * lax.rsqrt(jnp.mean(qh * qh, axis=-1, keepdims=True) + NORM_EPS)
        q_ref[hh] = (qn * qg_ref[...] * (hd ** -0.5)).astype(BF16)
        kh = qkv[:, d + hh * hd:d + (hh + 1) * hd]
        kn = kh * lax.rsqrt(jnp.mean(kh * kh, axis=-1, keepdims=True) + NORM_EPS) * kg_ref[...]
        kext_ref[hh, :, 0:hd] = kn.astype(BF16)
        kext_ref[hh, :, hd:2 * hd] = onehot
        km_ref[:, hh * hd:(hh + 1) * hd] = jnp.mean(kn, axis=0, keepdims=True)
        v_ref[hh] = qkv[:, 2 * d + hh * hd:2 * d + (hh + 1) * hd].astype(BF16)


def _attn_kernel(q_ref, kext_ref, v_ref, km_ref, o_ref):
    tq, hd = q_ref.shape
    i = pl.program_id(2)
    q = q_ref[...]
    nt = (((1,), (1,)), ((), ()))

    gate = lax.dot_general(q.astype(F32), km_ref[...], nt, precision=HIGHEST,
                           preferred_element_type=F32)
    lane = lax.broadcasted_iota(jnp.int32, (tq, LANES), 1)
    lane_f = lane.astype(F32)
    past = lane < i
    g = jnp.where(past, gate, -jnp.inf)
    sel = lane == i
    for _ in range(MOBA_TOPK):
        m = jnp.max(g, axis=-1, keepdims=True)
        idx = jnp.min(jnp.where(g == m, lane_f, float(LANES)), axis=-1, keepdims=True)
        hit = lane_f == idx
        sel = sel | (hit & past)
        g = jnp.where(hit, -jnp.inf, g)
    bias = jnp.where(sel, 0.0, MASK_BIAS).astype(BF16)
    q_ext = jnp.concatenate([q, bias], axis=1)

    r0 = pl.multiple_of(i * tq, tq)
    s = lax.dot_general(q_ext, kext_ref[pl.ds(r0, tq), :], nt, preferred_element_type=F32)
    rr = lax.broadcasted_iota(jnp.int32, (tq, tq), 0)
    cc = lax.broadcasted_iota(jnp.int32, (tq, tq), 1)
    s = jnp.where(cc <= rr, s, MASK_BIAS)
    m0 = jnp.max(s, axis=-1, keepdims=True)
    p = jnp.exp(s - m0)
    l0 = jnp.sum(p, axis=-1, keepdims=True)
    acc0 = jnp.dot(p.astype(BF16), v_ref[pl.ds(r0, tq), :], preferred_element_type=F32)

    def past_block(j, carry):
        m, l, acc = carry
        c0 = pl.multiple_of(j * tq, tq)
        sj = lax.dot_general(q_ext, kext_ref[pl.ds(c0, tq), :], nt, preferred_element_type=F32)
        m_new = jnp.maximum(m, jnp.max(sj, axis=-1, keepdims=True))
        alpha = jnp.exp(m - m_new)
        pj = jnp.exp(sj - m_new)
        l = alpha * l + jnp.sum(pj, axis=-1, keepdims=True)
        acc = alpha * acc + jnp.dot(pj.astype(BF16), v_ref[pl.ds(c0, tq), :],
                                    preferred_element_type=F32)
        return m_new, l, acc

    _, l, acc = lax.fori_loop(0, i, past_block, (m0, l0, acc0))
    o_ref[...] = (acc / l).astype(BF16)


def _attn_out_kernel(o_ref, x_ref, g1_ref, w_ref, y_ref):
    nh = o_ref.shape[0]
    o = jnp.concatenate([o_ref[hh] for hh in range(nh)], axis=1)
    out = jnp.dot(o, w_ref[...], preferred_element_type=F32)
    y_ref[...] = x_ref[...] + g1_ref[...] * out


def _moba_layer(x, mod, ng, w_qkv, q_g, k_g, w_out, ts_out=512):
    b, s, d = x.shape
    nh = ATTN_HEADS
    hd = d // nh
    tq = MOBA_BLOCK
    nb = s // tq
    assert s % tq == 0 and nb <= LANES and hd == LANES
    full = lambda shape: pl.BlockSpec(shape, lambda bi, si: (0,) * len(shape))

    q, kext, v, km = pl.pallas_call(
        _qkv_kernel,
        grid=(b, nb),
        in_specs=[
            pl.BlockSpec((None, tq, d), lambda bi, si: (bi, si, 0)),
            _mod_spec(0, d), _mod_spec(1, d),
            full((1, d)), full((d, 3 * d)), full((1, hd)), full((1, hd)),
        ],
        out_specs=[
            pl.BlockSpec((None, nh, tq, hd), lambda bi, si: (bi, 0, si, 0)),
            pl.BlockSpec((None, nh, tq, 2 * hd), lambda bi, si: (bi, 0, si, 0)),
            pl.BlockSpec((None, nh, tq, hd), lambda bi, si: (bi, 0, si, 0)),
            pl.BlockSpec((None, None, 1, d), lambda bi, si: (bi, si, 0, 0)),
        ],
        out_shape=[
            jax.ShapeDtypeStruct((b, nh, s, hd), BF16),
            jax.ShapeDtypeStruct((b, nh, s, 2 * hd), BF16),
            jax.ShapeDtypeStruct((b, nh, s, hd), BF16),
            jax.ShapeDtypeStruct((b, nb, 1, d), F32),
        ],
        compiler_params=_cparams(("arbitrary", "arbitrary")),
    )(x, mod, mod, ng.reshape(1, d), w_qkv.astype(BF16), q_g.reshape(1, hd), k_g.reshape(1, hd))

    km = km.reshape(b, nb, nh, hd).transpose(0, 2, 1, 3)
    km = jnp.pad(km, ((0, 0), (0, 0), (0, LANES - nb), (0, 0)))

    o = pl.pallas_call(
        _attn_kernel,
        grid=(b, nh, nb),
        in_specs=[
            pl.BlockSpec((None, None, tq, hd), lambda bi, hi, qi: (bi, hi, qi, 0)),
            pl.BlockSpec((None, None, s, 2 * hd), lambda bi, hi, qi: (bi, hi, 0, 0)),
            pl.BlockSpec((None, None, s, hd), lambda bi, hi, qi: (bi, hi, 0, 0)),
            pl.BlockSpec((None, None, LANES, hd), lambda bi, hi, qi: (bi, hi, 0, 0)),
        ],
        out_specs=pl.BlockSpec((None, None, tq, hd), lambda bi, hi, qi: (bi, hi, qi, 0)),
        out_shape=jax.ShapeDtypeStruct((b, nh, s, hd), BF16),
        compiler_params=_cparams(("arbitrary", "arbitrary", "arbitrary")),
    )(q, kext, v, km)

    return pl.pallas_call(
        _attn_out_kernel,
        grid=(b, s // ts_out),
        in_specs=[
            pl.BlockSpec((None, nh, ts_out, hd), lambda bi, si: (bi, 0, si, 0)),
            pl.BlockSpec((None, ts_out, d), lambda bi, si: (bi, si, 0)),
            _mod_spec(2, d),
            full((d, d)),
        ],
        out_specs=pl.BlockSpec((None, ts_out, d), lambda bi, si: (bi, si, 0)),
        out_shape=jax.ShapeDtypeStruct(x.shape, F32),
        compiler_params=_cparams(("arbitrary", "arbitrary")),
    )(o, x, mod, w_out.astype(BF16))


def _route_kernel(x_ref, sh_ref, sc_ref, ng_ref, rw_ref, rb_ref,
                  h_ref, e_ref, w_ref, r_ref, cnt_ref, run_ref):
    tm = x_ref.shape[0]

    @pl.when(pl.program_id(0) == 0)
    def _():
        run_ref[...] = jnp.zeros_like(run_ref)

    h = _modulated_norm(x_ref[...], ng_ref[...], sc_ref[...], sh_ref[...])
    h_ref[...] = h
    logits = jnp.dot(h, rw_ref[...], preferred_element_type=F32, precision=HIGHEST) + rb_ref[...]
    lane = lax.broadcasted_iota(jnp.int32, (tm, LANES), 1)
    lane_f = lane.astype(F32)
    lg = jnp.where(lane < N_EXPERTS, logits, -jnp.inf)
    tops, hits = [], []
    for _ in range(TOP_K):
        m = jnp.max(lg, axis=-1, keepdims=True)
        idx = jnp.min(jnp.where(lg == m, lane_f, float(LANES)), axis=-1, keepdims=True)
        hit = lane_f == idx
        tops.append((m, idx))
        hits.append(hit)
        lg = jnp.where(hit, -jnp.inf, lg)
    exps = [jnp.exp(m - tops[0][0]) for m, _ in tops]
    denom = exps[0]
    for ex in exps[1:]:
        denom = denom + ex

    member = jnp.zeros((tm, LANES), F32)
    for hit in hits:
        member = member + jnp.where(hit, 1.0, 0.0)
    rr = lax.broadcasted_iota(jnp.int32, (tm, tm), 0)
    cc = lax.broadcasted_iota(jnp.int32, (tm, tm), 1)
    earlier = jnp.where(cc < rr, 1.0, 0.0).astype(BF16)
    before = jnp.dot(earlier, member.astype(BF16), preferred_element_type=F32) + run_ref[...]

    e_out = jnp.zeros((tm, LANES), F32)
    w_out = jnp.zeros((tm, LANES), F32)
    r_out = jnp.zeros((tm, LANES), F32)
    for k in range(TOP_K):
        rank = jnp.sum(jnp.where(hits[k], before, 0.0), axis=-1, keepdims=True)
        e_out = jnp.where(lane == k, tops[k][1], e_out)
        w_out = jnp.where(lane == k, exps[k] / denom, w_out)
        r_out = jnp.where(lane == k, rank, r_out)
    e_ref[...] = e_out.astype(jnp.int32)
    w_ref[...] = w_out
    r_ref[...] = r_out.astype(jnp.int32)
    run = run_ref[...] + jnp.sum(member, axis=0, keepdims=True)
    run_ref[...] = run
    cnt_ref[...] = run.astype(jnp.int32)


def _dispatch_kernel(dest_ref, h_ref, xs_in_hbm, xs_hbm, sem):
    del xs_in_hbm
    tm = h_ref.shape[0]

    def issue(t, _):
        for k in range(TOP_K):
            pltpu.make_async_copy(h_ref.at[pl.ds(t, 1), :],
                                  xs_hbm.at[pl.ds(dest_ref[0, t * TOP_K + k], 1), :],
                                  sem).start()
        return 0

    lax.fori_loop(0, tm, issue, 0)
    for _ in range(TOP_K):
        pltpu.make_async_copy(h_ref, xs_hbm.at[pl.ds(0, tm), :], sem).wait()


def _expert_kernel(be_ref, first_ref, nu_ref, xs_ref, wgu_ref, bgu_ref, wd_ref, bd_ref,
                   o_ref, wgu_bf, wd_bf):
    i = pl.program_id(0)
    de = wd_ref.shape[0]

    @pl.when(i < nu_ref[0])
    def _():
        @pl.when(first_ref[i] == 1)
        def _():
            wgu_bf[...] = wgu_ref[...].astype(BF16)
            wd_bf[...] = wd_ref[...].astype(BF16)

        gu = jnp.dot(xs_ref[...].astype(BF16), wgu_bf[...], preferred_element_type=F32) + bgu_ref[...]
        gate = jnp.minimum(gu[:, :de], SWIGLU_LIMIT)
        up = jnp.clip(gu[:, de:], -SWIGLU_LIMIT, SWIGLU_LIMIT)
        act = gate * _sigmoid(SWIGLU_ALPHA * gate) * (up + 1.0)
        o_ref[...] = jnp.dot(act.astype(BF16), wd_bf[...], preferred_element_type=F32) + bd_ref[...]

    @pl.when(i >= nu_ref[0])
    def _():
        o_ref[...] = jnp.zeros_like(o_ref)


def _combine_kernel(dest_ref, x_ref, w_ref, g2_ref, y_hbm, o_ref, buf, sem):
    tm = x_ref.shape[0]

    def issue(r, _):
        pltpu.make_async_copy(y_hbm.at[pl.ds(dest_ref[0, r], 1), :], buf.at[pl.ds(r, 1), :],
                              sem).start()
        return 0

    lax.fori_loop(0, tm, issue, 0)
    pltpu.make_async_copy(y_hbm.at[pl.ds(0, tm), :], buf, sem).wait()
    o_ref[...] = x_ref[...] + g2_ref[...] * (w_ref[...] * buf[...])


def _tok_mod_spec(j, tm, s, d):
    return pl.BlockSpec((None, None, 1, d), lambda i: (j, (i * tm) // s, 0, 0))


def _moe_route(xt, t, s, mod, ng, router_w, router_b, tm=256):
    d = xt.shape[1]
    assert t % tm == 0
    rw = jnp.pad(router_w, ((0, 0), (0, LANES - N_EXPERTS)))
    rb = jnp.pad(router_b, (0, LANES - N_EXPERTS)).reshape(1, LANES)
    return pl.pallas_call(
        _route_kernel,
        grid=(t // tm,),
        in_specs=[
            pl.BlockSpec((tm, d), lambda i: (i, 0)),
            _tok_mod_spec(3, tm, s, d), _tok_mod_spec(4, tm, s, d),
            pl.BlockSpec((1, d), lambda i: (0, 0)),
            pl.BlockSpec((d, LANES), lambda i: (0, 0)),
            pl.BlockSpec((1, LANES), lambda i: (0, 0)),
        ],
        out_specs=[
            pl.BlockSpec((tm, d), lambda i: (i, 0)),
            pl.BlockSpec((tm, LANES), lambda i: (i, 0)),
            pl.BlockSpec((tm, LANES), lambda i: (i, 0)),
            pl.BlockSpec((tm, LANES), lambda i: (i, 0)),
            pl.BlockSpec((1, LANES), lambda i: (0, 0)),
        ],
        out_shape=[
            jax.ShapeDtypeStruct((t, d), F32),
            jax.ShapeDtypeStruct((t, LANES), jnp.int32),
            jax.ShapeDtypeStruct((t, LANES), F32),
            jax.ShapeDtypeStruct((t, LANES), jnp.int32),
            jax.ShapeDtypeStruct((1, LANES), jnp.int32),
        ],
        scratch_shapes=[pltpu.VMEM((1, LANES), F32)],
        compiler_params=_cparams(("arbitrary",)),
    )(xt, mod, mod, ng.reshape(1, d), rw, rb)


def _moe_plan(top_e, rank, counts, tm):
    t = top_e.shape[0]
    n_e = N_EXPERTS
    counts = counts[0, :n_e]
    padded = (counts + tm - 1) // tm * tm
    pad_end = jnp.cumsum(padded)
    pad_start = pad_end - padded
    dest = pad_start[top_e[:, :TOP_K]] + rank[:, :TOP_K]
    n_tiles = (t * TOP_K) // tm + n_e
    tile_start = jnp.arange(n_tiles, dtype=jnp.int32) * tm
    n_used = (pad_end[-1] // tm).astype(jnp.int32)
    tile_expert = jnp.minimum(jnp.searchsorted(pad_end, tile_start, side='right'), n_e - 1)
    tile_expert = jnp.where(jnp.arange(n_tiles) < n_used, tile_expert,
                            tile_expert[jnp.maximum(n_used - 1, 0)]).astype(jnp.int32)
    first = jnp.concatenate([jnp.ones((1,), jnp.int32),
                             (tile_expert[1:] != tile_expert[:-1]).astype(jnp.int32)])
    return dest.astype(jnp.int32), tile_expert, first, n_used.reshape(1), n_tiles


def _moe_dispatch(h2, dest, n_rows, tm=256):
    t, d = h2.shape
    return pl.pallas_call(
        _dispatch_kernel,
        grid=(t // tm,),
        in_specs=[
            pl.BlockSpec((None, 1, tm * TOP_K), lambda i: (i, 0, 0), memory_space=pltpu.SMEM),
            pl.BlockSpec((tm, d), lambda i: (i, 0)),
            pl.BlockSpec(memory_space=pl.ANY),
        ],
        out_specs=pl.BlockSpec(memory_space=pl.ANY),
        out_shape=jax.ShapeDtypeStruct((n_rows, d), F32),
        scratch_shapes=[pltpu.SemaphoreType.DMA],
        input_output_aliases={2: 0},
        compiler_params=_cparams(("arbitrary",)),
    )(dest.reshape(t // tm, 1, tm * TOP_K), h2, jnp.zeros((n_rows, d), F32))


def _moe_experts(xs, tile_expert, first, n_used, w_gu, b_gu, w_down, b_down, layer, tm):
    n_rows, d = xs.shape
    n_e = N_EXPERTS
    de = w_down.shape[2]
    tile_idx = lambda i, be, fi, nu: (jnp.minimum(i, nu[0] - 1), 0)
    w_idx = lambda i, be, fi, nu: (layer, be[i], 0, 0)
    return pl.pallas_call(
        _expert_kernel,
        grid_spec=pltpu.PrefetchScalarGridSpec(
            num_scalar_prefetch=3,
            grid=(n_rows // tm,),
            in_specs=[
                pl.BlockSpec((tm, d), tile_idx),
                pl.BlockSpec((None, None, d, 2 * de), w_idx),
                pl.BlockSpec((None, None, 1, 2 * de), w_idx),
                pl.BlockSpec((None, None, de, d), w_idx),
                pl.BlockSpec((None, None, 1, d), w_idx),
            ],
            out_specs=pl.BlockSpec((tm, d), lambda i, be, fi, nu: (i, 0)),
            scratch_shapes=[pltpu.VMEM((d, 2 * de), BF16), pltpu.VMEM((de, d), BF16)],
        ),
        out_shape=jax.ShapeDtypeStruct((n_rows, d), F32),
        compiler_params=_cparams(("arbitrary",)),
    )(tile_expert, first, n_used, xs, w_gu, b_gu.reshape(b_gu.shape[0], n_e, 1, 2 * de),
      w_down, b_down.reshape(b_down.shape[0], n_e, 1, d))


def _moe_combine(xt, s, mod, dest_item, w_item, y_rows, tm=256):
    t, d = xt.shape
    return pl.pallas_call(
        _combine_kernel,
        grid=(t // tm,),
        in_specs=[
            pl.BlockSpec((None, 1, tm), lambda i: (i, 0, 0), memory_space=pltpu.SMEM),
            pl.BlockSpec((tm, d), lambda i: (i, 0)),
            pl.BlockSpec((tm, 1), lambda i: (i, 0)),
            _tok_mod_spec(5, tm, s, d),
            pl.BlockSpec(memory_space=pl.ANY),
        ],
        out_specs=pl.BlockSpec((tm, d), lambda i: (i, 0)),
        out_shape=jax.ShapeDtypeStruct((t, d), F32),
        scratch_shapes=[pltpu.VMEM((tm, d), F32), pltpu.SemaphoreType.DMA],
        compiler_params=_cparams(("arbitrary",)),
    )(dest_item.reshape(t // tm, 1, tm), xt, w_item, mod, y_rows)


def _moe_layer(x, mod, ng, router_w, router_b, w_gu, b_gu, w_down, b_down, layer, tm_exp=256):
    b, s, d = x.shape
    t = b * s
    xt = x.reshape(t, d)
    assert t % TOP_K == 0
    n_tok = t // TOP_K
    h2, top_e, top_w, rank, counts = _moe_route(xt, n_tok, s, mod, ng, router_w, router_b)
    dest, tile_expert, first, n_used, n_tiles = _moe_plan(top_e, rank, counts, tm_exp)
    xs = _moe_dispatch(h2, dest, n_tiles * tm_exp)
    y_rows = _moe_experts(xs, tile_expert, first, n_used, w_gu, b_gu, w_down, b_down, layer, tm_exp)
    dest_item = dest.reshape(t)
    w_item = top_w[:, :TOP_K].reshape(t, 1)
    return _moe_combine(xt, s, mod, dest_item, w_item, y_rows).reshape(b, s, d)


def kernel(x, c, norm_mix_g, norm_ffn_g, w_mod, b_mod, lru_w_in, lru_conv_w, lru_conv_b, lru_w_a, lru_b_a, lru_w_x, lru_b_x, lru_lambda, lru_w_out, attn_w_qkv, attn_q_norm_g, attn_k_norm_g, attn_w_out, router_w, router_b, expert_w_gu, expert_b_gu, expert_w_down, expert_b_down):
    depth = w_mod.shape[0]
    mods = _modulation(c, w_mod, b_mod)
    for i in range(depth):
        mod = mods[i]
        j = i // 2
        if i % 2 == 0:
            x = _rglru_layer(x, mod, norm_mix_g[i], lru_w_in[j], lru_conv_w[j], lru_conv_b[j],
                             lru_w_a[j], lru_b_a[j], lru_w_x[j], lru_b_x[j], lru_lambda[j],
                             lru_w_out[j])
        else:
            x = _moba_layer(x, mod, norm_mix_g[i], attn_w_qkv[j], attn_q_norm_g[j],
                            attn_k_norm_g[j], attn_w_out[j])
        x = _moe_layer(x, mod, norm_ffn_g[i], router_w[i], router_b[i], expert_w_gu,
                       expert_b_gu, expert_w_down, expert_b_down, i)
    return x
```

```python
import functools

import jax
import jax.numpy as jnp
from jax import lax
from jax.experimental import pallas as pl
from jax.experimental.pallas import tpu as pltpu

NORM_EPS = 1e-6
N_MOD = 6
LRU_BLOCKS = 4
CONV_WIDTH = 4
LRU_C = 8.0
ATTN_HEADS = 8
MOBA_BLOCK = 256
MOBA_TOPK = 3
N_EXPERTS = 32
TOP_K = 4
SWIGLU_LIMIT = 7.0
SWIGLU_ALPHA = 1.702

LANES = 128
SUBLANES = 8
MASK_BIAS = -1e30
VMEM_LIMIT = 52 * 1024 * 1024

F32 = jnp.float32
BF16 = jnp.bfloat16
HIGHEST = lax.Precision.HIGHEST


def _cparams(sem):
    return pltpu.CompilerParams(dimension_semantics=sem, vmem_limit_bytes=VMEM_LIMIT)


def _sigmoid(z):
    return 1.0 / (1.0 + jnp.exp(-z))


def _modulated_norm(x, g, sc, sh):
    ms = jnp.mean(x * x, axis=-1, keepdims=True)
    return x * lax.rsqrt(ms + NORM_EPS) * g * (1.0 + sc) + sh


def _mod_kernel(c_ref, w_ref, b_ref, o_ref):
    c = c_ref[...]
    cond = c * _sigmoid(c)
    o_ref[...] = jnp.dot(cond, w_ref[...], preferred_element_type=F32,
                         precision=HIGHEST) + b_ref[...]


def _modulation(c, w_mod, b_mod):
    depth, d, _ = w_mod.shape
    b = c.shape[0]
    rows = -(-b // SUBLANES) * SUBLANES
    c_pad = jnp.pad(c, ((0, rows - b), (0, 0)))
    out = pl.pallas_call(
        _mod_kernel,
        grid=(depth, N_MOD),
        in_specs=[
            pl.BlockSpec((rows, d), lambda i, j: (0, 0)),
            pl.BlockSpec((None, d, d), lambda i, j: (i, 0, j)),
            pl.BlockSpec((None, None, 1, d), lambda i, j: (i, j, 0, 0)),
        ],
        out_specs=pl.BlockSpec((None, None, rows, d), lambda i, j: (i, j, 0, 0)),
        out_shape=jax.ShapeDtypeStruct((depth, N_MOD, rows, d), F32),
        compiler_params=_cparams(("arbitrary", "arbitrary")),
    )(c_pad, w_mod, b_mod.reshape(depth, N_MOD, 1, d))
    return out[:, :, :b].reshape(depth, N_MOD, b, 1, d)


def _mod_spec(j, d):
    return pl.BlockSpec((None, None, 1, d), lambda b, s: (j, b, 0, 0))


def _rglru_kernel(x_ref, sh_ref, sc_ref, g1_ref, ng_ref, win_ref, cw_ref, cb_ref,
                  wa_ref, ba_ref, wx_ref, bx_ref, lam_ref, wout_ref, o_ref,
                  ext_ref, a_ref, u_ref, h_ref):
    ts, d = x_ref.shape
    bw = d // LRU_BLOCKS

    @pl.when(pl.program_id(1) == 0)
    def _():
        ext_ref[0:SUBLANES, :] = jnp.zeros((SUBLANES, d), F32)
        h_ref[...] = jnp.zeros_like(h_ref)

    x = x_ref[...]
    h = _modulated_norm(x, ng_ref[...], sc_ref[...], sh_ref[...])
    gr = jnp.dot(h.astype(BF16), win_ref[...], preferred_element_type=F32)
    gate_branch = gr[:, :d]
    rec = gr[:, d:]

    ext_ref[SUBLANES:, :] = rec
    xc = cb_ref[...] + cw_ref[CONV_WIDTH - 1:CONV_WIDTH, :] * rec
    for k in range(CONV_WIDTH - 1):
        off = SUBLANES - (CONV_WIDTH - 1) + k
        xc = xc + cw_ref[k:k + 1, :] * ext_ref[off:off + ts, :]
    ext_ref[0:SUBLANES, :] = rec[ts - SUBLANES:, :]

    xcb = xc.astype(BF16)
    ra = jnp.concatenate(
        [jnp.dot(xcb[:, g * bw:(g + 1) * bw], wa_ref[g], preferred_element_type=F32)
         for g in range(LRU_BLOCKS)], axis=1) + ba_ref[...]
    rx = jnp.concatenate(
        [jnp.dot(xcb[:, g * bw:(g + 1) * bw], wx_ref[g], preferred_element_type=F32)
         for g in range(LRU_BLOCKS)], axis=1) + bx_ref[...]
    r = _sigmoid(ra)
    ig = _sigmoid(rx)
    z = -lam_ref[...]
    softplus = jnp.maximum(z, 0.0) + jnp.log(1.0 + jnp.exp(-jnp.abs(z)))
    a = jnp.exp(-LRU_C * r * softplus)
    u = jnp.sqrt(1.0 - a * a) * (ig * xc)

    row = lax.broadcasted_iota(jnp.int32, (ts, d), 0) % SUBLANES
    k = 1
    while k < SUBLANES:
        a_prev = pltpu.roll(a, k, 0)
        u_prev = pltpu.roll(u, k, 0)
        m = row >= k
        u = jnp.where(m, a * u_prev + u, u)
        a = jnp.where(m, a * a_prev, a)
        k *= 2
    a_ref[...] = a
    u_ref[...] = u

    def group(j, hc):
        r0 = pl.multiple_of(j * SUBLANES, SUBLANES)
        hs = u_ref[pl.ds(r0, SUBLANES), :] + a_ref[pl.ds(r0, SUBLANES), :] * hc
        u_ref[pl.ds(r0, SUBLANES), :] = hs
        return hs[SUBLANES - 1:SUBLANES, :]

    h_ref[0:1, :] = lax.fori_loop(0, ts // SUBLANES, group, h_ref[0:1, :])

    y = jax.nn.gelu(gate_branch, approximate=True) * u_ref[...]
    out = jnp.dot(y.astype(BF16), wout_ref[...], preferred_element_type=F32)
    o_ref[...] = x + g1_ref[...] * out


def _rglru_layer(x, mod, ng, w_in, conv_w, conv_b, w_a, b_a, w_x, b_x, lam, w_out, ts=256):
    b, s, d = x.shape
    full = lambda shape: pl.BlockSpec(shape, lambda bi, si: (0,) * len(shape))
    row = lambda v: v.reshape(1, d)
    return pl.pallas_call(
        _rglru_kernel,
        grid=(b, s // ts),
        in_specs=[
            pl.BlockSpec((None, ts, d), lambda bi, si: (bi, si, 0)),
            _mod_spec(0, d), _mod_spec(1, d), _mod_spec(2, d),
            full((1, d)), full((d, 2 * d)), full((CONV_WIDTH, d)), full((1, d)),
            full(w_a.shape), full((1, d)), full(w_x.shape), full((1, d)), full((1, d)),
            full((d, d)),
        ],
        out_specs=pl.BlockSpec((None, ts, d), lambda bi, si: (bi, si, 0)),
        out_shape=jax.ShapeDtypeStruct(x.shape, F32),
        scratch_shapes=[
            pltpu.VMEM((ts + SUBLANES, d), F32),
            pltpu.VMEM((ts, d), F32),
            pltpu.VMEM((ts, d), F32),
            pltpu.VMEM((SUBLANES, d), F32),
        ],
        compiler_params=_cparams(("arbitrary", "arbitrary")),
    )(x, mod, mod, mod, row(ng), w_in.astype(BF16), conv_w, row(conv_b),
      w_a.astype(BF16), row(b_a), w_x.astype(BF16), row(b_x), row(lam), w_out.astype(BF16))


def _qkv_kernel(x_ref, sh_ref, sc_ref, ng_ref, w_ref, qg_ref, kg_ref,
                q_ref, kext_ref, v_ref, km_ref):
    ts, d = x_ref.shape
    nh, _, hd = q_ref.shape
    blk = pl.program_id(1)
    h = _modulated_norm(x_ref[...], ng_ref[...], sc_ref[...], sh_ref[...])
    qkv = jnp.dot(h.astype(BF16), w_ref[...], preferred_element_type=F32)
    lane = lax.broadcasted_iota(jnp.int32, (ts, hd), 1)
    onehot = jnp.where(lane == blk, 1.0, 0.0).astype(BF16)
    for hh in range(nh):
        qh = qkv[:, hh * hd:(hh + 1) * hd]
        qn = qh * lax.rsqrt(jnp.mean(qh * qh, axis=-1, keepdims=True) + NORM_EPS)
        q_ref[hh] = (qn * qg_ref[...] * (hd ** -0.5)).astype(BF16)
        kh = qkv[:, d + hh * hd:d + (hh + 1) * hd]
        kn = kh * lax.rsqrt(jnp.mean(kh * kh, axis=-1, keepdims=True) + NORM_EPS) * kg_ref[...]
        kext_ref[hh, :, 0:hd] = kn.astype(BF16)
        kext_ref[hh, :, hd:2 * hd] = onehot
        km_ref[:, hh * hd:(hh + 1) * hd] = jnp.mean(kn, axis=0, keepdims=True)
        v_ref[hh] = qkv[:, 2 * d + hh * hd:2 * d + (hh + 1) * hd].astype(BF16)


def _attn_query_block(qi, q_ref, kext_ref, v_ref, km_ref, o_ref, s_ref):
    nhs, tq, hd = q_ref.shape
    nt = (((1,), (1,)), ((), ()))
    lane = lax.broadcasted_iota(jnp.int32, (tq, LANES), 1)
    lane_f = lane.astype(F32)
    past = lane < qi
    rr = lax.broadcasted_iota(jnp.int32, (tq, tq), 0)
    cc = lax.broadcasted_iota(jnp.int32, (tq, tq), 1)
    for hh in range(nhs):
        q = q_ref[hh]
        gate = lax.dot_general(q.astype(F32), km_ref[hh], nt, precision=HIGHEST,
                               preferred_element_type=F32)
        g = jnp.where(past, gate, -jnp.inf)
        sel = lane == qi
        for _ in range(min(MOBA_TOPK, qi)):
            m = jnp.max(g, axis=-1, keepdims=True)
            idx = jnp.min(jnp.where(g == m, lane_f, float(LANES)), axis=-1, keepdims=True)
            hit = lane_f == idx
            sel = sel | (hit & past)
            g = jnp.where(hit, -jnp.inf, g)
        bias = jnp.where(sel, 0.0, MASK_BIAS).astype(BF16)
        q_ext = jnp.concatenate([q, bias], axis=1)

        mx = None
        for j in range(qi + 1):
            sj = lax.dot_general(q_ext, kext_ref[hh, j * tq:(j + 1) * tq, :], nt,
                                 preferred_element_type=F32)
            if j == qi:
                sj = jnp.where(cc <= rr, sj, MASK_BIAS)
            s_ref[hh, j * tq:(j + 1) * tq, :] = sj
            mx = sj if mx is None else jnp.maximum(mx, sj)
        m = jnp.max(mx, axis=-1, keepdims=True)

        psum, acc = None, None
        for j in range(qi + 1):
            p = jnp.exp(s_ref[hh, j * tq:(j + 1) * tq, :] - m)
            pv = jnp.dot(p.astype(BF16), v_ref[hh, j * tq:(j + 1) * tq, :],
                         preferred_element_type=F32)
            psum = p if psum is None else psum + p
            acc = pv if acc is None else acc + pv
        l = jnp.sum(psum, axis=-1, keepdims=True)
        o_ref[hh] = (acc / l).astype(BF16)


def _attn_kernel(q_ref, kext_ref, v_ref, km_ref, o_ref, s_ref):
    nb = kext_ref.shape[1] // q_ref.shape[1]
    for qi in range(nb):
        @pl.when(pl.program_id(2) == qi)
        def _(qi=qi):
            _attn_query_block(qi, q_ref, kext_ref, v_ref, km_ref, o_ref, s_ref)


def _attn_out_kernel(o_ref, x_ref, g1_ref, w_ref, y_ref):
    nh = o_ref.shape[0]
    o = jnp.concatenate([o_ref[hh] for hh in range(nh)], axis=1)
    out = jnp.dot(o, w_ref[...], preferred_element_type=F32)
    y_ref[...] = x_ref[...] + g1_ref[...] * out


def _moba_layer(x, mod, ng, w_qkv, q_g, k_g, w_out, ts_out=512, heads_per_step=1):
    b, s, d = x.shape
    nh = ATTN_HEADS
    hd = d // nh
    tq = MOBA_BLOCK
    nb = s // tq
    assert s % tq == 0 and nb <= LANES and hd == LANES
    full = lambda shape: pl.BlockSpec(shape, lambda bi, si: (0,) * len(shape))

    q, kext, v, km = pl.pallas_call(
        _qkv_kernel,
        grid=(b, nb),
        in_specs=[
            pl.BlockSpec((None, tq, d), lambda bi, si: (bi, si, 0)),
            _mod_spec(0, d), _mod_spec(1, d),
            full((1, d)), full((d, 3 * d)), full((1, hd)), full((1, hd)),
        ],
        out_specs=[
            pl.BlockSpec((None, nh, tq, hd), lambda bi, si: (bi, 0, si, 0)),
            pl.BlockSpec((None, nh, tq, 2 * hd), lambda bi, si: (bi, 0, si, 0)),
            pl.BlockSpec((None, nh, tq, hd), lambda bi, si: (bi, 0, si, 0)),
            pl.BlockSpec((None, None, 1, d), lambda bi, si: (bi, si, 0, 0)),
        ],
        out_shape=[
            jax.ShapeDtypeStruct((b, nh, s, hd), BF16),
            jax.ShapeDtypeStruct((b, nh, s, 2 * hd), BF16),
            jax.ShapeDtypeStruct((b, nh, s, hd), BF16),
            jax.ShapeDtypeStruct((b, nb, 1, d), F32),
        ],
        compiler_params=_cparams(("arbitrary", "arbitrary")),
    )(x, mod, mod, ng.reshape(1, d), w_qkv.astype(BF16), q_g.reshape(1, hd), k_g.reshape(1, hd))

    km = km.reshape(b, nb, nh, hd).transpose(0, 2, 1, 3)
    km = jnp.pad(km, ((0, 0), (0, 0), (0, LANES - nb), (0, 0)))

    o = pl.pallas_call(
        _attn_kernel,
        grid=(b, nh // heads_per_step, nb),
        in_specs=[
            pl.BlockSpec((None, heads_per_step, tq, hd), lambda bi, hi, qi: (bi, hi, qi, 0)),
            pl.BlockSpec((None, heads_per_step, s, 2 * hd), lambda bi, hi, qi: (bi, hi, 0, 0)),
            pl.BlockSpec((None, heads_per_step, s, hd), lambda bi, hi, qi: (bi, hi, 0, 0)),
            pl.BlockSpec((None, heads_per_step, LANES, hd), lambda bi, hi, qi: (bi, hi, 0, 0)),
        ],
        out_specs=pl.BlockSpec((None, heads_per_step, tq, hd),
                               lambda bi, hi, qi: (bi, hi, qi, 0)),
        out_shape=jax.ShapeDtypeStruct((b, nh, s, hd), BF16),
        scratch_shapes=[pltpu.VMEM((heads_per_step, s, tq), F32)],
        compiler_params=_cparams(("arbitrary", "arbitrary", "arbitrary")),
    )(q, kext, v, km)

    return pl.pallas_call(
        _attn_out_kernel,
        grid=(b, s // ts_out),
        in_specs=[
            pl.BlockSpec((None, nh, ts_out, hd), lambda bi, si: (bi, 0, si, 0)),
            pl.BlockSpec((None, ts_out, d), lambda bi, si: (bi, si, 0)),
            _mod_spec(2, d),
            full((d, d)),
        ],
        out_specs=pl.BlockSpec((None, ts_out, d), lambda bi, si: (bi, si, 0)),
        out_shape=jax.ShapeDtypeStruct(x.shape, F32),
        compiler_params=_cparams(("arbitrary", "arbitrary")),
    )(o, x, mod, w_out.astype(BF16))


def _route_kernel(x_ref, sh_ref, sc_ref, ng_ref, rw_ref, rb_ref,
                  h_ref, e_ref, w_ref, r_ref, cnt_ref, run_ref):
    tm = x_ref.shape[0]

    @pl.when(pl.program_id(0) == 0)
    def _():
        run_ref[...] = jnp.zeros_like(run_ref)

    h = _modulated_norm(x_ref[...], ng_ref[...], sc_ref[...], sh_ref[...])
    h_ref[...] = h
    logits = jnp.dot(h, rw_ref[...], preferred_element_type=F32, precision=HIGHEST) + rb_ref[...]
    lane = lax.broadcasted_iota(jnp.int32, (tm, LANES), 1)
    lane_f = lane.astype(F32)
    lg = jnp.where(lane < N_EXPERTS, logits, -jnp.inf)
    tops, hits = [], []
    for _ in range(TOP_K):
        m = jnp.max(lg, axis=-1, keepdims=True)
        idx = jnp.min(jnp.where(lg == m, lane_f, float(LANES)), axis=-1, keepdims=True)
        hit = lane_f == idx
        tops.append((m, idx))
        hits.append(hit)
        lg = jnp.where(hit, -jnp.inf, lg)
    exps = [jnp.exp(m - tops[0][0]) for m, _ in tops]
    denom = exps[0]
    for ex in exps[1:]:
        denom = denom + ex

    member = jnp.zeros((tm, LANES), F32)
    for hit in hits:
        member = member + jnp.where(hit, 1.0, 0.0)
    rr = lax.broadcasted_iota(jnp.int32, (tm, tm), 0)
    cc = lax.broadcasted_iota(jnp.int32, (tm, tm), 1)
    earlier = jnp.where(cc < rr, 1.0, 0.0).astype(BF16)
    before = jnp.dot(earlier, member.astype(BF16), preferred_element_type=F32) + run_ref[...]

    e_out = jnp.zeros((tm, LANES), F32)
    w_out = jnp.zeros((tm, LANES), F32)
    r_out = jnp.zeros((tm, LANES), F32)
    for k in range(TOP_K):
        rank = jnp.sum(jnp.where(hits[k], before, 0.0), axis=-1, keepdims=True)
        e_out = jnp.where(lane == k, tops[k][1], e_out)
        w_out = jnp.where(lane == k, exps[k] / denom, w_out)
        r_out = jnp.where(lane == k, rank, r_out)
    e_ref[...] = e_out.astype(jnp.int32)
    w_ref[...] = w_out
    r_ref[...] = r_out.astype(jnp.int32)
    run = run_ref[...] + jnp.sum(member, axis=0, keepdims=True)
    run_ref[...] = run
    cnt_ref[...] = run.astype(jnp.int32)


def _dispatch_kernel(dest_ref, h_ref, xs_in_hbm, xs_hbm, sem):
    del xs_in_hbm
    tm = h_ref.shape[0]

    def issue(t, _):
        for k in range(TOP_K):
            pltpu.make_async_copy(h_ref.at[pl.ds(t, 1), :],
                                  xs_hbm.at[pl.ds(dest_ref[0, t * TOP_K + k], 1), :],
                                  sem).start()
        return 0

    lax.fori_loop(0, tm, issue, 0)
    for _ in range(TOP_K):
        pltpu.make_async_copy(h_ref, xs_hbm.at[pl.ds(0, tm), :], sem).wait()


def _expert_kernel(be_ref, first_ref, nu_ref, xs_ref, wgu_ref, bgu_ref, wd_ref, bd_ref,
                   o_ref, wgu_bf, wd_bf):
    i = pl.program_id(0)
    de = wd_ref.shape[0]

    @pl.when(i < nu_ref[0])
    def _():
        @pl.when(first_ref[i] == 1)
        def _():
            wgu_bf[...] = wgu_ref[...].astype(BF16)
            wd_bf[...] = wd_ref[...].astype(BF16)

        gu = jnp.dot(xs_ref[...].astype(BF16), wgu_bf[...], preferred_element_type=F32) + bgu_ref[...]
        gate = jnp.minimum(gu[:, :de], SWIGLU_LIMIT)
        up = jnp.clip(gu[:, de:], -SWIGLU_LIMIT, SWIGLU_LIMIT)
        act = gate * _sigmoid(SWIGLU_ALPHA * gate) * (up + 1.0)
        o_ref[...] = jnp.dot(act.astype(BF16), wd_bf[...], preferred_element_type=F32) + bd_ref[...]

    @pl.when(i >= nu_ref[0])
    def _():
        o_ref[...] = jnp.zeros_like(o_ref)


def _combine_kernel(dest_ref, x_ref, w_ref, g2_ref, y_hbm, o_ref, buf, sem):
    tm = x_ref.shape[0]

    def issue(r, _):
        pltpu.make_async_copy(y_hbm.at[pl.ds(dest_ref[0, r], 1), :], buf.at[pl.ds(r, 1), :],
                              sem).start()
        return 0

    lax.fori_loop(0, tm, issue, 0)
    pltpu.make_async_copy(y_hbm.at[pl.ds(0, tm), :], buf, sem).wait()
    o_ref[...] = x_ref[...] + g2_ref[...] * (w_ref[...] * buf[...])


def _tok_mod_spec(j, tm, s, d):
    return pl.BlockSpec((None, None, 1, d), lambda i: (j, (i * tm) // s, 0, 0))


def _moe_route(xt, t, s, mod, ng, router_w, router_b, tm=256):
    d = xt.shape[1]
    assert t % tm == 0
    rw = jnp.pad(router_w, ((0, 0), (0, LANES - N_EXPERTS)))
    rb = jnp.pad(router_b, (0, LANES - N_EXPERTS)).reshape(1, LANES)
    return pl.pallas_call(
        _route_kernel,
        grid=(t // tm,),
        in_specs=[
            pl.BlockSpec((tm, d), lambda i: (i, 0)),
            _tok_mod_spec(3, tm, s, d), _tok_mod_spec(4, tm, s, d),
            pl.BlockSpec((1, d), lambda i: (0, 0)),
            pl.BlockSpec((d, LANES), lambda i: (0, 0)),
            pl.BlockSpec((1, LANES), lambda i: (0, 0)),
        ],
        out_specs=[
            pl.BlockSpec((tm, d), lambda i: (i, 0)),
            pl.BlockSpec((tm, LANES), lambda i: (i, 0)),
            pl.BlockSpec((tm, LANES), lambda i: (i, 0)),
            pl.BlockSpec((tm, LANES), lambda i: (i, 0)),
            pl.BlockSpec((1, LANES), lambda i: (0, 0)),
        ],
        out_shape=[
            jax.ShapeDtypeStruct((t, d), F32),
            jax.ShapeDtypeStruct((t, LANES), jnp.int32),
            jax.ShapeDtypeStruct((t, LANES), F32),
            jax.ShapeDtypeStruct((t, LANES), jnp.int32),
            jax.ShapeDtypeStruct((1, LANES), jnp.int32),
        ],
        scratch_shapes=[pltpu.VMEM((1, LANES), F32)],
        compiler_params=_cparams(("arbitrary",)),
    )(xt, mod, mod, ng.reshape(1, d), rw, rb)


def _moe_plan(top_e, rank, counts, tm):
    t = top_e.shape[0]
    n_e = N_EXPERTS
    counts = counts[0, :n_e]
    padded = (counts + tm - 1) // tm * tm
    pad_end = jnp.cumsum(padded)
    pad_start = pad_end - padded
    dest = pad_start[top_e[:, :TOP_K]] + rank[:, :TOP_K]
    n_tiles = (t * TOP_K) // tm + n_e
    tile_start = jnp.arange(n_tiles, dtype=jnp.int32) * tm
    n_used = (pad_end[-1] // tm).astype(jnp.int32)
    tile_expert = jnp.minimum(jnp.searchsorted(pad_end, tile_start, side='right'), n_e - 1)
    tile_expert = jnp.where(jnp.arange(n_tiles) < n_used, tile_expert,
                            tile_expert[jnp.maximum(n_used - 1, 0)]).astype(jnp.int32)
    first = jnp.concatenate([jnp.ones((1,), jnp.int32),
                             (tile_expert[1:] != tile_expert[:-1]).astype(jnp.int32)])
    return dest.astype(jnp.int32), tile_expert, first, n_used.reshape(1), n_tiles


def _moe_dispatch(h2, dest, n_rows, tm=256):
    t, d = h2.shape
    return pl.pallas_call(
        _dispatch_kernel,
        grid=(t // tm,),
        in_specs=[
            pl.BlockSpec((None, 1, tm * TOP_K), lambda i: (i, 0, 0), memory_space=pltpu.SMEM),
            pl.BlockSpec((tm, d), lambda i: (i, 0)),
            pl.BlockSpec(memory_space=pl.ANY),
        ],
        out_specs=pl.BlockSpec(memory_space=pl.ANY),
        out_shape=jax.ShapeDtypeStruct((n_rows, d), F32),
        scratch_shapes=[pltpu.SemaphoreType.DMA],
        input_output_aliases={2: 0},
        compiler_params=_cparams(("arbitrary",)),
    )(dest.reshape(t // tm, 1, tm * TOP_K), h2, jnp.zeros((n_rows, d), F32))


def _moe_experts(xs, tile_expert, first, n_used, w_gu, b_gu, w_down, b_down, layer, tm):
    n_rows, d = xs.shape
    n_e = N_EXPERTS
    de = w_down.shape[2]
    tile_idx = lambda i, be, fi, nu: (jnp.maximum(jnp.minimum(i, nu[0] - 1), 0), 0)
    w_idx = lambda i, be, fi, nu: (layer, be[i], 0, 0)
    return pl.pallas_call(
        _expert_kernel,
        grid_spec=pltpu.PrefetchScalarGridSpec(
            num_scalar_prefetch=3,
            grid=(n_rows // tm,),
            in_specs=[
                pl.BlockSpec((tm, d), tile_idx),
                pl.BlockSpec((None, None, d, 2 * de), w_idx),
                pl.BlockSpec((None, None, 1, 2 * de), w_idx),
                pl.BlockSpec((None, None, de, d), w_idx),
                pl.BlockSpec((None, None, 1, d), w_idx),
            ],
            out_specs=pl.BlockSpec((tm, d), lambda i, be, fi, nu: (i, 0)),
            scratch_shapes=[pltpu.VMEM((d, 2 * de), BF16), pltpu.VMEM((de, d), BF16)],
        ),
        out_shape=jax.ShapeDtypeStruct((n_rows, d), F32),
        compiler_params=_cparams(("arbitrary",)),
    )(tile_expert, first, n_used, xs, w_gu, b_gu.reshape(b_gu.shape[0], n_e, 1, 2 * de),
      w_down, b_down.reshape(b_down.shape[0], n_e, 1, d))


def _moe_combine(xt, s, mod, dest_item, w_item, y_rows, tm=256):
    t, d = xt.shape
    return pl.pallas_call(
        _combine_kernel,
        grid=(t // tm,),
        in_specs=[
            pl.BlockSpec((None, 1, tm), lambda i: (i, 0, 0), memory_space=pltpu.SMEM),
            pl.BlockSpec((tm, d), lambda i: (i, 0)),
            pl.BlockSpec((tm, 1), lambda i: (i, 0)),
            _tok_mod_spec(5, tm, s, d),
            pl.BlockSpec(memory_space=pl.ANY),
        ],
        out_specs=pl.BlockSpec((tm, d), lambda i: (i, 0)),
        out_shape=jax.ShapeDtypeStruct((t, d), F32),
        scratch_shapes=[pltpu.VMEM((tm, d), F32), pltpu.SemaphoreType.DMA],
        compiler_params=_cparams(("arbitrary",)),
    )(dest_item.reshape(t // tm, 1, tm), xt, w_item, mod, y_rows)


def _moe_layer(x, mod, ng, router_w, router_b, w_gu, b_gu, w_down, b_down, layer, tm_exp=256):
    b, s, d = x.shape
    t = b * s
    xt = x.reshape(t, d)
    assert t % TOP_K == 0
    n_tok = t // TOP_K
    h2, top_e, top_w, rank, counts = _moe_route(xt, n_tok, s, mod, ng, router_w, router_b)
    dest, tile_expert, first, n_used, n_tiles = _moe_plan(top_e, rank, counts, tm_exp)
    xs = _moe_dispatch(h2, dest, n_tiles * tm_exp)
    y_rows = _moe_experts(xs, tile_expert, first, n_used, w_gu, b_gu, w_down, b_down, layer, tm_exp)
    dest_item = dest.reshape(t)
    w_item = top_w[:, :TOP_K].reshape(t, 1)
    return _moe_combine(xt, s, mod, dest_item, w_item, y_rows).reshape(b, s, d)


def kernel(x, c, norm_mix_g, norm_ffn_g, w_mod, b_mod, lru_w_in, lru_conv_w, lru_conv_b, lru_w_a, lru_b_a, lru_w_x, lru_b_x, lru_lambda, lru_w_out, attn_w_qkv, attn_q_norm_g, attn_k_norm_g, attn_w_out, router_w, router_b, expert_w_gu, expert_b_gu, expert_w_down, expert_b_down):
    depth = w_mod.shape[0]
    mods = _modulation(c, w_mod, b_mod)
    for i in range(depth):
        mod = mods[i]
        j = i // 2
        if i % 2 == 0:
            x = _rglru_layer(x, mod, norm_mix_g[i], lru_w_in[j], lru_conv_w[j], lru_conv_b[j],
                             lru_w_a[j], lru_b_a[j], lru_w_x[j], lru_b_x[j], lru_lambda[j],
                             lru_w_out[j])
        else:
            x = _moba_layer(x, mod, norm_mix_g[i], attn_w_qkv[j], attn_q_norm_g[j],
                            attn_k_norm_g[j], attn_w_out[j])
        x = _moe_layer(x, mod, norm_ffn_g[i], router_w[i], router_b[i], expert_w_gu,
                       expert_b_gu, expert_w_down, expert_b_down, i)
    return x
```

```python
import functools

import jax
import jax.numpy as jnp
from jax import lax
from jax.experimental import pallas as pl
from jax.experimental.pallas import tpu as pltpu

NORM_EPS = 1e-6
N_MOD = 6
LRU_BLOCKS = 4
CONV_WIDTH = 4
LRU_C = 8.0
ATTN_HEADS = 8
MOBA_BLOCK = 256
MOBA_TOPK = 3
N_EXPERTS = 32
TOP_K = 4
SWIGLU_LIMIT = 7.0
SWIGLU_ALPHA = 1.702

LANES = 128
SUBLANES = 8
MASK_BIAS = -(2.0 ** 100)
LOG2_E = 1.4426950408889634
VMEM_LIMIT = 52 * 1024 * 1024

F32 = jnp.float32
BF16 = jnp.bfloat16
HIGHEST = lax.Precision.HIGHEST


def _cparams(sem):
    return pltpu.CompilerParams(dimension_semantics=sem, vmem_limit_bytes=VMEM_LIMIT)


def _sigmoid(z):
    return 1.0 / (1.0 + jnp.exp(-z))


def _modulated_norm(x, g, sc, sh):
    ms = jnp.mean(x * x, axis=-1, keepdims=True)
    return x * lax.rsqrt(ms + NORM_EPS) * g * (1.0 + sc) + sh


def _mod_kernel(c_ref, w_ref, b_ref, o_ref):
    c = c_ref[...]
    cond = c * _sigmoid(c)
    o_ref[...] = jnp.dot(cond, w_ref[...], preferred_element_type=F32,
                         precision=HIGHEST) + b_ref[...]


def _modulation(c, w_mod, b_mod):
    depth, d, _ = w_mod.shape
    b = c.shape[0]
    rows = -(-b // SUBLANES) * SUBLANES
    c_pad = jnp.pad(c, ((0, rows - b), (0, 0)))
    out = pl.pallas_call(
        _mod_kernel,
        grid=(depth, N_MOD),
        in_specs=[
            pl.BlockSpec((rows, d), lambda i, j: (0, 0)),
            pl.BlockSpec((None, d, d), lambda i, j: (i, 0, j)),
            pl.BlockSpec((None, None, 1, d), lambda i, j: (i, j, 0, 0)),
        ],
        out_specs=pl.BlockSpec((None, None, rows, d), lambda i, j: (i, j, 0, 0)),
        out_shape=jax.ShapeDtypeStruct((depth, N_MOD, rows, d), F32),
        compiler_params=_cparams(("arbitrary", "arbitrary")),
    )(c_pad, w_mod, b_mod.reshape(depth, N_MOD, 1, d))
    return out[:, :, :b].reshape(depth, N_MOD, b, 1, d)


def _mod_spec(j, d):
    return pl.BlockSpec((None, None, 1, d), lambda b, s: (j, b, 0, 0))


def _rglru_kernel(x_ref, sh_ref, sc_ref, g1_ref, ng_ref, win_ref, cw_ref, cb_ref,
                  wa_ref, ba_ref, wx_ref, bx_ref, lam_ref, wout_ref, o_ref,
                  ext_ref, a_ref, u_ref, h_ref):
    ts, d = x_ref.shape
    bw = d // LRU_BLOCKS

    @pl.when(pl.program_id(1) == 0)
    def _():
        ext_ref[0:SUBLANES, :] = jnp.zeros((SUBLANES, d), F32)
        h_ref[...] = jnp.zeros_like(h_ref)

    x = x_ref[...]
    h = _modulated_norm(x, ng_ref[...], sc_ref[...], sh_ref[...])
    gr = jnp.dot(h.astype(BF16), win_ref[...], preferred_element_type=F32)
    gate_branch = gr[:, :d]
    rec = gr[:, d:]

    ext_ref[SUBLANES:, :] = rec
    xc = cb_ref[...] + cw_ref[CONV_WIDTH - 1:CONV_WIDTH, :] * rec
    for k in range(CONV_WIDTH - 1):
        off = SUBLANES - (CONV_WIDTH - 1) + k
        xc = xc + cw_ref[k:k + 1, :] * ext_ref[off:off + ts, :]
    ext_ref[0:SUBLANES, :] = rec[ts - SUBLANES:, :]

    xcb = xc.astype(BF16)
    ra = jnp.concatenate(
        [jnp.dot(xcb[:, g * bw:(g + 1) * bw], wa_ref[g], preferred_element_type=F32)
         for g in range(LRU_BLOCKS)], axis=1) + ba_ref[...]
    rx = jnp.concatenate(
        [jnp.dot(xcb[:, g * bw:(g + 1) * bw], wx_ref[g], preferred_element_type=F32)
         for g in range(LRU_BLOCKS)], axis=1) + bx_ref[...]
    r = _sigmoid(ra)
    ig = _sigmoid(rx)
    z = -lam_ref[...]
    softplus = jnp.maximum(z, 0.0) + jnp.log(1.0 + jnp.exp(-jnp.abs(z)))
    a = jnp.exp(-LRU_C * r * softplus)
    u = jnp.sqrt(1.0 - a * a) * (ig * xc)

    row = lax.broadcasted_iota(jnp.int32, (ts, d), 0) % SUBLANES
    k = 1
    while k < SUBLANES:
        a_prev = pltpu.roll(a, k, 0)
        u_prev = pltpu.roll(u, k, 0)
        m = row >= k
        u = jnp.where(m, a * u_prev + u, u)
        a = jnp.where(m, a * a_prev, a)
        k *= 2
    a_ref[...] = a
    u_ref[...] = u

    def group(j, hc):
        r0 = pl.multiple_of(j * SUBLANES, SUBLANES)
        hs = u_ref[pl.ds(r0, SUBLANES), :] + a_ref[pl.ds(r0, SUBLANES), :] * hc
        u_ref[pl.ds(r0, SUBLANES), :] = hs
        return hs[SUBLANES - 1:SUBLANES, :]

    h_ref[0:1, :] = lax.fori_loop(0, ts // SUBLANES, group, h_ref[0:1, :])

    y = jax.nn.gelu(gate_branch, approximate=True) * u_ref[...]
    out = jnp.dot(y.astype(BF16), wout_ref[...], preferred_element_type=F32)
    o_ref[...] = x + g1_ref[...] * out


def _rglru_layer(x, mod, ng, w_in, conv_w, conv_b, w_a, b_a, w_x, b_x, lam, w_out, ts=256):
    b, s, d = x.shape
    full = lambda shape: pl.BlockSpec(shape, lambda bi, si: (0,) * len(shape))
    row = lambda v: v.reshape(1, d)
    return pl.pallas_call(
        _rglru_kernel,
        grid=(b, s // ts),
        in_specs=[
            pl.BlockSpec((None, ts, d), lambda bi, si: (bi, si, 0)),
            _mod_spec(0, d), _mod_spec(1, d), _mod_spec(2, d),
            full((1, d)), full((d, 2 * d)), full((CONV_WIDTH, d)), full((1, d)),
            full(w_a.shape), full((1, d)), full(w_x.shape), full((1, d)), full((1, d)),
            full((d, d)),
        ],
        out_specs=pl.BlockSpec((None, ts, d), lambda bi, si: (bi, si, 0)),
        out_shape=jax.ShapeDtypeStruct(x.shape, F32),
        scratch_shapes=[
            pltpu.VMEM((ts + SUBLANES, d), F32),
            pltpu.VMEM((ts, d), F32),
            pltpu.VMEM((ts, d), F32),
            pltpu.VMEM((SUBLANES, d), F32),
        ],
        compiler_params=_cparams(("arbitrary", "arbitrary")),
    )(x, mod, mod, mod, row(ng), w_in.astype(BF16), conv_w, row(conv_b),
      w_a.astype(BF16), row(b_a), w_x.astype(BF16), row(b_x), row(lam), w_out.astype(BF16))


def _qkv_kernel(x_ref, sh_ref, sc_ref, ng_ref, w_ref, qg_ref, kg_ref,
                q_ref, kext_ref, v_ref, km_ref):
    ts, d = x_ref.shape
    nh, _, hd = q_ref.shape
    blk = pl.program_id(1)
    h = _modulated_norm(x_ref[...], ng_ref[...], sc_ref[...], sh_ref[...])
    qkv = jnp.dot(h.astype(BF16), w_ref[...], preferred_element_type=F32)
    lane = lax.broadcasted_iota(jnp.int32, (ts, hd), 1)
    onehot = jnp.where(lane == blk, 1.0, 0.0).astype(BF16)
    for hh in range(nh):
        qh = qkv[:, hh * hd:(hh + 1) * hd]
        qn = qh * lax.rsqrt(jnp.mean(qh * qh, axis=-1, keepdims=True) + NORM_EPS)
        q_ref[hh] = (qn * qg_ref[...] * (hd ** -0.5 * LOG2_E)).astype(BF16)
        kh = qkv[:, d + hh * hd:d + (hh + 1) * hd]
        kn = kh * lax.rsqrt(jnp.mean(kh * kh, axis=-1, keepdims=True) + NORM_EPS) * kg_ref[...]
        kext_ref[hh, :, 0:hd] = kn.astype(BF16)
        kext_ref[hh, :, hd:2 * hd] = onehot
        km_ref[:, hh * hd:(hh + 1) * hd] = jnp.mean(kn, axis=0, keepdims=True)
        v_ref[hh] = qkv[:, 2 * d + hh * hd:2 * d + (hh + 1) * hd].astype(BF16)


def _attn_query_block(qi, q_ref, kext_ref, v_ref, km_ref, o_ref, s_ref):
    nhs, tq, hd = q_ref.shape
    nt = (((1,), (1,)), ((), ()))
    nbp = -(-(kext_ref.shape[1] // tq) // SUBLANES) * SUBLANES
    blk = lax.broadcasted_iota(jnp.int32, (nbp, tq), 0)
    blk_f = blk.astype(F32)
    past = blk < qi
    rr = lax.broadcasted_iota(jnp.int32, (tq, tq), 0)
    cc = lax.broadcasted_iota(jnp.int32, (tq, tq), 1)
    for hh in range(nhs):
        q = q_ref[hh]
        gate = lax.dot_general(km_ref[hh], q.astype(F32), nt, precision=HIGHEST,
                               preferred_element_type=F32)[:nbp]
        g = jnp.where(past, gate, -jnp.inf)
        sel = blk == qi
        for _ in range(min(MOBA_TOPK, qi)):
            m = jnp.max(g, axis=0, keepdims=True)
            idx = jnp.min(jnp.where(g == m, blk_f, float(nbp)), axis=0, keepdims=True)
            hit = blk_f == idx
            sel = sel | (hit & past)
            g = jnp.where(hit, -jnp.inf, g)
        bias_t = jnp.where(sel, 0.0, MASK_BIAS)
        bias = jnp.concatenate([bias_t, jnp.zeros((LANES - nbp, tq), F32)], axis=0).T
        q_ext = jnp.concatenate([q, bias.astype(BF16)], axis=1)

        mx = None
        for j in range(qi + 1):
            sj = lax.dot_general(q_ext, kext_ref[hh, j * tq:(j + 1) * tq, :], nt,
                                 preferred_element_type=F32)
            if j == qi:
                sj = jnp.where(cc <= rr, sj, MASK_BIAS)
            s_ref[hh, j * tq:(j + 1) * tq, :] = sj
            mx = sj if mx is None else jnp.maximum(mx, sj)
        m = jnp.max(mx, axis=-1, keepdims=True)

        psum, acc = None, None
        for j in range(qi + 1):
            p = jnp.exp2(s_ref[hh, j * tq:(j + 1) * tq, :] - m)
            pv = jnp.dot(p.astype(BF16), v_ref[hh, j * tq:(j + 1) * tq, :],
                         preferred_element_type=F32)
            psum = p if psum is None else psum + p
            acc = pv if acc is None else acc + pv
        l = jnp.sum(psum, axis=-1, keepdims=True)
        o_ref[hh] = (acc / l).astype(BF16)


def _attn_kernel(q_ref, kext_ref, v_ref, km_ref, o_ref, s_ref):
    nb = kext_ref.shape[1] // q_ref.shape[1]
    for qi in range(nb):
        @pl.when(pl.program_id(2) == qi)
        def _(qi=qi):
            _attn_query_block(qi, q_ref, kext_ref, v_ref, km_ref, o_ref, s_ref)


def _attn_out_kernel(o_ref, x_ref, g1_ref, w_ref, y_ref):
    nh = o_ref.shape[0]
    o = jnp.concatenate([o_ref[hh] for hh in range(nh)], axis=1)
    out = jnp.dot(o, w_ref[...], preferred_element_type=F32)
    y_ref[...] = x_ref[...] + g1_ref[...] * out


def _moba_layer(x, mod, ng, w_qkv, q_g, k_g, w_out, ts_out=512, heads_per_step=1):
    b, s, d = x.shape
    nh = ATTN_HEADS
    hd = d // nh
    tq = MOBA_BLOCK
    nb = s // tq
    assert s % tq == 0 and nb <= LANES and hd == LANES
    full = lambda shape: pl.BlockSpec(shape, lambda bi, si: (0,) * len(shape))

    q, kext, v, km = pl.pallas_call(
        _qkv_kernel,
        grid=(b, nb),
        in_specs=[
            pl.BlockSpec((None, tq, d), lambda bi, si: (bi, si, 0)),
            _mod_spec(0, d), _mod_spec(1, d),
            full((1, d)), full((d, 3 * d)), full((1, hd)), full((1, hd)),
        ],
        out_specs=[
            pl.BlockSpec((None, nh, tq, hd), lambda bi, si: (bi, 0, si, 0)),
            pl.BlockSpec((None, nh, tq, 2 * hd), lambda bi, si: (bi, 0, si, 0)),
            pl.BlockSpec((None, nh, tq, hd), lambda bi, si: (bi, 0, si, 0)),
            pl.BlockSpec((None, None, 1, d), lambda bi, si: (bi, si, 0, 0)),
        ],
        out_shape=[
            jax.ShapeDtypeStruct((b, nh, s, hd), BF16),
            jax.ShapeDtypeStruct((b, nh, s, 2 * hd), BF16),
            jax.ShapeDtypeStruct((b, nh, s, hd), BF16),
            jax.ShapeDtypeStruct((b, nb, 1, d), F32),
        ],
        compiler_params=_cparams(("arbitrary", "arbitrary")),
    )(x, mod, mod, ng.reshape(1, d), w_qkv.astype(BF16), q_g.reshape(1, hd), k_g.reshape(1, hd))

    km = km.reshape(b, nb, nh, hd).transpose(0, 2, 1, 3)
    km = jnp.pad(km, ((0, 0), (0, 0), (0, LANES - nb), (0, 0)))

    o = pl.pallas_call(
        _attn_kernel,
        grid=(b, nh // heads_per_step, nb),
        in_specs=[
            pl.BlockSpec((None, heads_per_step, tq, hd), lambda bi, hi, qi: (bi, hi, qi, 0)),
            pl.BlockSpec((None, heads_per_step, s, 2 * hd), lambda bi, hi, qi: (bi, hi, 0, 0)),
            pl.BlockSpec((None, heads_per_step, s, hd), lambda bi, hi, qi: (bi, hi, 0, 0)),
            pl.BlockSpec((None, heads_per_step, LANES, hd), lambda bi, hi, qi: (bi, hi, 0, 0)),
        ],
        out_specs=pl.BlockSpec((None, heads_per_step, tq, hd),
                               lambda bi, hi, qi: (bi, hi, qi, 0)),
        out_shape=jax.ShapeDtypeStruct((b, nh, s, hd), BF16),
        scratch_shapes=[pltpu.VMEM((heads_per_step, s, tq), F32)],
        compiler_params=_cparams(("arbitrary", "arbitrary", "arbitrary")),
    )(q, kext, v, km)

    return pl.pallas_call(
        _attn_out_kernel,
        grid=(b, s // ts_out),
        in_specs=[
            pl.BlockSpec((None, nh, ts_out, hd), lambda bi, si: (bi, 0, si, 0)),
            pl.BlockSpec((None, ts_out, d), lambda bi, si: (bi, si, 0)),
            _mod_spec(2, d),
            full((d, d)),
        ],
        out_specs=pl.BlockSpec((None, ts_out, d), lambda bi, si: (bi, si, 0)),
        out_shape=jax.ShapeDtypeStruct(x.shape, F32),
        compiler_params=_cparams(("arbitrary", "arbitrary")),
    )(o, x, mod, w_out.astype(BF16))


def _route_kernel(x_ref, sh_ref, sc_ref, ng_ref, rw_ref, rb_ref,
                  h_ref, e_ref, w_ref, r_ref, cnt_ref, run_ref):
    tm = x_ref.shape[0]

    @pl.when(pl.program_id(0) == 0)
    def _():
        run_ref[...] = jnp.zeros_like(run_ref)

    h = _modulated_norm(x_ref[...], ng_ref[...], sc_ref[...], sh_ref[...])
    h_ref[...] = h
    logits = jnp.dot(h, rw_ref[...], preferred_element_type=F32, precision=HIGHEST) + rb_ref[...]
    lane = lax.broadcasted_iota(jnp.int32, (tm, LANES), 1)
    lane_f = lane.astype(F32)
    lg = jnp.where(lane < N_EXPERTS, logits, -jnp.inf)
    tops, hits = [], []
    for _ in range(TOP_K):
        m = jnp.max(lg, axis=-1, keepdims=True)
        idx = jnp.min(jnp.where(lg == m, lane_f, float(LANES)), axis=-1, keepdims=True)
        hit = lane_f == idx
        tops.append((m, idx))
        hits.append(hit)
        lg = jnp.where(hit, -jnp.inf, lg)
    exps = [jnp.exp(m - tops[0][0]) for m, _ in tops]
    denom = exps[0]
    for ex in exps[1:]:
        denom = denom + ex

    member = jnp.zeros((tm, LANES), F32)
    for hit in hits:
        member = member + jnp.where(hit, 1.0, 0.0)
    rr = lax.broadcasted_iota(jnp.int32, (tm, tm), 0)
    cc = lax.broadcasted_iota(jnp.int32, (tm, tm), 1)
    earlier = jnp.where(cc < rr, 1.0, 0.0).astype(BF16)
    before = jnp.dot(earlier, member.astype(BF16), preferred_element_type=F32) + run_ref[...]

    e_out = jnp.zeros((tm, LANES), F32)
    w_out = jnp.zeros((tm, LANES), F32)
    r_out = jnp.zeros((tm, LANES), F32)
    for k in range(TOP_K):
        rank = jnp.sum(jnp.where(hits[k], before, 0.0), axis=-1, keepdims=True)
        e_out = jnp.where(lane == k, tops[k][1], e_out)
        w_out = jnp.where(lane == k, exps[k] / denom, w_out)
        r_out = jnp.where(lane == k, rank, r_out)
    e_ref[...] = e_out.astype(jnp.int32)
    w_ref[...] = w_out
    r_ref[...] = r_out.astype(jnp.int32)
    run = run_ref[...] + jnp.sum(member, axis=0, keepdims=True)
    run_ref[...] = run
    cnt_ref[...] = run.astype(jnp.int32)


def _dispatch_kernel(dest_ref, h_ref, xs_in_hbm, xs_hbm, sem):
    del xs_in_hbm
    tm = h_ref.shape[0]

    def issue(t, _):
        for k in range(TOP_K):
            pltpu.make_async_copy(h_ref.at[pl.ds(t, 1), :],
                                  xs_hbm.at[pl.ds(dest_ref[0, t * TOP_K + k], 1), :],
                                  sem).start()
        return 0

    lax.fori_loop(0, tm, issue, 0)
    for _ in range(TOP_K):
        pltpu.make_async_copy(h_ref, xs_hbm.at[pl.ds(0, tm), :], sem).wait()


def _expert_kernel(be_ref, first_ref, nu_ref, xs_ref, wgu_ref, bgu_ref, wd_ref, bd_ref,
                   o_ref, wgu_bf, wd_bf):
    i = pl.program_id(0)
    de = wd_ref.shape[0]

    @pl.when(i < nu_ref[0])
    def _():
        @pl.when(first_ref[i] == 1)
        def _():
            wgu_bf[...] = wgu_ref[...].astype(BF16)
            wd_bf[...] = wd_ref[...].astype(BF16)

        gu = jnp.dot(xs_ref[...].astype(BF16), wgu_bf[...], preferred_element_type=F32) + bgu_ref[...]
        gate = jnp.minimum(gu[:, :de], SWIGLU_LIMIT)
        up = jnp.clip(gu[:, de:], -SWIGLU_LIMIT, SWIGLU_LIMIT)
        act = gate * _sigmoid(SWIGLU_ALPHA * gate) * (up + 1.0)
        o_ref[...] = jnp.dot(act.astype(BF16), wd_bf[...], preferred_element_type=F32) + bd_ref[...]

    @pl.when(i >= nu_ref[0])
    def _():
        o_ref[...] = jnp.zeros_like(o_ref)


def _combine_kernel(dest_ref, x_ref, w_ref, g2_ref, y_hbm, o_ref, buf, sem):
    tm = x_ref.shape[0]

    def issue(r, _):
        pltpu.make_async_copy(y_hbm.at[pl.ds(dest_ref[0, r], 1), :], buf.at[pl.ds(r, 1), :],
                              sem).start()
        return 0

    lax.fori_loop(0, tm, issue, 0)
    pltpu.make_async_copy(y_hbm.at[pl.ds(0, tm), :], buf, sem).wait()
    o_ref[...] = x_ref[...] + g2_ref[...] * (w_ref[...] * buf[...])


def _tok_mod_spec(j, tm, s, d):
    return pl.BlockSpec((None, None, 1, d), lambda i: (j, (i * tm) // s, 0, 0))


def _moe_route(xt, t, s, mod, ng, router_w, router_b, tm=256):
    d = xt.shape[1]
    assert t % tm == 0
    rw = jnp.pad(router_w, ((0, 0), (0, LANES - N_EXPERTS)))
    rb = jnp.pad(router_b, (0, LANES - N_EXPERTS)).reshape(1, LANES)
    return pl.pallas_call(
        _route_kernel,
        grid=(t // tm,),
        in_specs=[
            pl.BlockSpec((tm, d), lambda i: (i, 0)),
            _tok_mod_spec(3, tm, s, d), _tok_mod_spec(4, tm, s, d),
            pl.BlockSpec((1, d), lambda i: (0, 0)),
            pl.BlockSpec((d, LANES), lambda i: (0, 0)),
            pl.BlockSpec((1, LANES), lambda i: (0, 0)),
        ],
        out_specs=[
            pl.BlockSpec((tm, d), lambda i: (i, 0)),
            pl.BlockSpec((tm, LANES), lambda i: (i, 0)),
            pl.BlockSpec((tm, LANES), lambda i: (i, 0)),
            pl.BlockSpec((tm, LANES), lambda i: (i, 0)),
            pl.BlockSpec((1, LANES), lambda i: (0, 0)),
        ],
        out_shape=[
            jax.ShapeDtypeStruct((t, d), F32),
            jax.ShapeDtypeStruct((t, LANES), jnp.int32),
            jax.ShapeDtypeStruct((t, LANES), F32),
            jax.ShapeDtypeStruct((t, LANES), jnp.int32),
            jax.ShapeDtypeStruct((1, LANES), jnp.int32),
        ],
        scratch_shapes=[pltpu.VMEM((1, LANES), F32)],
        compiler_params=_cparams(("arbitrary",)),
    )(xt, mod, mod, ng.reshape(1, d), rw, rb)


def _moe_plan(top_e, rank, counts, tm):
    t = top_e.shape[0]
    n_e = N_EXPERTS
    counts = counts[0, :n_e]
    padded = (counts + tm - 1) // tm * tm
    pad_end = jnp.cumsum(padded)
    pad_start = pad_end - padded
    dest = pad_start[top_e[:, :TOP_K]] + rank[:, :TOP_K]
    n_tiles = (t * TOP_K) // tm + n_e
    tile_start = jnp.arange(n_tiles, dtype=jnp.int32) * tm
    n_used = (pad_end[-1] // tm).astype(jnp.int32)
    tile_expert = jnp.minimum(jnp.searchsorted(pad_end, tile_start, side='right'), n_e - 1)
    tile_expert = jnp.where(jnp.arange(n_tiles) < n_used, tile_expert,
                            tile_expert[jnp.maximum(n_used - 1, 0)]).astype(jnp.int32)
    first = jnp.concatenate([jnp.ones((1,), jnp.int32),
                             (tile_expert[1:] != tile_expert[:-1]).astype(jnp.int32)])
    return dest.astype(jnp.int32), tile_expert, first, n_used.reshape(1), n_tiles


def _moe_dispatch(h2, dest, n_rows, tm=256):
    t, d = h2.shape
    return pl.pallas_call(
        _dispatch_kernel,
        grid=(t // tm,),
        in_specs=[
            pl.BlockSpec((None, 1, tm * TOP_K), lambda i: (i, 0, 0), memory_space=pltpu.SMEM),
            pl.BlockSpec((tm, d), lambda i: (i, 0)),
            pl.BlockSpec(memory_space=pl.ANY),
        ],
        out_specs=pl.BlockSpec(memory_space=pl.ANY),
        out_shape=jax.ShapeDtypeStruct((n_rows, d), F32),
        scratch_shapes=[pltpu.SemaphoreType.DMA],
        input_output_aliases={2: 0},
        compiler_params=_cparams(("arbitrary",)),
    )(dest.reshape(t // tm, 1, tm * TOP_K), h2, jnp.zeros((n_rows, d), F32))


def _moe_experts(xs, tile_expert, first, n_used, w_gu, b_gu, w_down, b_down, layer, tm):
    n_rows, d = xs.shape
    n_e = N_EXPERTS
    de = w_down.shape[2]
    tile_idx = lambda i, be, fi, nu: (jnp.maximum(jnp.minimum(i, nu[0] - 1), 0), 0)
    w_idx = lambda i, be, fi, nu: (layer, be[i], 0, 0)
    return pl.pallas_call(
        _expert_kernel,
        grid_spec=pltpu.PrefetchScalarGridSpec(
            num_scalar_prefetch=3,
            grid=(n_rows // tm,),
            in_specs=[
                pl.BlockSpec((tm, d), tile_idx),
                pl.BlockSpec((None, None, d, 2 * de), w_idx),
                pl.BlockSpec((None, None, 1, 2 * de), w_idx),
                pl.BlockSpec((None, None, de, d), w_idx),
                pl.BlockSpec((None, None, 1, d), w_idx),
            ],
            out_specs=pl.BlockSpec((tm, d), lambda i, be, fi, nu: (i, 0)),
            scratch_shapes=[pltpu.VMEM((d, 2 * de), BF16), pltpu.VMEM((de, d), BF16)],
        ),
        out_shape=jax.ShapeDtypeStruct((n_rows, d), F32),
        compiler_params=_cparams(("arbitrary",)),
    )(tile_expert, first, n_used, xs, w_gu, b_gu.reshape(b_gu.shape[0], n_e, 1, 2 * de),
      w_down, b_down.reshape(b_down.shape[0], n_e, 1, d))


def _moe_combine(xt, s, mod, dest_item, w_item, y_rows, tm=256):
    t, d = xt.shape
    return pl.pallas_call(
        _combine_kernel,
        grid=(t // tm,),
        in_specs=[
            pl.BlockSpec((None, 1, tm), lambda i: (i, 0, 0), memory_space=pltpu.SMEM),
            pl.BlockSpec((tm, d), lambda i: (i, 0)),
            pl.BlockSpec((tm, 1), lambda i: (i, 0)),
            _tok_mod_spec(5, tm, s, d),
            pl.BlockSpec(memory_space=pl.ANY),
        ],
        out_specs=pl.BlockSpec((tm, d), lambda i: (i, 0)),
        out_shape=jax.ShapeDtypeStruct((t, d), F32),
        scratch_shapes=[pltpu.VMEM((tm, d), F32), pltpu.SemaphoreType.DMA],
        compiler_params=_cparams(("arbitrary",)),
    )(dest_item.reshape(t // tm, 1, tm), xt, w_item, mod, y_rows)


def _moe_layer(x, mod, ng, router_w, router_b, w_gu, b_gu, w_down, b_down, layer, tm_exp=256):
    b, s, d = x.shape
    t = b * s
    xt = x.reshape(t, d)
    assert t % TOP_K == 0
    n_tok = t // TOP_K
    h2, top_e, top_w, rank, counts = _moe_route(xt, n_tok, s, mod, ng, router_w, router_b)
    dest, tile_expert, first, n_used, n_tiles = _moe_plan(top_e, rank, counts, tm_exp)
    xs = _moe_dispatch(h2, dest, n_tiles * tm_exp)
    y_rows = _moe_experts(xs, tile_expert, first, n_used, w_gu, b_gu, w_down, b_down, layer, tm_exp)
    dest_item = dest.reshape(t)
    w_item = top_w[:, :TOP_K].reshape(t, 1)
    return _moe_combine(xt, s, mod, dest_item, w_item, y_rows).reshape(b, s, d)


def kernel(x, c, norm_mix_g, norm_ffn_g, w_mod, b_mod, lru_w_in, lru_conv_w, lru_conv_b, lru_w_a, lru_b_a, lru_w_x, lru_b_x, lru_lambda, lru_w_out, attn_w_qkv, attn_q_norm_g, attn_k_norm_g, attn_w_out, router_w, router_b, expert_w_gu, expert_b_gu, expert_w_down, expert_b_down):
    depth = w_mod.shape[0]
    mods = _modulation(c, w_mod, b_mod)
    for i in range(depth):
        mod = mods[i]
        j = i // 2
        if i % 2 == 0:
            x = _rglru_layer(x, mod, norm_mix_g[i], lru_w_in[j], lru_conv_w[j], lru_conv_b[j],
                             lru_w_a[j], lru_b_a[j], lru_w_x[j], lru_b_x[j], lru_lambda[j],
                             lru_w_out[j])
        else:
            x = _moba_layer(x, mod, norm_mix_g[i], attn_w_qkv[j], attn_q_norm_g[j],
                            attn_k_norm_g[j], attn_w_out[j])
        x = _moe_layer(x, mod, norm_ffn_g[i], router_w[i], router_b[i], expert_w_gu,
                       expert_b_gu, expert_w_down, expert_b_down, i)
    return x
```

```python
import functools

import jax
import jax.numpy as jnp
from jax import lax
from jax.experimental import pallas as pl
from jax.experimental.pallas import tpu as pltpu

NORM_EPS = 1e-6
N_MOD = 6
LRU_BLOCKS = 4
CONV_WIDTH = 4
LRU_C = 8.0
ATTN_HEADS = 8
MOBA_BLOCK = 256
MOBA_TOPK = 3
N_EXPERTS = 32
TOP_K = 4
SWIGLU_LIMIT = 7.0
SWIGLU_ALPHA = 1.702

LANES = 128
SUBLANES = 8
MASK_BIAS = -(2.0 ** 100)
LOG2_E = 1.4426950408889634
VMEM_LIMIT = 52 * 1024 * 1024

F32 = jnp.float32
BF16 = jnp.bfloat16
HIGHEST = lax.Precision.HIGHEST


def _cparams(sem):
    return pltpu.CompilerParams(dimension_semantics=sem, vmem_limit_bytes=VMEM_LIMIT)


def _sigmoid(z):
    return 1.0 / (1.0 + jnp.exp(-z))


def _modulated_norm(x, g, sc, sh):
    ms = jnp.mean(x * x, axis=-1, keepdims=True)
    return x * lax.rsqrt(ms + NORM_EPS) * g * (1.0 + sc) + sh


def _mod_kernel(c_ref, w_ref, b_ref, o_ref):
    c = c_ref[...]
    cond = c * _sigmoid(c)
    o_ref[...] = jnp.dot(cond, w_ref[...], preferred_element_type=F32,
                         precision=HIGHEST) + b_ref[...]


def _modulation(c, w_mod, b_mod):
    depth, d, _ = w_mod.shape
    b = c.shape[0]
    rows = -(-b // SUBLANES) * SUBLANES
    c_pad = jnp.pad(c, ((0, rows - b), (0, 0)))
    out = pl.pallas_call(
        _mod_kernel,
        grid=(depth, N_MOD),
        in_specs=[
            pl.BlockSpec((rows, d), lambda i, j: (0, 0)),
            pl.BlockSpec((None, d, d), lambda i, j: (i, 0, j)),
            pl.BlockSpec((None, None, 1, d), lambda i, j: (i, j, 0, 0)),
        ],
        out_specs=pl.BlockSpec((None, None, rows, d), lambda i, j: (i, j, 0, 0)),
        out_shape=jax.ShapeDtypeStruct((depth, N_MOD, rows, d), F32),
        compiler_params=_cparams(("arbitrary", "arbitrary")),
    )(c_pad, w_mod, b_mod.reshape(depth, N_MOD, 1, d))
    return out[:, :, :b].reshape(depth, N_MOD, b, 1, d)


def _mod_spec(j, d):
    return pl.BlockSpec((None, None, 1, d), lambda b, s: (j, b, 0, 0))


def _rglru_kernel(x_ref, sh_ref, sc_ref, g1_ref, ng_ref, win_ref, cw_ref, cb_ref,
                  wa_ref, ba_ref, wx_ref, bx_ref, lam_ref, wout_ref, o_ref,
                  ext_ref, a_ref, u_ref, h_ref):
    ts, d = x_ref.shape
    bw = d // LRU_BLOCKS

    @pl.when(pl.program_id(1) == 0)
    def _():
        ext_ref[0:SUBLANES, :] = jnp.zeros((SUBLANES, d), F32)
        h_ref[...] = jnp.zeros_like(h_ref)

    x = x_ref[...]
    h = _modulated_norm(x, ng_ref[...], sc_ref[...], sh_ref[...])
    gr = jnp.dot(h.astype(BF16), win_ref[...], preferred_element_type=F32)
    gate_branch = gr[:, :d]
    rec = gr[:, d:]

    ext_ref[SUBLANES:, :] = rec
    xc = cb_ref[...] + cw_ref[CONV_WIDTH - 1:CONV_WIDTH, :] * rec
    for k in range(CONV_WIDTH - 1):
        off = SUBLANES - (CONV_WIDTH - 1) + k
        xc = xc + cw_ref[k:k + 1, :] * ext_ref[off:off + ts, :]
    ext_ref[0:SUBLANES, :] = rec[ts - SUBLANES:, :]

    xcb = xc.astype(BF16)
    ra = jnp.concatenate(
        [jnp.dot(xcb[:, g * bw:(g + 1) * bw], wa_ref[g], preferred_element_type=F32)
         for g in range(LRU_BLOCKS)], axis=1) + ba_ref[...]
    rx = jnp.concatenate(
        [jnp.dot(xcb[:, g * bw:(g + 1) * bw], wx_ref[g], preferred_element_type=F32)
         for g in range(LRU_BLOCKS)], axis=1) + bx_ref[...]
    r = _sigmoid(ra)
    ig = _sigmoid(rx)
    z = -lam_ref[...]
    softplus = jnp.maximum(z, 0.0) + jnp.log(1.0 + jnp.exp(-jnp.abs(z)))
    a = jnp.exp(-LRU_C * r * softplus)
    u = jnp.sqrt(1.0 - a * a) * (ig * xc)

    row = lax.broadcasted_iota(jnp.int32, (ts, d), 0) % SUBLANES
    k = 1
    while k < SUBLANES:
        a_prev = pltpu.roll(a, k, 0)
        u_prev = pltpu.roll(u, k, 0)
        m = row >= k
        u = jnp.where(m, a * u_prev + u, u)
        a = jnp.where(m, a * a_prev, a)
        k *= 2
    a_ref[...] = a
    u_ref[...] = u

    def group(j, hc):
        r0 = pl.multiple_of(j * SUBLANES, SUBLANES)
        hs = u_ref[pl.ds(r0, SUBLANES), :] + a_ref[pl.ds(r0, SUBLANES), :] * hc
        u_ref[pl.ds(r0, SUBLANES), :] = hs
        return hs[SUBLANES - 1:SUBLANES, :]

    h_ref[0:1, :] = lax.fori_loop(0, ts // SUBLANES, group, h_ref[0:1, :])

    y = jax.nn.gelu(gate_branch, approximate=True) * u_ref[...]
    out = jnp.dot(y.astype(BF16), wout_ref[...], preferred_element_type=F32)
    o_ref[...] = x + g1_ref[...] * out


def _rglru_layer(x, mod, ng, w_in, conv_w, conv_b, w_a, b_a, w_x, b_x, lam, w_out, ts=256):
    b, s, d = x.shape
    full = lambda shape: pl.BlockSpec(shape, lambda bi, si: (0,) * len(shape))
    row = lambda v: v.reshape(1, d)
    return pl.pallas_call(
        _rglru_kernel,
        grid=(b, s // ts),
        in_specs=[
            pl.BlockSpec((None, ts, d), lambda bi, si: (bi, si, 0)),
            _mod_spec(0, d), _mod_spec(1, d), _mod_spec(2, d),
            full((1, d)), full((d, 2 * d)), full((CONV_WIDTH, d)), full((1, d)),
            full(w_a.shape), full((1, d)), full(w_x.shape), full((1, d)), full((1, d)),
            full((d, d)),
        ],
        out_specs=pl.BlockSpec((None, ts, d), lambda bi, si: (bi, si, 0)),
        out_shape=jax.ShapeDtypeStruct(x.shape, F32),
        scratch_shapes=[
            pltpu.VMEM((ts + SUBLANES, d), F32),
            pltpu.VMEM((ts, d), F32),
            pltpu.VMEM((ts, d), F32),
            pltpu.VMEM((SUBLANES, d), F32),
        ],
        compiler_params=_cparams(("arbitrary", "arbitrary")),
    )(x, mod, mod, mod, row(ng), w_in.astype(BF16), conv_w, row(conv_b),
      w_a.astype(BF16), row(b_a), w_x.astype(BF16), row(b_x), row(lam), w_out.astype(BF16))


def _qkv_kernel(x_ref, sh_ref, sc_ref, ng_ref, w_ref, qg_ref, kg_ref,
                q_ref, kext_ref, v_ref, km_ref):
    ts, d = x_ref.shape
    nh, _, hd = q_ref.shape
    blk = pl.program_id(1)
    h = _modulated_norm(x_ref[...], ng_ref[...], sc_ref[...], sh_ref[...])
    qkv = jnp.dot(h.astype(BF16), w_ref[...], preferred_element_type=F32)
    lane = lax.broadcasted_iota(jnp.int32, (ts, hd), 1)
    onehot = jnp.where(lane == blk, 1.0, 0.0).astype(BF16)
    for hh in range(nh):
        qh = qkv[:, hh * hd:(hh + 1) * hd]
        qn = qh * lax.rsqrt(jnp.mean(qh * qh, axis=-1, keepdims=True) + NORM_EPS)
        q_ref[hh] = (qn * qg_ref[...] * (hd ** -0.5 * LOG2_E)).astype(BF16)
        kh = qkv[:, d + hh * hd:d + (hh + 1) * hd]
        kn = kh * lax.rsqrt(jnp.mean(kh * kh, axis=-1, keepdims=True) + NORM_EPS) * kg_ref[...]
        kext_ref[hh, :, 0:hd] = kn.astype(BF16)
        kext_ref[hh, :, hd:2 * hd] = onehot
        km_ref[:, hh * hd:(hh + 1) * hd] = jnp.mean(kn, axis=0, keepdims=True)
        v_ref[hh] = qkv[:, 2 * d + hh * hd:2 * d + (hh + 1) * hd].astype(BF16)


def _attn_query_block(qi, q_ref, kext_ref, v_ref, km_ref, o_ref, s_ref):
    nhs, tq, hd = q_ref.shape
    nt = (((1,), (1,)), ((), ()))
    nbp = -(-(kext_ref.shape[1] // tq) // SUBLANES) * SUBLANES
    blk = lax.broadcasted_iota(jnp.int32, (nbp, tq), 0)
    blk_f = blk.astype(F32)
    past = blk < qi
    rr = lax.broadcasted_iota(jnp.int32, (tq, tq), 0)
    cc = lax.broadcasted_iota(jnp.int32, (tq, tq), 1)
    for hh in range(nhs):
        q = q_ref[hh]
        gate = lax.dot_general(km_ref[hh], q.astype(F32), nt, precision=HIGHEST,
                               preferred_element_type=F32)[:nbp]
        g = jnp.where(past, gate, -jnp.inf)
        sel = blk == qi
        for _ in range(min(MOBA_TOPK, qi)):
            m = jnp.max(g, axis=0, keepdims=True)
            idx = jnp.min(jnp.where(g == m, blk_f, float(nbp)), axis=0, keepdims=True)
            hit = blk_f == idx
            sel = sel | (hit & past)
            g = jnp.where(hit, -jnp.inf, g)
        bias_t = jnp.where(sel, 0.0, MASK_BIAS)
        bias = jnp.concatenate([bias_t, jnp.zeros((LANES - nbp, tq), F32)], axis=0).T
        q_ext = jnp.concatenate([q, bias.astype(BF16)], axis=1)

        mx = None
        for j in range(qi + 1):
            sj = lax.dot_general(q_ext, kext_ref[hh, j * tq:(j + 1) * tq, :], nt,
                                 preferred_element_type=F32)
            if j == qi:
                sj = jnp.where(cc <= rr, sj, MASK_BIAS)
            s_ref[hh, j * tq:(j + 1) * tq, :] = sj
            mx = sj if mx is None else jnp.maximum(mx, sj)
        m = jnp.max(mx, axis=-1, keepdims=True)

        psum, acc = None, None
        for j in range(qi + 1):
            p = jnp.exp2(s_ref[hh, j * tq:(j + 1) * tq, :] - m)
            pv = jnp.dot(p.astype(BF16), v_ref[hh, j * tq:(j + 1) * tq, :],
                         preferred_element_type=F32)
            psum = p if psum is None else psum + p
            acc = pv if acc is None else acc + pv
        l = jnp.sum(psum, axis=-1, keepdims=True)
        o_ref[hh] = (acc / l).astype(BF16)


def _attn_kernel(q_ref, kext_ref, v_ref, km_ref, o_ref, s_ref):
    nb = kext_ref.shape[1] // q_ref.shape[1]
    for qi in range(nb):
        @pl.when(pl.program_id(2) == qi)
        def _(qi=qi):
            _attn_query_block(qi, q_ref, kext_ref, v_ref, km_ref, o_ref, s_ref)


def _attn_out_kernel(o_ref, x_ref, g1_ref, w_ref, y_ref):
    nh = o_ref.shape[0]
    o = jnp.concatenate([o_ref[hh] for hh in range(nh)], axis=1)
    out = jnp.dot(o, w_ref[...], preferred_element_type=F32)
    y_ref[...] = x_ref[...] + g1_ref[...] * out


def _moba_layer(x, mod, ng, w_qkv, q_g, k_g, w_out, ts_out=512, heads_per_step=1):
    b, s, d = x.shape
    nh = ATTN_HEADS
    hd = d // nh
    tq = MOBA_BLOCK
    nb = s // tq
    assert s % tq == 0 and nb <= LANES and hd == LANES
    full = lambda shape: pl.BlockSpec(shape, lambda bi, si: (0,) * len(shape))

    q, kext, v, km = pl.pallas_call(
        _qkv_kernel,
        grid=(b, nb),
        in_specs=[
            pl.BlockSpec((None, tq, d), lambda bi, si: (bi, si, 0)),
            _mod_spec(0, d), _mod_spec(1, d),
            full((1, d)), full((d, 3 * d)), full((1, hd)), full((1, hd)),
        ],
        out_specs=[
            pl.BlockSpec((None, nh, tq, hd), lambda bi, si: (bi, 0, si, 0)),
            pl.BlockSpec((None, nh, tq, 2 * hd), lambda bi, si: (bi, 0, si, 0)),
            pl.BlockSpec((None, nh, tq, hd), lambda bi, si: (bi, 0, si, 0)),
            pl.BlockSpec((None, None, 1, d), lambda bi, si: (bi, si, 0, 0)),
        ],
        out_shape=[
            jax.ShapeDtypeStruct((b, nh, s, hd), BF16),
            jax.ShapeDtypeStruct((b, nh, s, 2 * hd), BF16),
            jax.ShapeDtypeStruct((b, nh, s, hd), BF16),
            jax.ShapeDtypeStruct((b, nb, 1, d), F32),
        ],
        compiler_params=_cparams(("arbitrary", "arbitrary")),
    )(x, mod, mod, ng.reshape(1, d), w_qkv.astype(BF16), q_g.reshape(1, hd), k_g.reshape(1, hd))

    km = km.reshape(b, nb, nh, hd).transpose(0, 2, 1, 3)
    km = jnp.pad(km, ((0, 0), (0, 0), (0, LANES - nb), (0, 0)))

    o = pl.pallas_call(
        _attn_kernel,
        grid=(b, nh // heads_per_step, nb),
        in_specs=[
            pl.BlockSpec((None, heads_per_step, tq, hd), lambda bi, hi, qi: (bi, hi, qi, 0)),
            pl.BlockSpec((None, heads_per_step, s, 2 * hd), lambda bi, hi, qi: (bi, hi, 0, 0)),
            pl.BlockSpec((None, heads_per_step, s, hd), lambda bi, hi, qi: (bi, hi, 0, 0)),
            pl.BlockSpec((None, heads_per_step, LANES, hd), lambda bi, hi, qi: (bi, hi, 0, 0)),
        ],
        out_specs=pl.BlockSpec((None, heads_per_step, tq, hd),
                               lambda bi, hi, qi: (bi, hi, qi, 0)),
        out_shape=jax.ShapeDtypeStruct((b, nh, s, hd), BF16),
        scratch_shapes=[pltpu.VMEM((heads_per_step, s, tq), F32)],
        compiler_params=_cparams(("arbitrary", "arbitrary", "arbitrary")),
    )(q, kext, v, km)

    return pl.pallas_call(
        _attn_out_kernel,
        grid=(b, s // ts_out),
        in_specs=[
            pl.BlockSpec((None, nh, ts_out, hd), lambda bi, si: (bi, 0, si, 0)),
            pl.BlockSpec((None, ts_out, d), lambda bi, si: (bi, si, 0)),
            _mod_spec(2, d),
            full((d, d)),
        ],
        out_specs=pl.BlockSpec((None, ts_out, d), lambda bi, si: (bi, si, 0)),
        out_shape=jax.ShapeDtypeStruct(x.shape, F32),
        compiler_params=_cparams(("arbitrary", "arbitrary")),
    )(o, x, mod, w_out.astype(BF16))


def _route_kernel(x_ref, sh_ref, sc_ref, ng_ref, rw_ref, rb_ref,
                  h_ref, e_ref, w_ref, r_ref, cnt_ref, run_ref):
    tm = x_ref.shape[0]

    @pl.when(pl.program_id(0) == 0)
    def _():
        run_ref[...] = jnp.zeros_like(run_ref)

    h = _modulated_norm(x_ref[...], ng_ref[...], sc_ref[...], sh_ref[...])
    h_ref[...] = h
    logits = jnp.dot(h, rw_ref[...], preferred_element_type=F32, precision=HIGHEST) + rb_ref[...]
    lane = lax.broadcasted_iota(jnp.int32, (tm, LANES), 1)
    lane_f = lane.astype(F32)
    lg = jnp.where(lane < N_EXPERTS, logits, -jnp.inf)
    tops, hits = [], []
    for _ in range(TOP_K):
        m = jnp.max(lg, axis=-1, keepdims=True)
        idx = jnp.min(jnp.where(lg == m, lane_f, float(LANES)), axis=-1, keepdims=True)
        hit = lane_f == idx
        tops.append((m, idx))
        hits.append(hit)
        lg = jnp.where(hit, -jnp.inf, lg)
    exps = [jnp.exp(m - tops[0][0]) for m, _ in tops]
    denom = exps[0]
    for ex in exps[1:]:
        denom = denom + ex

    member = jnp.zeros((tm, LANES), F32)
    for hit in hits:
        member = member + jnp.where(hit, 1.0, 0.0)
    rr = lax.broadcasted_iota(jnp.int32, (tm, tm), 0)
    cc = lax.broadcasted_iota(jnp.int32, (tm, tm), 1)
    earlier = jnp.where(cc < rr, 1.0, 0.0).astype(BF16)
    before = jnp.dot(earlier, member.astype(BF16), preferred_element_type=F32) + run_ref[...]

    e_out = jnp.zeros((tm, LANES), F32)
    w_out = jnp.zeros((tm, LANES), F32)
    r_out = jnp.zeros((tm, LANES), F32)
    for k in range(TOP_K):
        rank = jnp.sum(jnp.where(hits[k], before, 0.0), axis=-1, keepdims=True)
        e_out = jnp.where(lane == k, tops[k][1], e_out)
        w_out = jnp.where(lane == k, exps[k] / denom, w_out)
        r_out = jnp.where(lane == k, rank, r_out)
    e_ref[...] = e_out.astype(jnp.int32)
    w_ref[...] = w_out
    r_ref[...] = r_out.astype(jnp.int32)
    run = run_ref[...] + jnp.sum(member, axis=0, keepdims=True)
    run_ref[...] = run
    cnt_ref[...] = run.astype(jnp.int32)


def _dispatch_kernel(dest_ref, h_ref, xs_in_hbm, xs_hbm, sem):
    del xs_in_hbm
    tm = h_ref.shape[0]

    def issue(t, _):
        for k in range(TOP_K):
            pltpu.make_async_copy(h_ref.at[pl.ds(t, 1), :],
                                  xs_hbm.at[pl.ds(dest_ref[0, t * TOP_K + k], 1), :],
                                  sem).start()
        return 0

    lax.fori_loop(0, tm, issue, 0)
    for _ in range(TOP_K):
        pltpu.make_async_copy(h_ref, xs_hbm.at[pl.ds(0, tm), :], sem).wait()


def _wait_rows(n, src_ref, dst_hbm, sem):
    p = 1
    while p <= src_ref.shape[0]:
        @pl.when((n & p) != 0)
        def _(p=p):
            pltpu.make_async_copy(src_ref.at[pl.ds(0, p), :], dst_hbm.at[pl.ds(0, p), :], sem).wait()
        p *= 2


def _expert_kernel(be_ref, first_ref, nu_ref, nv_ref, item_ref, xs_ref, wgu_ref, bgu_ref, wd_ref,
                   bd_ref, yt_hbm, wgu_bf, wd_bf, ybuf, sems):
    i = pl.program_id(0)
    de = wd_ref.shape[0]
    slot = i % 2
    nv = nv_ref[i]

    @pl.when(nv > 0)
    def _():
        @pl.when(first_ref[i] == 1)
        def _():
            wgu_bf[...] = wgu_ref[...].astype(BF16)
            wd_bf[...] = wd_ref[...].astype(BF16)

        gu = jnp.dot(xs_ref[...].astype(BF16), wgu_bf[...], preferred_element_type=F32) + bgu_ref[...]
        gate = jnp.minimum(gu[:, :de], SWIGLU_LIMIT)
        up = jnp.clip(gu[:, de:], -SWIGLU_LIMIT, SWIGLU_LIMIT)
        act = gate * _sigmoid(SWIGLU_ALPHA * gate) * (up + 1.0)
        ybuf[slot] = jnp.dot(act.astype(BF16), wd_bf[...], preferred_element_type=F32) + bd_ref[...]

        def issue(r, _):
            pltpu.make_async_copy(ybuf.at[slot, pl.ds(r, 1), :],
                                  yt_hbm.at[pl.ds(item_ref[0, r], 1), :], sems.at[slot]).start()
            return 0

        lax.fori_loop(0, nv, issue, 0)

    @pl.when(i > 0)
    def _():
        _wait_rows(nv_ref[jnp.maximum(i - 1, 0)], ybuf.at[1 - slot], yt_hbm, sems.at[1 - slot])

    @pl.when(i == pl.num_programs(0) - 1)
    def _():
        _wait_rows(nv, ybuf.at[slot], yt_hbm, sems.at[slot])


def _combine_kernel(x_ref, yt_ref, w_ref, g2_ref, o_ref):
    o_ref[...] = x_ref[...] + g2_ref[...] * (w_ref[...] * yt_ref[...])


def _tok_mod_spec(j, tm, s, d):
    return pl.BlockSpec((None, None, 1, d), lambda i: (j, (i * tm) // s, 0, 0))


def _moe_route(xt, t, s, mod, ng, router_w, router_b, tm=256):
    d = xt.shape[1]
    assert t % tm == 0
    rw = jnp.pad(router_w, ((0, 0), (0, LANES - N_EXPERTS)))
    rb = jnp.pad(router_b, (0, LANES - N_EXPERTS)).reshape(1, LANES)
    return pl.pallas_call(
        _route_kernel,
        grid=(t // tm,),
        in_specs=[
            pl.BlockSpec((tm, d), lambda i: (i, 0)),
            _tok_mod_spec(3, tm, s, d), _tok_mod_spec(4, tm, s, d),
            pl.BlockSpec((1, d), lambda i: (0, 0)),
            pl.BlockSpec((d, LANES), lambda i: (0, 0)),
            pl.BlockSpec((1, LANES), lambda i: (0, 0)),
        ],
        out_specs=[
            pl.BlockSpec((tm, d), lambda i: (i, 0)),
            pl.BlockSpec((tm, LANES), lambda i: (i, 0)),
            pl.BlockSpec((tm, LANES), lambda i: (i, 0)),
            pl.BlockSpec((tm, LANES), lambda i: (i, 0)),
            pl.BlockSpec((1, LANES), lambda i: (0, 0)),
        ],
        out_shape=[
            jax.ShapeDtypeStruct((t, d), F32),
            jax.ShapeDtypeStruct((t, LANES), jnp.int32),
            jax.ShapeDtypeStruct((t, LANES), F32),
            jax.ShapeDtypeStruct((t, LANES), jnp.int32),
            jax.ShapeDtypeStruct((1, LANES), jnp.int32),
        ],
        scratch_shapes=[pltpu.VMEM((1, LANES), F32)],
        compiler_params=_cparams(("arbitrary",)),
    )(xt, mod, mod, ng.reshape(1, d), rw, rb)


def _moe_plan(top_e, rank, counts, tm):
    t = top_e.shape[0]
    n_e = N_EXPERTS
    counts = counts[0, :n_e]
    padded = (counts + tm - 1) // tm * tm
    pad_end = jnp.cumsum(padded)
    pad_start = pad_end - padded
    dest = pad_start[top_e[:, :TOP_K]] + rank[:, :TOP_K]
    n_tiles = (t * TOP_K) // tm + n_e
    tile_start = jnp.arange(n_tiles, dtype=jnp.int32) * tm
    n_used = (pad_end[-1] // tm).astype(jnp.int32)
    tile_expert = jnp.minimum(jnp.searchsorted(pad_end, tile_start, side='right'), n_e - 1)
    tile_expert = jnp.where(jnp.arange(n_tiles) < n_used, tile_expert,
                            tile_expert[jnp.maximum(n_used - 1, 0)]).astype(jnp.int32)
    first = jnp.concatenate([jnp.ones((1,), jnp.int32),
                             (tile_expert[1:] != tile_expert[:-1]).astype(jnp.int32)])
    seg_end = (pad_start + counts)[tile_expert]
    n_valid = jnp.where(jnp.arange(n_tiles) < n_used,
                        jnp.clip(seg_end - tile_start, 0, tm), 0).astype(jnp.int32)
    dest = dest.astype(jnp.int32)
    row_item = jnp.zeros((n_tiles * tm,), jnp.int32).at[dest.reshape(-1)].set(
        jnp.arange(t * TOP_K, dtype=jnp.int32), unique_indices=True)
    return dest, tile_expert, first, n_used.reshape(1), n_valid, row_item, n_tiles


def _moe_dispatch(h2, dest, xs_buf, tm=256):
    t, d = h2.shape
    n_rows = xs_buf.shape[0]
    return pl.pallas_call(
        _dispatch_kernel,
        grid=(t // tm,),
        in_specs=[
            pl.BlockSpec((None, 1, tm * TOP_K), lambda i: (i, 0, 0), memory_space=pltpu.SMEM),
            pl.BlockSpec((tm, d), lambda i: (i, 0)),
            pl.BlockSpec(memory_space=pl.ANY),
        ],
        out_specs=pl.BlockSpec(memory_space=pl.ANY),
        out_shape=jax.ShapeDtypeStruct((n_rows, d), F32),
        scratch_shapes=[pltpu.SemaphoreType.DMA],
        input_output_aliases={2: 0},
        compiler_params=_cparams(("arbitrary",)),
    )(dest.reshape(t // tm, 1, tm * TOP_K), h2, xs_buf)


def _moe_experts(xs, n_items, tile_expert, first, n_used, n_valid, row_item,
                 w_gu, b_gu, w_down, b_down, layer, tm):
    n_rows, d = xs.shape
    n_e = N_EXPERTS
    de = w_down.shape[2]
    n_tiles = n_rows // tm
    tile_idx = lambda i, be, fi, nu, nv: (jnp.maximum(jnp.minimum(i, nu[0] - 1), 0), 0)
    w_idx = lambda i, be, fi, nu, nv: (layer, be[i], 0, 0)
    return pl.pallas_call(
        _expert_kernel,
        grid_spec=pltpu.PrefetchScalarGridSpec(
            num_scalar_prefetch=4,
            grid=(n_tiles,),
            in_specs=[
                pl.BlockSpec((None, 1, tm), lambda i, be, fi, nu, nv: (i, 0, 0),
                             memory_space=pltpu.SMEM),
                pl.BlockSpec((tm, d), tile_idx),
                pl.BlockSpec((None, None, d, 2 * de), w_idx),
                pl.BlockSpec((None, None, 1, 2 * de), w_idx),
                pl.BlockSpec((None, None, de, d), w_idx),
                pl.BlockSpec((None, None, 1, d), w_idx),
            ],
            out_specs=pl.BlockSpec(memory_space=pl.ANY),
            scratch_shapes=[pltpu.VMEM((d, 2 * de), BF16), pltpu.VMEM((de, d), BF16),
                            pltpu.VMEM((2, tm, d), F32), pltpu.SemaphoreType.DMA((2,))],
        ),
        out_shape=jax.ShapeDtypeStruct((n_items, d), F32),
        compiler_params=_cparams(("arbitrary",)),
    )(tile_expert, first, n_used, n_valid, row_item.reshape(n_tiles, 1, tm), xs, w_gu,
      b_gu.reshape(b_gu.shape[0], n_e, 1, 2 * de), w_down, b_down.reshape(b_down.shape[0], n_e, 1, d))


def _moe_combine(xt, s, mod, w_item, yt, tm=512):
    t, d = xt.shape
    return pl.pallas_call(
        _combine_kernel,
        grid=(t // tm,),
        in_specs=[
            pl.BlockSpec((tm, d), lambda i: (i, 0)),
            pl.BlockSpec((tm, d), lambda i: (i, 0)),
            pl.BlockSpec((tm, 1), lambda i: (i, 0)),
            _tok_mod_spec(5, tm, s, d),
        ],
        out_specs=pl.BlockSpec((tm, d), lambda i: (i, 0)),
        out_shape=jax.ShapeDtypeStruct((t, d), F32),
        compiler_params=_cparams(("arbitrary",)),
    )(xt, yt, w_item, mod)


def _moe_rows(n_tokens, tm_exp=256):
    return (n_tokens // tm_exp + N_EXPERTS) * tm_exp


def _moe_layer(x, xs_buf, mod, ng, router_w, router_b, w_gu, b_gu, w_down, b_down, layer,
               tm_exp=256):
    b, s, d = x.shape
    t = b * s
    xt = x.reshape(t, d)
    assert t % TOP_K == 0
    n_tok = t // TOP_K
    h2, top_e, top_w, rank, counts = _moe_route(xt, n_tok, s, mod, ng, router_w, router_b)
    dest, tile_expert, first, n_used, n_valid, row_item, n_tiles = _moe_plan(
        top_e, rank, counts, tm_exp)
    assert xs_buf.shape[0] == n_tiles * tm_exp
    xs = _moe_dispatch(h2, dest, xs_buf)
    yt = _moe_experts(xs, t, tile_expert, first, n_used, n_valid, row_item,
                      w_gu, b_gu, w_down, b_down, layer, tm_exp)
    w_item = top_w[:, :TOP_K].reshape(t, 1)
    return _moe_combine(xt, s, mod, w_item, yt).reshape(b, s, d), xs


def kernel(x, c, norm_mix_g, norm_ffn_g, w_mod, b_mod, lru_w_in, lru_conv_w, lru_conv_b, lru_w_a, lru_b_a, lru_w_x, lru_b_x, lru_lambda, lru_w_out, attn_w_qkv, attn_q_norm_g, attn_k_norm_g, attn_w_out, router_w, router_b, expert_w_gu, expert_b_gu, expert_w_down, expert_b_down):
    depth = w_mod.shape[0]
    mods = _modulation(c, w_mod, b_mod)
    n_tokens = x.shape[0] * x.shape[1]
    xs_buf = jnp.zeros((_moe_rows(n_tokens), x.shape[2]), F32)
    for i in range(depth):
        mod = mods[i]
        j = i // 2
        if i % 2 == 0:
            x = _rglru_layer(x, mod, norm_mix_g[i], lru_w_in[j], lru_conv_w[j], lru_conv_b[j],
                             lru_w_a[j], lru_b_a[j], lru_w_x[j], lru_b_x[j], lru_lambda[j],
                             lru_w_out[j])
        else:
            x = _moba_layer(x, mod, norm_mix_g[i], attn_w_qkv[j], attn_q_norm_g[j],
                            attn_k_norm_g[j], attn_w_out[j])
        x, xs_buf = _moe_layer(x, xs_buf, mod, norm_ffn_g[i], router_w[i], router_b[i],
                               expert_w_gu, expert_b_gu, expert_w_down, expert_b_down, i)
    return x
```

```python
import functools

import jax
import jax.numpy as jnp
from jax import lax
from jax.experimental import pallas as pl
from jax.experimental.pallas import tpu as pltpu

NORM_EPS = 1e-6
N_MOD = 6
LRU_BLOCKS = 4
CONV_WIDTH = 4
LRU_C = 8.0
ATTN_HEADS = 8
MOBA_BLOCK = 256
MOBA_TOPK = 3
N_EXPERTS = 32
TOP_K = 4
SWIGLU_LIMIT = 7.0
SWIGLU_ALPHA = 1.702

LANES = 128
SUBLANES = 8
MASK_BIAS = -(2.0 ** 100)
LOG2_E = 1.4426950408889634
VMEM_LIMIT = 52 * 1024 * 1024

F32 = jnp.float32
BF16 = jnp.bfloat16
HIGHEST = lax.Precision.HIGHEST


def _cparams(sem):
    return pltpu.CompilerParams(dimension_semantics=sem, vmem_limit_bytes=VMEM_LIMIT)


def _sigmoid(z):
    return 1.0 / (1.0 + jnp.exp(-z))


def _modulated_norm(x, g, sc, sh):
    ms = jnp.mean(x * x, axis=-1, keepdims=True)
    return x * lax.rsqrt(ms + NORM_EPS) * g * (1.0 + sc) + sh


def _mod_kernel(c_ref, w_ref, b_ref, o_ref):
    c = c_ref[...]
    cond = c * _sigmoid(c)
    o_ref[...] = jnp.dot(cond, w_ref[...], preferred_element_type=F32,
                         precision=HIGHEST) + b_ref[...]


def _modulation(c, w_mod, b_mod):
    depth, d, _ = w_mod.shape
    b = c.shape[0]
    rows = -(-b // SUBLANES) * SUBLANES
    c_pad = jnp.pad(c, ((0, rows - b), (0, 0)))
    out = pl.pallas_call(
        _mod_kernel,
        grid=(depth, N_MOD),
        in_specs=[
            pl.BlockSpec((rows, d), lambda i, j: (0, 0)),
            pl.BlockSpec((None, d, d), lambda i, j: (i, 0, j)),
            pl.BlockSpec((None, None, 1, d), lambda i, j: (i, j, 0, 0)),
        ],
        out_specs=pl.BlockSpec((None, None, rows, d), lambda i, j: (i, j, 0, 0)),
        out_shape=jax.ShapeDtypeStruct((depth, N_MOD, rows, d), F32),
        compiler_params=_cparams(("arbitrary", "arbitrary")),
    )(c_pad, w_mod, b_mod.reshape(depth, N_MOD, 1, d))
    return out[:, :, :b].reshape(depth, N_MOD, b, 1, d)


def _mod_spec(j, d):
    return pl.BlockSpec((None, None, 1, d), lambda b, s: (j, b, 0, 0))


def _rglru_kernel(x_ref, sh_ref, sc_ref, g1_ref, ng_ref, win_ref, cw_ref, cb_ref,
                  wa_ref, ba_ref, wx_ref, bx_ref, lam_ref, wout_ref, o_ref,
                  ext_ref, a_ref, u_ref, h_ref):
    ts, d = x_ref.shape
    bw = d // LRU_BLOCKS

    @pl.when(pl.program_id(1) == 0)
    def _():
        ext_ref[0:SUBLANES, :] = jnp.zeros((SUBLANES, d), F32)
        h_ref[...] = jnp.zeros_like(h_ref)

    x = x_ref[...]
    h = _modulated_norm(x, ng_ref[...], sc_ref[...], sh_ref[...])
    gr = jnp.dot(h.astype(BF16), win_ref[...], preferred_element_type=F32)
    gate_branch = gr[:, :d]
    rec = gr[:, d:]

    ext_ref[SUBLANES:, :] = rec
    xc = cb_ref[...] + cw_ref[CONV_WIDTH - 1:CONV_WIDTH, :] * rec
    for k in range(CONV_WIDTH - 1):
        off = SUBLANES - (CONV_WIDTH - 1) + k
        xc = xc + cw_ref[k:k + 1, :] * ext_ref[off:off + ts, :]
    ext_ref[0:SUBLANES, :] = rec[ts - SUBLANES:, :]

    xcb = xc.astype(BF16)
    ra = jnp.concatenate(
        [jnp.dot(xcb[:, g * bw:(g + 1) * bw], wa_ref[g], preferred_element_type=F32)
         for g in range(LRU_BLOCKS)], axis=1) + ba_ref[...]
    rx = jnp.concatenate(
        [jnp.dot(xcb[:, g * bw:(g + 1) * bw], wx_ref[g], preferred_element_type=F32)
         for g in range(LRU_BLOCKS)], axis=1) + bx_ref[...]
    r = _sigmoid(ra)
    ig = _sigmoid(rx)
    z = -lam_ref[...]
    softplus = jnp.maximum(z, 0.0) + jnp.log(1.0 + jnp.exp(-jnp.abs(z)))
    a = jnp.exp(-LRU_C * r * softplus)
    u = jnp.sqrt(1.0 - a * a) * (ig * xc)

    row = lax.broadcasted_iota(jnp.int32, (ts, d), 0) % SUBLANES
    k = 1
    while k < SUBLANES:
        a_prev = pltpu.roll(a, k, 0)
        u_prev = pltpu.roll(u, k, 0)
        m = row >= k
        u = jnp.where(m, a * u_prev + u, u)
        a = jnp.where(m, a * a_prev, a)
        k *= 2
    a_ref[...] = a
    u_ref[...] = u

    def group(j, hc):
        r0 = pl.multiple_of(j * SUBLANES, SUBLANES)
        hs = u_ref[pl.ds(r0, SUBLANES), :] + a_ref[pl.ds(r0, SUBLANES), :] * hc
        u_ref[pl.ds(r0, SUBLANES), :] = hs
        return hs[SUBLANES - 1:SUBLANES, :]

    h_ref[0:1, :] = lax.fori_loop(0, ts // SUBLANES, group, h_ref[0:1, :])

    y = jax.nn.gelu(gate_branch, approximate=True) * u_ref[...]
    out = jnp.dot(y.astype(BF16), wout_ref[...], preferred_element_type=F32)
    o_ref[...] = x + g1_ref[...] * out


def _rglru_layer(x, mod, ng, w_in, conv_w, conv_b, w_a, b_a, w_x, b_x, lam, w_out, ts=256):
    b, s, d = x.shape
    full = lambda shape: pl.BlockSpec(shape, lambda bi, si: (0,) * len(shape))
    row = lambda v: v.reshape(1, d)
    return pl.pallas_call(
        _rglru_kernel,
        grid=(b, s // ts),
        in_specs=[
            pl.BlockSpec((None, ts, d), lambda bi, si: (bi, si, 0)),
            _mod_spec(0, d), _mod_spec(1, d), _mod_spec(2, d),
            full((1, d)), full((d, 2 * d)), full((CONV_WIDTH, d)), full((1, d)),
            full(w_a.shape), full((1, d)), full(w_x.shape), full((1, d)), full((1, d)),
            full((d, d)),
        ],
        out_specs=pl.BlockSpec((None, ts, d), lambda bi, si: (bi, si, 0)),
        out_shape=jax.ShapeDtypeStruct(x.shape, F32),
        scratch_shapes=[
            pltpu.VMEM((ts + SUBLANES, d), F32),
            pltpu.VMEM((ts, d), F32),
            pltpu.VMEM((ts, d), F32),
            pltpu.VMEM((SUBLANES, d), F32),
        ],
        compiler_params=_cparams(("arbitrary", "arbitrary")),
    )(x, mod, mod, mod, row(ng), w_in.astype(BF16), conv_w, row(conv_b),
      w_a.astype(BF16), row(b_a), w_x.astype(BF16), row(b_x), row(lam), w_out.astype(BF16))


def _qkv_kernel(x_ref, sh_ref, sc_ref, ng_ref, w_ref, qg_ref, kg_ref,
                q_ref, kext_ref, v_ref, km_ref):
    ts, d = x_ref.shape
    nh, _, hd = q_ref.shape
    blk = pl.program_id(1)
    h = _modulated_norm(x_ref[...], ng_ref[...], sc_ref[...], sh_ref[...])
    qkv = jnp.dot(h.astype(BF16), w_ref[...], preferred_element_type=F32)
    lane = lax.broadcasted_iota(jnp.int32, (ts, hd), 1)
    onehot = jnp.where(lane == blk, 1.0, 0.0).astype(BF16)
    for hh in range(nh):
        qh = qkv[:, hh * hd:(hh + 1) * hd]
        qn = qh * lax.rsqrt(jnp.mean(qh * qh, axis=-1, keepdims=True) + NORM_EPS)
        q_ref[hh] = (qn * qg_ref[...] * (hd ** -0.5 * LOG2_E)).astype(BF16)
        kh = qkv[:, d + hh * hd:d + (hh + 1) * hd]
        kn = kh * lax.rsqrt(jnp.mean(kh * kh, axis=-1, keepdims=True) + NORM_EPS) * kg_ref[...]
        kext_ref[hh, :, 0:hd] = kn.astype(BF16)
        kext_ref[hh, :, hd:2 * hd] = onehot
        km_ref[:, hh * hd:(hh + 1) * hd] = jnp.mean(kn, axis=0, keepdims=True)
        v_ref[hh] = qkv[:, 2 * d + hh * hd:2 * d + (hh + 1) * hd].astype(BF16)


def _attn_query_block(qi, q_ref, kext_ref, v_ref, km_ref, o_ref, s_ref):
    nhs, tq, hd = q_ref.shape
    nt = (((1,), (1,)), ((), ()))
    nbp = -(-(kext_ref.shape[1] // tq) // SUBLANES) * SUBLANES
    blk = lax.broadcasted_iota(jnp.int32, (nbp, tq), 0)
    blk_f = blk.astype(F32)
    past = blk < qi
    rr = lax.broadcasted_iota(jnp.int32, (tq, tq), 0)
    cc = lax.broadcasted_iota(jnp.int32, (tq, tq), 1)
    for hh in range(nhs):
        q = q_ref[hh]
        gate = lax.dot_general(km_ref[hh], q.astype(F32), nt, precision=HIGHEST,
                               preferred_element_type=F32)[:nbp]
        g = jnp.where(past, gate, -jnp.inf)
        sel = blk == qi
        for _ in range(min(MOBA_TOPK, qi)):
            m = jnp.max(g, axis=0, keepdims=True)
            idx = jnp.min(jnp.where(g == m, blk_f, float(nbp)), axis=0, keepdims=True)
            hit = blk_f == idx
            sel = sel | (hit & past)
            g = jnp.where(hit, -jnp.inf, g)
        bias_t = jnp.where(sel, 0.0, MASK_BIAS)
        bias = jnp.concatenate([bias_t, jnp.zeros((LANES - nbp, tq), F32)], axis=0).T
        q_ext = jnp.concatenate([q, bias.astype(BF16)], axis=1)

        mx = None
        for j in range(qi + 1):
            sj = lax.dot_general(q_ext, kext_ref[hh, j * tq:(j + 1) * tq, :], nt,
                                 preferred_element_type=F32)
            if j == qi:
                sj = jnp.where(cc <= rr, sj, MASK_BIAS)
            s_ref[hh, j * tq:(j + 1) * tq, :] = sj
            mx = sj if mx is None else jnp.maximum(mx, sj)
        m = jnp.max(mx, axis=-1, keepdims=True)

        m = jnp.broadcast_to(m, (tq, tq))
        psum, acc = None, None
        for j in range(qi + 1):
            p = jnp.exp2(s_ref[hh, j * tq:(j + 1) * tq, :] - m)
            pv = jnp.dot(p.astype(BF16), v_ref[hh, j * tq:(j + 1) * tq, :],
                         preferred_element_type=F32)
            psum = p if psum is None else psum + p
            acc = pv if acc is None else acc + pv
        l = jnp.sum(psum, axis=-1, keepdims=True)
        o_ref[hh] = (acc / l).astype(BF16)


def _attn_kernel(q_ref, kext_ref, v_ref, km_ref, o_ref, s_ref):
    nb = kext_ref.shape[1] // q_ref.shape[1]
    for qi in range(nb):
        @pl.when(pl.program_id(2) == qi)
        def _(qi=qi):
            _attn_query_block(qi, q_ref, kext_ref, v_ref, km_ref, o_ref, s_ref)


def _attn_out_kernel(o_ref, x_ref, g1_ref, w_ref, y_ref):
    nh = o_ref.shape[0]
    o = jnp.concatenate([o_ref[hh] for hh in range(nh)], axis=1)
    out = jnp.dot(o, w_ref[...], preferred_element_type=F32)
    y_ref[...] = x_ref[...] + g1_ref[...] * out


def _moba_layer(x, mod, ng, w_qkv, q_g, k_g, w_out, ts_out=512, heads_per_step=1):
    b, s, d = x.shape
    nh = ATTN_HEADS
    hd = d // nh
    tq = MOBA_BLOCK
    nb = s // tq
    assert s % tq == 0 and nb <= LANES and hd == LANES
    full = lambda shape: pl.BlockSpec(shape, lambda bi, si: (0,) * len(shape))

    q, kext, v, km = pl.pallas_call(
        _qkv_kernel,
        grid=(b, nb),
        in_specs=[
            pl.BlockSpec((None, tq, d), lambda bi, si: (bi, si, 0)),
            _mod_spec(0, d), _mod_spec(1, d),
            full((1, d)), full((d, 3 * d)), full((1, hd)), full((1, hd)),
        ],
        out_specs=[
            pl.BlockSpec((None, nh, tq, hd), lambda bi, si: (bi, 0, si, 0)),
            pl.BlockSpec((None, nh, tq, 2 * hd), lambda bi, si: (bi, 0, si, 0)),
            pl.BlockSpec((None, nh, tq, hd), lambda bi, si: (bi, 0, si, 0)),
            pl.BlockSpec((None, None, 1, d), lambda bi, si: (bi, si, 0, 0)),
        ],
        out_shape=[
            jax.ShapeDtypeStruct((b, nh, s, hd), BF16),
            jax.ShapeDtypeStruct((b, nh, s, 2 * hd), BF16),
            jax.ShapeDtypeStruct((b, nh, s, hd), BF16),
            jax.ShapeDtypeStruct((b, nb, 1, d), F32),
        ],
        compiler_params=_cparams(("arbitrary", "arbitrary")),
    )(x, mod, mod, ng.reshape(1, d), w_qkv.astype(BF16), q_g.reshape(1, hd), k_g.reshape(1, hd))

    km = km.reshape(b, nb, nh, hd).transpose(0, 2, 1, 3)
    km = jnp.pad(km, ((0, 0), (0, 0), (0, LANES - nb), (0, 0)))

    o = pl.pallas_call(
        _attn_kernel,
        grid=(b, nh // heads_per_step, nb),
        in_specs=[
            pl.BlockSpec((None, heads_per_step, tq, hd), lambda bi, hi, qi: (bi, hi, qi, 0)),
            pl.BlockSpec((None, heads_per_step, s, 2 * hd), lambda bi, hi, qi: (bi, hi, 0, 0)),
            pl.BlockSpec((None, heads_per_step, s, hd), lambda bi, hi, qi: (bi, hi, 0, 0)),
            pl.BlockSpec((None, heads_per_step, LANES, hd), lambda bi, hi, qi: (bi, hi, 0, 0)),
        ],
        out_specs=pl.BlockSpec((None, heads_per_step, tq, hd),
                               lambda bi, hi, qi: (bi, hi, qi, 0)),
        out_shape=jax.ShapeDtypeStruct((b, nh, s, hd), BF16),
        scratch_shapes=[pltpu.VMEM((heads_per_step, s, tq), F32)],
        compiler_params=_cparams(("arbitrary", "arbitrary", "arbitrary")),
    )(q, kext, v, km)

    return pl.pallas_call(
        _attn_out_kernel,
        grid=(b, s // ts_out),
        in_specs=[
            pl.BlockSpec((None, nh, ts_out, hd), lambda bi, si: (bi, 0, si, 0)),
            pl.BlockSpec((None, ts_out, d), lambda bi, si: (bi, si, 0)),
            _mod_spec(2, d),
            full((d, d)),
        ],
        out_specs=pl.BlockSpec((None, ts_out, d), lambda bi, si: (bi, si, 0)),
        out_shape=jax.ShapeDtypeStruct(x.shape, F32),
        compiler_params=_cparams(("arbitrary", "arbitrary")),
    )(o, x, mod, w_out.astype(BF16))


def _route_kernel(x_ref, sh_ref, sc_ref, ng_ref, rw_ref, rb_ref,
                  h_ref, e_ref, w_ref, r_ref, cnt_ref, run_ref):
    tm = x_ref.shape[0]

    @pl.when(pl.program_id(0) == 0)
    def _():
        run_ref[...] = jnp.zeros_like(run_ref)

    h = _modulated_norm(x_ref[...], ng_ref[...], sc_ref[...], sh_ref[...])
    h_ref[...] = h
    logits = jnp.dot(h, rw_ref[...], preferred_element_type=F32, precision=HIGHEST) + rb_ref[...]
    lane = lax.broadcasted_iota(jnp.int32, (tm, LANES), 1)
    lane_f = lane.astype(F32)
    lg = jnp.where(lane < N_EXPERTS, logits, -jnp.inf)
    tops, hits = [], []
    for _ in range(TOP_K):
        m = jnp.max(lg, axis=-1, keepdims=True)
        idx = jnp.min(jnp.where(lg == m, lane_f, float(LANES)), axis=-1, keepdims=True)
        hit = lane_f == idx
        tops.append((m, idx))
        hits.append(hit)
        lg = jnp.where(hit, -jnp.inf, lg)
    exps = [jnp.exp(m - tops[0][0]) for m, _ in tops]
    denom = exps[0]
    for ex in exps[1:]:
        denom = denom + ex

    member = jnp.zeros((tm, LANES), F32)
    for hit in hits:
        member = member + jnp.where(hit, 1.0, 0.0)
    rr = lax.broadcasted_iota(jnp.int32, (tm, tm), 0)
    cc = lax.broadcasted_iota(jnp.int32, (tm, tm), 1)
    earlier = jnp.where(cc < rr, 1.0, 0.0).astype(BF16)
    before = jnp.dot(earlier, member.astype(BF16), preferred_element_type=F32) + run_ref[...]

    e_out = jnp.zeros((tm, LANES), F32)
    w_out = jnp.zeros((tm, LANES), F32)
    r_out = jnp.zeros((tm, LANES), F32)
    for k in range(TOP_K):
        rank = jnp.sum(jnp.where(hits[k], before, 0.0), axis=-1, keepdims=True)
        e_out = jnp.where(lane == k, tops[k][1], e_out)
        w_out = jnp.where(lane == k, exps[k] / denom, w_out)
        r_out = jnp.where(lane == k, rank, r_out)
    e_ref[...] = e_out.astype(jnp.int32)
    w_ref[...] = w_out
    r_ref[...] = r_out.astype(jnp.int32)
    run = run_ref[...] + jnp.sum(member, axis=0, keepdims=True)
    run_ref[...] = run
    cnt_ref[...] = run.astype(jnp.int32)


def _dispatch_kernel(start_ref, e_ref, r_ref, h_ref, xs_in_hbm, xs_hbm, item_ref, sem):
    del xs_in_hbm
    tm = h_ref.shape[0]
    n_tiles, tile_rows = item_ref.shape
    shift = tile_rows.bit_length() - 1
    assert tile_rows == 1 << shift
    unroll = 16

    @pl.when(pl.program_id(0) == 0)
    def _():
        def clear(c, _):
            tile = c // (tile_rows // unroll)
            col = (c % (tile_rows // unroll)) * unroll
            for u in range(unroll):
                item_ref[tile, col + u] = 0
            return 0
        lax.fori_loop(0, n_tiles * (tile_rows // unroll), clear, 0)

    base = pl.program_id(0) * tm * TOP_K

    def issue(t, _):
        for k in range(TOP_K):
            j = t * TOP_K + k
            row = start_ref[e_ref[0, j]] + r_ref[0, j]
            item_ref[lax.shift_right_logical(row, shift), row & (tile_rows - 1)] = base + j
            pltpu.make_async_copy(h_ref.at[pl.ds(t, 1), :], xs_hbm.at[pl.ds(row, 1), :],
                                  sem).start()
        return 0

    lax.fori_loop(0, tm, issue, 0)
    for _ in range(TOP_K):
        pltpu.make_async_copy(h_ref, xs_hbm.at[pl.ds(0, tm), :], sem).wait()


def _wait_rows(n, src_ref, dst_hbm, sem):
    p = 1
    while p <= src_ref.shape[0]:
        @pl.when((n & p) != 0)
        def _(p=p):
            pltpu.make_async_copy(src_ref.at[pl.ds(0, p), :], dst_hbm.at[pl.ds(0, p), :], sem).wait()
        p *= 2


def _expert_kernel(be_ref, first_ref, nu_ref, nv_ref, item_ref, xs_ref, wgu_ref, bgu_ref, wd_ref,
                   bd_ref, yt_hbm, wgu_bf, wd_bf, ybuf, sems):
    i = pl.program_id(0)
    de = wd_ref.shape[0]
    slot = i % 2
    nv = nv_ref[i]

    @pl.when(nv > 0)
    def _():
        @pl.when(first_ref[i] == 1)
        def _():
            wgu_bf[...] = wgu_ref[...].astype(BF16)
            wd_bf[...] = wd_ref[...].astype(BF16)

        gu = jnp.dot(xs_ref[...].astype(BF16), wgu_bf[...], preferred_element_type=F32) + bgu_ref[...]
        gate = jnp.minimum(gu[:, :de], SWIGLU_LIMIT)
        up = jnp.clip(gu[:, de:], -SWIGLU_LIMIT, SWIGLU_LIMIT)
        act = gate * _sigmoid(SWIGLU_ALPHA * gate) * (up + 1.0)
        ybuf[slot] = jnp.dot(act.astype(BF16), wd_bf[...], preferred_element_type=F32) + bd_ref[...]

        def issue(r, _):
            pltpu.make_async_copy(ybuf.at[slot, pl.ds(r, 1), :],
                                  yt_hbm.at[pl.ds(item_ref[0, r], 1), :], sems.at[slot]).start()
            return 0

        lax.fori_loop(0, nv, issue, 0)

    @pl.when(i > 0)
    def _():
        _wait_rows(nv_ref[jnp.maximum(i - 1, 0)], ybuf.at[1 - slot], yt_hbm, sems.at[1 - slot])

    @pl.when(i == pl.num_programs(0) - 1)
    def _():
        _wait_rows(nv, ybuf.at[slot], yt_hbm, sems.at[slot])


def _combine_kernel(x_ref, yt_ref, w_ref, g2_ref, o_ref):
    o_ref[...] = x_ref[...] + g2_ref[...] * (w_ref[...] * yt_ref[...])


def _tok_mod_spec(j, tm, s, d):
    return pl.BlockSpec((None, None, 1, d), lambda i: (j, (i * tm) // s, 0, 0))


def _moe_route(xt, t, s, mod, ng, router_w, router_b, tm=256):
    d = xt.shape[1]
    assert t % tm == 0
    rw = jnp.pad(router_w, ((0, 0), (0, LANES - N_EXPERTS)))
    rb = jnp.pad(router_b, (0, LANES - N_EXPERTS)).reshape(1, LANES)
    return pl.pallas_call(
        _route_kernel,
        grid=(t // tm,),
        in_specs=[
            pl.BlockSpec((tm, d), lambda i: (i, 0)),
            _tok_mod_spec(3, tm, s, d), _tok_mod_spec(4, tm, s, d),
            pl.BlockSpec((1, d), lambda i: (0, 0)),
            pl.BlockSpec((d, LANES), lambda i: (0, 0)),
            pl.BlockSpec((1, LANES), lambda i: (0, 0)),
        ],
        out_specs=[
            pl.BlockSpec((tm, d), lambda i: (i, 0)),
            pl.BlockSpec((tm, LANES), lambda i: (i, 0)),
            pl.BlockSpec((tm, LANES), lambda i: (i, 0)),
            pl.BlockSpec((tm, LANES), lambda i: (i, 0)),
            pl.BlockSpec((1, LANES), lambda i: (0, 0)),
        ],
        out_shape=[
            jax.ShapeDtypeStruct((t, d), F32),
            jax.ShapeDtypeStruct((t, LANES), jnp.int32),
            jax.ShapeDtypeStruct((t, LANES), F32),
            jax.ShapeDtypeStruct((t, LANES), jnp.int32),
            jax.ShapeDtypeStruct((1, LANES), jnp.int32),
        ],
        scratch_shapes=[pltpu.VMEM((1, LANES), F32)],
        compiler_params=_cparams(("arbitrary",)),
    )(xt, mod, mod, ng.reshape(1, d), rw, rb)


def _moe_plan(t, counts, tm):
    n_e = N_EXPERTS
    counts = counts[0, :n_e]
    padded = (counts + tm - 1) // tm * tm
    pad_end = jnp.cumsum(padded)
    pad_start = pad_end - padded
    n_tiles = (t * TOP_K) // tm + n_e
    tile_start = jnp.arange(n_tiles, dtype=jnp.int32) * tm
    n_used = (pad_end[-1] // tm).astype(jnp.int32)
    tile_expert = jnp.minimum(jnp.searchsorted(pad_end, tile_start, side='right'), n_e - 1)
    tile_expert = jnp.where(jnp.arange(n_tiles) < n_used, tile_expert,
                            tile_expert[jnp.maximum(n_used - 1, 0)]).astype(jnp.int32)
    first = jnp.concatenate([jnp.ones((1,), jnp.int32),
                             (tile_expert[1:] != tile_expert[:-1]).astype(jnp.int32)])
    seg_end = (pad_start + counts)[tile_expert]
    n_valid = jnp.where(jnp.arange(n_tiles) < n_used,
                        jnp.clip(seg_end - tile_start, 0, tm), 0).astype(jnp.int32)
    return pad_start.astype(jnp.int32), tile_expert, first, n_used.reshape(1), n_valid, n_tiles


def _moe_dispatch(h2, pad_start, top_e, rank, xs_buf, tm_exp, tm=256):
    t, d = h2.shape
    n_rows = xs_buf.shape[0]
    n_tiles = n_rows // tm_exp
    flat = lambda a: a[:, :TOP_K].reshape(t // tm, 1, tm * TOP_K)
    item_spec = pl.BlockSpec((None, 1, tm * TOP_K), lambda i: (i, 0, 0), memory_space=pltpu.SMEM)
    return pl.pallas_call(
        _dispatch_kernel,
        grid=(t // tm,),
        in_specs=[
            pl.BlockSpec(memory_space=pltpu.SMEM),
            item_spec, item_spec,
            pl.BlockSpec((tm, d), lambda i: (i, 0)),
            pl.BlockSpec(memory_space=pl.ANY),
        ],
        out_specs=[pl.BlockSpec(memory_space=pl.ANY), pl.BlockSpec(memory_space=pltpu.SMEM)],
        out_shape=[jax.ShapeDtypeStruct((n_rows, d), F32),
                   jax.ShapeDtypeStruct((n_tiles, tm_exp), jnp.int32)],
        scratch_shapes=[pltpu.SemaphoreType.DMA],
        input_output_aliases={4: 0},
        compiler_params=_cparams(("arbitrary",)),
    )(pad_start, flat(top_e), flat(rank), h2, xs_buf)


def _moe_experts(xs, n_items, tile_expert, first, n_used, n_valid, row_item,
                 w_gu, b_gu, w_down, b_down, layer, tm):
    n_rows, d = xs.shape
    n_e = N_EXPERTS
    de = w_down.shape[2]
    n_tiles = n_rows // tm
    tile_idx = lambda i, be, fi, nu, nv: (jnp.maximum(jnp.minimum(i, nu[0] - 1), 0), 0)
    w_idx = lambda i, be, fi, nu, nv: (layer, be[i], 0, 0)
    return pl.pallas_call(
        _expert_kernel,
        grid_spec=pltpu.PrefetchScalarGridSpec(
            num_scalar_prefetch=4,
            grid=(n_tiles,),
            in_specs=[
                pl.BlockSpec((None, 1, tm), lambda i, be, fi, nu, nv: (i, 0, 0),
                             memory_space=pltpu.SMEM),
                pl.BlockSpec((tm, d), tile_idx),
                pl.BlockSpec((None, None, d, 2 * de), w_idx),
                pl.BlockSpec((None, None, 1, 2 * de), w_idx),
                pl.BlockSpec((None, None, de, d), w_idx),
                pl.BlockSpec((None, None, 1, d), w_idx),
            ],
            out_specs=pl.BlockSpec(memory_space=pl.ANY),
            scratch_shapes=[pltpu.VMEM((d, 2 * de), BF16), pltpu.VMEM((de, d), BF16),
                            pltpu.VMEM((2, tm, d), F32), pltpu.SemaphoreType.DMA((2,))],
        ),
        out_shape=jax.ShapeDtypeStruct((n_items, d), F32),
        compiler_params=_cparams(("arbitrary",)),
    )(tile_expert, first, n_used, n_valid, row_item.reshape(n_tiles, 1, tm), xs, w_gu,
      b_gu.reshape(b_gu.shape[0], n_e, 1, 2 * de), w_down, b_down.reshape(b_down.shape[0], n_e, 1, d))


def _moe_combine(xt, s, mod, w_item, yt, tm=512):
    t, d = xt.shape
    return pl.pallas_call(
        _combine_kernel,
        grid=(t // tm,),
        in_specs=[
            pl.BlockSpec((tm, d), lambda i: (i, 0)),
            pl.BlockSpec((tm, d), lambda i: (i, 0)),
            pl.BlockSpec((tm, 1), lambda i: (i, 0)),
            _tok_mod_spec(5, tm, s, d),
        ],
        out_specs=pl.BlockSpec((tm, d), lambda i: (i, 0)),
        out_shape=jax.ShapeDtypeStruct((t, d), F32),
        compiler_params=_cparams(("arbitrary",)),
    )(xt, yt, w_item, mod)


def _moe_rows(n_tokens, tm_exp=256):
    return (n_tokens // tm_exp + N_EXPERTS) * tm_exp


def _moe_layer(x, xs_buf, mod, ng, router_w, router_b, w_gu, b_gu, w_down, b_down, layer,
               tm_exp=256):
    b, s, d = x.shape
    t = b * s
    xt = x.reshape(t, d)
    assert t % TOP_K == 0
    n_tok = t // TOP_K
    h2, top_e, top_w, rank, counts = _moe_route(xt, n_tok, s, mod, ng, router_w, router_b)
    pad_start, tile_expert, first, n_used, n_valid, n_tiles = _moe_plan(n_tok, counts, tm_exp)
    assert xs_buf.shape[0] == n_tiles * tm_exp
    xs, row_item = _moe_dispatch(h2, pad_start, top_e, rank, xs_buf, tm_exp)
    yt = _moe_experts(xs, t, tile_expert, first, n_used, n_valid, row_item,
                      w_gu, b_gu, w_down, b_down, layer, tm_exp)
    w_item = top_w[:, :TOP_K].reshape(t, 1)
    return _moe_combine(xt, s, mod, w_item, yt).reshape(b, s, d), xs


def kernel(x, c, norm_mix_g, norm_ffn_g, w_mod, b_mod, lru_w_in, lru_conv_w, lru_conv_b, lru_w_a, lru_b_a, lru_w_x, lru_b_x, lru_lambda, lru_w_out, attn_w_qkv, attn_q_norm_g, attn_k_norm_g, attn_w_out, router_w, router_b, expert_w_gu, expert_b_gu, expert_w_down, expert_b_down):
    depth = w_mod.shape[0]
    mods = _modulation(c, w_mod, b_mod)
    n_tokens = x.shape[0] * x.shape[1]
    xs_buf = jnp.zeros((_moe_rows(n_tokens), x.shape[2]), F32)
    for i in range(depth):
        mod = mods[i]
        j = i // 2
        if i % 2 == 0:
            x = _rglru_layer(x, mod, norm_mix_g[i], lru_w_in[j], lru_conv_w[j], lru_conv_b[j],
                             lru_w_a[j], lru_b_a[j], lru_w_x[j], lru_b_x[j], lru_lambda[j],
                             lru_w_out[j])
        else:
            x = _moba_layer(x, mod, norm_mix_g[i], attn_w_qkv[j], attn_q_norm_g[j],
                            attn_k_norm_g[j], attn_w_out[j])
        x, xs_buf = _moe_layer(x, xs_buf, mod, norm_ffn_g[i], router_w[i], router_b[i],
                               expert_w_gu, expert_b_gu, expert_w_down, expert_b_down, i)
    return x
```

```python
import functools

import jax
import jax.numpy as jnp
from jax import lax
from jax.experimental import pallas as pl
from jax.experimental.pallas import tpu as pltpu

NORM_EPS = 1e-6
N_MOD = 6
LRU_BLOCKS = 4
CONV_WIDTH = 4
LRU_C = 8.0
ATTN_HEADS = 8
MOBA_BLOCK = 256
MOBA_TOPK = 3
N_EXPERTS = 32
TOP_K = 4
SWIGLU_LIMIT = 7.0
SWIGLU_ALPHA = 1.702

LANES = 128
SUBLANES = 8
MASK_BIAS = -(2.0 ** 100)
LOG2_E = 1.4426950408889634
VMEM_LIMIT = 52 * 1024 * 1024

F32 = jnp.float32
BF16 = jnp.bfloat16
HIGHEST = lax.Precision.HIGHEST


def _cparams(sem):
    return pltpu.CompilerParams(dimension_semantics=sem, vmem_limit_bytes=VMEM_LIMIT)


def _sigmoid(z):
    return 1.0 / (1.0 + jnp.exp(-z))


def _modulated_norm(x, g, sc, sh):
    ms = jnp.mean(x * x, axis=-1, keepdims=True)
    return x * lax.rsqrt(ms + NORM_EPS) * g * (1.0 + sc) + sh


def _mod_kernel(c_ref, w_ref, b_ref, o_ref):
    c = c_ref[...]
    cond = c * _sigmoid(c)
    o_ref[...] = jnp.dot(cond, w_ref[...], preferred_element_type=F32,
                         precision=HIGHEST) + b_ref[...]


def _modulation(c, w_mod, b_mod):
    depth, d, _ = w_mod.shape
    b = c.shape[0]
    rows = -(-b // SUBLANES) * SUBLANES
    c_pad = jnp.pad(c, ((0, rows - b), (0, 0)))
    out = pl.pallas_call(
        _mod_kernel,
        grid=(depth, N_MOD),
        in_specs=[
            pl.BlockSpec((rows, d), lambda i, j: (0, 0)),
            pl.BlockSpec((None, d, d), lambda i, j: (i, 0, j)),
            pl.BlockSpec((None, None, 1, d), lambda i, j: (i, j, 0, 0)),
        ],
        out_specs=pl.BlockSpec((None, None, rows, d), lambda i, j: (i, j, 0, 0)),
        out_shape=jax.ShapeDtypeStruct((depth, N_MOD, rows, d), F32),
        compiler_params=_cparams(("arbitrary", "arbitrary")),
    )(c_pad, w_mod, b_mod.reshape(depth, N_MOD, 1, d))
    return out[:, :, :b].reshape(depth, N_MOD, b, 1, d)


def _mod_spec(j, d):
    return pl.BlockSpec((None, None, 1, d), lambda b, s: (j, b, 0, 0))


def _rglru_kernel(x_ref, sh_ref, sc_ref, g1_ref, ng_ref, win_ref, cw_ref, cb_ref,
                  wa_ref, ba_ref, wx_ref, bx_ref, lam_ref, wout_ref, o_ref,
                  ext_ref, a_ref, u_ref, h_ref):
    ts, d = x_ref.shape
    bw = d // LRU_BLOCKS

    @pl.when(pl.program_id(1) == 0)
    def _():
        ext_ref[0:SUBLANES, :] = jnp.zeros((SUBLANES, d), F32)
        h_ref[...] = jnp.zeros_like(h_ref)

    x = x_ref[...]
    h = _modulated_norm(x, ng_ref[...], sc_ref[...], sh_ref[...])
    gr = jnp.dot(h.astype(BF16), win_ref[...], preferred_element_type=F32)
    gate_branch = gr[:, :d]
    rec = gr[:, d:]

    ext_ref[SUBLANES:, :] = rec
    xc = cb_ref[...] + cw_ref[CONV_WIDTH - 1:CONV_WIDTH, :] * rec
    for k in range(CONV_WIDTH - 1):
        off = SUBLANES - (CONV_WIDTH - 1) + k
        xc = xc + cw_ref[k:k + 1, :] * ext_ref[off:off + ts, :]
    ext_ref[0:SUBLANES, :] = rec[ts - SUBLANES:, :]

    xcb = xc.astype(BF16)
    ra = jnp.concatenate(
        [jnp.dot(xcb[:, g * bw:(g + 1) * bw], wa_ref[g], preferred_element_type=F32)
         for g in range(LRU_BLOCKS)], axis=1) + ba_ref[...]
    rx = jnp.concatenate(
        [jnp.dot(xcb[:, g * bw:(g + 1) * bw], wx_ref[g], preferred_element_type=F32)
         for g in range(LRU_BLOCKS)], axis=1) + bx_ref[...]
    r = _sigmoid(ra)
    ig = _sigmoid(rx)
    z = -lam_ref[...]
    softplus = jnp.maximum(z, 0.0) + jnp.log(1.0 + jnp.exp(-jnp.abs(z)))
    a = jnp.exp(-LRU_C * r * softplus)
    u = jnp.sqrt(1.0 - a * a) * (ig * xc)

    row = lax.broadcasted_iota(jnp.int32, (ts, d), 0) % SUBLANES
    k = 1
    while k < SUBLANES:
        a_prev = pltpu.roll(a, k, 0)
        u_prev = pltpu.roll(u, k, 0)
        m = row >= k
        u = jnp.where(m, a * u_prev + u, u)
        a = jnp.where(m, a * a_prev, a)
        k *= 2
    a_ref[...] = a
    u_ref[...] = u

    def group(j, hc):
        r0 = pl.multiple_of(j * SUBLANES, SUBLANES)
        hs = u_ref[pl.ds(r0, SUBLANES), :] + a_ref[pl.ds(r0, SUBLANES), :] * hc
        u_ref[pl.ds(r0, SUBLANES), :] = hs
        return hs[SUBLANES - 1:SUBLANES, :]

    h_ref[0:1, :] = lax.fori_loop(0, ts // SUBLANES, group, h_ref[0:1, :])

    y = jax.nn.gelu(gate_branch, approximate=True) * u_ref[...]
    out = jnp.dot(y.astype(BF16), wout_ref[...], preferred_element_type=F32)
    o_ref[...] = x + g1_ref[...] * out


def _rglru_layer(x, mod, ng, w_in, conv_w, conv_b, w_a, b_a, w_x, b_x, lam, w_out, ts=256):
    b, s, d = x.shape
    full = lambda shape: pl.BlockSpec(shape, lambda bi, si: (0,) * len(shape))
    row = lambda v: v.reshape(1, d)
    return pl.pallas_call(
        _rglru_kernel,
        grid=(b, s // ts),
        in_specs=[
            pl.BlockSpec((None, ts, d), lambda bi, si: (bi, si, 0)),
            _mod_spec(0, d), _mod_spec(1, d), _mod_spec(2, d),
            full((1, d)), full((d, 2 * d)), full((CONV_WIDTH, d)), full((1, d)),
            full(w_a.shape), full((1, d)), full(w_x.shape), full((1, d)), full((1, d)),
            full((d, d)),
        ],
        out_specs=pl.BlockSpec((None, ts, d), lambda bi, si: (bi, si, 0)),
        out_shape=jax.ShapeDtypeStruct(x.shape, F32),
        scratch_shapes=[
            pltpu.VMEM((ts + SUBLANES, d), F32),
            pltpu.VMEM((ts, d), F32),
            pltpu.VMEM((ts, d), F32),
            pltpu.VMEM((SUBLANES, d), F32),
        ],
        compiler_params=_cparams(("arbitrary", "arbitrary")),
    )(x, mod, mod, mod, row(ng), w_in.astype(BF16), conv_w, row(conv_b),
      w_a.astype(BF16), row(b_a), w_x.astype(BF16), row(b_x), row(lam), w_out.astype(BF16))


def _qkv_kernel(x_ref, sh_ref, sc_ref, ng_ref, w_ref, qg_ref, kg_ref,
                q_ref, kext_ref, v_ref, km_ref):
    ts, d = x_ref.shape
    nh, _, hd = q_ref.shape
    blk = pl.program_id(1)
    h = _modulated_norm(x_ref[...], ng_ref[...], sc_ref[...], sh_ref[...])
    qkv = jnp.dot(h.astype(BF16), w_ref[...], preferred_element_type=F32)
    lane = lax.broadcasted_iota(jnp.int32, (ts, hd), 1)
    onehot = jnp.where(lane == blk, 1.0, 0.0).astype(BF16)
    for hh in range(nh):
        qh = qkv[:, hh * hd:(hh + 1) * hd]
        qn = qh * lax.rsqrt(jnp.mean(qh * qh, axis=-1, keepdims=True) + NORM_EPS)
        q_ref[hh] = (qn * qg_ref[...] * (hd ** -0.5 * LOG2_E)).astype(BF16)
        kh = qkv[:, d + hh * hd:d + (hh + 1) * hd]
        kn = kh * lax.rsqrt(jnp.mean(kh * kh, axis=-1, keepdims=True) + NORM_EPS) * kg_ref[...]
        kext_ref[hh, :, 0:hd] = kn.astype(BF16)
        kext_ref[hh, :, hd:2 * hd] = onehot
        km_ref[:, hh * hd:(hh + 1) * hd] = jnp.mean(kn, axis=0, keepdims=True)
        v_ref[hh] = qkv[:, 2 * d + hh * hd:2 * d + (hh + 1) * hd].astype(BF16)


def _attn_query_block(qi, q_ref, kext_ref, v_ref, km_ref, o_ref, s_ref):
    nhs, tq, hd = q_ref.shape
    nt = (((1,), (1,)), ((), ()))
    nbp = -(-(kext_ref.shape[1] // tq) // SUBLANES) * SUBLANES
    blk = lax.broadcasted_iota(jnp.int32, (nbp, tq), 0)
    blk_f = blk.astype(F32)
    past = blk < qi
    rr = lax.broadcasted_iota(jnp.int32, (tq, tq), 0)
    cc = lax.broadcasted_iota(jnp.int32, (tq, tq), 1)
    for hh in range(nhs):
        q = q_ref[hh]
        gate = lax.dot_general(km_ref[hh], q.astype(F32), nt, precision=HIGHEST,
                               preferred_element_type=F32)[:nbp]
        g = jnp.where(past, gate, -jnp.inf)
        sel = blk == qi
        for _ in range(min(MOBA_TOPK, qi)):
            m = jnp.max(g, axis=0, keepdims=True)
            idx = jnp.min(jnp.where(g == m, blk_f, float(nbp)), axis=0, keepdims=True)
            hit = blk_f == idx
            sel = sel | (hit & past)
            g = jnp.where(hit, -jnp.inf, g)
        bias_t = jnp.where(sel, 0.0, MASK_BIAS)
        bias = jnp.concatenate([bias_t, jnp.zeros((LANES - nbp, tq), F32)], axis=0).T
        q_ext = jnp.concatenate([q, bias.astype(BF16)], axis=1)

        mx = None
        for j in range(qi + 1):
            sj = lax.dot_general(q_ext, kext_ref[hh, j * tq:(j + 1) * tq, :], nt,
                                 preferred_element_type=F32)
            if j == qi:
                sj = jnp.where(cc <= rr, sj, MASK_BIAS)
            s_ref[hh, j * tq:(j + 1) * tq, :] = sj
            mx = sj if mx is None else jnp.maximum(mx, sj)
        m = jnp.max(mx, axis=-1, keepdims=True)

        m = jnp.broadcast_to(m, (tq, tq))
        psum, acc = None, None
        for j in range(qi + 1):
            p = jnp.exp2(s_ref[hh, j * tq:(j + 1) * tq, :] - m)
            pv = jnp.dot(p.astype(BF16), v_ref[hh, j * tq:(j + 1) * tq, :],
                         preferred_element_type=F32)
            psum = p if psum is None else psum + p
            acc = pv if acc is None else acc + pv
        l = jnp.sum(psum, axis=-1, keepdims=True)
        o_ref[hh] = (acc / l).astype(BF16)


def _attn_kernel(q_ref, kext_ref, v_ref, km_ref, o_ref, s_ref):
    nb = kext_ref.shape[1] // q_ref.shape[1]
    for qi in range(nb):
        @pl.when(pl.program_id(2) == qi)
        def _(qi=qi):
            _attn_query_block(qi, q_ref, kext_ref, v_ref, km_ref, o_ref, s_ref)


def _attn_out_kernel(o_ref, x_ref, g1_ref, w_ref, y_ref):
    nh = o_ref.shape[0]
    o = jnp.concatenate([o_ref[hh] for hh in range(nh)], axis=1)
    out = jnp.dot(o, w_ref[...], preferred_element_type=F32)
    y_ref[...] = x_ref[...] + g1_ref[...] * out


def _moba_layer(x, mod, ng, w_qkv, q_g, k_g, w_out, ts_out=512, heads_per_step=1):
    b, s, d = x.shape
    nh = ATTN_HEADS
    hd = d // nh
    tq = MOBA_BLOCK
    nb = s // tq
    assert s % tq == 0 and nb <= LANES and hd == LANES
    full = lambda shape: pl.BlockSpec(shape, lambda bi, si: (0,) * len(shape))

    q, kext, v, km = pl.pallas_call(
        _qkv_kernel,
        grid=(b, nb),
        in_specs=[
            pl.BlockSpec((None, tq, d), lambda bi, si: (bi, si, 0)),
            _mod_spec(0, d), _mod_spec(1, d),
            full((1, d)), full((d, 3 * d)), full((1, hd)), full((1, hd)),
        ],
        out_specs=[
            pl.BlockSpec((None, nh, tq, hd), lambda bi, si: (bi, 0, si, 0)),
            pl.BlockSpec((None, nh, tq, 2 * hd), lambda bi, si: (bi, 0, si, 0)),
            pl.BlockSpec((None, nh, tq, hd), lambda bi, si: (bi, 0, si, 0)),
            pl.BlockSpec((None, None, 1, d), lambda bi, si: (bi, si, 0, 0)),
        ],
        out_shape=[
            jax.ShapeDtypeStruct((b, nh, s, hd), BF16),
            jax.ShapeDtypeStruct((b, nh, s, 2 * hd), BF16),
            jax.ShapeDtypeStruct((b, nh, s, hd), BF16),
            jax.ShapeDtypeStruct((b, nb, 1, d), F32),
        ],
        compiler_params=_cparams(("arbitrary", "arbitrary")),
    )(x, mod, mod, ng.reshape(1, d), w_qkv.astype(BF16), q_g.reshape(1, hd), k_g.reshape(1, hd))

    km = km.reshape(b, nb, nh, hd).transpose(0, 2, 1, 3)
    km = jnp.pad(km, ((0, 0), (0, 0), (0, LANES - nb), (0, 0)))

    o = pl.pallas_call(
        _attn_kernel,
        grid=(b, nh // heads_per_step, nb),
        in_specs=[
            pl.BlockSpec((None, heads_per_step, tq, hd), lambda bi, hi, qi: (bi, hi, qi, 0)),
            pl.BlockSpec((None, heads_per_step, s, 2 * hd), lambda bi, hi, qi: (bi, hi, 0, 0)),
            pl.BlockSpec((None, heads_per_step, s, hd), lambda bi, hi, qi: (bi, hi, 0, 0)),
            pl.BlockSpec((None, heads_per_step, LANES, hd), lambda bi, hi, qi: (bi, hi, 0, 0)),
        ],
        out_specs=pl.BlockSpec((None, heads_per_step, tq, hd),
                               lambda bi, hi, qi: (bi, hi, qi, 0)),
        out_shape=jax.ShapeDtypeStruct((b, nh, s, hd), BF16),
        scratch_shapes=[pltpu.VMEM((heads_per_step, s, tq), F32)],
        compiler_params=_cparams(("arbitrary", "arbitrary", "arbitrary")),
    )(q, kext, v, km)

    return pl.pallas_call(
        _attn_out_kernel,
        grid=(b, s // ts_out),
        in_specs=[
            pl.BlockSpec((None, nh, ts_out, hd), lambda bi, si: (bi, 0, si, 0)),
            pl.BlockSpec((None, ts_out, d), lambda bi, si: (bi, si, 0)),
            _mod_spec(2, d),
            full((d, d)),
        ],
        out_specs=pl.BlockSpec((None, ts_out, d), lambda bi, si: (bi, si, 0)),
        out_shape=jax.ShapeDtypeStruct(x.shape, F32),
        compiler_params=_cparams(("arbitrary", "arbitrary")),
    )(o, x, mod, w_out.astype(BF16))


def _route_kernel(x_ref, sh_ref, sc_ref, ng_ref, rw_ref, rb_ref,
                  h_ref, e_ref, w_ref, r_ref, cnt_ref, run_ref):
    tm = x_ref.shape[0]

    @pl.when(pl.program_id(0) == 0)
    def _():
        run_ref[...] = jnp.zeros_like(run_ref)

    h = _modulated_norm(x_ref[...], ng_ref[...], sc_ref[...], sh_ref[...])
    h_ref[...] = h
    logits = jnp.dot(h, rw_ref[...], preferred_element_type=F32, precision=HIGHEST) + rb_ref[...]
    lane = lax.broadcasted_iota(jnp.int32, (tm, LANES), 1)
    lane_f = lane.astype(F32)
    lg = jnp.where(lane < N_EXPERTS, logits, -jnp.inf)
    tops, hits = [], []
    for _ in range(TOP_K):
        m = jnp.max(lg, axis=-1, keepdims=True)
        idx = jnp.min(jnp.where(lg == m, lane_f, float(LANES)), axis=-1, keepdims=True)
        hit = lane_f == idx
        tops.append((m, idx))
        hits.append(hit)
        lg = jnp.where(hit, -jnp.inf, lg)
    exps = [jnp.exp(m - tops[0][0]) for m, _ in tops]
    denom = exps[0]
    for ex in exps[1:]:
        denom = denom + ex

    member = jnp.zeros((tm, LANES), F32)
    for hit in hits:
        member = member + jnp.where(hit, 1.0, 0.0)
    rr = lax.broadcasted_iota(jnp.int32, (tm, tm), 0)
    cc = lax.broadcasted_iota(jnp.int32, (tm, tm), 1)
    earlier = jnp.where(cc < rr, 1.0, 0.0).astype(BF16)
    before = jnp.dot(earlier, member.astype(BF16), preferred_element_type=F32) + run_ref[...]

    e_out = jnp.zeros((tm, LANES), F32)
    w_out = jnp.zeros((tm, LANES), F32)
    r_out = jnp.zeros((tm, LANES), F32)
    for k in range(TOP_K):
        rank = jnp.sum(jnp.where(hits[k], before, 0.0), axis=-1, keepdims=True)
        e_out = jnp.where(lane == k, tops[k][1], e_out)
        w_out = jnp.where(lane == k, exps[k] / denom, w_out)
        r_out = jnp.where(lane == k, rank, r_out)
    e_ref[...] = e_out.astype(jnp.int32)
    w_ref[...] = w_out
    r_ref[...] = r_out.astype(jnp.int32)
    run = run_ref[...] + jnp.sum(member, axis=0, keepdims=True)
    run_ref[...] = run
    cnt_ref[...] = run.astype(jnp.int32)


def _dispatch_kernel(start_ref, e_ref, r_ref, h_ref, xs_in_hbm, xs_hbm, item_ref, sem):
    del xs_in_hbm
    tm = h_ref.shape[0]
    unroll = 32
    assert item_ref.shape[0] % unroll == 0

    @pl.when(pl.program_id(0) == 0)
    def _():
        def clear(c, _):
            for u in range(unroll):
                item_ref[c * unroll + u] = 0
            return 0
        lax.fori_loop(0, item_ref.shape[0] // unroll, clear, 0)

    base = pl.program_id(0) * tm * TOP_K

    def issue(t, _):
        for k in range(TOP_K):
            j = t * TOP_K + k
            row = start_ref[e_ref[0, j]] + r_ref[0, j]
            item_ref[row] = base + j
            pltpu.make_async_copy(h_ref.at[pl.ds(t, 1), :], xs_hbm.at[pl.ds(row, 1), :],
                                  sem).start()
        return 0

    lax.fori_loop(0, tm, issue, 0)
    for _ in range(TOP_K):
        pltpu.make_async_copy(h_ref, xs_hbm.at[pl.ds(0, tm), :], sem).wait()


def _wait_rows(n, src_ref, dst_hbm, sem):
    p = 1
    while p <= src_ref.shape[0]:
        @pl.when((n & p) != 0)
        def _(p=p):
            pltpu.make_async_copy(src_ref.at[pl.ds(0, p), :], dst_hbm.at[pl.ds(0, p), :], sem).wait()
        p *= 2


def _expert_kernel(layer, be_ref, first_ref, nu_ref, nv_ref, wslot_ref, next_ref, item_ref,
                   xs_ref, wgu_hbm, bgu_ref, wd_hbm, bd_ref, yt_hbm,
                   wgu_f32, wd_f32, wgu_bf, wd_bf, ybuf, sems, wsems):
    i = pl.program_id(0)
    de = wd_bf.shape[0]
    slot = i % 2
    nv = nv_ref[i]

    def weight_copies(expert, ws):
        return (pltpu.make_async_copy(wgu_hbm.at[layer, expert], wgu_f32.at[ws], wsems.at[ws, 0]),
                pltpu.make_async_copy(wd_hbm.at[layer, expert], wd_f32.at[ws], wsems.at[ws, 1]))

    @pl.when(nv > 0)
    def _():
        @pl.when(first_ref[i] == 1)
        def _():
            ws = wslot_ref[i]

            @pl.when(i == 0)
            def _():
                for cp in weight_copies(be_ref[0], 0):
                    cp.start()

            for cp in weight_copies(be_ref[i], ws):
                cp.wait()

            @pl.when(next_ref[i] >= 0)
            def _():
                for cp in weight_copies(next_ref[i], 1 - ws):
                    cp.start()

            wgu_bf[...] = wgu_f32[ws].astype(BF16)
            wd_bf[...] = wd_f32[ws].astype(BF16)

        gu = jnp.dot(xs_ref[...].astype(BF16), wgu_bf[...], preferred_element_type=F32) + bgu_ref[...]
        gate = jnp.minimum(gu[:, :de], SWIGLU_LIMIT)
        up = jnp.clip(gu[:, de:], -SWIGLU_LIMIT, SWIGLU_LIMIT)
        act = gate * _sigmoid(SWIGLU_ALPHA * gate) * (up + 1.0)
        ybuf[slot] = jnp.dot(act.astype(BF16), wd_bf[...], preferred_element_type=F32) + bd_ref[...]

        def issue(r, _):
            pltpu.make_async_copy(ybuf.at[slot, pl.ds(r, 1), :],
                                  yt_hbm.at[pl.ds(item_ref[0, r], 1), :], sems.at[slot]).start()
            return 0

        lax.fori_loop(0, nv, issue, 0)

    @pl.when(i > 0)
    def _():
        _wait_rows(nv_ref[jnp.maximum(i - 1, 0)], ybuf.at[1 - slot], yt_hbm, sems.at[1 - slot])

    @pl.when(i == pl.num_programs(0) - 1)
    def _():
        _wait_rows(nv, ybuf.at[slot], yt_hbm, sems.at[slot])


def _combine_kernel(x_ref, yt_ref, w_ref, g2_ref, o_ref):
    o_ref[...] = x_ref[...] + g2_ref[...] * (w_ref[...] * yt_ref[...])


def _tok_mod_spec(j, tm, s, d):
    return pl.BlockSpec((None, None, 1, d), lambda i: (j, (i * tm) // s, 0, 0))


def _moe_route(xt, t, s, mod, ng, router_w, router_b, tm=256):
    d = xt.shape[1]
    assert t % tm == 0
    rw = jnp.pad(router_w, ((0, 0), (0, LANES - N_EXPERTS)))
    rb = jnp.pad(router_b, (0, LANES - N_EXPERTS)).reshape(1, LANES)
    return pl.pallas_call(
        _route_kernel,
        grid=(t // tm,),
        in_specs=[
            pl.BlockSpec((tm, d), lambda i: (i, 0)),
            _tok_mod_spec(3, tm, s, d), _tok_mod_spec(4, tm, s, d),
            pl.BlockSpec((1, d), lambda i: (0, 0)),
            pl.BlockSpec((d, LANES), lambda i: (0, 0)),
            pl.BlockSpec((1, LANES), lambda i: (0, 0)),
        ],
        out_specs=[
            pl.BlockSpec((tm, d), lambda i: (i, 0)),
            pl.BlockSpec((tm, LANES), lambda i: (i, 0)),
            pl.BlockSpec((tm, LANES), lambda i: (i, 0)),
            pl.BlockSpec((tm, LANES), lambda i: (i, 0)),
            pl.BlockSpec((1, LANES), lambda i: (0, 0)),
        ],
        out_shape=[
            jax.ShapeDtypeStruct((t, d), F32),
            jax.ShapeDtypeStruct((t, LANES), jnp.int32),
            jax.ShapeDtypeStruct((t, LANES), F32),
            jax.ShapeDtypeStruct((t, LANES), jnp.int32),
            jax.ShapeDtypeStruct((1, LANES), jnp.int32),
        ],
        scratch_shapes=[pltpu.VMEM((1, LANES), F32)],
        compiler_params=_cparams(("arbitrary",)),
    )(xt, mod, mod, ng.reshape(1, d), rw, rb)


def _moe_plan(t, counts, tm):
    n_e = N_EXPERTS
    counts = counts[0, :n_e]
    padded = (counts + tm - 1) // tm * tm
    pad_end = jnp.cumsum(padded)
    pad_start = pad_end - padded
    n_tiles = (t * TOP_K) // tm + n_e
    tile_start = jnp.arange(n_tiles, dtype=jnp.int32) * tm
    n_used = (pad_end[-1] // tm).astype(jnp.int32)
    tile_expert = jnp.minimum(jnp.searchsorted(pad_end, tile_start, side='right'), n_e - 1)
    tile_expert = jnp.where(jnp.arange(n_tiles) < n_used, tile_expert,
                            tile_expert[jnp.maximum(n_used - 1, 0)]).astype(jnp.int32)
    first = jnp.concatenate([jnp.ones((1,), jnp.int32),
                             (tile_expert[1:] != tile_expert[:-1]).astype(jnp.int32)])
    seg_end = (pad_start + counts)[tile_expert]
    n_valid = jnp.where(jnp.arange(n_tiles) < n_used,
                        jnp.clip(seg_end - tile_start, 0, tm), 0).astype(jnp.int32)
    used = jnp.arange(n_tiles) < n_used
    w_slot = ((jnp.cumsum(first) - 1) % 2).astype(jnp.int32)
    first_pos = jnp.where((first == 1) & used, jnp.arange(n_tiles), n_tiles)
    next_first = lax.cummin(first_pos[::-1])[::-1]
    next_first = jnp.concatenate([next_first[1:], jnp.full((1,), n_tiles)])
    next_expert = jnp.where(next_first < n_tiles,
                            tile_expert[jnp.minimum(next_first, n_tiles - 1)], -1).astype(jnp.int32)
    return (pad_start.astype(jnp.int32), tile_expert, first, n_used.reshape(1), n_valid,
            w_slot, next_expert, n_tiles)


def _moe_dispatch(h2, pad_start, top_e, rank, xs_buf, tm=256):
    t, d = h2.shape
    n_rows = xs_buf.shape[0]
    flat = lambda a: a[:, :TOP_K].reshape(t // tm, 1, tm * TOP_K)
    item_spec = pl.BlockSpec((None, 1, tm * TOP_K), lambda i: (i, 0, 0), memory_space=pltpu.SMEM)
    return pl.pallas_call(
        _dispatch_kernel,
        grid=(t // tm,),
        in_specs=[
            pl.BlockSpec(memory_space=pltpu.SMEM),
            item_spec, item_spec,
            pl.BlockSpec((tm, d), lambda i: (i, 0)),
            pl.BlockSpec(memory_space=pl.ANY),
        ],
        out_specs=[pl.BlockSpec(memory_space=pl.ANY), pl.BlockSpec(memory_space=pltpu.SMEM)],
        out_shape=[jax.ShapeDtypeStruct((n_rows, d), F32),
                   jax.ShapeDtypeStruct((n_rows,), jnp.int32)],
        scratch_shapes=[pltpu.SemaphoreType.DMA],
        input_output_aliases={4: 0},
        compiler_params=_cparams(("arbitrary",)),
    )(pad_start, flat(top_e), flat(rank), h2, xs_buf)


def _moe_experts(xs, n_items, tile_expert, first, n_used, n_valid, w_slot, next_expert, row_item,
                 w_gu, b_gu, w_down, b_down, layer, tm):
    n_rows, d = xs.shape
    n_e = N_EXPERTS
    de = w_down.shape[2]
    n_tiles = n_rows // tm
    tile_idx = lambda i, be, fi, nu, *_: (jnp.maximum(jnp.minimum(i, nu[0] - 1), 0), 0)
    b_idx = lambda i, be, *_: (layer, be[i], 0, 0)
    return pl.pallas_call(
        functools.partial(_expert_kernel, layer),
        grid_spec=pltpu.PrefetchScalarGridSpec(
            num_scalar_prefetch=6,
            grid=(n_tiles,),
            in_specs=[
                pl.BlockSpec((None, 1, tm), lambda i, *_: (i, 0, 0), memory_space=pltpu.SMEM),
                pl.BlockSpec((tm, d), tile_idx),
                pl.BlockSpec(memory_space=pl.ANY),
                pl.BlockSpec((None, None, 1, 2 * de), b_idx),
                pl.BlockSpec(memory_space=pl.ANY),
                pl.BlockSpec((None, None, 1, d), b_idx),
            ],
            out_specs=pl.BlockSpec(memory_space=pl.ANY),
            scratch_shapes=[pltpu.VMEM((2, d, 2 * de), F32), pltpu.VMEM((2, de, d), F32),
                            pltpu.VMEM((d, 2 * de), BF16), pltpu.VMEM((de, d), BF16),
                            pltpu.VMEM((2, tm, d), F32), pltpu.SemaphoreType.DMA((2,)),
                            pltpu.SemaphoreType.DMA((2, 2))],
        ),
        out_shape=jax.ShapeDtypeStruct((n_items, d), F32),
        compiler_params=_cparams(("arbitrary",)),
    )(tile_expert, first, n_used, n_valid, w_slot, next_expert, row_item.reshape(n_tiles, 1, tm),
      xs, w_gu, b_gu.reshape(b_gu.shape[0], n_e, 1, 2 * de), w_down,
      b_down.reshape(b_down.shape[0], n_e, 1, d))


def _moe_combine(xt, s, mod, w_item, yt, tm=512):
    t, d = xt.shape
    return pl.pallas_call(
        _combine_kernel,
        grid=(t // tm,),
        in_specs=[
            pl.BlockSpec((tm, d), lambda i: (i, 0)),
            pl.BlockSpec((tm, d), lambda i: (i, 0)),
            pl.BlockSpec((tm, 1), lambda i: (i, 0)),
            _tok_mod_spec(5, tm, s, d),
        ],
        out_specs=pl.BlockSpec((tm, d), lambda i: (i, 0)),
        out_shape=jax.ShapeDtypeStruct((t, d), F32),
        compiler_params=_cparams(("arbitrary",)),
    )(xt, yt, w_item, mod)


def _moe_rows(n_tokens, tm_exp=256):
    return (n_tokens // tm_exp + N_EXPERTS) * tm_exp


def _moe_layer(x, xs_buf, mod, ng, router_w, router_b, w_gu, b_gu, w_down, b_down, layer,
               tm_exp=256):
    b, s, d = x.shape
    t = b * s
    xt = x.reshape(t, d)
    assert t % TOP_K == 0
    n_tok = t // TOP_K
    h2, top_e, top_w, rank, counts = _moe_route(xt, n_tok, s, mod, ng, router_w, router_b)
    (pad_start, tile_expert, first, n_used, n_valid, w_slot, next_expert,
     n_tiles) = _moe_plan(n_tok, counts, tm_exp)
    assert xs_buf.shape[0] == n_tiles * tm_exp
    xs, row_item = _moe_dispatch(h2, pad_start, top_e, rank, xs_buf)
    yt = _moe_experts(xs, t, tile_expert, first, n_used, n_valid, w_slot, next_expert, row_item,
                      w_gu, b_gu, w_down, b_down, layer, tm_exp)
    w_item = top_w[:, :TOP_K].reshape(t, 1)
    return _moe_combine(xt, s, mod, w_item, yt).reshape(b, s, d), xs


def kernel(x, c, norm_mix_g, norm_ffn_g, w_mod, b_mod, lru_w_in, lru_conv_w, lru_conv_b, lru_w_a, lru_b_a, lru_w_x, lru_b_x, lru_lambda, lru_w_out, attn_w_qkv, attn_q_norm_g, attn_k_norm_g, attn_w_out, router_w, router_b, expert_w_gu, expert_b_gu, expert_w_down, expert_b_down):
    depth = w_mod.shape[0]
    mods = _modulation(c, w_mod, b_mod)
    n_tokens = x.shape[0] * x.shape[1]
    xs_buf = jnp.zeros((_moe_rows(n_tokens), x.shape[2]), F32)
    for i in range(depth):
        mod = mods[i]
        j = i // 2
        if i % 2 == 0:
            x = _rglru_layer(x, mod, norm_mix_g[i], lru_w_in[j], lru_conv_w[j], lru_conv_b[j],
                             lru_w_a[j], lru_b_a[j], lru_w_x[j], lru_b_x[j], lru_lambda[j],
                             lru_w_out[j])
        else:
            x = _moba_layer(x, mod, norm_mix_g[i], attn_w_qkv[j], attn_q_norm_g[j],
                            attn_k_norm_g[j], attn_w_out[j])
        x, xs_buf = _moe_layer(x, xs_buf, mod, norm_ffn_g[i], router_w[i], router_b[i],
                               expert_w_gu, expert_b_gu, expert_w_down, expert_b_down, i)
    return x
```

```python
import functools

import jax
import jax.numpy as jnp
from jax import lax
from jax.experimental import pallas as pl
from jax.experimental.pallas import tpu as pltpu

NORM_EPS = 1e-6
N_MOD = 6
LRU_BLOCKS = 4
CONV_WIDTH = 4
LRU_C = 8.0
ATTN_HEADS = 8
MOBA_BLOCK = 256
MOBA_TOPK = 3
N_EXPERTS = 32
TOP_K = 4
SWIGLU_LIMIT = 7.0
SWIGLU_ALPHA = 1.702

LANES = 128
SUBLANES = 8
MASK_BIAS = -(2.0 ** 100)
LOG2_E = 1.4426950408889634
ISSUE_UNROLL = 8
VMEM_LIMIT = 52 * 1024 * 1024

F32 = jnp.float32
BF16 = jnp.bfloat16
HIGHEST = lax.Precision.HIGHEST


def _cparams(sem):
    return pltpu.CompilerParams(dimension_semantics=sem, vmem_limit_bytes=VMEM_LIMIT)


def _sigmoid(z):
    return 1.0 / (1.0 + jnp.exp(-z))


def _modulated_norm(x, g, sc, sh):
    ms = jnp.mean(x * x, axis=-1, keepdims=True)
    return x * lax.rsqrt(ms + NORM_EPS) * g * (1.0 + sc) + sh


def _mod_kernel(c_ref, w_ref, b_ref, o_ref):
    c = c_ref[...]
    cond = c * _sigmoid(c)
    o_ref[...] = jnp.dot(cond, w_ref[...], preferred_element_type=F32,
                         precision=HIGHEST) + b_ref[...]


def _modulation(c, w_mod, b_mod):
    depth, d, _ = w_mod.shape
    b = c.shape[0]
    rows = -(-b // SUBLANES) * SUBLANES
    c_pad = jnp.pad(c, ((0, rows - b), (0, 0)))
    out = pl.pallas_call(
        _mod_kernel,
        grid=(depth, N_MOD),
        in_specs=[
            pl.BlockSpec((rows, d), lambda i, j: (0, 0)),
            pl.BlockSpec((None, d, d), lambda i, j: (i, 0, j)),
            pl.BlockSpec((None, None, 1, d), lambda i, j: (i, j, 0, 0)),
        ],
        out_specs=pl.BlockSpec((None, None, rows, d), lambda i, j: (i, j, 0, 0)),
        out_shape=jax.ShapeDtypeStruct((depth, N_MOD, rows, d), F32),
        compiler_params=_cparams(("arbitrary", "arbitrary")),
    )(c_pad, w_mod, b_mod.reshape(depth, N_MOD, 1, d))
    return out[:, :, :b].reshape(depth, N_MOD, b, 1, d)


def _mod_spec(j, d):
    return pl.BlockSpec((None, None, 1, d), lambda b, s: (j, b, 0, 0))


def _rglru_kernel(x_ref, sh_ref, sc_ref, g1_ref, ng_ref, win_ref, cw_ref, cb_ref,
                  wa_ref, ba_ref, wx_ref, bx_ref, lam_ref, wout_ref, o_ref,
                  ext_ref, a_ref, u_ref, h_ref):
    ts, d = x_ref.shape
    bw = d // LRU_BLOCKS

    @pl.when(pl.program_id(1) == 0)
    def _():
        ext_ref[0:SUBLANES, :] = jnp.zeros((SUBLANES, d), F32)
        h_ref[...] = jnp.zeros_like(h_ref)

    x = x_ref[...]
    h = _modulated_norm(x, ng_ref[...], sc_ref[...], sh_ref[...])
    gr = jnp.dot(h.astype(BF16), win_ref[...], preferred_element_type=F32)
    gate_branch = gr[:, :d]
    rec = gr[:, d:]

    ext_ref[SUBLANES:, :] = rec
    xc = cb_ref[...] + cw_ref[CONV_WIDTH - 1:CONV_WIDTH, :] * rec
    for k in range(CONV_WIDTH - 1):
        off = SUBLANES - (CONV_WIDTH - 1) + k
        xc = xc + cw_ref[k:k + 1, :] * ext_ref[off:off + ts, :]
    ext_ref[0:SUBLANES, :] = rec[ts - SUBLANES:, :]

    xcb = xc.astype(BF16)
    ra = jnp.concatenate(
        [jnp.dot(xcb[:, g * bw:(g + 1) * bw], wa_ref[g], preferred_element_type=F32)
         for g in range(LRU_BLOCKS)], axis=1) + ba_ref[...]
    rx = jnp.concatenate(
        [jnp.dot(xcb[:, g * bw:(g + 1) * bw], wx_ref[g], preferred_element_type=F32)
         for g in range(LRU_BLOCKS)], axis=1) + bx_ref[...]
    r = _sigmoid(ra)
    ig = _sigmoid(rx)
    z = -lam_ref[...]
    softplus = jnp.maximum(z, 0.0) + jnp.log(1.0 + jnp.exp(-jnp.abs(z)))
    a = jnp.exp(-LRU_C * r * softplus)
    v = 1.0 - a * a
    u = jnp.where(v > 0.0, v * lax.rsqrt(v), 0.0) * (ig * xc)

    a = a.reshape(ts // SUBLANES, SUBLANES, d)
    u = u.reshape(ts // SUBLANES, SUBLANES, d)
    row = lax.broadcasted_iota(jnp.int32, a.shape, 1)
    k = 1
    while k < SUBLANES:
        a_prev = pltpu.roll(a, k, 1)
        u_prev = pltpu.roll(u, k, 1)
        m = row >= k
        u = jnp.where(m, a * u_prev + u, u)
        a = jnp.where(m, a * a_prev, a)
        k *= 2
    a_ref[...] = a.reshape(ts, d)
    u_ref[...] = u.reshape(ts, d)

    def group(j, hc):
        r0 = pl.multiple_of(j * SUBLANES, SUBLANES)
        hs = u_ref[pl.ds(r0, SUBLANES), :] + a_ref[pl.ds(r0, SUBLANES), :] * hc
        u_ref[pl.ds(r0, SUBLANES), :] = hs
        return hs[SUBLANES - 1:SUBLANES, :]

    h_ref[0:1, :] = lax.fori_loop(0, ts // SUBLANES, group, h_ref[0:1, :])

    y = jax.nn.gelu(gate_branch, approximate=True) * u_ref[...]
    out = jnp.dot(y.astype(BF16), wout_ref[...], preferred_element_type=F32)
    o_ref[...] = x + g1_ref[...] * out


def _rglru_layer(x, mod, ng, w_in, conv_w, conv_b, w_a, b_a, w_x, b_x, lam, w_out, ts=256):
    b, s, d = x.shape
    full = lambda shape: pl.BlockSpec(shape, lambda bi, si: (0,) * len(shape))
    row = lambda v: v.reshape(1, d)
    return pl.pallas_call(
        _rglru_kernel,
        grid=(b, s // ts),
        in_specs=[
            pl.BlockSpec((None, ts, d), lambda bi, si: (bi, si, 0)),
            _mod_spec(0, d), _mod_spec(1, d), _mod_spec(2, d),
            full((1, d)), full((d, 2 * d)), full((CONV_WIDTH, d)), full((1, d)),
            full(w_a.shape), full((1, d)), full(w_x.shape), full((1, d)), full((1, d)),
            full((d, d)),
        ],
        out_specs=pl.BlockSpec((None, ts, d), lambda bi, si: (bi, si, 0)),
        out_shape=jax.ShapeDtypeStruct(x.shape, F32),
        scratch_shapes=[
            pltpu.VMEM((ts + SUBLANES, d), F32),
            pltpu.VMEM((ts, d), F32),
            pltpu.VMEM((ts, d), F32),
            pltpu.VMEM((SUBLANES, d), F32),
        ],
        compiler_params=_cparams(("arbitrary", "arbitrary")),
    )(x, mod, mod, mod, row(ng), w_in.astype(BF16), conv_w, row(conv_b),
      w_a.astype(BF16), row(b_a), w_x.astype(BF16), row(b_x), row(lam), w_out.astype(BF16))


def _qkv_kernel(x_ref, sh_ref, sc_ref, ng_ref, w_ref, qg_ref, kg_ref,
                q_ref, kext_ref, v_ref, km_ref):
    ts, d = x_ref.shape
    nh, _, hd = q_ref.shape
    blk = pl.program_id(1)
    h = _modulated_norm(x_ref[...], ng_ref[...], sc_ref[...], sh_ref[...])
    qkv = jnp.dot(h.astype(BF16), w_ref[...], preferred_element_type=F32)
    lane = lax.broadcasted_iota(jnp.int32, (ts, hd), 1)
    onehot = jnp.where(lane == blk, 1.0, 0.0).astype(BF16)
    for hh in range(nh):
        qh = qkv[:, hh * hd:(hh + 1) * hd]
        qn = qh * lax.rsqrt(jnp.mean(qh * qh, axis=-1, keepdims=True) + NORM_EPS)
        q_ref[hh] = (qn * qg_ref[...] * (hd ** -0.5 * LOG2_E)).astype(BF16)
        kh = qkv[:, d + hh * hd:d + (hh + 1) * hd]
        kn = kh * lax.rsqrt(jnp.mean(kh * kh, axis=-1, keepdims=True) + NORM_EPS) * kg_ref[...]
        kext_ref[hh, :, 0:hd] = kn.astype(BF16)
        kext_ref[hh, :, hd:2 * hd] = onehot
        km_ref[:, hh * hd:(hh + 1) * hd] = jnp.mean(kn, axis=0, keepdims=True)
        v_ref[hh] = qkv[:, 2 * d + hh * hd:2 * d + (hh + 1) * hd].astype(BF16)


def _attn_query_block(qi, q_ref, kext_ref, v_ref, km_ref, o_ref, s_ref):
    nhs, tq, hd = q_ref.shape
    nt = (((1,), (1,)), ((), ()))
    nbp = -(-(kext_ref.shape[1] // tq) // SUBLANES) * SUBLANES
    blk = lax.broadcasted_iota(jnp.int32, (nbp, tq), 0)
    blk_f = blk.astype(F32)
    past = blk < qi
    rr = lax.broadcasted_iota(jnp.int32, (tq, tq), 0)
    cc = lax.broadcasted_iota(jnp.int32, (tq, tq), 1)
    for hh in range(nhs):
        q = q_ref[hh]
        gate = lax.dot_general(km_ref[hh], q.astype(F32), nt, precision=HIGHEST,
                               preferred_element_type=F32)[:nbp]
        g = jnp.where(past, gate, -jnp.inf)
        sel = blk == qi
        for _ in range(min(MOBA_TOPK, qi)):
            m = jnp.max(g, axis=0, keepdims=True)
            idx = jnp.min(jnp.where(g == m, blk_f, float(nbp)), axis=0, keepdims=True)
            hit = blk_f == idx
            sel = sel | (hit & past)
            g = jnp.where(hit, -jnp.inf, g)
        bias_t = jnp.where(sel, 0.0, MASK_BIAS)
        bias = jnp.concatenate([bias_t, jnp.zeros((LANES - nbp, tq), F32)], axis=0).T
        q_ext = jnp.concatenate([q, bias.astype(BF16)], axis=1)

        mx = None
        for j in range(qi + 1):
            sj = lax.dot_general(q_ext, kext_ref[hh, j * tq:(j + 1) * tq, :], nt,
                                 preferred_element_type=F32)
            if j == qi:
                sj = jnp.where(cc <= rr, sj, MASK_BIAS)
            s_ref[hh, j * tq:(j + 1) * tq, :] = sj
            mx = sj if mx is None else jnp.maximum(mx, sj)
        m = jnp.max(mx, axis=-1, keepdims=True)

        m = jnp.broadcast_to(m, (tq, tq))
        psum, acc = None, None
        for j in range(qi + 1):
            p = jnp.exp2(s_ref[hh, j * tq:(j + 1) * tq, :] - m)
            pv = jnp.dot(p.astype(BF16), v_ref[hh, j * tq:(j + 1) * tq, :],
                         preferred_element_type=F32)
            psum = p if psum is None else psum + p
            acc = pv if acc is None else acc + pv
        l = jnp.sum(psum, axis=-1, keepdims=True)
        o_ref[hh] = (acc / l).astype(BF16)


def _attn_kernel(q_ref, kext_ref, v_ref, km_ref, o_ref, s_ref):
    nb = kext_ref.shape[1] // q_ref.shape[1]
    for qi in range(nb):
        @pl.when(pl.program_id(2) == qi)
        def _(qi=qi):
            _attn_query_block(qi, q_ref, kext_ref, v_ref, km_ref, o_ref, s_ref)


def _attn_out_kernel(o_ref, x_ref, g1_ref, w_ref, y_ref):
    nh = o_ref.shape[0]
    o = jnp.concatenate([o_ref[hh] for hh in range(nh)], axis=1)
    out = jnp.dot(o, w_ref[...], preferred_element_type=F32)
    y_ref[...] = x_ref[...] + g1_ref[...] * out


def _moba_layer(x, mod, ng, w_qkv, q_g, k_g, w_out, ts_out=512, heads_per_step=1):
    b, s, d = x.shape
    nh = ATTN_HEADS
    hd = d // nh
    tq = MOBA_BLOCK
    nb = s // tq
    assert s % tq == 0 and nb <= LANES and hd == LANES
    full = lambda shape: pl.BlockSpec(shape, lambda bi, si: (0,) * len(shape))

    q, kext, v, km = pl.pallas_call(
        _qkv_kernel,
        grid=(b, nb),
        in_specs=[
            pl.BlockSpec((None, tq, d), lambda bi, si: (bi, si, 0)),
            _mod_spec(0, d), _mod_spec(1, d),
            full((1, d)), full((d, 3 * d)), full((1, hd)), full((1, hd)),
        ],
        out_specs=[
            pl.BlockSpec((None, nh, tq, hd), lambda bi, si: (bi, 0, si, 0)),
            pl.BlockSpec((None, nh, tq, 2 * hd), lambda bi, si: (bi, 0, si, 0)),
            pl.BlockSpec((None, nh, tq, hd), lambda bi, si: (bi, 0, si, 0)),
            pl.BlockSpec((None, None, 1, d), lambda bi, si: (bi, si, 0, 0)),
        ],
        out_shape=[
            jax.ShapeDtypeStruct((b, nh, s, hd), BF16),
            jax.ShapeDtypeStruct((b, nh, s, 2 * hd), BF16),
            jax.ShapeDtypeStruct((b, nh, s, hd), BF16),
            jax.ShapeDtypeStruct((b, nb, 1, d), F32),
        ],
        compiler_params=_cparams(("arbitrary", "arbitrary")),
    )(x, mod, mod, ng.reshape(1, d), w_qkv.astype(BF16), q_g.reshape(1, hd), k_g.reshape(1, hd))

    km = km.reshape(b, nb, nh, hd).transpose(0, 2, 1, 3)
    km = jnp.pad(km, ((0, 0), (0, 0), (0, LANES - nb), (0, 0)))

    o = pl.pallas_call(
        _attn_kernel,
        grid=(b, nh // heads_per_step, nb),
        in_specs=[
            pl.BlockSpec((None, heads_per_step, tq, hd), lambda bi, hi, qi: (bi, hi, qi, 0)),
            pl.BlockSpec((None, heads_per_step, s, 2 * hd), lambda bi, hi, qi: (bi, hi, 0, 0)),
            pl.BlockSpec((None, heads_per_step, s, hd), lambda bi, hi, qi: (bi, hi, 0, 0)),
            pl.BlockSpec((None, heads_per_step, LANES, hd), lambda bi, hi, qi: (bi, hi, 0, 0)),
        ],
        out_specs=pl.BlockSpec((None, heads_per_step, tq, hd),
                               lambda bi, hi, qi: (bi, hi, qi, 0)),
        out_shape=jax.ShapeDtypeStruct((b, nh, s, hd), BF16),
        scratch_shapes=[pltpu.VMEM((heads_per_step, s, tq), F32)],
        compiler_params=_cparams(("arbitrary", "arbitrary", "arbitrary")),
    )(q, kext, v, km)

    return pl.pallas_call(
        _attn_out_kernel,
        grid=(b, s // ts_out),
        in_specs=[
            pl.BlockSpec((None, nh, ts_out, hd), lambda bi, si: (bi, 0, si, 0)),
            pl.BlockSpec((None, ts_out, d), lambda bi, si: (bi, si, 0)),
            _mod_spec(2, d),
            full((d, d)),
        ],
        out_specs=pl.BlockSpec((None, ts_out, d), lambda bi, si: (bi, si, 0)),
        out_shape=jax.ShapeDtypeStruct(x.shape, F32),
        compiler_params=_cparams(("arbitrary", "arbitrary")),
    )(o, x, mod, w_out.astype(BF16))


def _route_kernel(x_ref, sh_ref, sc_ref, ng_ref, rw_ref, rb_ref,
                  h_ref, e_ref, w_ref, r_ref, cnt_ref, run_ref):
    tm = x_ref.shape[0]

    @pl.when(pl.program_id(0) == 0)
    def _():
        run_ref[...] = jnp.zeros_like(run_ref)

    h = _modulated_norm(x_ref[...], ng_ref[...], sc_ref[...], sh_ref[...])
    h_ref[...] = h
    logits = jnp.dot(h, rw_ref[...], preferred_element_type=F32, precision=HIGHEST) + rb_ref[...]
    lane = lax.broadcasted_iota(jnp.int32, (tm, LANES), 1)
    lane_f = lane.astype(F32)
    lg = jnp.where(lane < N_EXPERTS, logits, -jnp.inf)
    tops, hits = [], []
    for _ in range(TOP_K):
        m = jnp.max(lg, axis=-1, keepdims=True)
        idx = jnp.min(jnp.where(lg == m, lane_f, float(LANES)), axis=-1, keepdims=True)
        hit = lane_f == idx
        tops.append((m, idx))
        hits.append(hit)
        lg = jnp.where(hit, -jnp.inf, lg)
    exps = [jnp.exp(m - tops[0][0]) for m, _ in tops]
    denom = exps[0]
    for ex in exps[1:]:
        denom = denom + ex

    member = jnp.zeros((tm, LANES), F32)
    for hit in hits:
        member = member + jnp.where(hit, 1.0, 0.0)
    rr = lax.broadcasted_iota(jnp.int32, (tm, tm), 0)
    cc = lax.broadcasted_iota(jnp.int32, (tm, tm), 1)
    earlier = jnp.where(cc < rr, 1.0, 0.0).astype(BF16)
    before = jnp.dot(earlier, member.astype(BF16), preferred_element_type=F32) + run_ref[...]

    e_out = jnp.zeros((tm, LANES), F32)
    w_out = jnp.zeros((tm, LANES), F32)
    r_out = jnp.zeros((tm, LANES), F32)
    for k in range(TOP_K):
        rank = jnp.sum(jnp.where(hits[k], before, 0.0), axis=-1, keepdims=True)
        e_out = jnp.where(lane == k, tops[k][1], e_out)
        w_out = jnp.where(lane == k, exps[k] / denom, w_out)
        r_out = jnp.where(lane == k, rank, r_out)
    e_ref[...] = e_out.astype(jnp.int32)
    w_ref[...] = w_out
    r_ref[...] = r_out.astype(jnp.int32)
    run = run_ref[...] + jnp.sum(member, axis=0, keepdims=True)
    run_ref[...] = run
    cnt_ref[...] = run.astype(jnp.int32)


def _dispatch_kernel(start_ref, e_ref, r_ref, h_ref, xs_in_hbm, xs_hbm, item_ref, sem):
    del xs_in_hbm
    tm = h_ref.shape[0]
    unroll = 32
    assert item_ref.shape[0] % unroll == 0

    @pl.when(pl.program_id(0) == 0)
    def _():
        def clear(c, _):
            for u in range(unroll):
                item_ref[c * unroll + u] = 0
            return 0
        lax.fori_loop(0, item_ref.shape[0] // unroll, clear, 0)

    base = pl.program_id(0) * tm * TOP_K

    tokens_per_iter = ISSUE_UNROLL // TOP_K
    assert tm % tokens_per_iter == 0

    def issue(g, _):
        rows = [start_ref[e_ref[0, g * ISSUE_UNROLL + u]] + r_ref[0, g * ISSUE_UNROLL + u]
                for u in range(ISSUE_UNROLL)]
        for u, row in enumerate(rows):
            item_ref[row] = base + g * ISSUE_UNROLL + u
            pltpu.make_async_copy(h_ref.at[pl.ds(g * tokens_per_iter + u // TOP_K, 1), :],
                                  xs_hbm.at[pl.ds(row, 1), :], sem).start()
        return 0

    lax.fori_loop(0, tm // tokens_per_iter, issue, 0)
    for _ in range(TOP_K):
        pltpu.make_async_copy(h_ref, xs_hbm.at[pl.ds(0, tm), :], sem).wait()


def _wait_rows(n, src_ref, dst_hbm, sem):
    p = 1
    while p <= src_ref.shape[0]:
        @pl.when((n & p) != 0)
        def _(p=p):
            pltpu.make_async_copy(src_ref.at[pl.ds(0, p), :], dst_hbm.at[pl.ds(0, p), :], sem).wait()
        p *= 2


def _expert_kernel(layer, be_ref, first_ref, nu_ref, nv_ref, wslot_ref, next_ref, item_ref,
                   xs_ref, wgu_hbm, bgu_ref, wd_hbm, bd_ref, yt_hbm,
                   wgu_f32, wd_f32, wgu_bf, wd_bf, ybuf, sems, wsems):
    i = pl.program_id(0)
    de = wd_bf.shape[0]
    slot = i % 2
    nv = nv_ref[i]

    def weight_copies(expert, ws):
        return (pltpu.make_async_copy(wgu_hbm.at[layer, expert], wgu_f32.at[ws], wsems.at[ws, 0]),
                pltpu.make_async_copy(wd_hbm.at[layer, expert], wd_f32.at[ws], wsems.at[ws, 1]))

    @pl.when(nv > 0)
    def _():
        @pl.when(first_ref[i] == 1)
        def _():
            ws = wslot_ref[i]

            @pl.when(i == 0)
            def _():
                for cp in weight_copies(be_ref[0], 0):
                    cp.start()

            for cp in weight_copies(be_ref[i], ws):
                cp.wait()

            @pl.when(next_ref[i] >= 0)
            def _():
                for cp in weight_copies(next_ref[i], 1 - ws):
                    cp.start()

            wgu_bf[...] = wgu_f32[ws].astype(BF16)
            wd_bf[...] = wd_f32[ws].astype(BF16)

        gu = jnp.dot(xs_ref[...].astype(BF16), wgu_bf[...], preferred_element_type=F32) + bgu_ref[...]
        gate = jnp.minimum(gu[:, :de], SWIGLU_LIMIT)
        up = jnp.clip(gu[:, de:], -SWIGLU_LIMIT, SWIGLU_LIMIT)
        act = gate * _sigmoid(SWIGLU_ALPHA * gate) * (up + 1.0)
        ybuf[slot] = jnp.dot(act.astype(BF16), wd_bf[...], preferred_element_type=F32) + bd_ref[...]

        def issue(r0, count):
            for u in range(count):
                pltpu.make_async_copy(ybuf.at[slot, pl.ds(r0 + u, 1), :],
                                      yt_hbm.at[pl.ds(item_ref[0, r0 + u], 1), :],
                                      sems.at[slot]).start()
            return 0

        n_groups = lax.shift_right_logical(nv, ISSUE_UNROLL.bit_length() - 1)
        lax.fori_loop(0, n_groups, lambda g, _: issue(g * ISSUE_UNROLL, ISSUE_UNROLL), 0)
        lax.fori_loop(n_groups * ISSUE_UNROLL, nv, lambda r, _: issue(r, 1), 0)

    @pl.when(i > 0)
    def _():
        _wait_rows(nv_ref[jnp.maximum(i - 1, 0)], ybuf.at[1 - slot], yt_hbm, sems.at[1 - slot])

    @pl.when(i == pl.num_programs(0) - 1)
    def _():
        _wait_rows(nv, ybuf.at[slot], yt_hbm, sems.at[slot])


def _combine_kernel(x_ref, yt_ref, w_ref, g2_ref, o_ref):
    o_ref[...] = x_ref[...] + g2_ref[...] * (w_ref[...] * yt_ref[...])


def _tok_mod_spec(j, tm, s, d):
    return pl.BlockSpec((None, None, 1, d), lambda i: (j, (i * tm) // s, 0, 0))


def _moe_route(xt, t, s, mod, ng, router_w, router_b, tm=256):
    d = xt.shape[1]
    assert t % tm == 0
    rw = jnp.pad(router_w, ((0, 0), (0, LANES - N_EXPERTS)))
    rb = jnp.pad(router_b, (0, LANES - N_EXPERTS)).reshape(1, LANES)
    return pl.pallas_call(
        _route_kernel,
        grid=(t // tm,),
        in_specs=[
            pl.BlockSpec((tm, d), lambda i: (i, 0)),
            _tok_mod_spec(3, tm, s, d), _tok_mod_spec(4, tm, s, d),
            pl.BlockSpec((1, d), lambda i: (0, 0)),
            pl.BlockSpec((d, LANES), lambda i: (0, 0)),
            pl.BlockSpec((1, LANES), lambda i: (0, 0)),
        ],
        out_specs=[
            pl.BlockSpec((tm, d), lambda i: (i, 0)),
            pl.BlockSpec((tm, LANES), lambda i: (i, 0)),
            pl.BlockSpec((tm, LANES), lambda i: (i, 0)),
            pl.BlockSpec((tm, LANES), lambda i: (i, 0)),
            pl.BlockSpec((1, LANES), lambda i: (0, 0)),
        ],
        out_shape=[
            jax.ShapeDtypeStruct((t, d), F32),
            jax.ShapeDtypeStruct((t, LANES), jnp.int32),
            jax.ShapeDtypeStruct((t, LANES), F32),
            jax.ShapeDtypeStruct((t, LANES), jnp.int32),
            jax.ShapeDtypeStruct((1, LANES), jnp.int32),
        ],
        scratch_shapes=[pltpu.VMEM((1, LANES), F32)],
        compiler_params=_cparams(("arbitrary",)),
    )(xt, mod, mod, ng.reshape(1, d), rw, rb)


def _moe_plan(t, counts, tm):
    n_e = N_EXPERTS
    counts = counts[0, :n_e]
    padded = (counts + tm - 1) // tm * tm
    pad_end = jnp.cumsum(padded)
    pad_start = pad_end - padded
    n_tiles = (t * TOP_K) // tm + n_e
    tile_start = jnp.arange(n_tiles, dtype=jnp.int32) * tm
    n_used = (pad_end[-1] // tm).astype(jnp.int32)
    tile_expert = jnp.minimum(jnp.searchsorted(pad_end, tile_start, side='right'), n_e - 1)
    tile_expert = jnp.where(jnp.arange(n_tiles) < n_used, tile_expert,
                            tile_expert[jnp.maximum(n_used - 1, 0)]).astype(jnp.int32)
    first = jnp.concatenate([jnp.ones((1,), jnp.int32),
                             (tile_expert[1:] != tile_expert[:-1]).astype(jnp.int32)])
    seg_end = (pad_start + counts)[tile_expert]
    n_valid = jnp.where(jnp.arange(n_tiles) < n_used,
                        jnp.clip(seg_end - tile_start, 0, tm), 0).astype(jnp.int32)
    used = jnp.arange(n_tiles) < n_used
    w_slot = ((jnp.cumsum(first) - 1) % 2).astype(jnp.int32)
    first_pos = jnp.where((first == 1) & used, jnp.arange(n_tiles), n_tiles)
    next_first = lax.cummin(first_pos[::-1])[::-1]
    next_first = jnp.concatenate([next_first[1:], jnp.full((1,), n_tiles)])
    next_expert = jnp.where(next_first < n_tiles,
                            tile_expert[jnp.minimum(next_first, n_tiles - 1)], -1).astype(jnp.int32)
    return (pad_start.astype(jnp.int32), tile_expert, first, n_used.reshape(1), n_valid,
            w_slot, next_expert, n_tiles)


def _moe_dispatch(h2, pad_start, top_e, rank, xs_buf, tm=256):
    t, d = h2.shape
    n_rows = xs_buf.shape[0]
    flat = lambda a: a[:, :TOP_K].reshape(t // tm, 1, tm * TOP_K)
    item_spec = pl.BlockSpec((None, 1, tm * TOP_K), lambda i: (i, 0, 0), memory_space=pltpu.SMEM)
    return pl.pallas_call(
        _dispatch_kernel,
        grid=(t // tm,),
        in_specs=[
            pl.BlockSpec(memory_space=pltpu.SMEM),
            item_spec, item_spec,
            pl.BlockSpec((tm, d), lambda i: (i, 0)),
            pl.BlockSpec(memory_space=pl.ANY),
        ],
        out_specs=[pl.BlockSpec(memory_space=pl.ANY), pl.BlockSpec(memory_space=pltpu.SMEM)],
        out_shape=[jax.ShapeDtypeStruct((n_rows, d), F32),
                   jax.ShapeDtypeStruct((n_rows,), jnp.int32)],
        scratch_shapes=[pltpu.SemaphoreType.DMA],
        input_output_aliases={4: 0},
        compiler_params=_cparams(("arbitrary",)),
    )(pad_start, flat(top_e), flat(rank), h2, xs_buf)


def _moe_experts(xs, n_items, tile_expert, first, n_used, n_valid, w_slot, next_expert, row_item,
                 w_gu, b_gu, w_down, b_down, layer, tm):
    n_rows, d = xs.shape
    n_e = N_EXPERTS
    de = w_down.shape[2]
    n_tiles = n_rows // tm
    tile_idx = lambda i, be, fi, nu, *_: (jnp.maximum(jnp.minimum(i, nu[0] - 1), 0), 0)
    b_idx = lambda i, be, *_: (layer, be[i], 0, 0)
    return pl.pallas_call(
        functools.partial(_expert_kernel, layer),
        grid_spec=pltpu.PrefetchScalarGridSpec(
            num_scalar_prefetch=6,
            grid=(n_tiles,),
            in_specs=[
                pl.BlockSpec((None, 1, tm), lambda i, *_: (i, 0, 0), memory_space=pltpu.SMEM),
                pl.BlockSpec((tm, d), tile_idx),
                pl.BlockSpec(memory_space=pl.ANY),
                pl.BlockSpec((None, None, 1, 2 * de), b_idx),
                pl.BlockSpec(memory_space=pl.ANY),
                pl.BlockSpec((None, None, 1, d), b_idx),
            ],
            out_specs=pl.BlockSpec(memory_space=pl.ANY),
            scratch_shapes=[pltpu.VMEM((2, d, 2 * de), F32), pltpu.VMEM((2, de, d), F32),
                            pltpu.VMEM((d, 2 * de), BF16), pltpu.VMEM((de, d), BF16),
                            pltpu.VMEM((2, tm, d), F32), pltpu.SemaphoreType.DMA((2,)),
                            pltpu.SemaphoreType.DMA((2, 2))],
        ),
        out_shape=jax.ShapeDtypeStruct((n_items, d), F32),
        compiler_params=_cparams(("arbitrary",)),
    )(tile_expert, first, n_used, n_valid, w_slot, next_expert, row_item.reshape(n_tiles, 1, tm),
      xs, w_gu, b_gu.reshape(b_gu.shape[0], n_e, 1, 2 * de), w_down,
      b_down.reshape(b_down.shape[0], n_e, 1, d))


def _moe_combine(xt, s, mod, w_item, yt, tm=512):
    t, d = xt.shape
    return pl.pallas_call(
        _combine_kernel,
        grid=(t // tm,),
        in_specs=[
            pl.BlockSpec((tm, d), lambda i: (i, 0)),
            pl.BlockSpec((tm, d), lambda i: (i, 0)),
            pl.BlockSpec((tm, 1), lambda i: (i, 0)),
            _tok_mod_spec(5, tm, s, d),
        ],
        out_specs=pl.BlockSpec((tm, d), lambda i: (i, 0)),
        out_shape=jax.ShapeDtypeStruct((t, d), F32),
        compiler_params=_cparams(("arbitrary",)),
    )(xt, yt, w_item, mod)


def _moe_rows(n_tokens, tm_exp=256):
    return (n_tokens // tm_exp + N_EXPERTS) * tm_exp


def _moe_layer(x, xs_buf, mod, ng, router_w, router_b, w_gu, b_gu, w_down, b_down, layer,
               tm_exp=256):
    b, s, d = x.shape
    t = b * s
    xt = x.reshape(t, d)
    assert t % TOP_K == 0
    n_tok = t // TOP_K
    h2, top_e, top_w, rank, counts = _moe_route(xt, n_tok, s, mod, ng, router_w, router_b)
    (pad_start, tile_expert, first, n_used, n_valid, w_slot, next_expert,
     n_tiles) = _moe_plan(n_tok, counts, tm_exp)
    assert xs_buf.shape[0] == n_tiles * tm_exp
    xs, row_item = _moe_dispatch(h2, pad_start, top_e, rank, xs_buf)
    yt = _moe_experts(xs, t, tile_expert, first, n_used, n_valid, w_slot, next_expert, row_item,
                      w_gu, b_gu, w_down, b_down, layer, tm_exp)
    w_item = top_w[:, :TOP_K].reshape(t, 1)
    return _moe_combine(xt, s, mod, w_item, yt).reshape(b, s, d), xs


def kernel(x, c, norm_mix_g, norm_ffn_g, w_mod, b_mod, lru_w_in, lru_conv_w, lru_conv_b, lru_w_a, lru_b_a, lru_w_x, lru_b_x, lru_lambda, lru_w_out, attn_w_qkv, attn_q_norm_g, attn_k_norm_g, attn_w_out, router_w, router_b, expert_w_gu, expert_b_gu, expert_w_down, expert_b_down):
    depth = w_mod.shape[0]
    mods = _modulation(c, w_mod, b_mod)
    n_tokens = x.shape[0] * x.shape[1]
    xs_buf = jnp.zeros((_moe_rows(n_tokens), x.shape[2]), F32)
    for i in range(depth):
        mod = mods[i]
        j = i // 2
        if i % 2 == 0:
            x = _rglru_layer(x, mod, norm_mix_g[i], lru_w_in[j], lru_conv_w[j], lru_conv_b[j],
                             lru_w_a[j], lru_b_a[j], lru_w_x[j], lru_b_x[j], lru_lambda[j],
                             lru_w_out[j])
        else:
            x = _moba_layer(x, mod, norm_mix_g[i], attn_w_qkv[j], attn_q_norm_g[j],
                            attn_k_norm_g[j], attn_w_out[j])
        x, xs_buf = _moe_layer(x, xs_buf, mod, norm_ffn_g[i], router_w[i], router_b[i],
                               expert_w_gu, expert_b_gu, expert_w_down, expert_b_down, i)
    return x
```

```python
import functools

import jax
import jax.numpy as jnp
from jax import lax
from jax.experimental import pallas as pl
from jax.experimental.pallas import tpu as pltpu

NORM_EPS = 1e-6
N_MOD = 6
LRU_BLOCKS = 4
CONV_WIDTH = 4
LRU_C = 8.0
ATTN_HEADS = 8
MOBA_BLOCK = 256
MOBA_TOPK = 3
N_EXPERTS = 32
TOP_K = 4
SWIGLU_LIMIT = 7.0
SWIGLU_ALPHA = 1.702

LANES = 128
SUBLANES = 8
MASK_BIAS = -(2.0 ** 100)
LOG2_E = 1.4426950408889634
ISSUE_UNROLL = 8
VMEM_LIMIT = 52 * 1024 * 1024

F32 = jnp.float32
BF16 = jnp.bfloat16
HIGHEST = lax.Precision.HIGHEST


def _cparams(sem):
    return pltpu.CompilerParams(dimension_semantics=sem, vmem_limit_bytes=VMEM_LIMIT)


def _sigmoid(z):
    return 1.0 / (1.0 + jnp.exp(-z))


def _modulated_norm(x, g, sc, sh):
    ms = jnp.mean(x * x, axis=-1, keepdims=True)
    return x * lax.rsqrt(ms + NORM_EPS) * g * (1.0 + sc) + sh


def _mod_kernel(c_ref, w_ref, b_ref, o_ref):
    c = c_ref[...]
    cond = c * _sigmoid(c)
    o_ref[...] = jnp.dot(cond, w_ref[...], preferred_element_type=F32,
                         precision=HIGHEST) + b_ref[...]


def _modulation(c, w_mod, b_mod):
    depth, d, _ = w_mod.shape
    b = c.shape[0]
    rows = -(-b // SUBLANES) * SUBLANES
    c_pad = jnp.pad(c, ((0, rows - b), (0, 0)))
    out = pl.pallas_call(
        _mod_kernel,
        grid=(depth, N_MOD),
        in_specs=[
            pl.BlockSpec((rows, d), lambda i, j: (0, 0)),
            pl.BlockSpec((None, d, d), lambda i, j: (i, 0, j)),
            pl.BlockSpec((None, None, 1, d), lambda i, j: (i, j, 0, 0)),
        ],
        out_specs=pl.BlockSpec((None, None, rows, d), lambda i, j: (i, j, 0, 0)),
        out_shape=jax.ShapeDtypeStruct((depth, N_MOD, rows, d), F32),
        compiler_params=_cparams(("arbitrary", "arbitrary")),
    )(c_pad, w_mod, b_mod.reshape(depth, N_MOD, 1, d))
    return out[:, :, :b].reshape(depth, N_MOD, b, 1, d)


def _mod_spec(j, d):
    return pl.BlockSpec((None, None, 1, d), lambda b, s: (j, b, 0, 0))


def _rglru_kernel(x_ref, sh_ref, sc_ref, g1_ref, ng_ref, win_ref, cw_ref, cb_ref,
                  wa_ref, ba_ref, wx_ref, bx_ref, lam_ref, wout_ref, o_ref,
                  ext_ref, a_ref, u_ref, h_ref):
    ts, d = x_ref.shape
    bw = d // LRU_BLOCKS

    @pl.when(pl.program_id(1) == 0)
    def _():
        ext_ref[0:SUBLANES, :] = jnp.zeros((SUBLANES, d), F32)
        h_ref[...] = jnp.zeros_like(h_ref)

    x = x_ref[...]
    h = _modulated_norm(x, ng_ref[...], sc_ref[...], sh_ref[...])
    gr = jnp.dot(h.astype(BF16), win_ref[...], preferred_element_type=F32)
    gate_branch = gr[:, :d]
    rec = gr[:, d:]

    ext_ref[SUBLANES:, :] = rec
    xc = cb_ref[...] + cw_ref[CONV_WIDTH - 1:CONV_WIDTH, :] * rec
    for k in range(CONV_WIDTH - 1):
        off = SUBLANES - (CONV_WIDTH - 1) + k
        xc = xc + cw_ref[k:k + 1, :] * ext_ref[off:off + ts, :]
    ext_ref[0:SUBLANES, :] = rec[ts - SUBLANES:, :]

    xcb = xc.astype(BF16)
    ra = jnp.concatenate(
        [jnp.dot(xcb[:, g * bw:(g + 1) * bw], wa_ref[g], preferred_element_type=F32)
         for g in range(LRU_BLOCKS)], axis=1) + ba_ref[...]
    rx = jnp.concatenate(
        [jnp.dot(xcb[:, g * bw:(g + 1) * bw], wx_ref[g], preferred_element_type=F32)
         for g in range(LRU_BLOCKS)], axis=1) + bx_ref[...]
    r = _sigmoid(ra)
    ig = _sigmoid(rx)
    z = -lam_ref[...]
    softplus = jnp.maximum(z, 0.0) + jnp.log(1.0 + jnp.exp(-jnp.abs(z)))
    a = jnp.exp(-LRU_C * r * softplus)
    v = 1.0 - a * a
    u = jnp.where(v > 0.0, v * lax.rsqrt(v), 0.0) * (ig * xc)

    a = a.reshape(ts // SUBLANES, SUBLANES, d)
    u = u.reshape(ts // SUBLANES, SUBLANES, d)
    row = lax.broadcasted_iota(jnp.int32, a.shape, 1)
    k = 1
    while k < SUBLANES:
        a_prev = pltpu.roll(a, k, 1)
        u_prev = pltpu.roll(u, k, 1)
        m = row >= k
        u = jnp.where(m, a * u_prev + u, u)
        a = jnp.where(m, a * a_prev, a)
        k *= 2
    a_ref[...] = a.reshape(ts, d)
    u_ref[...] = u.reshape(ts, d)

    def group(j, hc):
        r0 = pl.multiple_of(j * SUBLANES, SUBLANES)
        hs = u_ref[pl.ds(r0, SUBLANES), :] + a_ref[pl.ds(r0, SUBLANES), :] * hc
        u_ref[pl.ds(r0, SUBLANES), :] = hs
        return hs[SUBLANES - 1:SUBLANES, :]

    h_ref[0:1, :] = lax.fori_loop(0, ts // SUBLANES, group, h_ref[0:1, :])

    y = jax.nn.gelu(gate_branch, approximate=True) * u_ref[...]
    out = jnp.dot(y.astype(BF16), wout_ref[...], preferred_element_type=F32)
    o_ref[...] = x + g1_ref[...] * out


def _rglru_layer(x, mod, ng, w_in, conv_w, conv_b, w_a, b_a, w_x, b_x, lam, w_out, ts=256):
    b, s, d = x.shape
    full = lambda shape: pl.BlockSpec(shape, lambda bi, si: (0,) * len(shape))
    row = lambda v: v.reshape(1, d)
    return pl.pallas_call(
        _rglru_kernel,
        grid=(b, s // ts),
        in_specs=[
            pl.BlockSpec((None, ts, d), lambda bi, si: (bi, si, 0)),
            _mod_spec(0, d), _mod_spec(1, d), _mod_spec(2, d),
            full((1, d)), full((d, 2 * d)), full((CONV_WIDTH, d)), full((1, d)),
            full(w_a.shape), full((1, d)), full(w_x.shape), full((1, d)), full((1, d)),
            full((d, d)),
        ],
        out_specs=pl.BlockSpec((None, ts, d), lambda bi, si: (bi, si, 0)),
        out_shape=jax.ShapeDtypeStruct(x.shape, F32),
        scratch_shapes=[
            pltpu.VMEM((ts + SUBLANES, d), F32),
            pltpu.VMEM((ts, d), F32),
            pltpu.VMEM((ts, d), F32),
            pltpu.VMEM((SUBLANES, d), F32),
        ],
        compiler_params=_cparams(("arbitrary", "arbitrary")),
    )(x, mod, mod, mod, row(ng), w_in.astype(BF16), conv_w, row(conv_b),
      w_a.astype(BF16), row(b_a), w_x.astype(BF16), row(b_x), row(lam), w_out.astype(BF16))


def _qkv_kernel(x_ref, sh_ref, sc_ref, ng_ref, w_ref, qg_ref, kg_ref,
                q_ref, kext_ref, v_ref, km_ref):
    ts, d = x_ref.shape
    nh, _, hd = q_ref.shape
    blk = pl.program_id(1)
    h = _modulated_norm(x_ref[...], ng_ref[...], sc_ref[...], sh_ref[...])
    qkv = jnp.dot(h.astype(BF16), w_ref[...], preferred_element_type=F32)
    lane = lax.broadcasted_iota(jnp.int32, (ts, hd), 1)
    onehot = jnp.where(lane == blk, 1.0, 0.0).astype(BF16)
    for hh in range(nh):
        qh = qkv[:, hh * hd:(hh + 1) * hd]
        qn = qh * lax.rsqrt(jnp.mean(qh * qh, axis=-1, keepdims=True) + NORM_EPS)
        q_ref[hh] = (qn * qg_ref[...] * (hd ** -0.5 * LOG2_E)).astype(BF16)
        kh = qkv[:, d + hh * hd:d + (hh + 1) * hd]
        kn = kh * lax.rsqrt(jnp.mean(kh * kh, axis=-1, keepdims=True) + NORM_EPS) * kg_ref[...]
        kext_ref[hh, :, 0:hd] = kn.astype(BF16)
        kext_ref[hh, :, hd:2 * hd] = onehot
        km_ref[:, hh * hd:(hh + 1) * hd] = jnp.mean(kn, axis=0, keepdims=True)
        v_ref[hh] = qkv[:, 2 * d + hh * hd:2 * d + (hh + 1) * hd].astype(BF16)


def _attn_select_blocks(q_ref, km_ref, bias_ref, tq):
    nhs, s, hd = q_ref.shape
    nt = (((1,), (1,)), ((), ()))
    nbp = -(-(s // tq) // SUBLANES) * SUBLANES
    blk = lax.broadcasted_iota(jnp.int32, (nbp, s), 0)
    blk_f = blk.astype(F32)
    own = lax.broadcasted_iota(jnp.int32, (nbp, s), 1) // tq
    past = blk < own
    for hh in range(nhs):
        gate = lax.dot_general(km_ref[hh], q_ref[hh].astype(F32), nt, precision=HIGHEST,
                               preferred_element_type=F32)[:nbp]
        g = jnp.where(past, gate, -jnp.inf)
        sel = blk == own
        for _ in range(MOBA_TOPK):
            m = jnp.max(g, axis=0, keepdims=True)
            idx = jnp.min(jnp.where(g == m, blk_f, float(nbp)), axis=0, keepdims=True)
            hit = blk_f == idx
            sel = sel | (hit & past)
            g = jnp.where(hit, -jnp.inf, g)
        bias_t = jnp.where(sel, 0.0, MASK_BIAS)
        pad = jnp.zeros((LANES - nbp, tq), F32)
        for c in range(s // tq):
            tile = jnp.concatenate([bias_t[:, c * tq:(c + 1) * tq], pad], axis=0).T
            bias_ref[hh, c * tq:(c + 1) * tq, :] = tile.astype(BF16)


def _attn_query_block(qi, q_ref, bias_ref, kext_ref, v_ref, o_ref, s_ref):
    nhs, tq = o_ref.shape[0], o_ref.shape[1]
    nt = (((1,), (1,)), ((), ()))
    rr = lax.broadcasted_iota(jnp.int32, (tq, tq), 0)
    cc = lax.broadcasted_iota(jnp.int32, (tq, tq), 1)
    for hh in range(nhs):
        rows = slice(qi * tq, (qi + 1) * tq)
        q_ext = jnp.concatenate([q_ref[hh, rows, :], bias_ref[hh, rows, :]], axis=1)

        mx = None
        for j in range(qi + 1):
            sj = lax.dot_general(q_ext, kext_ref[hh, j * tq:(j + 1) * tq, :], nt,
                                 preferred_element_type=F32)
            if j == qi:
                sj = jnp.where(cc <= rr, sj, MASK_BIAS)
            s_ref[hh, j * tq:(j + 1) * tq, :] = sj
            mx = sj if mx is None else jnp.maximum(mx, sj)
        m = jnp.max(mx, axis=-1, keepdims=True)

        m = jnp.broadcast_to(m, (tq, tq))
        l, acc = None, None
        for j in range(qi + 1):
            p = jnp.exp2(s_ref[hh, j * tq:(j + 1) * tq, :] - m)
            pv = jnp.dot(p.astype(BF16), v_ref[hh, j * tq:(j + 1) * tq, :],
                         preferred_element_type=F32)
            lj = jnp.sum(p, axis=-1, keepdims=True)
            l = lj if l is None else l + lj
            acc = pv if acc is None else acc + pv
        o_ref[hh] = (acc / l).astype(BF16)


def _attn_kernel(q_ref, kext_ref, v_ref, km_ref, o_ref, s_ref, bias_ref):
    tq = o_ref.shape[1]
    nb = kext_ref.shape[1] // tq

    @pl.when(pl.program_id(2) == 0)
    def _():
        _attn_select_blocks(q_ref, km_ref, bias_ref, tq)

    for qi in range(nb):
        @pl.when(pl.program_id(2) == qi)
        def _(qi=qi):
            _attn_query_block(qi, q_ref, bias_ref, kext_ref, v_ref, o_ref, s_ref)


def _attn_out_kernel(o_ref, x_ref, g1_ref, w_ref, y_ref):
    nh = o_ref.shape[0]
    o = jnp.concatenate([o_ref[hh] for hh in range(nh)], axis=1)
    out = jnp.dot(o, w_ref[...], preferred_element_type=F32)
    y_ref[...] = x_ref[...] + g1_ref[...] * out


def _moba_layer(x, mod, ng, w_qkv, q_g, k_g, w_out, ts_out=512, heads_per_step=1):
    b, s, d = x.shape
    nh = ATTN_HEADS
    hd = d // nh
    tq = MOBA_BLOCK
    nb = s // tq
    assert s % tq == 0 and nb <= LANES and hd == LANES
    full = lambda shape: pl.BlockSpec(shape, lambda bi, si: (0,) * len(shape))

    q, kext, v, km = pl.pallas_call(
        _qkv_kernel,
        grid=(b, nb),
        in_specs=[
            pl.BlockSpec((None, tq, d), lambda bi, si: (bi, si, 0)),
            _mod_spec(0, d), _mod_spec(1, d),
            full((1, d)), full((d, 3 * d)), full((1, hd)), full((1, hd)),
        ],
        out_specs=[
            pl.BlockSpec((None, nh, tq, hd), lambda bi, si: (bi, 0, si, 0)),
            pl.BlockSpec((None, nh, tq, 2 * hd), lambda bi, si: (bi, 0, si, 0)),
            pl.BlockSpec((None, nh, tq, hd), lambda bi, si: (bi, 0, si, 0)),
            pl.BlockSpec((None, None, 1, d), lambda bi, si: (bi, si, 0, 0)),
        ],
        out_shape=[
            jax.ShapeDtypeStruct((b, nh, s, hd), BF16),
            jax.ShapeDtypeStruct((b, nh, s, 2 * hd), BF16),
            jax.ShapeDtypeStruct((b, nh, s, hd), BF16),
            jax.ShapeDtypeStruct((b, nb, 1, d), F32),
        ],
        compiler_params=_cparams(("arbitrary", "arbitrary")),
    )(x, mod, mod, ng.reshape(1, d), w_qkv.astype(BF16), q_g.reshape(1, hd), k_g.reshape(1, hd))

    km = km.reshape(b, nb, nh, hd).transpose(0, 2, 1, 3)
    km = jnp.pad(km, ((0, 0), (0, 0), (0, LANES - nb), (0, 0)))

    o = pl.pallas_call(
        _attn_kernel,
        grid=(b, nh // heads_per_step, nb),
        in_specs=[
            pl.BlockSpec((None, heads_per_step, s, hd), lambda bi, hi, qi: (bi, hi, 0, 0)),
            pl.BlockSpec((None, heads_per_step, s, 2 * hd), lambda bi, hi, qi: (bi, hi, 0, 0)),
            pl.BlockSpec((None, heads_per_step, s, hd), lambda bi, hi, qi: (bi, hi, 0, 0)),
            pl.BlockSpec((None, heads_per_step, LANES, hd), lambda bi, hi, qi: (bi, hi, 0, 0)),
        ],
        out_specs=pl.BlockSpec((None, heads_per_step, tq, hd),
                               lambda bi, hi, qi: (bi, hi, qi, 0)),
        out_shape=jax.ShapeDtypeStruct((b, nh, s, hd), BF16),
        scratch_shapes=[pltpu.VMEM((heads_per_step, s, tq), F32),
                        pltpu.VMEM((heads_per_step, s, LANES), BF16)],
        compiler_params=_cparams(("arbitrary", "arbitrary", "arbitrary")),
    )(q, kext, v, km)

    return pl.pallas_call(
        _attn_out_kernel,
        grid=(b, s // ts_out),
        in_specs=[
            pl.BlockSpec((None, nh, ts_out, hd), lambda bi, si: (bi, 0, si, 0)),
            pl.BlockSpec((None, ts_out, d), lambda bi, si: (bi, si, 0)),
            _mod_spec(2, d),
            full((d, d)),
        ],
        out_specs=pl.BlockSpec((None, ts_out, d), lambda bi, si: (bi, si, 0)),
        out_shape=jax.ShapeDtypeStruct(x.shape, F32),
        compiler_params=_cparams(("arbitrary", "arbitrary")),
    )(o, x, mod, w_out.astype(BF16))


def _route_kernel(x_ref, sh_ref, sc_ref, ng_ref, rw_ref, rb_ref,
                  h_ref, e_ref, w_ref, r_ref, cnt_ref, run_ref):
    tm = x_ref.shape[0]

    @pl.when(pl.program_id(0) == 0)
    def _():
        run_ref[...] = jnp.zeros_like(run_ref)

    h = _modulated_norm(x_ref[...], ng_ref[...], sc_ref[...], sh_ref[...])
    h_ref[...] = h
    logits = jnp.dot(h, rw_ref[...], preferred_element_type=F32, precision=HIGHEST) + rb_ref[...]
    lane = lax.broadcasted_iota(jnp.int32, (tm, LANES), 1)
    lane_f = lane.astype(F32)
    lg = jnp.where(lane < N_EXPERTS, logits, -jnp.inf)
    tops, hits = [], []
    for _ in range(TOP_K):
        m = jnp.max(lg, axis=-1, keepdims=True)
        idx = jnp.min(jnp.where(lg == m, lane_f, float(LANES)), axis=-1, keepdims=True)
        hit = lane_f == idx
        tops.append((m, idx))
        hits.append(hit)
        lg = jnp.where(hit, -jnp.inf, lg)
    exps = [jnp.exp(m - tops[0][0]) for m, _ in tops]
    denom = exps[0]
    for ex in exps[1:]:
        denom = denom + ex

    member = jnp.zeros((tm, LANES), F32)
    for hit in hits:
        member = member + jnp.where(hit, 1.0, 0.0)
    rr = lax.broadcasted_iota(jnp.int32, (tm, tm), 0)
    cc = lax.broadcasted_iota(jnp.int32, (tm, tm), 1)
    earlier = jnp.where(cc < rr, 1.0, 0.0).astype(BF16)
    before = jnp.dot(earlier, member.astype(BF16), preferred_element_type=F32) + run_ref[...]

    e_out = jnp.zeros((tm, LANES), F32)
    w_out = jnp.zeros((tm, LANES), F32)
    r_out = jnp.zeros((tm, LANES), F32)
    for k in range(TOP_K):
        rank = jnp.sum(jnp.where(hits[k], before, 0.0), axis=-1, keepdims=True)
        e_out = jnp.where(lane == k, tops[k][1], e_out)
        w_out = jnp.where(lane == k, exps[k] / denom, w_out)
        r_out = jnp.where(lane == k, rank, r_out)
    e_ref[...] = e_out.astype(jnp.int32)
    w_ref[...] = w_out
    r_ref[...] = r_out.astype(jnp.int32)
    run = run_ref[...] + jnp.sum(member, axis=0, keepdims=True)
    run_ref[...] = run
    cnt_ref[...] = run.astype(jnp.int32)


def _dispatch_kernel(start_ref, e_ref, r_ref, h_ref, xs_in_hbm, xs_hbm, item_ref, sem):
    del xs_in_hbm
    tm = h_ref.shape[0]
    unroll = 32
    assert item_ref.shape[0] % unroll == 0

    @pl.when(pl.program_id(0) == 0)
    def _():
        def clear(c, _):
            for u in range(unroll):
                item_ref[c * unroll + u] = 0
            return 0
        lax.fori_loop(0, item_ref.shape[0] // unroll, clear, 0)

    base = pl.program_id(0) * tm * TOP_K

    tokens_per_iter = ISSUE_UNROLL // TOP_K
    assert tm % tokens_per_iter == 0

    def issue(g, _):
        rows = [start_ref[e_ref[0, g * ISSUE_UNROLL + u]] + r_ref[0, g * ISSUE_UNROLL + u]
                for u in range(ISSUE_UNROLL)]
        for u, row in enumerate(rows):
            item_ref[row] = base + g * ISSUE_UNROLL + u
            pltpu.make_async_copy(h_ref.at[pl.ds(g * tokens_per_iter + u // TOP_K, 1), :],
                                  xs_hbm.at[pl.ds(row, 1), :], sem).start()
        return 0

    lax.fori_loop(0, tm // tokens_per_iter, issue, 0)
    for _ in range(TOP_K):
        pltpu.make_async_copy(h_ref, xs_hbm.at[pl.ds(0, tm), :], sem).wait()


def _wait_rows(n, src_ref, dst_hbm, sem):
    p = 1
    while p <= src_ref.shape[0]:
        @pl.when((n & p) != 0)
        def _(p=p):
            pltpu.make_async_copy(src_ref.at[pl.ds(0, p), :], dst_hbm.at[pl.ds(0, p), :], sem).wait()
        p *= 2


def _expert_kernel(layer, be_ref, first_ref, nu_ref, nv_ref, wslot_ref, next_ref, item_ref,
                   xs_ref, wgu_hbm, bgu_ref, wd_hbm, bd_ref, yt_hbm,
                   wgu_f32, wd_f32, wgu_bf, wd_bf, ybuf, sems, wsems):
    i = pl.program_id(0)
    de = wd_bf.shape[0]
    slot = i % 2
    nv = nv_ref[i]

    def weight_copies(expert, ws):
        return (pltpu.make_async_copy(wgu_hbm.at[layer, expert], wgu_f32.at[ws], wsems.at[ws, 0]),
                pltpu.make_async_copy(wd_hbm.at[layer, expert], wd_f32.at[ws], wsems.at[ws, 1]))

    @pl.when(nv > 0)
    def _():
        @pl.when(first_ref[i] == 1)
        def _():
            ws = wslot_ref[i]

            @pl.when(i == 0)
            def _():
                for cp in weight_copies(be_ref[0], 0):
                    cp.start()

            for cp in weight_copies(be_ref[i], ws):
                cp.wait()

            @pl.when(next_ref[i] >= 0)
            def _():
                for cp in weight_copies(next_ref[i], 1 - ws):
                    cp.start()

            wgu_bf[...] = wgu_f32[ws].astype(BF16)
            wd_bf[...] = wd_f32[ws].astype(BF16)

        gu = jnp.dot(xs_ref[...].astype(BF16), wgu_bf[...], preferred_element_type=F32) + bgu_ref[...]
        gate = jnp.minimum(gu[:, :de], SWIGLU_LIMIT)
        up = jnp.clip(gu[:, de:], -SWIGLU_LIMIT, SWIGLU_LIMIT)
        act = gate * _sigmoid(SWIGLU_ALPHA * gate) * (up + 1.0)
        ybuf[slot] = jnp.dot(act.astype(BF16), wd_bf[...], preferred_element_type=F32) + bd_ref[...]

        def issue(r0, count):
            for u in range(count):
                pltpu.make_async_copy(ybuf.at[slot, pl.ds(r0 + u, 1), :],
                                      yt_hbm.at[pl.ds(item_ref[0, r0 + u], 1), :],
                                      sems.at[slot]).start()
            return 0

        n_groups = lax.shift_right_logical(nv, ISSUE_UNROLL.bit_length() - 1)
        lax.fori_loop(0, n_groups, lambda g, _: issue(g * ISSUE_UNROLL, ISSUE_UNROLL), 0)
        lax.fori_loop(n_groups * ISSUE_UNROLL, nv, lambda r, _: issue(r, 1), 0)

    @pl.when(i > 0)
    def _():
        _wait_rows(nv_ref[jnp.maximum(i - 1, 0)], ybuf.at[1 - slot], yt_hbm, sems.at[1 - slot])

    @pl.when(i == pl.num_programs(0) - 1)
    def _():
        _wait_rows(nv, ybuf.at[slot], yt_hbm, sems.at[slot])


def _combine_kernel(x_ref, yt_ref, w_ref, g2_ref, o_ref):
    o_ref[...] = x_ref[...] + g2_ref[...] * (w_ref[...] * yt_ref[...])


def _tok_mod_spec(j, tm, s, d):
    return pl.BlockSpec((None, None, 1, d), lambda i: (j, (i * tm) // s, 0, 0))


def _moe_route(xt, t, s, mod, ng, router_w, router_b, tm=256):
    d = xt.shape[1]
    assert t % tm == 0
    rw = jnp.pad(router_w, ((0, 0), (0, LANES - N_EXPERTS)))
    rb = jnp.pad(router_b, (0, LANES - N_EXPERTS)).reshape(1, LANES)
    return pl.pallas_call(
        _route_kernel,
        grid=(t // tm,),
        in_specs=[
            pl.BlockSpec((tm, d), lambda i: (i, 0)),
            _tok_mod_spec(3, tm, s, d), _tok_mod_spec(4, tm, s, d),
            pl.BlockSpec((1, d), lambda i: (0, 0)),
            pl.BlockSpec((d, LANES), lambda i: (0, 0)),
            pl.BlockSpec((1, LANES), lambda i: (0, 0)),
        ],
        out_specs=[
            pl.BlockSpec((tm, d), lambda i: (i, 0)),
            pl.BlockSpec((tm, LANES), lambda i: (i, 0)),
            pl.BlockSpec((tm, LANES), lambda i: (i, 0)),
            pl.BlockSpec((tm, LANES), lambda i: (i, 0)),
            pl.BlockSpec((1, LANES), lambda i: (0, 0)),
        ],
        out_shape=[
            jax.ShapeDtypeStruct((t, d), F32),
            jax.ShapeDtypeStruct((t, LANES), jnp.int32),
            jax.ShapeDtypeStruct((t, LANES), F32),
            jax.ShapeDtypeStruct((t, LANES), jnp.int32),
            jax.ShapeDtypeStruct((1, LANES), jnp.int32),
        ],
        scratch_shapes=[pltpu.VMEM((1, LANES), F32)],
        compiler_params=_cparams(("arbitrary",)),
    )(xt, mod, mod, ng.reshape(1, d), rw, rb)


def _moe_plan(t, counts, tm):
    n_e = N_EXPERTS
    counts = counts[0, :n_e]
    padded = (counts + tm - 1) // tm * tm
    pad_end = jnp.cumsum(padded)
    pad_start = pad_end - padded
    n_tiles = (t * TOP_K) // tm + n_e
    tile_start = jnp.arange(n_tiles, dtype=jnp.int32) * tm
    n_used = (pad_end[-1] // tm).astype(jnp.int32)
    tile_expert = jnp.minimum(jnp.searchsorted(pad_end, tile_start, side='right'), n_e - 1)
    tile_expert = jnp.where(jnp.arange(n_tiles) < n_used, tile_expert,
                            tile_expert[jnp.maximum(n_used - 1, 0)]).astype(jnp.int32)
    first = jnp.concatenate([jnp.ones((1,), jnp.int32),
                             (tile_expert[1:] != tile_expert[:-1]).astype(jnp.int32)])
    seg_end = (pad_start + counts)[tile_expert]
    n_valid = jnp.where(jnp.arange(n_tiles) < n_used,
                        jnp.clip(seg_end - tile_start, 0, tm), 0).astype(jnp.int32)
    used = jnp.arange(n_tiles) < n_used
    w_slot = ((jnp.cumsum(first) - 1) % 2).astype(jnp.int32)
    first_pos = jnp.where((first == 1) & used, jnp.arange(n_tiles), n_tiles)
    next_first = lax.cummin(first_pos[::-1])[::-1]
    next_first = jnp.concatenate([next_first[1:], jnp.full((1,), n_tiles)])
    next_expert = jnp.where(next_first < n_tiles,
                            tile_expert[jnp.minimum(next_first, n_tiles - 1)], -1).astype(jnp.int32)
    return (pad_start.astype(jnp.int32), tile_expert, first, n_used.reshape(1), n_valid,
            w_slot, next_expert, n_tiles)


def _moe_dispatch(h2, pad_start, top_e, rank, xs_buf, tm=256):
    t, d = h2.shape
    n_rows = xs_buf.shape[0]
    flat = lambda a: a[:, :TOP_K].reshape(t // tm, 1, tm * TOP_K)
    item_spec = pl.BlockSpec((None, 1, tm * TOP_K), lambda i: (i, 0, 0), memory_space=pltpu.SMEM)
    return pl.pallas_call(
        _dispatch_kernel,
        grid=(t // tm,),
        in_specs=[
            pl.BlockSpec(memory_space=pltpu.SMEM),
            item_spec, item_spec,
            pl.BlockSpec((tm, d), lambda i: (i, 0)),
            pl.BlockSpec(memory_space=pl.ANY),
        ],
        out_specs=[pl.BlockSpec(memory_space=pl.ANY), pl.BlockSpec(memory_space=pltpu.SMEM)],
        out_shape=[jax.ShapeDtypeStruct((n_rows, d), F32),
                   jax.ShapeDtypeStruct((n_rows,), jnp.int32)],
        scratch_shapes=[pltpu.SemaphoreType.DMA],
        input_output_aliases={4: 0},
        compiler_params=_cparams(("arbitrary",)),
    )(pad_start, flat(top_e), flat(rank), h2, xs_buf)


def _moe_experts(xs, n_items, tile_expert, first, n_used, n_valid, w_slot, next_expert, row_item,
                 w_gu, b_gu, w_down, b_down, layer, tm):
    n_rows, d = xs.shape
    n_e = N_EXPERTS
    de = w_down.shape[2]
    n_tiles = n_rows // tm
    tile_idx = lambda i, be, fi, nu, *_: (jnp.maximum(jnp.minimum(i, nu[0] - 1), 0), 0)
    b_idx = lambda i, be, *_: (layer, be[i], 0, 0)
    return pl.pallas_call(
        functools.partial(_expert_kernel, layer),
        grid_spec=pltpu.PrefetchScalarGridSpec(
            num_scalar_prefetch=6,
            grid=(n_tiles,),
            in_specs=[
                pl.BlockSpec((None, 1, tm), lambda i, *_: (i, 0, 0), memory_space=pltpu.SMEM),
                pl.BlockSpec((tm, d), tile_idx),
                pl.BlockSpec(memory_space=pl.ANY),
                pl.BlockSpec((None, None, 1, 2 * de), b_idx),
                pl.BlockSpec(memory_space=pl.ANY),
                pl.BlockSpec((None, None, 1, d), b_idx),
            ],
            out_specs=pl.BlockSpec(memory_space=pl.ANY),
            scratch_shapes=[pltpu.VMEM((2, d, 2 * de), F32), pltpu.VMEM((2, de, d), F32),
                            pltpu.VMEM((d, 2 * de), BF16), pltpu.VMEM((de, d), BF16),
                            pltpu.VMEM((2, tm, d), F32), pltpu.SemaphoreType.DMA((2,)),
                            pltpu.SemaphoreType.DMA((2, 2))],
        ),
        out_shape=jax.ShapeDtypeStruct((n_items, d), F32),
        compiler_params=_cparams(("arbitrary",)),
    )(tile_expert, first, n_used, n_valid, w_slot, next_expert, row_item.reshape(n_tiles, 1, tm),
      xs, w_gu, b_gu.reshape(b_gu.shape[0], n_e, 1, 2 * de), w_down,
      b_down.reshape(b_down.shape[0], n_e, 1, d))


def _moe_combine(xt, s, mod, w_item, yt, tm=512):
    t, d = xt.shape
    return pl.pallas_call(
        _combine_kernel,
        grid=(t // tm,),
        in_specs=[
            pl.BlockSpec((tm, d), lambda i: (i, 0)),
            pl.BlockSpec((tm, d), lambda i: (i, 0)),
            pl.BlockSpec((tm, 1), lambda i: (i, 0)),
            _tok_mod_spec(5, tm, s, d),
        ],
        out_specs=pl.BlockSpec((tm, d), lambda i: (i, 0)),
        out_shape=jax.ShapeDtypeStruct((t, d), F32),
        compiler_params=_cparams(("arbitrary",)),
    )(xt, yt, w_item, mod)


def _moe_rows(n_tokens, tm_exp=256):
    return (n_tokens // tm_exp + N_EXPERTS) * tm_exp


def _moe_layer(x, xs_buf, mod, ng, router_w, router_b, w_gu, b_gu, w_down, b_down, layer,
               tm_exp=256):
    b, s, d = x.shape
    t = b * s
    xt = x.reshape(t, d)
    assert t % TOP_K == 0
    n_tok = t // TOP_K
    h2, top_e, top_w, rank, counts = _moe_route(xt, n_tok, s, mod, ng, router_w, router_b)
    (pad_start, tile_expert, first, n_used, n_valid, w_slot, next_expert,
     n_tiles) = _moe_plan(n_tok, counts, tm_exp)
    assert xs_buf.shape[0] == n_tiles * tm_exp
    xs, row_item = _moe_dispatch(h2, pad_start, top_e, rank, xs_buf)
    yt = _moe_experts(xs, t, tile_expert, first, n_used, n_valid, w_slot, next_expert, row_item,
                      w_gu, b_gu, w_down, b_down, layer, tm_exp)
    w_item = top_w[:, :TOP_K].reshape(t, 1)
    return _moe_combine(xt, s, mod, w_item, yt).reshape(b, s, d), xs


def kernel(x, c, norm_mix_g, norm_ffn_g, w_mod, b_mod, lru_w_in, lru_conv_w, lru_conv_b, lru_w_a, lru_b_a, lru_w_x, lru_b_x, lru_lambda, lru_w_out, attn_w_qkv, attn_q_norm_g, attn_k_norm_g, attn_w_out, router_w, router_b, expert_w_gu, expert_b_gu, expert_w_down, expert_b_down):
    depth = w_mod.shape[0]
    mods = _modulation(c, w_mod, b_mod)
    n_tokens = x.shape[0] * x.shape[1]
    xs_buf = jnp.zeros((_moe_rows(n_tokens), x.shape[2]), F32)
    for i in range(depth):
        mod = mods[i]
        j = i // 2
        if i % 2 == 0:
            x = _rglru_layer(x, mod, norm_mix_g[i], lru_w_in[j], lru_conv_w[j], lru_conv_b[j],
                             lru_w_a[j], lru_b_a[j], lru_w_x[j], lru_b_x[j], lru_lambda[j],
                             lru_w_out[j])
        else:
            x = _moba_layer(x, mod, norm_mix_g[i], attn_w_qkv[j], attn_q_norm_g[j],
                            attn_k_norm_g[j], attn_w_out[j])
        x, xs_buf = _moe_layer(x, xs_buf, mod, norm_ffn_g[i], router_w[i], router_b[i],
                               expert_w_gu, expert_b_gu, expert_w_down, expert_b_down, i)
    return x
```

```python
import functools

import jax
import jax.numpy as jnp
from jax import lax
from jax.experimental import pallas as pl
from jax.experimental.pallas import tpu as pltpu

NORM_EPS = 1e-6
N_MOD = 6
LRU_BLOCKS = 4
CONV_WIDTH = 4
LRU_C = 8.0
ATTN_HEADS = 8
MOBA_BLOCK = 256
MOBA_TOPK = 3
N_EXPERTS = 32
TOP_K = 4
SWIGLU_LIMIT = 7.0
SWIGLU_ALPHA = 1.702

LANES = 128
SUBLANES = 8
MASK_BIAS = -(2.0 ** 100)
LOG2_E = 1.4426950408889634
ISSUE_UNROLL = 8
ATTN_PARTS = 4
VMEM_LIMIT = 52 * 1024 * 1024

F32 = jnp.float32
BF16 = jnp.bfloat16
HIGHEST = lax.Precision.HIGHEST


def _cparams(sem):
    return pltpu.CompilerParams(dimension_semantics=sem, vmem_limit_bytes=VMEM_LIMIT)


def _sigmoid(z):
    return 1.0 / (1.0 + jnp.exp(-z))


def _modulated_norm(x, g, sc, sh):
    ms = jnp.mean(x * x, axis=-1, keepdims=True)
    return x * lax.rsqrt(ms + NORM_EPS) * g * (1.0 + sc) + sh


def _mod_kernel(c_ref, w_ref, b_ref, o_ref):
    c = c_ref[...]
    cond = c * _sigmoid(c)
    o_ref[...] = jnp.dot(cond, w_ref[...], preferred_element_type=F32,
                         precision=HIGHEST) + b_ref[...]


def _modulation(c, w_mod, b_mod):
    depth, d, _ = w_mod.shape
    b = c.shape[0]
    rows = -(-b // SUBLANES) * SUBLANES
    c_pad = jnp.pad(c, ((0, rows - b), (0, 0)))
    out = pl.pallas_call(
        _mod_kernel,
        grid=(depth, N_MOD),
        in_specs=[
            pl.BlockSpec((rows, d), lambda i, j: (0, 0)),
            pl.BlockSpec((None, d, d), lambda i, j: (i, 0, j)),
            pl.BlockSpec((None, None, 1, d), lambda i, j: (i, j, 0, 0)),
        ],
        out_specs=pl.BlockSpec((None, None, rows, d), lambda i, j: (i, j, 0, 0)),
        out_shape=jax.ShapeDtypeStruct((depth, N_MOD, rows, d), F32),
        compiler_params=_cparams(("arbitrary", "arbitrary")),
    )(c_pad, w_mod, b_mod.reshape(depth, N_MOD, 1, d))
    return out[:, :, :b].reshape(depth, N_MOD, b, 1, d)


def _mod_spec(j, d):
    return pl.BlockSpec((None, None, 1, d), lambda b, s: (j, b, 0, 0))


def _rglru_kernel(x_ref, sh_ref, sc_ref, g1_ref, ng_ref, win_ref, cw_ref, cb_ref,
                  wa_ref, ba_ref, wx_ref, bx_ref, lam_ref, wout_ref, o_ref,
                  ext_ref, a_ref, u_ref, h_ref):
    ts, d = x_ref.shape
    bw = d // LRU_BLOCKS

    @pl.when(pl.program_id(1) == 0)
    def _():
        ext_ref[0:SUBLANES, :] = jnp.zeros((SUBLANES, d), F32)
        h_ref[...] = jnp.zeros_like(h_ref)

    x = x_ref[...]
    h = _modulated_norm(x, ng_ref[...], sc_ref[...], sh_ref[...])
    gr = jnp.dot(h.astype(BF16), win_ref[...], preferred_element_type=F32)
    gate_branch = gr[:, :d]
    rec = gr[:, d:]

    ext_ref[SUBLANES:, :] = rec
    xc = cb_ref[...] + cw_ref[CONV_WIDTH - 1:CONV_WIDTH, :] * rec
    for k in range(CONV_WIDTH - 1):
        off = SUBLANES - (CONV_WIDTH - 1) + k
        xc = xc + cw_ref[k:k + 1, :] * ext_ref[off:off + ts, :]
    ext_ref[0:SUBLANES, :] = rec[ts - SUBLANES:, :]

    xcb = xc.astype(BF16)
    ra = jnp.concatenate(
        [jnp.dot(xcb[:, g * bw:(g + 1) * bw], wa_ref[g], preferred_element_type=F32)
         for g in range(LRU_BLOCKS)], axis=1) + ba_ref[...]
    rx = jnp.concatenate(
        [jnp.dot(xcb[:, g * bw:(g + 1) * bw], wx_ref[g], preferred_element_type=F32)
         for g in range(LRU_BLOCKS)], axis=1) + bx_ref[...]
    r = _sigmoid(ra)
    ig = _sigmoid(rx)
    z = -lam_ref[...]
    softplus = jnp.maximum(z, 0.0) + jnp.log(1.0 + jnp.exp(-jnp.abs(z)))
    a = jnp.exp(-LRU_C * r * softplus)
    v = 1.0 - a * a
    u = jnp.where(v > 0.0, v * lax.rsqrt(v), 0.0) * (ig * xc)

    a = a.reshape(ts // SUBLANES, SUBLANES, d)
    u = u.reshape(ts // SUBLANES, SUBLANES, d)
    row = lax.broadcasted_iota(jnp.int32, a.shape, 1)
    k = 1
    while k < SUBLANES:
        a_prev = pltpu.roll(a, k, 1)
        u_prev = pltpu.roll(u, k, 1)
        m = row >= k
        u = jnp.where(m, a * u_prev + u, u)
        a = jnp.where(m, a * a_prev, a)
        k *= 2
    a_ref[...] = a.reshape(ts, d)
    u_ref[...] = u.reshape(ts, d)

    def group(j, hc):
        r0 = pl.multiple_of(j * SUBLANES, SUBLANES)
        hs = u_ref[pl.ds(r0, SUBLANES), :] + a_ref[pl.ds(r0, SUBLANES), :] * hc
        u_ref[pl.ds(r0, SUBLANES), :] = hs
        return hs[SUBLANES - 1:SUBLANES, :]

    h_ref[0:1, :] = lax.fori_loop(0, ts // SUBLANES, group, h_ref[0:1, :])

    y = jax.nn.gelu(gate_branch, approximate=True) * u_ref[...]
    out = jnp.dot(y.astype(BF16), wout_ref[...], preferred_element_type=F32)
    o_ref[...] = x + g1_ref[...] * out


def _rglru_layer(x, mod, ng, w_in, conv_w, conv_b, w_a, b_a, w_x, b_x, lam, w_out, ts=256):
    b, s, d = x.shape
    full = lambda shape: pl.BlockSpec(shape, lambda bi, si: (0,) * len(shape))
    row = lambda v: v.reshape(1, d)
    return pl.pallas_call(
        _rglru_kernel,
        grid=(b, s // ts),
        in_specs=[
            pl.BlockSpec((None, ts, d), lambda bi, si: (bi, si, 0)),
            _mod_spec(0, d), _mod_spec(1, d), _mod_spec(2, d),
            full((1, d)), full((d, 2 * d)), full((CONV_WIDTH, d)), full((1, d)),
            full(w_a.shape), full((1, d)), full(w_x.shape), full((1, d)), full((1, d)),
            full((d, d)),
        ],
        out_specs=pl.BlockSpec((None, ts, d), lambda bi, si: (bi, si, 0)),
        out_shape=jax.ShapeDtypeStruct(x.shape, F32),
        scratch_shapes=[
            pltpu.VMEM((ts + SUBLANES, d), F32),
            pltpu.VMEM((ts, d), F32),
            pltpu.VMEM((ts, d), F32),
            pltpu.VMEM((SUBLANES, d), F32),
        ],
        compiler_params=_cparams(("arbitrary", "arbitrary")),
    )(x, mod, mod, mod, row(ng), w_in.astype(BF16), conv_w, row(conv_b),
      w_a.astype(BF16), row(b_a), w_x.astype(BF16), row(b_x), row(lam), w_out.astype(BF16))


def _qkv_kernel(x_ref, sh_ref, sc_ref, ng_ref, w_ref, qg_ref, kg_ref,
                q_ref, kext_ref, v_ref, km_ref):
    ts, d = x_ref.shape
    nh, _, hd = q_ref.shape
    blk = pl.program_id(1)
    h = _modulated_norm(x_ref[...], ng_ref[...], sc_ref[...], sh_ref[...])
    qkv = jnp.dot(h.astype(BF16), w_ref[...], preferred_element_type=F32)
    lane = lax.broadcasted_iota(jnp.int32, (ts, hd), 1)
    onehot = jnp.where(lane == blk, 1.0, 0.0).astype(BF16)
    for hh in range(nh):
        qh = qkv[:, hh * hd:(hh + 1) * hd]
        qn = qh * lax.rsqrt(jnp.mean(qh * qh, axis=-1, keepdims=True) + NORM_EPS)
        q_ref[hh] = (qn * qg_ref[...] * (hd ** -0.5 * LOG2_E)).astype(BF16)
        kh = qkv[:, d + hh * hd:d + (hh + 1) * hd]
        kn = kh * lax.rsqrt(jnp.mean(kh * kh, axis=-1, keepdims=True) + NORM_EPS) * kg_ref[...]
        kext_ref[hh, :, 0:hd] = kn.astype(BF16)
        kext_ref[hh, :, hd:2 * hd] = onehot
        km_ref[:, hh * hd:(hh + 1) * hd] = jnp.mean(kn, axis=0, keepdims=True)
        v_ref[hh] = qkv[:, 2 * d + hh * hd:2 * d + (hh + 1) * hd].astype(BF16)


def _attn_select_blocks(q_ref, km_ref, bias_ref, tq):
    nhs, s, hd = q_ref.shape
    nt = (((1,), (1,)), ((), ()))
    nbp = -(-(s // tq) // SUBLANES) * SUBLANES
    blk = lax.broadcasted_iota(jnp.int32, (nbp, s), 0)
    blk_f = blk.astype(F32)
    own = lax.broadcasted_iota(jnp.int32, (nbp, s), 1) // tq
    past = blk < own
    for hh in range(nhs):
        gate = lax.dot_general(km_ref[hh], q_ref[hh].astype(F32), nt, precision=HIGHEST,
                               preferred_element_type=F32)[:nbp]
        g = jnp.where(past, gate, -jnp.inf)
        sel = blk == own
        for _ in range(MOBA_TOPK):
            m = jnp.max(g, axis=0, keepdims=True)
            idx = jnp.min(jnp.where(g == m, blk_f, float(nbp)), axis=0, keepdims=True)
            hit = blk_f == idx
            sel = sel | (hit & past)
            g = jnp.where(hit, -jnp.inf, g)
        bias_t = jnp.where(sel, 0.0, MASK_BIAS)
        pad = jnp.zeros((LANES - nbp, tq), F32)
        for c in range(s // tq):
            tile = jnp.concatenate([bias_t[:, c * tq:(c + 1) * tq], pad], axis=0).T
            bias_ref[hh, c * tq:(c + 1) * tq, :] = tile.astype(BF16)


def _attn_query_block(qi, q_ref, bias_ref, kext_ref, v_ref, o_ref, s_ref):
    nhs, tq = s_ref.shape[1], s_ref.shape[3]
    buf = qi % s_ref.shape[0]
    nt = (((1,), (1,)), ((), ()))
    rr = lax.broadcasted_iota(jnp.int32, (tq, tq), 0)
    cc = lax.broadcasted_iota(jnp.int32, (tq, tq), 1)
    for hh in range(nhs):
        rows = slice(qi * tq, (qi + 1) * tq)
        q_ext = jnp.concatenate([q_ref[hh, rows, :], bias_ref[hh, rows, :]], axis=1)

        mx = None
        for j in range(qi + 1):
            sj = lax.dot_general(q_ext, kext_ref[hh, j * tq:(j + 1) * tq, :], nt,
                                 preferred_element_type=F32)
            if j == qi:
                sj = jnp.where(cc <= rr, sj, MASK_BIAS)
            s_ref[buf, hh, j * tq:(j + 1) * tq, :] = sj
            mx = sj if mx is None else jnp.maximum(mx, sj)
        m = jnp.max(mx, axis=-1, keepdims=True)

        m = jnp.broadcast_to(m, (tq, tq))
        l, acc = None, None
        for j in range(qi + 1):
            p = jnp.exp2(s_ref[buf, hh, j * tq:(j + 1) * tq, :] - m)
            pv = jnp.dot(p.astype(BF16), v_ref[hh, j * tq:(j + 1) * tq, :],
                         preferred_element_type=F32)
            lj = jnp.sum(p, axis=-1, keepdims=True)
            l = lj if l is None else l + lj
            acc = pv if acc is None else acc + pv
        o_ref[hh, rows, :] = (acc / l).astype(BF16)


def _attn_kernel(q_ref, kext_ref, v_ref, km_ref, o_ref, s_ref, bias_ref):
    tq = s_ref.shape[3]
    nb = kext_ref.shape[1] // tq
    total = nb * (nb + 1) // 2
    bounds = [0]
    for part in range(1, ATTN_PARTS):
        qi = bounds[-1]
        while qi < nb and qi * (qi + 1) // 2 < total * part // ATTN_PARTS:
            qi += 1
        bounds.append(qi)
    bounds.append(nb)
    for part in range(ATTN_PARTS):
        @pl.when(pl.program_id(2) == part)
        def _(part=part):
            if part == 0:
                _attn_select_blocks(q_ref, km_ref, bias_ref, tq)
            for qi in range(bounds[part], bounds[part + 1]):
                _attn_query_block(qi, q_ref, bias_ref, kext_ref, v_ref, o_ref, s_ref)


def _attn_out_kernel(o_ref, x_ref, g1_ref, w_ref, y_ref):
    nh = o_ref.shape[0]
    o = jnp.concatenate([o_ref[hh] for hh in range(nh)], axis=1)
    out = jnp.dot(o, w_ref[...], preferred_element_type=F32)
    y_ref[...] = x_ref[...] + g1_ref[...] * out


def _moba_layer(x, mod, ng, w_qkv, q_g, k_g, w_out, ts_out=512, heads_per_step=1):
    b, s, d = x.shape
    nh = ATTN_HEADS
    hd = d // nh
    tq = MOBA_BLOCK
    nb = s // tq
    assert s % tq == 0 and nb <= LANES and hd == LANES
    full = lambda shape: pl.BlockSpec(shape, lambda bi, si: (0,) * len(shape))

    q, kext, v, km = pl.pallas_call(
        _qkv_kernel,
        grid=(b, nb),
        in_specs=[
            pl.BlockSpec((None, tq, d), lambda bi, si: (bi, si, 0)),
            _mod_spec(0, d), _mod_spec(1, d),
            full((1, d)), full((d, 3 * d)), full((1, hd)), full((1, hd)),
        ],
        out_specs=[
            pl.BlockSpec((None, nh, tq, hd), lambda bi, si: (bi, 0, si, 0)),
            pl.BlockSpec((None, nh, tq, 2 * hd), lambda bi, si: (bi, 0, si, 0)),
            pl.BlockSpec((None, nh, tq, hd), lambda bi, si: (bi, 0, si, 0)),
            pl.BlockSpec((None, None, 1, d), lambda bi, si: (bi, si, 0, 0)),
        ],
        out_shape=[
            jax.ShapeDtypeStruct((b, nh, s, hd), BF16),
            jax.ShapeDtypeStruct((b, nh, s, 2 * hd), BF16),
            jax.ShapeDtypeStruct((b, nh, s, hd), BF16),
            jax.ShapeDtypeStruct((b, nb, 1, d), F32),
        ],
        compiler_params=_cparams(("arbitrary", "arbitrary")),
    )(x, mod, mod, ng.reshape(1, d), w_qkv.astype(BF16), q_g.reshape(1, hd), k_g.reshape(1, hd))

    km = km.reshape(b, nb, nh, hd).transpose(0, 2, 1, 3)
    km = jnp.pad(km, ((0, 0), (0, 0), (0, LANES - nb), (0, 0)))

    o = pl.pallas_call(
        _attn_kernel,
        grid=(b, nh // heads_per_step, ATTN_PARTS),
        in_specs=[
            pl.BlockSpec((None, heads_per_step, s, hd), lambda bi, hi, pi: (bi, hi, 0, 0)),
            pl.BlockSpec((None, heads_per_step, s, 2 * hd), lambda bi, hi, pi: (bi, hi, 0, 0)),
            pl.BlockSpec((None, heads_per_step, s, hd), lambda bi, hi, pi: (bi, hi, 0, 0)),
            pl.BlockSpec((None, heads_per_step, LANES, hd), lambda bi, hi, pi: (bi, hi, 0, 0)),
        ],
        out_specs=pl.BlockSpec((None, heads_per_step, s, hd), lambda bi, hi, pi: (bi, hi, 0, 0)),
        out_shape=jax.ShapeDtypeStruct((b, nh, s, hd), BF16),
        scratch_shapes=[pltpu.VMEM((2, heads_per_step, s, tq), F32),
                        pltpu.VMEM((heads_per_step, s, LANES), BF16)],
        compiler_params=_cparams(("arbitrary", "arbitrary", "arbitrary")),
    )(q, kext, v, km)

    return pl.pallas_call(
        _attn_out_kernel,
        grid=(b, s // ts_out),
        in_specs=[
            pl.BlockSpec((None, nh, ts_out, hd), lambda bi, si: (bi, 0, si, 0)),
            pl.BlockSpec((None, ts_out, d), lambda bi, si: (bi, si, 0)),
            _mod_spec(2, d),
            full((d, d)),
        ],
        out_specs=pl.BlockSpec((None, ts_out, d), lambda bi, si: (bi, si, 0)),
        out_shape=jax.ShapeDtypeStruct(x.shape, F32),
        compiler_params=_cparams(("arbitrary", "arbitrary")),
    )(o, x, mod, w_out.astype(BF16))


def _route_kernel(x_ref, sh_ref, sc_ref, ng_ref, rw_ref, rb_ref,
                  h_ref, e_ref, w_ref, r_ref, cnt_ref, run_ref):
    tm = x_ref.shape[0]

    @pl.when(pl.program_id(0) == 0)
    def _():
        run_ref[...] = jnp.zeros_like(run_ref)

    h = _modulated_norm(x_ref[...], ng_ref[...], sc_ref[...], sh_ref[...])
    h_ref[...] = h
    logits = jnp.dot(h, rw_ref[...], preferred_element_type=F32, precision=HIGHEST) + rb_ref[...]
    lane = lax.broadcasted_iota(jnp.int32, (tm, LANES), 1)
    lane_f = lane.astype(F32)
    lg = jnp.where(lane < N_EXPERTS, logits, -jnp.inf)
    tops, hits = [], []
    for _ in range(TOP_K):
        m = jnp.max(lg, axis=-1, keepdims=True)
        idx = jnp.min(jnp.where(lg == m, lane_f, float(LANES)), axis=-1, keepdims=True)
        hit = lane_f == idx
        tops.append((m, idx))
        hits.append(hit)
        lg = jnp.where(hit, -jnp.inf, lg)
    exps = [jnp.exp(m - tops[0][0]) for m, _ in tops]
    denom = exps[0]
    for ex in exps[1:]:
        denom = denom + ex

    member = jnp.zeros((tm, LANES), F32)
    for hit in hits:
        member = member + jnp.where(hit, 1.0, 0.0)
    rr = lax.broadcasted_iota(jnp.int32, (tm, tm), 0)
    cc = lax.broadcasted_iota(jnp.int32, (tm, tm), 1)
    earlier = jnp.where(cc < rr, 1.0, 0.0).astype(BF16)
    before = jnp.dot(earlier, member.astype(BF16), preferred_element_type=F32) + run_ref[...]

    e_out = jnp.zeros((tm, LANES), F32)
    w_out = jnp.zeros((tm, LANES), F32)
    r_out = jnp.zeros((tm, LANES), F32)
    for k in range(TOP_K):
        rank = jnp.sum(jnp.where(hits[k], before, 0.0), axis=-1, keepdims=True)
        e_out = jnp.where(lane == k, tops[k][1], e_out)
        w_out = jnp.where(lane == k, exps[k] / denom, w_out)
        r_out = jnp.where(lane == k, rank, r_out)
    e_ref[...] = e_out.astype(jnp.int32)
    w_ref[...] = w_out
    r_ref[...] = r_out.astype(jnp.int32)
    run = run_ref[...] + jnp.sum(member, axis=0, keepdims=True)
    run_ref[...] = run
    cnt_ref[...] = run.astype(jnp.int32)


def _dispatch_kernel(start_ref, e_ref, r_ref, h_ref, xs_in_hbm, xs_hbm, item_ref, sem):
    del xs_in_hbm
    tm = h_ref.shape[0]
    unroll = 32
    assert item_ref.shape[0] % unroll == 0

    @pl.when(pl.program_id(0) == 0)
    def _():
        def clear(c, _):
            for u in range(unroll):
                item_ref[c * unroll + u] = 0
            return 0
        lax.fori_loop(0, item_ref.shape[0] // unroll, clear, 0)

    base = pl.program_id(0) * tm * TOP_K

    tokens_per_iter = ISSUE_UNROLL // TOP_K
    assert tm % tokens_per_iter == 0

    def issue(g, _):
        rows = [start_ref[e_ref[0, g * ISSUE_UNROLL + u]] + r_ref[0, g * ISSUE_UNROLL + u]
                for u in range(ISSUE_UNROLL)]
        for u, row in enumerate(rows):
            item_ref[row] = base + g * ISSUE_UNROLL + u
            pltpu.make_async_copy(h_ref.at[pl.ds(g * tokens_per_iter + u // TOP_K, 1), :],
                                  xs_hbm.at[pl.ds(row, 1), :], sem).start()
        return 0

    lax.fori_loop(0, tm // tokens_per_iter, issue, 0)
    for _ in range(TOP_K):
        pltpu.make_async_copy(h_ref, xs_hbm.at[pl.ds(0, tm), :], sem).wait()


def _wait_rows(n, src_ref, dst_hbm, sem):
    p = 1
    while p <= src_ref.shape[0]:
        @pl.when((n & p) != 0)
        def _(p=p):
            pltpu.make_async_copy(src_ref.at[pl.ds(0, p), :], dst_hbm.at[pl.ds(0, p), :], sem).wait()
        p *= 2


def _expert_kernel(layer, be_ref, first_ref, nu_ref, nv_ref, wslot_ref, next_ref, item_ref,
                   xs_ref, wgu_hbm, bgu_ref, wd_hbm, bd_ref, yt_hbm,
                   wgu_f32, wd_f32, wgu_bf, wd_bf, ybuf, sems, wsems):
    i = pl.program_id(0)
    de = wd_bf.shape[0]
    slot = i % 2
    nv = nv_ref[i]

    def weight_copies(expert, ws):
        return (pltpu.make_async_copy(wgu_hbm.at[layer, expert], wgu_f32.at[ws], wsems.at[ws, 0]),
                pltpu.make_async_copy(wd_hbm.at[layer, expert], wd_f32.at[ws], wsems.at[ws, 1]))

    @pl.when(nv > 0)
    def _():
        @pl.when(first_ref[i] == 1)
        def _():
            ws = wslot_ref[i]

            @pl.when(i == 0)
            def _():
                for cp in weight_copies(be_ref[0], 0):
                    cp.start()

            for cp in weight_copies(be_ref[i], ws):
                cp.wait()

            @pl.when(next_ref[i] >= 0)
            def _():
                for cp in weight_copies(next_ref[i], 1 - ws):
                    cp.start()

            wgu_bf[...] = wgu_f32[ws].astype(BF16)
            wd_bf[...] = wd_f32[ws].astype(BF16)

        gu = jnp.dot(xs_ref[...].astype(BF16), wgu_bf[...], preferred_element_type=F32) + bgu_ref[...]
        gate = jnp.minimum(gu[:, :de], SWIGLU_LIMIT)
        up = jnp.clip(gu[:, de:], -SWIGLU_LIMIT, SWIGLU_LIMIT)
        act = gate * _sigmoid(SWIGLU_ALPHA * gate) * (up + 1.0)
        ybuf[slot] = jnp.dot(act.astype(BF16), wd_bf[...], preferred_element_type=F32) + bd_ref[...]

        def issue(r0, count):
            for u in range(count):
                pltpu.make_async_copy(ybuf.at[slot, pl.ds(r0 + u, 1), :],
                                      yt_hbm.at[pl.ds(item_ref[0, r0 + u], 1), :],
                                      sems.at[slot]).start()
            return 0

        n_groups = lax.shift_right_logical(nv, ISSUE_UNROLL.bit_length() - 1)
        lax.fori_loop(0, n_groups, lambda g, _: issue(g * ISSUE_UNROLL, ISSUE_UNROLL), 0)
        lax.fori_loop(n_groups * ISSUE_UNROLL, nv, lambda r, _: issue(r, 1), 0)

    @pl.when(i > 0)
    def _():
        _wait_rows(nv_ref[jnp.maximum(i - 1, 0)], ybuf.at[1 - slot], yt_hbm, sems.at[1 - slot])

    @pl.when(i == pl.num_programs(0) - 1)
    def _():
        _wait_rows(nv, ybuf.at[slot], yt_hbm, sems.at[slot])


def _combine_kernel(x_ref, yt_ref, w_ref, g2_ref, o_ref):
    o_ref[...] = x_ref[...] + g2_ref[...] * (w_ref[...] * yt_ref[...])


def _tok_mod_spec(j, tm, s, d):
    return pl.BlockSpec((None, None, 1, d), lambda i: (j, (i * tm) // s, 0, 0))


def _moe_route(xt, t, s, mod, ng, router_w, router_b, tm=256):
    d = xt.shape[1]
    assert t % tm == 0
    rw = jnp.pad(router_w, ((0, 0), (0, LANES - N_EXPERTS)))
    rb = jnp.pad(router_b, (0, LANES - N_EXPERTS)).reshape(1, LANES)
    return pl.pallas_call(
        _route_kernel,
        grid=(t // tm,),
        in_specs=[
            pl.BlockSpec((tm, d), lambda i: (i, 0)),
            _tok_mod_spec(3, tm, s, d), _tok_mod_spec(4, tm, s, d),
            pl.BlockSpec((1, d), lambda i: (0, 0)),
            pl.BlockSpec((d, LANES), lambda i: (0, 0)),
            pl.BlockSpec((1, LANES), lambda i: (0, 0)),
        ],
        out_specs=[
            pl.BlockSpec((tm, d), lambda i: (i, 0)),
            pl.BlockSpec((tm, LANES), lambda i: (i, 0)),
            pl.BlockSpec((tm, LANES), lambda i: (i, 0)),
            pl.BlockSpec((tm, LANES), lambda i: (i, 0)),
            pl.BlockSpec((1, LANES), lambda i: (0, 0)),
        ],
        out_shape=[
            jax.ShapeDtypeStruct((t, d), F32),
            jax.ShapeDtypeStruct((t, LANES), jnp.int32),
            jax.ShapeDtypeStruct((t, LANES), F32),
            jax.ShapeDtypeStruct((t, LANES), jnp.int32),
            jax.ShapeDtypeStruct((1, LANES), jnp.int32),
        ],
        scratch_shapes=[pltpu.VMEM((1, LANES), F32)],
        compiler_params=_cparams(("arbitrary",)),
    )(xt, mod, mod, ng.reshape(1, d), rw, rb)


def _moe_plan(t, counts, tm):
    n_e = N_EXPERTS
    counts = counts[0, :n_e]
    padded = (counts + tm - 1) // tm * tm
    pad_end = jnp.cumsum(padded)
    pad_start = pad_end - padded
    n_tiles = (t * TOP_K) // tm + n_e
    tile_start = jnp.arange(n_tiles, dtype=jnp.int32) * tm
    n_used = (pad_end[-1] // tm).astype(jnp.int32)
    tile_expert = jnp.minimum(jnp.searchsorted(pad_end, tile_start, side='right'), n_e - 1)
    tile_expert = jnp.where(jnp.arange(n_tiles) < n_used, tile_expert,
                            tile_expert[jnp.maximum(n_used - 1, 0)]).astype(jnp.int32)
    first = jnp.concatenate([jnp.ones((1,), jnp.int32),
                             (tile_expert[1:] != tile_expert[:-1]).astype(jnp.int32)])
    seg_end = (pad_start + counts)[tile_expert]
    n_valid = jnp.where(jnp.arange(n_tiles) < n_used,
                        jnp.clip(seg_end - tile_start, 0, tm), 0).astype(jnp.int32)
    used = jnp.arange(n_tiles) < n_used
    w_slot = ((jnp.cumsum(first) - 1) % 2).astype(jnp.int32)
    first_pos = jnp.where((first == 1) & used, jnp.arange(n_tiles), n_tiles)
    next_first = lax.cummin(first_pos[::-1])[::-1]
    next_first = jnp.concatenate([next_first[1:], jnp.full((1,), n_tiles)])
    next_expert = jnp.where(next_first < n_tiles,
                            tile_expert[jnp.minimum(next_first, n_tiles - 1)], -1).astype(jnp.int32)
    return (pad_start.astype(jnp.int32), tile_expert, first, n_used.reshape(1), n_valid,
            w_slot, next_expert, n_tiles)


def _moe_dispatch(h2, pad_start, top_e, rank, xs_buf, tm=256):
    t, d = h2.shape
    n_rows = xs_buf.shape[0]
    flat = lambda a: a[:, :TOP_K].reshape(t // tm, 1, tm * TOP_K)
    item_spec = pl.BlockSpec((None, 1, tm * TOP_K), lambda i: (i, 0, 0), memory_space=pltpu.SMEM)
    return pl.pallas_call(
        _dispatch_kernel,
        grid=(t // tm,),
        in_specs=[
            pl.BlockSpec(memory_space=pltpu.SMEM),
            item_spec, item_spec,
            pl.BlockSpec((tm, d), lambda i: (i, 0)),
            pl.BlockSpec(memory_space=pl.ANY),
        ],
        out_specs=[pl.BlockSpec(memory_space=pl.ANY), pl.BlockSpec(memory_space=pltpu.SMEM)],
        out_shape=[jax.ShapeDtypeStruct((n_rows, d), F32),
                   jax.ShapeDtypeStruct((n_rows,), jnp.int32)],
        scratch_shapes=[pltpu.SemaphoreType.DMA],
        input_output_aliases={4: 0},
        compiler_params=_cparams(("arbitrary",)),
    )(pad_start, flat(top_e), flat(rank), h2, xs_buf)


def _moe_experts(xs, n_items, tile_expert, first, n_used, n_valid, w_slot, next_expert, row_item,
                 w_gu, b_gu, w_down, b_down, layer, tm):
    n_rows, d = xs.shape
    n_e = N_EXPERTS
    de = w_down.shape[2]
    n_tiles = n_rows // tm
    tile_idx = lambda i, be, fi, nu, *_: (jnp.maximum(jnp.minimum(i, nu[0] - 1), 0), 0)
    b_idx = lambda i, be, *_: (layer, be[i], 0, 0)
    return pl.pallas_call(
        functools.partial(_expert_kernel, layer),
        grid_spec=pltpu.PrefetchScalarGridSpec(
            num_scalar_prefetch=6,
            grid=(n_tiles,),
            in_specs=[
                pl.BlockSpec((None, 1, tm), lambda i, *_: (i, 0, 0), memory_space=pltpu.SMEM),
                pl.BlockSpec((tm, d), tile_idx),
                pl.BlockSpec(memory_space=pl.ANY),
                pl.BlockSpec((None, None, 1, 2 * de), b_idx),
                pl.BlockSpec(memory_space=pl.ANY),
                pl.BlockSpec((None, None, 1, d), b_idx),
            ],
            out_specs=pl.BlockSpec(memory_space=pl.ANY),
            scratch_shapes=[pltpu.VMEM((2, d, 2 * de), F32), pltpu.VMEM((2, de, d), F32),
                            pltpu.VMEM((d, 2 * de), BF16), pltpu.VMEM((de, d), BF16),
                            pltpu.VMEM((2, tm, d), F32), pltpu.SemaphoreType.DMA((2,)),
                            pltpu.SemaphoreType.DMA((2, 2))],
        ),
        out_shape=jax.ShapeDtypeStruct((n_items, d), F32),
        compiler_params=_cparams(("arbitrary",)),
    )(tile_expert, first, n_used, n_valid, w_slot, next_expert, row_item.reshape(n_tiles, 1, tm),
      xs, w_gu, b_gu.reshape(b_gu.shape[0], n_e, 1, 2 * de), w_down,
      b_down.reshape(b_down.shape[0], n_e, 1, d))


def _moe_combine(xt, s, mod, w_item, yt, tm=512):
    t, d = xt.shape
    return pl.pallas_call(
        _combine_kernel,
        grid=(t // tm,),
        in_specs=[
            pl.BlockSpec((tm, d), lambda i: (i, 0)),
            pl.BlockSpec((tm, d), lambda i: (i, 0)),
            pl.BlockSpec((tm, 1), lambda i: (i, 0)),
            _tok_mod_spec(5, tm, s, d),
        ],
        out_specs=pl.BlockSpec((tm, d), lambda i: (i, 0)),
        out_shape=jax.ShapeDtypeStruct((t, d), F32),
        compiler_params=_cparams(("arbitrary",)),
    )(xt, yt, w_item, mod)


def _moe_rows(n_tokens, tm_exp=256):
    return (n_tokens // tm_exp + N_EXPERTS) * tm_exp


def _moe_layer(x, xs_buf, mod, ng, router_w, router_b, w_gu, b_gu, w_down, b_down, layer,
               tm_exp=256):
    b, s, d = x.shape
    t = b * s
    xt = x.reshape(t, d)
    assert t % TOP_K == 0
    n_tok = t // TOP_K
    h2, top_e, top_w, rank, counts = _moe_route(xt, n_tok, s, mod, ng, router_w, router_b)
    (pad_start, tile_expert, first, n_used, n_valid, w_slot, next_expert,
     n_tiles) = _moe_plan(n_tok, counts, tm_exp)
    assert xs_buf.shape[0] == n_tiles * tm_exp
    xs, row_item = _moe_dispatch(h2, pad_start, top_e, rank, xs_buf)
    yt = _moe_experts(xs, t, tile_expert, first, n_used, n_valid, w_slot, next_expert, row_item,
                      w_gu, b_gu, w_down, b_down, layer, tm_exp)
    w_item = top_w[:, :TOP_K].reshape(t, 1)
    return _moe_combine(xt, s, mod, w_item, yt).reshape(b, s, d), xs


def kernel(x, c, norm_mix_g, norm_ffn_g, w_mod, b_mod, lru_w_in, lru_conv_w, lru_conv_b, lru_w_a, lru_b_a, lru_w_x, lru_b_x, lru_lambda, lru_w_out, attn_w_qkv, attn_q_norm_g, attn_k_norm_g, attn_w_out, router_w, router_b, expert_w_gu, expert_b_gu, expert_w_down, expert_b_down):
    depth = w_mod.shape[0]
    mods = _modulation(c, w_mod, b_mod)
    n_tokens = x.shape[0] * x.shape[1]
    xs_buf = jnp.zeros((_moe_rows(n_tokens), x.shape[2]), F32)
    for i in range(depth):
        mod = mods[i]
        j = i // 2
        if i % 2 == 0:
            x = _rglru_layer(x, mod, norm_mix_g[i], lru_w_in[j], lru_conv_w[j], lru_conv_b[j],
                             lru_w_a[j], lru_b_a[j], lru_w_x[j], lru_b_x[j], lru_lambda[j],
                             lru_w_out[j])
        else:
            x = _moba_layer(x, mod, norm_mix_g[i], attn_w_qkv[j], attn_q_norm_g[j],
                            attn_k_norm_g[j], attn_w_out[j])
        x, xs_buf = _moe_layer(x, xs_buf, mod, norm_ffn_g[i], router_w[i], router_b[i],
                               expert_w_gu, expert_b_gu, expert_w_down, expert_b_down, i)
    return x
```

```python
import functools

import jax
import jax.numpy as jnp
from jax import lax
from jax.experimental import pallas as pl
from jax.experimental.pallas import tpu as pltpu

NORM_EPS = 1e-6
N_MOD = 6
LRU_BLOCKS = 4
CONV_WIDTH = 4
LRU_C = 8.0
ATTN_HEADS = 8
MOBA_BLOCK = 256
MOBA_TOPK = 3
N_EXPERTS = 32
TOP_K = 4
SWIGLU_LIMIT = 7.0
SWIGLU_ALPHA = 1.702

LANES = 128
SUBLANES = 8
MASK_BIAS = -(2.0 ** 100)
LOG2_E = 1.4426950408889634
ATTN_PARTS = 4
VMEM_LIMIT = 52 * 1024 * 1024

F32 = jnp.float32
BF16 = jnp.bfloat16
HIGHEST = lax.Precision.HIGHEST


def _cparams(sem):
    return pltpu.CompilerParams(dimension_semantics=sem, vmem_limit_bytes=VMEM_LIMIT)


def _sigmoid(z):
    return 1.0 / (1.0 + jnp.exp(-z))


def _modulated_norm(x, g, sc, sh):
    ms = jnp.mean(x * x, axis=-1, keepdims=True)
    return x * lax.rsqrt(ms + NORM_EPS) * g * (1.0 + sc) + sh


def _mod_kernel(c_ref, w_ref, b_ref, o_ref):
    c = c_ref[...]
    cond = c * _sigmoid(c)
    o_ref[...] = jnp.dot(cond, w_ref[...], preferred_element_type=F32,
                         precision=HIGHEST) + b_ref[...]


def _modulation(c, w_mod, b_mod):
    depth, d, _ = w_mod.shape
    b = c.shape[0]
    rows = -(-b // SUBLANES) * SUBLANES
    c_pad = jnp.pad(c, ((0, rows - b), (0, 0)))
    out = pl.pallas_call(
        _mod_kernel,
        grid=(depth, N_MOD),
        in_specs=[
            pl.BlockSpec((rows, d), lambda i, j: (0, 0)),
            pl.BlockSpec((None, d, d), lambda i, j: (i, 0, j)),
            pl.BlockSpec((None, None, 1, d), lambda i, j: (i, j, 0, 0)),
        ],
        out_specs=pl.BlockSpec((None, None, rows, d), lambda i, j: (i, j, 0, 0)),
        out_shape=jax.ShapeDtypeStruct((depth, N_MOD, rows, d), F32),
        compiler_params=_cparams(("arbitrary", "arbitrary")),
    )(c_pad, w_mod, b_mod.reshape(depth, N_MOD, 1, d))
    return out[:, :, :b].reshape(depth, N_MOD, b, 1, d)


def _mod_spec(j, d):
    return pl.BlockSpec((None, None, 1, d), lambda b, s: (j, b, 0, 0))


def _rglru_kernel(x_ref, sh_ref, sc_ref, g1_ref, ng_ref, win_ref, cw_ref, cb_ref,
                  wa_ref, ba_ref, wx_ref, bx_ref, lam_ref, wout_ref, o_ref,
                  ext_ref, a_ref, u_ref, h_ref):
    ts, d = x_ref.shape
    bw = d // LRU_BLOCKS

    @pl.when(pl.program_id(1) == 0)
    def _():
        ext_ref[0:SUBLANES, :] = jnp.zeros((SUBLANES, d), F32)
        h_ref[...] = jnp.zeros_like(h_ref)

    x = x_ref[...]
    h = _modulated_norm(x, ng_ref[...], sc_ref[...], sh_ref[...])
    gr = jnp.dot(h.astype(BF16), win_ref[...], preferred_element_type=F32)
    gate_branch = gr[:, :d]
    rec = gr[:, d:]

    ext_ref[SUBLANES:, :] = rec
    xc = cb_ref[...] + cw_ref[CONV_WIDTH - 1:CONV_WIDTH, :] * rec
    for k in range(CONV_WIDTH - 1):
        off = SUBLANES - (CONV_WIDTH - 1) + k
        xc = xc + cw_ref[k:k + 1, :] * ext_ref[off:off + ts, :]
    ext_ref[0:SUBLANES, :] = rec[ts - SUBLANES:, :]

    xcb = xc.astype(BF16)
    ra = jnp.concatenate(
        [jnp.dot(xcb[:, g * bw:(g + 1) * bw], wa_ref[g], preferred_element_type=F32)
         for g in range(LRU_BLOCKS)], axis=1) + ba_ref[...]
    rx = jnp.concatenate(
        [jnp.dot(xcb[:, g * bw:(g + 1) * bw], wx_ref[g], preferred_element_type=F32)
         for g in range(LRU_BLOCKS)], axis=1) + bx_ref[...]
    r = _sigmoid(ra)
    ig = _sigmoid(rx)
    z = -lam_ref[...]
    softplus = jnp.maximum(z, 0.0) + jnp.log(1.0 + jnp.exp(-jnp.abs(z)))
    a = jnp.exp(-LRU_C * r * softplus)
    v = 1.0 - a * a
    u = jnp.where(v > 0.0, v * lax.rsqrt(v), 0.0) * (ig * xc)

    a = a.reshape(ts // SUBLANES, SUBLANES, d)
    u = u.reshape(ts // SUBLANES, SUBLANES, d)
    row = lax.broadcasted_iota(jnp.int32, a.shape, 1)
    k = 1
    while k < SUBLANES:
        a_prev = pltpu.roll(a, k, 1)
        u_prev = pltpu.roll(u, k, 1)
        m = row >= k
        u = jnp.where(m, a * u_prev + u, u)
        a = jnp.where(m, a * a_prev, a)
        k *= 2
    a_ref[...] = a.reshape(ts, d)
    u_ref[...] = u.reshape(ts, d)

    def group(j, hc):
        r0 = pl.multiple_of(j * SUBLANES, SUBLANES)
        hs = u_ref[pl.ds(r0, SUBLANES), :] + a_ref[pl.ds(r0, SUBLANES), :] * hc
        u_ref[pl.ds(r0, SUBLANES), :] = hs
        return hs[SUBLANES - 1:SUBLANES, :]

    h_ref[0:1, :] = lax.fori_loop(0, ts // SUBLANES, group, h_ref[0:1, :])

    y = jax.nn.gelu(gate_branch, approximate=True) * u_ref[...]
    out = jnp.dot(y.astype(BF16), wout_ref[...], preferred_element_type=F32)
    o_ref[...] = x + g1_ref[...] * out


def _rglru_layer(x, mod, ng, w_in, conv_w, conv_b, w_a, b_a, w_x, b_x, lam, w_out, ts=256):
    b, s, d = x.shape
    full = lambda shape: pl.BlockSpec(shape, lambda bi, si: (0,) * len(shape))
    row = lambda v: v.reshape(1, d)
    return pl.pallas_call(
        _rglru_kernel,
        grid=(b, s // ts),
        in_specs=[
            pl.BlockSpec((None, ts, d), lambda bi, si: (bi, si, 0)),
            _mod_spec(0, d), _mod_spec(1, d), _mod_spec(2, d),
            full((1, d)), full((d, 2 * d)), full((CONV_WIDTH, d)), full((1, d)),
            full(w_a.shape), full((1, d)), full(w_x.shape), full((1, d)), full((1, d)),
            full((d, d)),
        ],
        out_specs=pl.BlockSpec((None, ts, d), lambda bi, si: (bi, si, 0)),
        out_shape=jax.ShapeDtypeStruct(x.shape, F32),
        scratch_shapes=[
            pltpu.VMEM((ts + SUBLANES, d), F32),
            pltpu.VMEM((ts, d), F32),
            pltpu.VMEM((ts, d), F32),
            pltpu.VMEM((SUBLANES, d), F32),
        ],
        compiler_params=_cparams(("arbitrary", "arbitrary")),
    )(x, mod, mod, mod, row(ng), w_in.astype(BF16), conv_w, row(conv_b),
      w_a.astype(BF16), row(b_a), w_x.astype(BF16), row(b_x), row(lam), w_out.astype(BF16))


def _qkv_kernel(x_ref, sh_ref, sc_ref, ng_ref, w_ref, qg_ref, kg_ref,
                q_ref, kext_ref, v_ref, km_ref):
    ts, d = x_ref.shape
    nh, _, hd = q_ref.shape
    blk = pl.program_id(1)
    h = _modulated_norm(x_ref[...], ng_ref[...], sc_ref[...], sh_ref[...])
    qkv = jnp.dot(h.astype(BF16), w_ref[...], preferred_element_type=F32)
    lane = lax.broadcasted_iota(jnp.int32, (ts, hd), 1)
    onehot = jnp.where(lane == blk, 1.0, 0.0).astype(BF16)
    for hh in range(nh):
        qh = qkv[:, hh * hd:(hh + 1) * hd]
        qn = qh * lax.rsqrt(jnp.mean(qh * qh, axis=-1, keepdims=True) + NORM_EPS)
        q_ref[hh] = (qn * qg_ref[...] * (hd ** -0.5 * LOG2_E)).astype(BF16)
        kh = qkv[:, d + hh * hd:d + (hh + 1) * hd]
        kn = kh * lax.rsqrt(jnp.mean(kh * kh, axis=-1, keepdims=True) + NORM_EPS) * kg_ref[...]
        kext_ref[hh, :, 0:hd] = kn.astype(BF16)
        kext_ref[hh, :, hd:2 * hd] = onehot
        km_ref[:, hh * hd:(hh + 1) * hd] = jnp.mean(kn, axis=0, keepdims=True)
        v_ref[hh] = qkv[:, 2 * d + hh * hd:2 * d + (hh + 1) * hd].astype(BF16)


def _attn_select_blocks(q_ref, km_ref, bias_ref, tq):
    nhs, s, hd = q_ref.shape
    nt = (((1,), (1,)), ((), ()))
    nbp = -(-(s // tq) // SUBLANES) * SUBLANES
    blk = lax.broadcasted_iota(jnp.int32, (nbp, s), 0)
    blk_f = blk.astype(F32)
    own = lax.broadcasted_iota(jnp.int32, (nbp, s), 1) // tq
    past = blk < own
    for hh in range(nhs):
        gate = lax.dot_general(km_ref[hh], q_ref[hh].astype(F32), nt, precision=HIGHEST,
                               preferred_element_type=F32)[:nbp]
        g = jnp.where(past, gate, -jnp.inf)
        sel = blk == own
        for _ in range(MOBA_TOPK):
            m = jnp.max(g, axis=0, keepdims=True)
            idx = jnp.min(jnp.where(g == m, blk_f, float(nbp)), axis=0, keepdims=True)
            hit = blk_f == idx
            sel = sel | (hit & past)
            g = jnp.where(hit, -jnp.inf, g)
        bias_t = jnp.where(sel, 0.0, MASK_BIAS)
        pad = jnp.zeros((LANES - nbp, tq), F32)
        for c in range(s // tq):
            tile = jnp.concatenate([bias_t[:, c * tq:(c + 1) * tq], pad], axis=0).T
            bias_ref[hh, c * tq:(c + 1) * tq, :] = tile.astype(BF16)


def _attn_query_block(qi, q_ref, bias_ref, kext_ref, v_ref, o_ref, s_ref):
    nhs, tq = s_ref.shape[1], s_ref.shape[3]
    buf = qi % s_ref.shape[0]
    nt = (((1,), (1,)), ((), ()))
    rr = lax.broadcasted_iota(jnp.int32, (tq, tq), 0)
    cc = lax.broadcasted_iota(jnp.int32, (tq, tq), 1)
    for hh in range(nhs):
        rows = slice(qi * tq, (qi + 1) * tq)
        q_ext = jnp.concatenate([q_ref[hh, rows, :], bias_ref[hh, rows, :]], axis=1)

        mx = None
        for j in range(qi + 1):
            sj = lax.dot_general(q_ext, kext_ref[hh, j * tq:(j + 1) * tq, :], nt,
                                 preferred_element_type=F32)
            if j == qi:
                sj = jnp.where(cc <= rr, sj, MASK_BIAS)
            s_ref[buf, hh, j * tq:(j + 1) * tq, :] = sj
            mx = sj if mx is None else jnp.maximum(mx, sj)
        m = jnp.max(mx, axis=-1, keepdims=True)

        m = jnp.broadcast_to(m, (tq, tq))
        l, acc = None, None
        for j in range(qi + 1):
            p = jnp.exp2(s_ref[buf, hh, j * tq:(j + 1) * tq, :] - m)
            pv = jnp.dot(p.astype(BF16), v_ref[hh, j * tq:(j + 1) * tq, :],
                         preferred_element_type=F32)
            lj = jnp.sum(p, axis=-1, keepdims=True)
            l = lj if l is None else l + lj
            acc = pv if acc is None else acc + pv
        o_ref[hh, rows, :] = (acc / l).astype(BF16)


def _attn_kernel(q_ref, kext_ref, v_ref, km_ref, o_ref, s_ref, bias_ref):
    tq = s_ref.shape[3]
    nb = kext_ref.shape[1] // tq
    total = nb * (nb + 1) // 2
    bounds = [0]
    for part in range(1, ATTN_PARTS):
        qi = bounds[-1]
        while qi < nb and qi * (qi + 1) // 2 < total * part // ATTN_PARTS:
            qi += 1
        bounds.append(qi)
    bounds.append(nb)
    for part in range(ATTN_PARTS):
        @pl.when(pl.program_id(2) == part)
        def _(part=part):
            if part == 0:
                _attn_select_blocks(q_ref, km_ref, bias_ref, tq)
            for qi in range(bounds[part], bounds[part + 1]):
                _attn_query_block(qi, q_ref, bias_ref, kext_ref, v_ref, o_ref, s_ref)


def _attn_out_kernel(o_ref, x_ref, g1_ref, w_ref, y_ref):
    nh = o_ref.shape[0]
    o = jnp.concatenate([o_ref[hh] for hh in range(nh)], axis=1)
    out = jnp.dot(o, w_ref[...], preferred_element_type=F32)
    y_ref[...] = x_ref[...] + g1_ref[...] * out


def _moba_layer(x, mod, ng, w_qkv, q_g, k_g, w_out, ts_out=512, heads_per_step=1):
    b, s, d = x.shape
    nh = ATTN_HEADS
    hd = d // nh
    tq = MOBA_BLOCK
    nb = s // tq
    assert s % tq == 0 and nb <= LANES and hd == LANES
    full = lambda shape: pl.BlockSpec(shape, lambda bi, si: (0,) * len(shape))

    q, kext, v, km = pl.pallas_call(
        _qkv_kernel,
        grid=(b, nb),
        in_specs=[
            pl.BlockSpec((None, tq, d), lambda bi, si: (bi, si, 0)),
            _mod_spec(0, d), _mod_spec(1, d),
            full((1, d)), full((d, 3 * d)), full((1, hd)), full((1, hd)),
        ],
        out_specs=[
            pl.BlockSpec((None, nh, tq, hd), lambda bi, si: (bi, 0, si, 0)),
            pl.BlockSpec((None, nh, tq, 2 * hd), lambda bi, si: (bi, 0, si, 0)),
            pl.BlockSpec((None, nh, tq, hd), lambda bi, si: (bi, 0, si, 0)),
            pl.BlockSpec((None, None, 1, d), lambda bi, si: (bi, si, 0, 0)),
        ],
        out_shape=[
            jax.ShapeDtypeStruct((b, nh, s, hd), BF16),
            jax.ShapeDtypeStruct((b, nh, s, 2 * hd), BF16),
            jax.ShapeDtypeStruct((b, nh, s, hd), BF16),
            jax.ShapeDtypeStruct((b, nb, 1, d), F32),
        ],
        compiler_params=_cparams(("arbitrary", "arbitrary")),
    )(x, mod, mod, ng.reshape(1, d), w_qkv.astype(BF16), q_g.reshape(1, hd), k_g.reshape(1, hd))

    km = km.reshape(b, nb, nh, hd).transpose(0, 2, 1, 3)
    km = jnp.pad(km, ((0, 0), (0, 0), (0, LANES - nb), (0, 0)))

    o = pl.pallas_call(
        _attn_kernel,
        grid=(b, nh // heads_per_step, ATTN_PARTS),
        in_specs=[
            pl.BlockSpec((None, heads_per_step, s, hd), lambda bi, hi, pi: (bi, hi, 0, 0)),
            pl.BlockSpec((None, heads_per_step, s, 2 * hd), lambda bi, hi, pi: (bi, hi, 0, 0)),
            pl.BlockSpec((None, heads_per_step, s, hd), lambda bi, hi, pi: (bi, hi, 0, 0)),
            pl.BlockSpec((None, heads_per_step, LANES, hd), lambda bi, hi, pi: (bi, hi, 0, 0)),
        ],
        out_specs=pl.BlockSpec((None, heads_per_step, s, hd), lambda bi, hi, pi: (bi, hi, 0, 0)),
        out_shape=jax.ShapeDtypeStruct((b, nh, s, hd), BF16),
        scratch_shapes=[pltpu.VMEM((2, heads_per_step, s, tq), F32),
                        pltpu.VMEM((heads_per_step, s, LANES), BF16)],
        compiler_params=_cparams(("arbitrary", "arbitrary", "arbitrary")),
    )(q, kext, v, km)

    return pl.pallas_call(
        _attn_out_kernel,
        grid=(b, s // ts_out),
        in_specs=[
            pl.BlockSpec((None, nh, ts_out, hd), lambda bi, si: (bi, 0, si, 0)),
            pl.BlockSpec((None, ts_out, d), lambda bi, si: (bi, si, 0)),
            _mod_spec(2, d),
            full((d, d)),
        ],
        out_specs=pl.BlockSpec((None, ts_out, d), lambda bi, si: (bi, si, 0)),
        out_shape=jax.ShapeDtypeStruct(x.shape, F32),
        compiler_params=_cparams(("arbitrary", "arbitrary")),
    )(o, x, mod, w_out.astype(BF16))


def _route_kernel(x_ref, sh_ref, sc_ref, ng_ref, rw_ref, rb_ref,
                  h_ref, e_ref, w_ref, r_ref, cnt_ref, run_ref):
    tm = x_ref.shape[0]

    @pl.when(pl.program_id(0) == 0)
    def _():
        run_ref[...] = jnp.zeros_like(run_ref)

    h = _modulated_norm(x_ref[...], ng_ref[...], sc_ref[...], sh_ref[...])
    h_ref[...] = h
    logits = jnp.dot(h, rw_ref[...], preferred_element_type=F32, precision=HIGHEST) + rb_ref[...]
    lane = lax.broadcasted_iota(jnp.int32, (tm, LANES), 1)
    lane_f = lane.astype(F32)
    lg = jnp.where(lane < N_EXPERTS, logits, -jnp.inf)
    tops, hits = [], []
    for _ in range(TOP_K):
        m = jnp.max(lg, axis=-1, keepdims=True)
        idx = jnp.min(jnp.where(lg == m, lane_f, float(LANES)), axis=-1, keepdims=True)
        hit = lane_f == idx
        tops.append((m, idx))
        hits.append(hit)
        lg = jnp.where(hit, -jnp.inf, lg)
    exps = [jnp.exp(m - tops[0][0]) for m, _ in tops]
    denom = exps[0]
    for ex in exps[1:]:
        denom = denom + ex

    member = jnp.zeros((tm, LANES), F32)
    for hit in hits:
        member = member + jnp.where(hit, 1.0, 0.0)
    rr = lax.broadcasted_iota(jnp.int32, (tm, tm), 0)
    cc = lax.broadcasted_iota(jnp.int32, (tm, tm), 1)
    earlier = jnp.where(cc < rr, 1.0, 0.0).astype(BF16)
    before = jnp.dot(earlier, member.astype(BF16), preferred_element_type=F32) + run_ref[...]

    e_out = jnp.zeros((tm, LANES), F32)
    w_out = jnp.zeros((tm, LANES), F32)
    r_out = jnp.zeros((tm, LANES), F32)
    for k in range(TOP_K):
        rank = jnp.sum(jnp.where(hits[k], before, 0.0), axis=-1, keepdims=True)
        e_out = jnp.where(lane == k, tops[k][1], e_out)
        w_out = jnp.where(lane == k, exps[k] / denom, w_out)
        r_out = jnp.where(lane == k, rank, r_out)
    e_ref[...] = e_out.astype(jnp.int32)
    w_ref[...] = w_out
    r_ref[...] = r_out.astype(jnp.int32)
    run = run_ref[...] + jnp.sum(member, axis=0, keepdims=True)
    run_ref[...] = run
    cnt_ref[...] = run.astype(jnp.int32)


def _dispatch_kernel(n_items, tile_rows, start_ref, e_ref, r_ref, h_ref, xs_in_hbm, xs_hbm,
                     item_ref, sem):
    del xs_in_hbm
    tm = h_ref.shape[0]
    unroll = 32
    assert item_ref.shape[0] % unroll == 0 and tile_rows % unroll == 0
    groups = tile_rows // unroll
    assert groups & (groups - 1) == 0

    @pl.when(pl.program_id(0) == 0)
    def _():
        def clear(c, _):
            tile = lax.shift_right_logical(c, groups.bit_length() - 1)
            dummy = n_items + (tile & 1) * tile_rows + (c & (groups - 1)) * unroll
            for u in range(unroll):
                item_ref[c * unroll + u] = dummy + u
            return 0
        lax.fori_loop(0, item_ref.shape[0] // unroll, clear, 0)

    base = pl.program_id(0) * tm * TOP_K
    for t in range(tm):
        for k in range(TOP_K):
            j = t * TOP_K + k
            row = start_ref[e_ref[0, j]] + r_ref[0, j]
            item_ref[row] = base + j
            pltpu.make_async_copy(h_ref.at[pl.ds(t, 1), :], xs_hbm.at[pl.ds(row, 1), :],
                                  sem).start()
    for _ in range(TOP_K):
        pltpu.make_async_copy(h_ref, xs_hbm.at[pl.ds(0, tm), :], sem).wait()


def _expert_kernel(layer, be_ref, first_ref, nu_ref, nv_ref, wslot_ref, next_ref, item_ref,
                   xs_ref, wgu_hbm, bgu_ref, wd_hbm, bd_ref, yt_hbm,
                   wgu_f32, wd_f32, wgu_bf, wd_bf, ybuf, sems, wsems):
    i = pl.program_id(0)
    de = wd_bf.shape[0]
    tm = xs_ref.shape[0]
    n_items = yt_hbm.shape[0] - 2 * tm
    slot = i % 2
    nv = nv_ref[i]

    @pl.when(i == 0)
    def _():
        ybuf[...] = jnp.zeros_like(ybuf)
        fills = [pltpu.make_async_copy(ybuf.at[par], yt_hbm.at[pl.ds(n_items + par * tm, tm), :],
                                       sems.at[par]) for par in range(2)]
        for cp in fills:
            cp.start()
        for cp in fills:
            cp.wait()

    def weight_copies(expert, ws):
        return (pltpu.make_async_copy(wgu_hbm.at[layer, expert], wgu_f32.at[ws], wsems.at[ws, 0]),
                pltpu.make_async_copy(wd_hbm.at[layer, expert], wd_f32.at[ws], wsems.at[ws, 1]))

    @pl.when(nv > 0)
    def _():
        @pl.when(first_ref[i] == 1)
        def _():
            ws = wslot_ref[i]

            @pl.when(i == 0)
            def _():
                for cp in weight_copies(be_ref[0], 0):
                    cp.start()

            for cp in weight_copies(be_ref[i], ws):
                cp.wait()

            @pl.when(next_ref[i] >= 0)
            def _():
                for cp in weight_copies(next_ref[i], 1 - ws):
                    cp.start()

            wgu_bf[...] = wgu_f32[ws].astype(BF16)
            wd_bf[...] = wd_f32[ws].astype(BF16)

        gu = jnp.dot(xs_ref[...].astype(BF16), wgu_bf[...], preferred_element_type=F32) + bgu_ref[...]
        gate = jnp.minimum(gu[:, :de], SWIGLU_LIMIT)
        up = jnp.clip(gu[:, de:], -SWIGLU_LIMIT, SWIGLU_LIMIT)
        act = gate * _sigmoid(SWIGLU_ALPHA * gate) * (up + 1.0)
        y = jnp.dot(act.astype(BF16), wd_bf[...], preferred_element_type=F32) + bd_ref[...]
        for par in range(2):
            @pl.when(slot == par)
            def _(par=par):
                ybuf[par] = y
                for r in range(tm):
                    pltpu.make_async_copy(ybuf.at[par, pl.ds(r, 1), :],
                                          yt_hbm.at[pl.ds(item_ref[0, r], 1), :],
                                          sems.at[par]).start()

    def wait_tile(par):
        pltpu.make_async_copy(ybuf.at[par], yt_hbm.at[pl.ds(0, tm), :], sems.at[par]).wait()

    for par in range(2):
        @pl.when((i > 0) & (slot == 1 - par) & (nv_ref[jnp.maximum(i - 1, 0)] > 0))
        def _(par=par):
            wait_tile(par)

        @pl.when((i == pl.num_programs(0) - 1) & (slot == par) & (nv > 0))
        def _(par=par):
            wait_tile(par)


def _combine_kernel(x_ref, yt_ref, w_ref, g2_ref, o_ref):
    o_ref[...] = x_ref[...] + g2_ref[...] * (w_ref[...] * yt_ref[...])


def _tok_mod_spec(j, tm, s, d):
    return pl.BlockSpec((None, None, 1, d), lambda i: (j, (i * tm) // s, 0, 0))


def _moe_route(xt, t, s, mod, ng, router_w, router_b, tm=256):
    d = xt.shape[1]
    assert t % tm == 0
    rw = jnp.pad(router_w, ((0, 0), (0, LANES - N_EXPERTS)))
    rb = jnp.pad(router_b, (0, LANES - N_EXPERTS)).reshape(1, LANES)
    return pl.pallas_call(
        _route_kernel,
        grid=(t // tm,),
        in_specs=[
            pl.BlockSpec((tm, d), lambda i: (i, 0)),
            _tok_mod_spec(3, tm, s, d), _tok_mod_spec(4, tm, s, d),
            pl.BlockSpec((1, d), lambda i: (0, 0)),
            pl.BlockSpec((d, LANES), lambda i: (0, 0)),
            pl.BlockSpec((1, LANES), lambda i: (0, 0)),
        ],
        out_specs=[
            pl.BlockSpec((tm, d), lambda i: (i, 0)),
            pl.BlockSpec((tm, LANES), lambda i: (i, 0)),
            pl.BlockSpec((tm, LANES), lambda i: (i, 0)),
            pl.BlockSpec((tm, LANES), lambda i: (i, 0)),
            pl.BlockSpec((1, LANES), lambda i: (0, 0)),
        ],
        out_shape=[
            jax.ShapeDtypeStruct((t, d), F32),
            jax.ShapeDtypeStruct((t, LANES), jnp.int32),
            jax.ShapeDtypeStruct((t, LANES), F32),
            jax.ShapeDtypeStruct((t, LANES), jnp.int32),
            jax.ShapeDtypeStruct((1, LANES), jnp.int32),
        ],
        scratch_shapes=[pltpu.VMEM((1, LANES), F32)],
        compiler_params=_cparams(("arbitrary",)),
    )(xt, mod, mod, ng.reshape(1, d), rw, rb)


def _moe_plan(t, counts, tm):
    n_e = N_EXPERTS
    counts = counts[0, :n_e]
    padded = (counts + tm - 1) // tm * tm
    pad_end = jnp.cumsum(padded)
    pad_start = pad_end - padded
    n_tiles = (t * TOP_K) // tm + n_e
    tile_start = jnp.arange(n_tiles, dtype=jnp.int32) * tm
    n_used = (pad_end[-1] // tm).astype(jnp.int32)
    tile_expert = jnp.minimum(jnp.searchsorted(pad_end, tile_start, side='right'), n_e - 1)
    tile_expert = jnp.where(jnp.arange(n_tiles) < n_used, tile_expert,
                            tile_expert[jnp.maximum(n_used - 1, 0)]).astype(jnp.int32)
    first = jnp.concatenate([jnp.ones((1,), jnp.int32),
                             (tile_expert[1:] != tile_expert[:-1]).astype(jnp.int32)])
    seg_end = (pad_start + counts)[tile_expert]
    n_valid = jnp.where(jnp.arange(n_tiles) < n_used,
                        jnp.clip(seg_end - tile_start, 0, tm), 0).astype(jnp.int32)
    used = jnp.arange(n_tiles) < n_used
    w_slot = ((jnp.cumsum(first) - 1) % 2).astype(jnp.int32)
    first_pos = jnp.where((first == 1) & used, jnp.arange(n_tiles), n_tiles)
    next_first = lax.cummin(first_pos[::-1])[::-1]
    next_first = jnp.concatenate([next_first[1:], jnp.full((1,), n_tiles)])
    next_expert = jnp.where(next_first < n_tiles,
                            tile_expert[jnp.minimum(next_first, n_tiles - 1)], -1).astype(jnp.int32)
    return (pad_start.astype(jnp.int32), tile_expert, first, n_used.reshape(1), n_valid,
            w_slot, next_expert, n_tiles)


def _moe_dispatch(h2, pad_start, top_e, rank, xs_buf, tm_exp, tm=128):
    t, d = h2.shape
    n_rows = xs_buf.shape[0]
    flat = lambda a: a[:, :TOP_K].reshape(t // tm, 1, tm * TOP_K)
    item_spec = pl.BlockSpec((None, 1, tm * TOP_K), lambda i: (i, 0, 0), memory_space=pltpu.SMEM)
    return pl.pallas_call(
        functools.partial(_dispatch_kernel, t * TOP_K, tm_exp),
        grid=(t // tm,),
        in_specs=[
            pl.BlockSpec(memory_space=pltpu.SMEM),
            item_spec, item_spec,
            pl.BlockSpec((tm, d), lambda i: (i, 0)),
            pl.BlockSpec(memory_space=pl.ANY),
        ],
        out_specs=[pl.BlockSpec(memory_space=pl.ANY), pl.BlockSpec(memory_space=pltpu.SMEM)],
        out_shape=[jax.ShapeDtypeStruct((n_rows, d), F32),
                   jax.ShapeDtypeStruct((n_rows,), jnp.int32)],
        scratch_shapes=[pltpu.SemaphoreType.DMA],
        input_output_aliases={4: 0},
        compiler_params=_cparams(("arbitrary",)),
    )(pad_start, flat(top_e), flat(rank), h2, xs_buf)


def _moe_experts(xs, n_items, tile_expert, first, n_used, n_valid, w_slot, next_expert, row_item,
                 w_gu, b_gu, w_down, b_down, layer, tm):
    n_rows, d = xs.shape
    n_e = N_EXPERTS
    de = w_down.shape[2]
    n_tiles = n_rows // tm
    tile_idx = lambda i, be, fi, nu, *_: (jnp.maximum(jnp.minimum(i, nu[0] - 1), 0), 0)
    b_idx = lambda i, be, *_: (layer, be[i], 0, 0)
    return pl.pallas_call(
        functools.partial(_expert_kernel, layer),
        grid_spec=pltpu.PrefetchScalarGridSpec(
            num_scalar_prefetch=6,
            grid=(n_tiles,),
            in_specs=[
                pl.BlockSpec((None, 1, tm), lambda i, *_: (i, 0, 0), memory_space=pltpu.SMEM),
                pl.BlockSpec((tm, d), tile_idx),
                pl.BlockSpec(memory_space=pl.ANY),
                pl.BlockSpec((None, None, 1, 2 * de), b_idx),
                pl.BlockSpec(memory_space=pl.ANY),
                pl.BlockSpec((None, None, 1, d), b_idx),
            ],
            out_specs=pl.BlockSpec(memory_space=pl.ANY),
            scratch_shapes=[pltpu.VMEM((2, d, 2 * de), F32), pltpu.VMEM((2, de, d), F32),
                            pltpu.VMEM((d, 2 * de), BF16), pltpu.VMEM((de, d), BF16),
                            pltpu.VMEM((2, tm, d), F32), pltpu.SemaphoreType.DMA((2,)),
                            pltpu.SemaphoreType.DMA((2, 2))],
        ),
        out_shape=jax.ShapeDtypeStruct((n_items + 2 * tm, d), F32),
        compiler_params=_cparams(("arbitrary",)),
    )(tile_expert, first, n_used, n_valid, w_slot, next_expert, row_item.reshape(n_tiles, 1, tm),
      xs, w_gu, b_gu.reshape(b_gu.shape[0], n_e, 1, 2 * de), w_down,
      b_down.reshape(b_down.shape[0], n_e, 1, d))


def _moe_combine(xt, s, mod, w_item, yt, tm=512):
    t, d = xt.shape
    return pl.pallas_call(
        _combine_kernel,
        grid=(t // tm,),
        in_specs=[
            pl.BlockSpec((tm, d), lambda i: (i, 0)),
            pl.BlockSpec((tm, d), lambda i: (i, 0)),
            pl.BlockSpec((tm, 1), lambda i: (i, 0)),
            _tok_mod_spec(5, tm, s, d),
        ],
        out_specs=pl.BlockSpec((tm, d), lambda i: (i, 0)),
        out_shape=jax.ShapeDtypeStruct((t, d), F32),
        compiler_params=_cparams(("arbitrary",)),
    )(xt, yt, w_item, mod)


def _moe_rows(n_tokens, tm_exp=256):
    return (n_tokens // tm_exp + N_EXPERTS) * tm_exp


def _moe_layer(x, xs_buf, mod, ng, router_w, router_b, w_gu, b_gu, w_down, b_down, layer,
               tm_exp=256):
    b, s, d = x.shape
    t = b * s
    xt = x.reshape(t, d)
    assert t % TOP_K == 0
    n_tok = t // TOP_K
    h2, top_e, top_w, rank, counts = _moe_route(xt, n_tok, s, mod, ng, router_w, router_b)
    (pad_start, tile_expert, first, n_used, n_valid, w_slot, next_expert,
     n_tiles) = _moe_plan(n_tok, counts, tm_exp)
    assert xs_buf.shape[0] == n_tiles * tm_exp
    xs, row_item = _moe_dispatch(h2, pad_start, top_e, rank, xs_buf, tm_exp)
    yt = _moe_experts(xs, t, tile_expert, first, n_used, n_valid, w_slot, next_expert, row_item,
                      w_gu, b_gu, w_down, b_down, layer, tm_exp)
    w_item = top_w[:, :TOP_K].reshape(t, 1)
    return _moe_combine(xt, s, mod, w_item, yt).reshape(b, s, d), xs


def kernel(x, c, norm_mix_g, norm_ffn_g, w_mod, b_mod, lru_w_in, lru_conv_w, lru_conv_b, lru_w_a, lru_b_a, lru_w_x, lru_b_x, lru_lambda, lru_w_out, attn_w_qkv, attn_q_norm_g, attn_k_norm_g, attn_w_out, router_w, router_b, expert_w_gu, expert_b_gu, expert_w_down, expert_b_down):
    depth = w_mod.shape[0]
    mods = _modulation(c, w_mod, b_mod)
    n_tokens = x.shape[0] * x.shape[1]
    xs_buf = jnp.zeros((_moe_rows(n_tokens), x.shape[2]), F32)
    for i in range(depth):
        mod = mods[i]
        j = i // 2
        if i % 2 == 0:
            x = _rglru_layer(x, mod, norm_mix_g[i], lru_w_in[j], lru_conv_w[j], lru_conv_b[j],
                             lru_w_a[j], lru_b_a[j], lru_w_x[j], lru_b_x[j], lru_lambda[j],
                             lru_w_out[j])
        else:
            x = _moba_layer(x, mod, norm_mix_g[i], attn_w_qkv[j], attn_q_norm_g[j],
                            attn_k_norm_g[j], attn_w_out[j])
        x, xs_buf = _moe_layer(x, xs_buf, mod, norm_ffn_g[i], router_w[i], router_b[i],
                               expert_w_gu, expert_b_gu, expert_w_down, expert_b_down, i)
    return x
```

```python
import functools

import jax
import jax.numpy as jnp
from jax import lax
from jax.experimental import pallas as pl
from jax.experimental.pallas import tpu as pltpu

NORM_EPS = 1e-6
N_MOD = 6
LRU_BLOCKS = 4
CONV_WIDTH = 4
LRU_C = 8.0
ATTN_HEADS = 8
MOBA_BLOCK = 256
MOBA_TOPK = 3
N_EXPERTS = 32
TOP_K = 4
SWIGLU_LIMIT = 7.0
SWIGLU_ALPHA = 1.702

LANES = 128
SUBLANES = 8
MASK_BIAS = -(2.0 ** 100)
LOG2_E = 1.4426950408889634
ATTN_PARTS = 4
VMEM_LIMIT = 52 * 1024 * 1024

F32 = jnp.float32
BF16 = jnp.bfloat16
HIGHEST = lax.Precision.HIGHEST


def _cparams(sem):
    return pltpu.CompilerParams(dimension_semantics=sem, vmem_limit_bytes=VMEM_LIMIT)


def _sigmoid(z):
    return 1.0 / (1.0 + jnp.exp(-z))


def _modulated_norm(x, g, sc, sh):
    ms = jnp.mean(x * x, axis=-1, keepdims=True)
    return x * lax.rsqrt(ms + NORM_EPS) * g * (1.0 + sc) + sh


def _mod_kernel(c_ref, w_ref, b_ref, o_ref):
    c = c_ref[...]
    cond = c * _sigmoid(c)
    o_ref[...] = jnp.dot(cond, w_ref[...], preferred_element_type=F32,
                         precision=HIGHEST) + b_ref[...]


def _modulation(c, w_mod, b_mod):
    depth, d, _ = w_mod.shape
    b = c.shape[0]
    rows = -(-b // SUBLANES) * SUBLANES
    c_pad = jnp.pad(c, ((0, rows - b), (0, 0)))
    out = pl.pallas_call(
        _mod_kernel,
        grid=(depth, N_MOD),
        in_specs=[
            pl.BlockSpec((rows, d), lambda i, j: (0, 0)),
            pl.BlockSpec((None, d, d), lambda i, j: (i, 0, j)),
            pl.BlockSpec((None, None, 1, d), lambda i, j: (i, j, 0, 0)),
        ],
        out_specs=pl.BlockSpec((None, None, rows, d), lambda i, j: (i, j, 0, 0)),
        out_shape=jax.ShapeDtypeStruct((depth, N_MOD, rows, d), F32),
        compiler_params=_cparams(("arbitrary", "arbitrary")),
    )(c_pad, w_mod, b_mod.reshape(depth, N_MOD, 1, d))
    return out[:, :, :b].reshape(depth, N_MOD, b, 1, d)


def _mod_spec(j, d):
    return pl.BlockSpec((None, None, 1, d), lambda b, s: (j, b, 0, 0))


def _rglru_kernel(x_ref, sh_ref, sc_ref, g1_ref, ng_ref, win_ref, cw_ref, cb_ref,
                  wa_ref, ba_ref, wx_ref, bx_ref, lam_ref, wout_ref, o_ref,
                  ext_ref, a_ref, u_ref, h_ref):
    ts, d = x_ref.shape
    bw = d // LRU_BLOCKS

    @pl.when(pl.program_id(1) == 0)
    def _():
        ext_ref[0:SUBLANES, :] = jnp.zeros((SUBLANES, d), F32)
        h_ref[...] = jnp.zeros_like(h_ref)

    x = x_ref[...]
    h = _modulated_norm(x, ng_ref[...], sc_ref[...], sh_ref[...])
    gr = jnp.dot(h.astype(BF16), win_ref[...], preferred_element_type=F32)
    gate_branch = gr[:, :d]
    rec = gr[:, d:]

    ext_ref[SUBLANES:, :] = rec
    xc = cb_ref[...] + cw_ref[CONV_WIDTH - 1:CONV_WIDTH, :] * rec
    for k in range(CONV_WIDTH - 1):
        off = SUBLANES - (CONV_WIDTH - 1) + k
        xc = xc + cw_ref[k:k + 1, :] * ext_ref[off:off + ts, :]
    ext_ref[0:SUBLANES, :] = rec[ts - SUBLANES:, :]

    xcb = xc.astype(BF16)
    ra = jnp.concatenate(
        [jnp.dot(xcb[:, g * bw:(g + 1) * bw], wa_ref[g], preferred_element_type=F32)
         for g in range(LRU_BLOCKS)], axis=1) + ba_ref[...]
    rx = jnp.concatenate(
        [jnp.dot(xcb[:, g * bw:(g + 1) * bw], wx_ref[g], preferred_element_type=F32)
         for g in range(LRU_BLOCKS)], axis=1) + bx_ref[...]
    r = _sigmoid(ra)
    ig = _sigmoid(rx)
    z = -lam_ref[...]
    softplus = jnp.maximum(z, 0.0) + jnp.log(1.0 + jnp.exp(-jnp.abs(z)))
    a = jnp.exp(-LRU_C * r * softplus)
    v = 1.0 - a * a
    u = jnp.where(v > 0.0, v * lax.rsqrt(v), 0.0) * (ig * xc)

    a = a.reshape(ts // SUBLANES, SUBLANES, d)
    u = u.reshape(ts // SUBLANES, SUBLANES, d)
    row = lax.broadcasted_iota(jnp.int32, a.shape, 1)
    k = 1
    while k < SUBLANES:
        a_prev = pltpu.roll(a, k, 1)
        u_prev = pltpu.roll(u, k, 1)
        m = row >= k
        u = jnp.where(m, a * u_prev + u, u)
        a = jnp.where(m, a * a_prev, a)
        k *= 2
    a_ref[...] = a.reshape(ts, d)
    u_ref[...] = u.reshape(ts, d)

    def group(j, hc):
        r0 = pl.multiple_of(j * SUBLANES, SUBLANES)
        hs = u_ref[pl.ds(r0, SUBLANES), :] + a_ref[pl.ds(r0, SUBLANES), :] * hc
        u_ref[pl.ds(r0, SUBLANES), :] = hs
        return hs[SUBLANES - 1:SUBLANES, :]

    h_ref[0:1, :] = lax.fori_loop(0, ts // SUBLANES, group, h_ref[0:1, :])

    y = jax.nn.gelu(gate_branch, approximate=True) * u_ref[...]
    out = jnp.dot(y.astype(BF16), wout_ref[...], preferred_element_type=F32)
    o_ref[...] = x + g1_ref[...] * out


def _rglru_layer(x, mod, ng, w_in, conv_w, conv_b, w_a, b_a, w_x, b_x, lam, w_out, ts=256):
    b, s, d = x.shape
    full = lambda shape: pl.BlockSpec(shape, lambda bi, si: (0,) * len(shape))
    row = lambda v: v.reshape(1, d)
    return pl.pallas_call(
        _rglru_kernel,
        grid=(b, s // ts),
        in_specs=[
            pl.BlockSpec((None, ts, d), lambda bi, si: (bi, si, 0)),
            _mod_spec(0, d), _mod_spec(1, d), _mod_spec(2, d),
            full((1, d)), full((d, 2 * d)), full((CONV_WIDTH, d)), full((1, d)),
            full(w_a.shape), full((1, d)), full(w_x.shape), full((1, d)), full((1, d)),
            full((d, d)),
        ],
        out_specs=pl.BlockSpec((None, ts, d), lambda bi, si: (bi, si, 0)),
        out_shape=jax.ShapeDtypeStruct(x.shape, F32),
        scratch_shapes=[
            pltpu.VMEM((ts + SUBLANES, d), F32),
            pltpu.VMEM((ts, d), F32),
            pltpu.VMEM((ts, d), F32),
            pltpu.VMEM((SUBLANES, d), F32),
        ],
        compiler_params=_cparams(("arbitrary", "arbitrary")),
    )(x, mod, mod, mod, row(ng), w_in.astype(BF16), conv_w, row(conv_b),
      w_a.astype(BF16), row(b_a), w_x.astype(BF16), row(b_x), row(lam), w_out.astype(BF16))


def _qkv_kernel(x_ref, sh_ref, sc_ref, ng_ref, w_ref, qg_ref, kg_ref,
                q_ref, kext_ref, v_ref, km_ref):
    ts, d = x_ref.shape
    nh, _, hd = q_ref.shape
    blk = pl.program_id(1)
    h = _modulated_norm(x_ref[...], ng_ref[...], sc_ref[...], sh_ref[...])
    qkv = jnp.dot(h.astype(BF16), w_ref[...], preferred_element_type=F32)
    lane = lax.broadcasted_iota(jnp.int32, (ts, hd), 1)
    onehot = jnp.where(lane == blk, 1.0, 0.0).astype(BF16)
    for hh in range(nh):
        qh = qkv[:, hh * hd:(hh + 1) * hd]
        qn = qh * lax.rsqrt(jnp.mean(qh * qh, axis=-1, keepdims=True) + NORM_EPS)
        q_ref[hh] = (qn * qg_ref[...] * (hd ** -0.5 * LOG2_E)).astype(BF16)
        kh = qkv[:, d + hh * hd:d + (hh + 1) * hd]
        kn = kh * lax.rsqrt(jnp.mean(kh * kh, axis=-1, keepdims=True) + NORM_EPS) * kg_ref[...]
        kext_ref[hh, :, 0:hd] = kn.astype(BF16)
        kext_ref[hh, :, hd:2 * hd] = onehot
        km_ref[:, hh * hd:(hh + 1) * hd] = jnp.mean(kn, axis=0, keepdims=True)
        v_ref[hh] = qkv[:, 2 * d + hh * hd:2 * d + (hh + 1) * hd].astype(BF16)


def _attn_select_blocks(q_ref, km_ref, bias_ref, tq):
    nhs, s, hd = q_ref.shape
    nt = (((1,), (1,)), ((), ()))
    nbp = -(-(s // tq) // SUBLANES) * SUBLANES
    blk = lax.broadcasted_iota(jnp.int32, (nbp, s), 0)
    blk_f = blk.astype(F32)
    own = lax.broadcasted_iota(jnp.int32, (nbp, s), 1) // tq
    past = blk < own
    for hh in range(nhs):
        pieces, rest = [], km_ref[hh, 0:nbp, :]
        for _ in range(3):
            piece = rest.astype(BF16)
            pieces.append(piece)
            rest = rest - piece.astype(F32)
        gate3 = lax.dot_general(jnp.concatenate(pieces, axis=0), q_ref[hh], nt,
                                preferred_element_type=F32)
        gate = gate3[0:nbp] + gate3[nbp:2 * nbp] + gate3[2 * nbp:3 * nbp]
        g = jnp.where(past, gate, -jnp.inf)
        sel = blk == own
        for _ in range(MOBA_TOPK):
            m = jnp.max(g, axis=0, keepdims=True)
            idx = jnp.min(jnp.where(g == m, blk_f, float(nbp)), axis=0, keepdims=True)
            hit = blk_f == idx
            sel = sel | (hit & past)
            g = jnp.where(hit, -jnp.inf, g)
        bias_t = jnp.where(sel, 0.0, MASK_BIAS)
        pad = jnp.zeros((LANES - nbp, tq), F32)
        for c in range(s // tq):
            tile = jnp.concatenate([bias_t[:, c * tq:(c + 1) * tq], pad], axis=0).T
            bias_ref[hh, c * tq:(c + 1) * tq, :] = tile.astype(BF16)


def _attn_query_block_steps(qi, hh, q_ref, bias_ref, kext_ref, v_ref, o_ref, s_ref):
    tq = s_ref.shape[3]
    buf = qi % s_ref.shape[0]
    nt = (((1,), (1,)), ((), ()))
    rows = slice(qi * tq, (qi + 1) * tq)
    st = {}

    def score(j):
        if j == 0:
            st['q'] = jnp.concatenate([q_ref[hh, rows, :], bias_ref[hh, rows, :]], axis=1)
        sj = lax.dot_general(st['q'], kext_ref[hh, j * tq:(j + 1) * tq, :], nt,
                             preferred_element_type=F32)
        if j == qi:
            rr = lax.broadcasted_iota(jnp.int32, (tq, tq), 0)
            cc = lax.broadcasted_iota(jnp.int32, (tq, tq), 1)
            sj = jnp.where(cc <= rr, sj, MASK_BIAS)
        s_ref[buf, hh, j * tq:(j + 1) * tq, :] = sj
        st['mx'] = sj if j == 0 else jnp.maximum(st['mx'], sj)
        if j == qi:
            st['m'] = jnp.broadcast_to(jnp.max(st['mx'], axis=-1, keepdims=True), (tq, tq))

    def weigh(j):
        p = jnp.exp2(s_ref[buf, hh, j * tq:(j + 1) * tq, :] - st['m'])
        pv = jnp.dot(p.astype(BF16), v_ref[hh, j * tq:(j + 1) * tq, :],
                     preferred_element_type=F32)
        lj = jnp.sum(p, axis=-1, keepdims=True)
        st['l'] = lj if j == 0 else st['l'] + lj
        st['acc'] = pv if j == 0 else st['acc'] + pv
        if j == qi:
            o_ref[hh, rows, :] = (st['acc'] / st['l']).astype(BF16)

    blocks = range(qi + 1)
    return ([functools.partial(score, j) for j in blocks],
            [functools.partial(weigh, j) for j in blocks])


def _attn_query_blocks(qis, q_ref, bias_ref, kext_ref, v_ref, o_ref, s_ref):
    todo = [_attn_query_block_steps(qi, hh, q_ref, bias_ref, kext_ref, v_ref, o_ref, s_ref)
            for hh in range(s_ref.shape[1]) for qi in qis]
    pending = []
    for scores, weighs in todo + [([], [])]:
        for k in range(max(len(scores), len(pending))):
            if k < len(scores):
                scores[k]()
            if k < len(pending):
                pending[k]()
        pending = weighs


def _attn_kernel(q_ref, kext_ref, v_ref, km_ref, o_ref, s_ref, bias_ref):
    tq = s_ref.shape[3]
    nb = kext_ref.shape[1] // tq
    total = nb * (nb + 1) // 2
    bounds = [0]
    for part in range(1, ATTN_PARTS):
        qi = bounds[-1]
        while qi < nb and qi * (qi + 1) // 2 < total * part // ATTN_PARTS:
            qi += 1
        bounds.append(qi)
    bounds.append(nb)
    for part in range(ATTN_PARTS):
        @pl.when(pl.program_id(2) == part)
        def _(part=part):
            if part == 0:
                _attn_select_blocks(q_ref, km_ref, bias_ref, tq)
            _attn_query_blocks(range(bounds[part], bounds[part + 1]),
                               q_ref, bias_ref, kext_ref, v_ref, o_ref, s_ref)


def _attn_out_kernel(o_ref, x_ref, g1_ref, w_ref, y_ref):
    nh = o_ref.shape[0]
    o = jnp.concatenate([o_ref[hh] for hh in range(nh)], axis=1)
    out = jnp.dot(o, w_ref[...], preferred_element_type=F32)
    y_ref[...] = x_ref[...] + g1_ref[...] * out


def _moba_layer(x, mod, ng, w_qkv, q_g, k_g, w_out, ts_out=512, heads_per_step=1):
    b, s, d = x.shape
    nh = ATTN_HEADS
    hd = d // nh
    tq = MOBA_BLOCK
    nb = s // tq
    assert s % tq == 0 and nb <= LANES and hd == LANES
    full = lambda shape: pl.BlockSpec(shape, lambda bi, si: (0,) * len(shape))

    q, kext, v, km = pl.pallas_call(
        _qkv_kernel,
        grid=(b, nb),
        in_specs=[
            pl.BlockSpec((None, tq, d), lambda bi, si: (bi, si, 0)),
            _mod_spec(0, d), _mod_spec(1, d),
            full((1, d)), full((d, 3 * d)), full((1, hd)), full((1, hd)),
        ],
        out_specs=[
            pl.BlockSpec((None, nh, tq, hd), lambda bi, si: (bi, 0, si, 0)),
            pl.BlockSpec((None, nh, tq, 2 * hd), lambda bi, si: (bi, 0, si, 0)),
            pl.BlockSpec((None, nh, tq, hd), lambda bi, si: (bi, 0, si, 0)),
            pl.BlockSpec((None, None, 1, d), lambda bi, si: (bi, si, 0, 0)),
        ],
        out_shape=[
            jax.ShapeDtypeStruct((b, nh, s, hd), BF16),
            jax.ShapeDtypeStruct((b, nh, s, 2 * hd), BF16),
            jax.ShapeDtypeStruct((b, nh, s, hd), BF16),
            jax.ShapeDtypeStruct((b, nb, 1, d), F32),
        ],
        compiler_params=_cparams(("arbitrary", "arbitrary")),
    )(x, mod, mod, ng.reshape(1, d), w_qkv.astype(BF16), q_g.reshape(1, hd), k_g.reshape(1, hd))

    km = km.reshape(b, nb, nh, hd).transpose(0, 2, 1, 3)
    km = jnp.pad(km, ((0, 0), (0, 0), (0, LANES - nb), (0, 0)))

    o = pl.pallas_call(
        _attn_kernel,
        grid=(b, nh // heads_per_step, ATTN_PARTS),
        in_specs=[
            pl.BlockSpec((None, heads_per_step, s, hd), lambda bi, hi, pi: (bi, hi, 0, 0)),
            pl.BlockSpec((None, heads_per_step, s, 2 * hd), lambda bi, hi, pi: (bi, hi, 0, 0)),
            pl.BlockSpec((None, heads_per_step, s, hd), lambda bi, hi, pi: (bi, hi, 0, 0)),
            pl.BlockSpec((None, heads_per_step, LANES, hd), lambda bi, hi, pi: (bi, hi, 0, 0)),
        ],
        out_specs=pl.BlockSpec((None, heads_per_step, s, hd), lambda bi, hi, pi: (bi, hi, 0, 0)),
        out_shape=jax.ShapeDtypeStruct((b, nh, s, hd), BF16),
        scratch_shapes=[pltpu.VMEM((2, heads_per_step, s, tq), F32),
                        pltpu.VMEM((heads_per_step, s, LANES), BF16)],
        compiler_params=_cparams(("arbitrary", "arbitrary", "arbitrary")),
    )(q, kext, v, km)

    return pl.pallas_call(
        _attn_out_kernel,
        grid=(b, s // ts_out),
        in_specs=[
            pl.BlockSpec((None, nh, ts_out, hd), lambda bi, si: (bi, 0, si, 0)),
            pl.BlockSpec((None, ts_out, d), lambda bi, si: (bi, si, 0)),
            _mod_spec(2, d),
            full((d, d)),
        ],
        out_specs=pl.BlockSpec((None, ts_out, d), lambda bi, si: (bi, si, 0)),
        out_shape=jax.ShapeDtypeStruct(x.shape, F32),
        compiler_params=_cparams(("arbitrary", "arbitrary")),
    )(o, x, mod, w_out.astype(BF16))


def _route_kernel(x_ref, sh_ref, sc_ref, ng_ref, rw_ref, rb_ref,
                  h_ref, e_ref, w_ref, r_ref, cnt_ref, run_ref):
    tm = x_ref.shape[0]

    @pl.when(pl.program_id(0) == 0)
    def _():
        run_ref[...] = jnp.zeros_like(run_ref)

    h = _modulated_norm(x_ref[...], ng_ref[...], sc_ref[...], sh_ref[...])
    h_ref[...] = h
    logits = jnp.dot(h, rw_ref[...], preferred_element_type=F32, precision=HIGHEST) + rb_ref[...]
    lane = lax.broadcasted_iota(jnp.int32, (tm, LANES), 1)
    lane_f = lane.astype(F32)
    lg = jnp.where(lane < N_EXPERTS, logits, -jnp.inf)
    tops, hits = [], []
    for _ in range(TOP_K):
        m = jnp.max(lg, axis=-1, keepdims=True)
        idx = jnp.min(jnp.where(lg == m, lane_f, float(LANES)), axis=-1, keepdims=True)
        hit = lane_f == idx
        tops.append((m, idx))
        hits.append(hit)
        lg = jnp.where(hit, -jnp.inf, lg)
    exps = [jnp.exp(m - tops[0][0]) for m, _ in tops]
    denom = exps[0]
    for ex in exps[1:]:
        denom = denom + ex

    member = jnp.zeros((tm, LANES), F32)
    for hit in hits:
        member = member + jnp.where(hit, 1.0, 0.0)
    rr = lax.broadcasted_iota(jnp.int32, (tm, tm), 0)
    cc = lax.broadcasted_iota(jnp.int32, (tm, tm), 1)
    earlier = jnp.where(cc < rr, 1.0, 0.0).astype(BF16)
    before = jnp.dot(earlier, member.astype(BF16), preferred_element_type=F32) + run_ref[...]

    e_out = jnp.zeros((tm, LANES), F32)
    w_out = jnp.zeros((tm, LANES), F32)
    r_out = jnp.zeros((tm, LANES), F32)
    for k in range(TOP_K):
        rank = jnp.sum(jnp.where(hits[k], before, 0.0), axis=-1, keepdims=True)
        e_out = jnp.where(lane == k, tops[k][1], e_out)
        w_out = jnp.where(lane == k, exps[k] / denom, w_out)
        r_out = jnp.where(lane == k, rank, r_out)
    e_ref[...] = e_out.astype(jnp.int32)
    w_ref[...] = w_out
    r_ref[...] = r_out.astype(jnp.int32)
    run = run_ref[...] + jnp.sum(member, axis=0, keepdims=True)
    run_ref[...] = run
    cnt_ref[...] = run.astype(jnp.int32)


def _dispatch_kernel(n_items, tile_rows, start_ref, e_ref, r_ref, h_ref, xs_in_hbm, xs_hbm,
                     item_ref, sem):
    del xs_in_hbm
    tm = h_ref.shape[0]
    unroll = 32
    assert item_ref.shape[0] % unroll == 0 and tile_rows % unroll == 0
    groups = tile_rows // unroll
    assert groups & (groups - 1) == 0

    @pl.when(pl.program_id(0) == 0)
    def _():
        def clear(c, _):
            tile = lax.shift_right_logical(c, groups.bit_length() - 1)
            dummy = n_items + (tile & 1) * tile_rows + (c & (groups - 1)) * unroll
            for u in range(unroll):
                item_ref[c * unroll + u] = dummy + u
            return 0
        lax.fori_loop(0, item_ref.shape[0] // unroll, clear, 0)

    base = pl.program_id(0) * tm * TOP_K
    for t in range(tm):
        for k in range(TOP_K):
            j = t * TOP_K + k
            row = start_ref[e_ref[0, j]] + r_ref[0, j]
            item_ref[row] = base + j
            pltpu.make_async_copy(h_ref.at[pl.ds(t, 1), :], xs_hbm.at[pl.ds(row, 1), :],
                                  sem).start()
    for _ in range(TOP_K):
        pltpu.make_async_copy(h_ref, xs_hbm.at[pl.ds(0, tm), :], sem).wait()


def _expert_kernel(layer, be_ref, first_ref, nu_ref, nv_ref, wslot_ref, next_ref, item_ref,
                   xs_ref, wgu_hbm, bgu_ref, wd_hbm, bd_ref, yt_hbm,
                   wgu_f32, wd_f32, wgu_bf, wd_bf, ybuf, sems, wsems):
    i = pl.program_id(0)
    de = wd_bf.shape[0]
    tm = xs_ref.shape[0]
    n_items = yt_hbm.shape[0] - 2 * tm
    slot = i % 2
    nv = nv_ref[i]

    @pl.when(i == 0)
    def _():
        ybuf[...] = jnp.zeros_like(ybuf)
        fills = [pltpu.make_async_copy(ybuf.at[par], yt_hbm.at[pl.ds(n_items + par * tm, tm), :],
                                       sems.at[par]) for par in range(2)]
        for cp in fills:
            cp.start()
        for cp in fills:
            cp.wait()

    def weight_copies(expert, ws):
        return (pltpu.make_async_copy(wgu_hbm.at[layer, expert], wgu_f32.at[ws], wsems.at[ws, 0]),
                pltpu.make_async_copy(wd_hbm.at[layer, expert], wd_f32.at[ws], wsems.at[ws, 1]))

    @pl.when(nv > 0)
    def _():
        @pl.when(first_ref[i] == 1)
        def _():
            ws = wslot_ref[i]

            @pl.when(i == 0)
            def _():
                for cp in weight_copies(be_ref[0], 0):
                    cp.start()

            for cp in weight_copies(be_ref[i], ws):
                cp.wait()

            @pl.when(next_ref[i] >= 0)
            def _():
                for cp in weight_copies(next_ref[i], 1 - ws):
                    cp.start()

            wgu_bf[...] = wgu_f32[ws].astype(BF16)
            wd_bf[...] = wd_f32[ws].astype(BF16)

        gu = jnp.dot(xs_ref[...].astype(BF16), wgu_bf[...], preferred_element_type=F32) + bgu_ref[...]
        gate = jnp.minimum(gu[:, :de], SWIGLU_LIMIT)
        up = jnp.clip(gu[:, de:], -SWIGLU_LIMIT, SWIGLU_LIMIT)
        act = gate * _sigmoid(SWIGLU_ALPHA * gate) * (up + 1.0)
        y = jnp.dot(act.astype(BF16), wd_bf[...], preferred_element_type=F32) + bd_ref[...]
        for par in range(2):
            @pl.when(slot == par)
            def _(par=par):
                ybuf[par] = y
                for r in range(tm):
                    pltpu.make_async_copy(ybuf.at[par, pl.ds(r, 1), :],
                                          yt_hbm.at[pl.ds(item_ref[0, r], 1), :],
                                          sems.at[par]).start()

    def wait_tile(par):
        pltpu.make_async_copy(ybuf.at[par], yt_hbm.at[pl.ds(0, tm), :], sems.at[par]).wait()

    for par in range(2):
        @pl.when((i > 0) & (slot == 1 - par) & (nv_ref[jnp.maximum(i - 1, 0)] > 0))
        def _(par=par):
            wait_tile(par)

        @pl.when((i == pl.num_programs(0) - 1) & (slot == par) & (nv > 0))
        def _(par=par):
            wait_tile(par)


def _combine_kernel(x_ref, yt_ref, w_ref, g2_ref, o_ref):
    o_ref[...] = x_ref[...] + g2_ref[...] * (w_ref[...] * yt_ref[...])


def _tok_mod_spec(j, tm, s, d):
    return pl.BlockSpec((None, None, 1, d), lambda i: (j, (i * tm) // s, 0, 0))


def _moe_route(xt, t, s, mod, ng, router_w, router_b, tm=256):
    d = xt.shape[1]
    assert t % tm == 0
    rw = jnp.pad(router_w, ((0, 0), (0, LANES - N_EXPERTS)))
    rb = jnp.pad(router_b, (0, LANES - N_EXPERTS)).reshape(1, LANES)
    return pl.pallas_call(
        _route_kernel,
        grid=(t // tm,),
        in_specs=[
            pl.BlockSpec((tm, d), lambda i: (i, 0)),
            _tok_mod_spec(3, tm, s, d), _tok_mod_spec(4, tm, s, d),
            pl.BlockSpec((1, d), lambda i: (0, 0)),
            pl.BlockSpec((d, LANES), lambda i: (0, 0)),
            pl.BlockSpec((1, LANES), lambda i: (0, 0)),
        ],
        out_specs=[
            pl.BlockSpec((tm, d), lambda i: (i, 0)),
            pl.BlockSpec((tm, LANES), lambda i: (i, 0)),
            pl.BlockSpec((tm, LANES), lambda i: (i, 0)),
            pl.BlockSpec((tm, LANES), lambda i: (i, 0)),
            pl.BlockSpec((1, LANES), lambda i: (0, 0)),
        ],
        out_shape=[
            jax.ShapeDtypeStruct((t, d), F32),
            jax.ShapeDtypeStruct((t, LANES), jnp.int32),
            jax.ShapeDtypeStruct((t, LANES), F32),
            jax.ShapeDtypeStruct((t, LANES), jnp.int32),
            jax.ShapeDtypeStruct((1, LANES), jnp.int32),
        ],
        scratch_shapes=[pltpu.VMEM((1, LANES), F32)],
        compiler_params=_cparams(("arbitrary",)),
    )(xt, mod, mod, ng.reshape(1, d), rw, rb)


def _moe_plan(t, counts, tm):
    n_e = N_EXPERTS
    counts = counts[0, :n_e]
    padded = (counts + tm - 1) // tm * tm
    pad_end = jnp.cumsum(padded)
    pad_start = pad_end - padded
    n_tiles = (t * TOP_K) // tm + n_e
    tile_start = jnp.arange(n_tiles, dtype=jnp.int32) * tm
    n_used = (pad_end[-1] // tm).astype(jnp.int32)
    tile_expert = jnp.minimum(jnp.searchsorted(pad_end, tile_start, side='right'), n_e - 1)
    tile_expert = jnp.where(jnp.arange(n_tiles) < n_used, tile_expert,
                            tile_expert[jnp.maximum(n_used - 1, 0)]).astype(jnp.int32)
    first = jnp.concatenate([jnp.ones((1,), jnp.int32),
                             (tile_expert[1:] != tile_expert[:-1]).astype(jnp.int32)])
    seg_end = (pad_start + counts)[tile_expert]
    n_valid = jnp.where(jnp.arange(n_tiles) < n_used,
                        jnp.clip(seg_end - tile_start, 0, tm), 0).astype(jnp.int32)
    used = jnp.arange(n_tiles) < n_used
    w_slot = ((jnp.cumsum(first) - 1) % 2).astype(jnp.int32)
    first_pos = jnp.where((first == 1) & used, jnp.arange(n_tiles), n_tiles)
    next_first = lax.cummin(first_pos[::-1])[::-1]
    next_first = jnp.concatenate([next_first[1:], jnp.full((1,), n_tiles)])
    next_expert = jnp.where(next_first < n_tiles,
                            tile_expert[jnp.minimum(next_first, n_tiles - 1)], -1).astype(jnp.int32)
    return (pad_start.astype(jnp.int32), tile_expert, first, n_used.reshape(1), n_valid,
            w_slot, next_expert, n_tiles)


def _moe_dispatch(h2, pad_start, top_e, rank, xs_buf, tm_exp, tm=128):
    t, d = h2.shape
    n_rows = xs_buf.shape[0]
    flat = lambda a: a[:, :TOP_K].reshape(t // tm, 1, tm * TOP_K)
    item_spec = pl.BlockSpec((None, 1, tm * TOP_K), lambda i: (i, 0, 0), memory_space=pltpu.SMEM)
    return pl.pallas_call(
        functools.partial(_dispatch_kernel, t * TOP_K, tm_exp),
        grid=(t // tm,),
        in_specs=[
            pl.BlockSpec(memory_space=pltpu.SMEM),
            item_spec, item_spec,
            pl.BlockSpec((tm, d), lambda i: (i, 0)),
            pl.BlockSpec(memory_space=pl.ANY),
        ],
        out_specs=[pl.BlockSpec(memory_space=pl.ANY), pl.BlockSpec(memory_space=pltpu.SMEM)],
        out_shape=[jax.ShapeDtypeStruct((n_rows, d), F32),
                   jax.ShapeDtypeStruct((n_rows,), jnp.int32)],
        scratch_shapes=[pltpu.SemaphoreType.DMA],
        input_output_aliases={4: 0},
        compiler_params=_cparams(("arbitrary",)),
    )(pad_start, flat(top_e), flat(rank), h2, xs_buf)


def _moe_experts(xs, n_items, tile_expert, first, n_used, n_valid, w_slot, next_expert, row_item,
                 w_gu, b_gu, w_down, b_down, layer, tm):
    n_rows, d = xs.shape
    n_e = N_EXPERTS
    de = w_down.shape[2]
    n_tiles = n_rows // tm
    tile_idx = lambda i, be, fi, nu, *_: (jnp.maximum(jnp.minimum(i, nu[0] - 1), 0), 0)
    b_idx = lambda i, be, *_: (layer, be[i], 0, 0)
    return pl.pallas_call(
        functools.partial(_expert_kernel, layer),
        grid_spec=pltpu.PrefetchScalarGridSpec(
            num_scalar_prefetch=6,
            grid=(n_tiles,),
            in_specs=[
                pl.BlockSpec((None, 1, tm), lambda i, *_: (i, 0, 0), memory_space=pltpu.SMEM),
                pl.BlockSpec((tm, d), tile_idx),
                pl.BlockSpec(memory_space=pl.ANY),
                pl.BlockSpec((None, None, 1, 2 * de), b_idx),
                pl.BlockSpec(memory_space=pl.ANY),
                pl.BlockSpec((None, None, 1, d), b_idx),
            ],
            out_specs=pl.BlockSpec(memory_space=pl.ANY),
            scratch_shapes=[pltpu.VMEM((2, d, 2 * de), F32), pltpu.VMEM((2, de, d), F32),
                            pltpu.VMEM((d, 2 * de), BF16), pltpu.VMEM((de, d), BF16),
                            pltpu.VMEM((2, tm, d), F32), pltpu.SemaphoreType.DMA((2,)),
                            pltpu.SemaphoreType.DMA((2, 2))],
        ),
        out_shape=jax.ShapeDtypeStruct((n_items + 2 * tm, d), F32),
        compiler_params=_cparams(("arbitrary",)),
    )(tile_expert, first, n_used, n_valid, w_slot, next_expert, row_item.reshape(n_tiles, 1, tm),
      xs, w_gu, b_gu.reshape(b_gu.shape[0], n_e, 1, 2 * de), w_down,
      b_down.reshape(b_down.shape[0], n_e, 1, d))


def _moe_combine(xt, s, mod, w_item, yt, tm=512):
    t, d = xt.shape
    return pl.pallas_call(
        _combine_kernel,
        grid=(t // tm,),
        in_specs=[
            pl.BlockSpec((tm, d), lambda i: (i, 0)),
            pl.BlockSpec((tm, d), lambda i: (i, 0)),
            pl.BlockSpec((tm, 1), lambda i: (i, 0)),
            _tok_mod_spec(5, tm, s, d),
        ],
        out_specs=pl.BlockSpec((tm, d), lambda i: (i, 0)),
        out_shape=jax.ShapeDtypeStruct((t, d), F32),
        compiler_params=_cparams(("arbitrary",)),
    )(xt, yt, w_item, mod)


def _moe_rows(n_tokens, tm_exp=256):
    return (n_tokens // tm_exp + N_EXPERTS) * tm_exp


def _moe_layer(x, xs_buf, mod, ng, router_w, router_b, w_gu, b_gu, w_down, b_down, layer,
               tm_exp=256):
    b, s, d = x.shape
    t = b * s
    xt = x.reshape(t, d)
    assert t % TOP_K == 0
    n_tok = t // TOP_K
    h2, top_e, top_w, rank, counts = _moe_route(xt, n_tok, s, mod, ng, router_w, router_b)
    (pad_start, tile_expert, first, n_used, n_valid, w_slot, next_expert,
     n_tiles) = _moe_plan(n_tok, counts, tm_exp)
    assert xs_buf.shape[0] == n_tiles * tm_exp
    xs, row_item = _moe_dispatch(h2, pad_start, top_e, rank, xs_buf, tm_exp)
    yt = _moe_experts(xs, t, tile_expert, first, n_used, n_valid, w_slot, next_expert, row_item,
                      w_gu, b_gu, w_down, b_down, layer, tm_exp)
    w_item = top_w[:, :TOP_K].reshape(t, 1)
    return _moe_combine(xt, s, mod, w_item, yt).reshape(b, s, d), xs


def kernel(x, c, norm_mix_g, norm_ffn_g, w_mod, b_mod, lru_w_in, lru_conv_w, lru_conv_b, lru_w_a, lru_b_a, lru_w_x, lru_b_x, lru_lambda, lru_w_out, attn_w_qkv, attn_q_norm_g, attn_k_norm_g, attn_w_out, router_w, router_b, expert_w_gu, expert_b_gu, expert_w_down, expert_b_down):
    depth = w_mod.shape[0]
    mods = _modulation(c, w_mod, b_mod)
    n_tokens = x.shape[0] * x.shape[1]
    xs_buf = jnp.zeros((_moe_rows(n_tokens), x.shape[2]), F32)
    for i in range(depth):
        mod = mods[i]
        j = i // 2
        if i % 2 == 0:
            x = _rglru_layer(x, mod, norm_mix_g[i], lru_w_in[j], lru_conv_w[j], lru_conv_b[j],
                             lru_w_a[j], lru_b_a[j], lru_w_x[j], lru_b_x[j], lru_lambda[j],
                             lru_w_out[j])
        else:
            x = _moba_layer(x, mod, norm_mix_g[i], attn_w_qkv[j], attn_q_norm_g[j],
                            attn_k_norm_g[j], attn_w_out[j])
        x, xs_buf = _moe_layer(x, xs_buf, mod, norm_ffn_g[i], router_w[i], router_b[i],
                               expert_w_gu, expert_b_gu, expert_w_down, expert_b_down, i)
    return x
```

```python
import functools

import jax
import jax.numpy as jnp
from jax import lax
from jax.experimental import pallas as pl
from jax.experimental.pallas import tpu as pltpu

NORM_EPS = 1e-6
N_MOD = 6
LRU_BLOCKS = 4
CONV_WIDTH = 4
LRU_C = 8.0
ATTN_HEADS = 8
MOBA_BLOCK = 256
MOBA_TOPK = 3
N_EXPERTS = 32
TOP_K = 4
SWIGLU_LIMIT = 7.0
SWIGLU_ALPHA = 1.702

LANES = 128
SUBLANES = 8
MASK_BIAS = -(2.0 ** 100)
LOG2_E = 1.4426950408889634
ATTN_PARTS = 2
VMEM_LIMIT = 52 * 1024 * 1024

F32 = jnp.float32
BF16 = jnp.bfloat16
HIGHEST = lax.Precision.HIGHEST


def _cparams(sem):
    return pltpu.CompilerParams(dimension_semantics=sem, vmem_limit_bytes=VMEM_LIMIT)


def _sigmoid(z):
    return 1.0 / (1.0 + jnp.exp(-z))


def _modulated_norm(x, g, sc, sh):
    ms = jnp.mean(x * x, axis=-1, keepdims=True)
    return x * lax.rsqrt(ms + NORM_EPS) * (g * (1.0 + sc)) + sh


def _mod_kernel(c_ref, w_ref, b_ref, o_ref):
    c = c_ref[...]
    cond = c * _sigmoid(c)
    o_ref[...] = jnp.dot(cond, w_ref[...], preferred_element_type=F32,
                         precision=HIGHEST) + b_ref[...]


def _modulation(c, w_mod, b_mod):
    depth, d, _ = w_mod.shape
    b = c.shape[0]
    rows = -(-b // SUBLANES) * SUBLANES
    c_pad = jnp.pad(c, ((0, rows - b), (0, 0)))
    out = pl.pallas_call(
        _mod_kernel,
        grid=(depth, N_MOD),
        in_specs=[
            pl.BlockSpec((rows, d), lambda i, j: (0, 0)),
            pl.BlockSpec((None, d, d), lambda i, j: (i, 0, j)),
            pl.BlockSpec((None, None, 1, d), lambda i, j: (i, j, 0, 0)),
        ],
        out_specs=pl.BlockSpec((None, None, rows, d), lambda i, j: (i, j, 0, 0)),
        out_shape=jax.ShapeDtypeStruct((depth, N_MOD, rows, d), F32),
        compiler_params=_cparams(("arbitrary", "arbitrary")),
    )(c_pad, w_mod, b_mod.reshape(depth, N_MOD, 1, d))
    return out[:, :, :b].reshape(depth, N_MOD, b, 1, d)


def _mod_spec(j, d):
    return pl.BlockSpec((None, None, 1, d), lambda b, s: (j, b, 0, 0))


def _rglru_kernel(x_ref, sh_ref, sc_ref, g1_ref, ng_ref, win_ref, cw_ref, cb_ref,
                  wa_ref, ba_ref, wx_ref, bx_ref, lam_ref, wout_ref, o_ref,
                  ext_ref, a_ref, u_ref, h_ref):
    ts, d = x_ref.shape
    bw = d // LRU_BLOCKS

    @pl.when(pl.program_id(1) == 0)
    def _():
        ext_ref[...] = jnp.zeros_like(ext_ref)
        h_ref[...] = jnp.zeros_like(h_ref)

    x = x_ref[...]
    h = _modulated_norm(x, ng_ref[...], sc_ref[...], sh_ref[...])
    gr = jnp.dot(h.astype(BF16), win_ref[...], preferred_element_type=F32)
    gate_branch = gr[:, :d]
    rec = gr[:, d:]

    groups = ts // SUBLANES
    rec3 = rec.reshape(groups, SUBLANES, d)
    tail3 = ext_ref[...].reshape(1, SUBLANES, d)
    grow = lax.broadcasted_iota(jnp.int32, rec3.shape, 1)
    xc3 = cb_ref[...] + cw_ref[CONV_WIDTH - 1:CONV_WIDTH, :] * rec3
    for s in range(1, CONV_WIDTH):
        rot = pltpu.roll(rec3, s, 1)
        rot_before = jnp.concatenate([pltpu.roll(tail3, s, 1), rot[:groups - 1]], axis=0)
        tap = cw_ref[CONV_WIDTH - 1 - s:CONV_WIDTH - s, :]
        xc3 = xc3 + tap * jnp.where(grow >= s, rot, rot_before)
    xc = xc3.reshape(ts, d)
    ext_ref[...] = rec[ts - SUBLANES:, :]

    xcb = xc.astype(BF16)
    ra = jnp.concatenate(
        [jnp.dot(xcb[:, g * bw:(g + 1) * bw], wa_ref[g], preferred_element_type=F32)
         for g in range(LRU_BLOCKS)], axis=1) + ba_ref[...]
    rx = jnp.concatenate(
        [jnp.dot(xcb[:, g * bw:(g + 1) * bw], wx_ref[g], preferred_element_type=F32)
         for g in range(LRU_BLOCKS)], axis=1) + bx_ref[...]
    r = _sigmoid(ra)
    ig = _sigmoid(rx)
    z = -lam_ref[...]
    softplus = jnp.maximum(z, 0.0) + jnp.log(1.0 + jnp.exp(-jnp.abs(z)))
    a = jnp.exp2(r * ((-LRU_C * LOG2_E) * softplus))
    v = 1.0 - a * a
    u = jnp.where(v > 0.0, v * lax.rsqrt(v), 0.0) * (ig * xc)

    a = a.reshape(ts // SUBLANES, SUBLANES, d)
    u = u.reshape(ts // SUBLANES, SUBLANES, d)
    row = lax.broadcasted_iota(jnp.int32, a.shape, 1)
    k = 1
    while k < SUBLANES:
        a_prev = pltpu.roll(a, k, 1)
        u_prev = pltpu.roll(u, k, 1)
        m = row >= k
        u = jnp.where(m, a * u_prev + u, u)
        a = jnp.where(m, a * a_prev, a)
        k *= 2
    a_ref[...] = a.reshape(ts, d)
    u_ref[...] = u.reshape(ts, d)

    def group(j, hc):
        r0 = pl.multiple_of(j * SUBLANES, SUBLANES)
        hs = u_ref[pl.ds(r0, SUBLANES), :] + a_ref[pl.ds(r0, SUBLANES), :] * hc
        u_ref[pl.ds(r0, SUBLANES), :] = hs
        return hs[SUBLANES - 1:SUBLANES, :]

    h_ref[0:1, :] = lax.fori_loop(0, ts // SUBLANES, group, h_ref[0:1, :])

    y = jax.nn.gelu(gate_branch, approximate=True) * u_ref[...]
    out = jnp.dot(y.astype(BF16), wout_ref[...], preferred_element_type=F32)
    o_ref[...] = x + g1_ref[...] * out


def _rglru_layer(x, mod, ng, w_in, conv_w, conv_b, w_a, b_a, w_x, b_x, lam, w_out, ts=256):
    b, s, d = x.shape
    full = lambda shape: pl.BlockSpec(shape, lambda bi, si: (0,) * len(shape))
    row = lambda v: v.reshape(1, d)
    return pl.pallas_call(
        _rglru_kernel,
        grid=(b, s // ts),
        in_specs=[
            pl.BlockSpec((None, ts, d), lambda bi, si: (bi, si, 0)),
            _mod_spec(0, d), _mod_spec(1, d), _mod_spec(2, d),
            full((1, d)), full((d, 2 * d)), full((CONV_WIDTH, d)), full((1, d)),
            full(w_a.shape), full((1, d)), full(w_x.shape), full((1, d)), full((1, d)),
            full((d, d)),
        ],
        out_specs=pl.BlockSpec((None, ts, d), lambda bi, si: (bi, si, 0)),
        out_shape=jax.ShapeDtypeStruct(x.shape, F32),
        scratch_shapes=[
            pltpu.VMEM((SUBLANES, d), F32),
            pltpu.VMEM((ts, d), F32),
            pltpu.VMEM((ts, d), F32),
            pltpu.VMEM((SUBLANES, d), F32),
        ],
        compiler_params=_cparams(("arbitrary", "arbitrary")),
    )(x, mod, mod, mod, row(ng), w_in.astype(BF16), conv_w, row(conv_b),
      w_a.astype(BF16), row(b_a), w_x.astype(BF16), row(b_x), row(lam), w_out.astype(BF16))


def _qkv_kernel(x_ref, sh_ref, sc_ref, ng_ref, w_ref, qg_ref, kg_ref,
                q_ref, kext_ref, v_ref, km_ref):
    ts, d = x_ref.shape
    nh, _, hd = q_ref.shape
    blk = pl.program_id(1)
    h = _modulated_norm(x_ref[...], ng_ref[...], sc_ref[...], sh_ref[...])
    qkv = jnp.dot(h.astype(BF16), w_ref[...], preferred_element_type=F32)
    lane = lax.broadcasted_iota(jnp.int32, (ts, hd), 1)
    onehot = jnp.where(lane == blk, 1.0, 0.0).astype(BF16)
    for hh in range(nh):
        qh = qkv[:, hh * hd:(hh + 1) * hd]
        qn = qh * lax.rsqrt(jnp.mean(qh * qh, axis=-1, keepdims=True) + NORM_EPS)
        q_ref[hh] = (qn * qg_ref[...] * (hd ** -0.5 * LOG2_E)).astype(BF16)
        kh = qkv[:, d + hh * hd:d + (hh + 1) * hd]
        kn = kh * lax.rsqrt(jnp.mean(kh * kh, axis=-1, keepdims=True) + NORM_EPS) * kg_ref[...]
        kext_ref[hh, :, 0:hd] = kn.astype(BF16)
        kext_ref[hh, :, hd:2 * hd] = onehot
        km_ref[:, hh * hd:(hh + 1) * hd] = jnp.mean(kn, axis=0, keepdims=True)
        v_ref[hh] = qkv[:, 2 * d + hh * hd:2 * d + (hh + 1) * hd].astype(BF16)


def _attn_select_blocks(q_ref, km_ref, bias_ref, tq):
    nhs, s, hd = q_ref.shape
    nt = (((1,), (1,)), ((), ()))
    nbp = -(-(s // tq) // SUBLANES) * SUBLANES
    blk = lax.broadcasted_iota(jnp.int32, (nbp, s), 0)
    blk_f = blk.astype(F32)
    own = lax.broadcasted_iota(jnp.int32, (nbp, s), 1) // tq
    past = blk < own
    for hh in range(nhs):
        pieces, rest = [], km_ref[hh, 0:nbp, :]
        for _ in range(3):
            piece = rest.astype(BF16)
            pieces.append(piece)
            rest = rest - piece.astype(F32)
        gate3 = lax.dot_general(jnp.concatenate(pieces, axis=0), q_ref[hh], nt,
                                preferred_element_type=F32)
        gate = gate3[0:nbp] + gate3[nbp:2 * nbp] + gate3[2 * nbp:3 * nbp]
        g = jnp.where(past, gate, -jnp.inf)
        sel = blk == own
        for _ in range(MOBA_TOPK):
            m = jnp.max(g, axis=0, keepdims=True)
            idx = jnp.min(jnp.where(g == m, blk_f, float(nbp)), axis=0, keepdims=True)
            hit = blk_f == idx
            sel = sel | (hit & past)
            g = jnp.where(hit, -jnp.inf, g)
        bias_t = jnp.where(sel, 0.0, MASK_BIAS)
        pad = jnp.zeros((LANES - nbp, tq), F32)
        for c in range(s // tq):
            tile = jnp.concatenate([bias_t[:, c * tq:(c + 1) * tq], pad], axis=0).T
            bias_ref[hh, c * tq:(c + 1) * tq, :] = tile.astype(BF16)


def _attn_query_block_steps(qi, hh, q_ref, bias_ref, kext_ref, v_ref, o_ref, s_ref):
    tq = s_ref.shape[3]
    buf = qi % s_ref.shape[0]
    nt = (((1,), (1,)), ((), ()))
    rows = slice(qi * tq, (qi + 1) * tq)
    st = {}

    def score(j):
        if j == 0:
            st['q'] = jnp.concatenate([q_ref[hh, rows, :], bias_ref[hh, rows, :]], axis=1)
        sj = lax.dot_general(st['q'], kext_ref[hh, j * tq:(j + 1) * tq, :], nt,
                             preferred_element_type=F32)
        if j == qi:
            rr = lax.broadcasted_iota(jnp.int32, (tq, tq), 0)
            cc = lax.broadcasted_iota(jnp.int32, (tq, tq), 1)
            sj = jnp.where(cc <= rr, sj, MASK_BIAS)
        s_ref[buf, hh, j * tq:(j + 1) * tq, :] = sj
        st['mx'] = sj if j == 0 else jnp.maximum(st['mx'], sj)
        if j == qi:
            st['m'] = jnp.broadcast_to(jnp.max(st['mx'], axis=-1, keepdims=True), (tq, tq))

    def weigh(j):
        p = jnp.exp2(s_ref[buf, hh, j * tq:(j + 1) * tq, :] - st['m'])
        pv = jnp.dot(p.astype(BF16), v_ref[hh, j * tq:(j + 1) * tq, :],
                     preferred_element_type=F32)
        lj = jnp.sum(p, axis=-1, keepdims=True)
        st['l'] = lj if j == 0 else st['l'] + lj
        st['acc'] = pv if j == 0 else st['acc'] + pv
        if j == qi:
            o_ref[hh, rows, :] = (st['acc'] / st['l']).astype(BF16)

    blocks = range(qi + 1)
    return ([functools.partial(score, j) for j in blocks],
            [functools.partial(weigh, j) for j in blocks])


def _attn_query_blocks(qis, q_ref, bias_ref, kext_ref, v_ref, o_ref, s_ref):
    todo = [_attn_query_block_steps(qi, hh, q_ref, bias_ref, kext_ref, v_ref, o_ref, s_ref)
            for hh in range(s_ref.shape[1]) for qi in qis]
    pending = []
    for scores, weighs in todo + [([], [])]:
        for k in range(max(len(scores), len(pending))):
            if k < len(scores):
                scores[k]()
            if k < len(pending):
                pending[k]()
        pending = weighs


def _attn_kernel(q_ref, kext_ref, v_ref, km_ref, o_ref, s_ref, bias_ref):
    tq = s_ref.shape[3]
    nb = kext_ref.shape[1] // tq
    total = nb * (nb + 1) // 2
    bounds = [0]
    for part in range(1, ATTN_PARTS):
        qi = bounds[-1]
        while qi < nb and qi * (qi + 1) // 2 < total * part // ATTN_PARTS:
            qi += 1
        bounds.append(qi)
    bounds.append(nb)
    for part in range(ATTN_PARTS):
        @pl.when(pl.program_id(2) == part)
        def _(part=part):
            if part == 0:
                _attn_select_blocks(q_ref, km_ref, bias_ref, tq)
            _attn_query_blocks(range(bounds[part], bounds[part + 1]),
                               q_ref, bias_ref, kext_ref, v_ref, o_ref, s_ref)


def _attn_out_kernel(o_ref, x_ref, g1_ref, w_ref, y_ref):
    nh = o_ref.shape[0]
    o = jnp.concatenate([o_ref[hh] for hh in range(nh)], axis=1)
    out = jnp.dot(o, w_ref[...], preferred_element_type=F32)
    y_ref[...] = x_ref[...] + g1_ref[...] * out


def _moba_layer(x, mod, ng, w_qkv, q_g, k_g, w_out, ts_out=512, heads_per_step=1):
    b, s, d = x.shape
    nh = ATTN_HEADS
    hd = d // nh
    tq = MOBA_BLOCK
    nb = s // tq
    assert s % tq == 0 and nb <= LANES and hd == LANES
    full = lambda shape: pl.BlockSpec(shape, lambda bi, si: (0,) * len(shape))

    q, kext, v, km = pl.pallas_call(
        _qkv_kernel,
        grid=(b, nb),
        in_specs=[
            pl.BlockSpec((None, tq, d), lambda bi, si: (bi, si, 0)),
            _mod_spec(0, d), _mod_spec(1, d),
            full((1, d)), full((d, 3 * d)), full((1, hd)), full((1, hd)),
        ],
        out_specs=[
            pl.BlockSpec((None, nh, tq, hd), lambda bi, si: (bi, 0, si, 0)),
            pl.BlockSpec((None, nh, tq, 2 * hd), lambda bi, si: (bi, 0, si, 0)),
            pl.BlockSpec((None, nh, tq, hd), lambda bi, si: (bi, 0, si, 0)),
            pl.BlockSpec((None, None, 1, d), lambda bi, si: (bi, si, 0, 0)),
        ],
        out_shape=[
            jax.ShapeDtypeStruct((b, nh, s, hd), BF16),
            jax.ShapeDtypeStruct((b, nh, s, 2 * hd), BF16),
            jax.ShapeDtypeStruct((b, nh, s, hd), BF16),
            jax.ShapeDtypeStruct((b, nb, 1, d), F32),
        ],
        compiler_params=_cparams(("arbitrary", "arbitrary")),
    )(x, mod, mod, ng.reshape(1, d), w_qkv.astype(BF16), q_g.reshape(1, hd), k_g.reshape(1, hd))

    km = km.reshape(b, nb, nh, hd).transpose(0, 2, 1, 3)
    km = jnp.pad(km, ((0, 0), (0, 0), (0, LANES - nb), (0, 0)))

    o = pl.pallas_call(
        _attn_kernel,
        grid=(b, nh // heads_per_step, ATTN_PARTS),
        in_specs=[
            pl.BlockSpec((None, heads_per_step, s, hd), lambda bi, hi, pi: (bi, hi, 0, 0)),
            pl.BlockSpec((None, heads_per_step, s, 2 * hd), lambda bi, hi, pi: (bi, hi, 0, 0)),
            pl.BlockSpec((None, heads_per_step, s, hd), lambda bi, hi, pi: (bi, hi, 0, 0)),
            pl.BlockSpec((None, heads_per_step, LANES, hd), lambda bi, hi, pi: (bi, hi, 0, 0)),
        ],
        out_specs=pl.BlockSpec((None, heads_per_step, s, hd), lambda bi, hi, pi: (bi, hi, 0, 0)),
        out_shape=jax.ShapeDtypeStruct((b, nh, s, hd), BF16),
        scratch_shapes=[pltpu.VMEM((2, heads_per_step, s, tq), F32),
                        pltpu.VMEM((heads_per_step, s, LANES), BF16)],
        compiler_params=_cparams(("arbitrary", "arbitrary", "arbitrary")),
    )(q, kext, v, km)

    return pl.pallas_call(
        _attn_out_kernel,
        grid=(b, s // ts_out),
        in_specs=[
            pl.BlockSpec((None, nh, ts_out, hd), lambda bi, si: (bi, 0, si, 0)),
            pl.BlockSpec((None, ts_out, d), lambda bi, si: (bi, si, 0)),
            _mod_spec(2, d),
            full((d, d)),
        ],
        out_specs=pl.BlockSpec((None, ts_out, d), lambda bi, si: (bi, si, 0)),
        out_shape=jax.ShapeDtypeStruct(x.shape, F32),
        compiler_params=_cparams(("arbitrary", "arbitrary")),
    )(o, x, mod, w_out.astype(BF16))


def _route_kernel(x_ref, sh_ref, sc_ref, ng_ref, rw_ref, rb_ref,
                  h_ref, e_ref, w_ref, r_ref, cnt_ref, run_ref):
    tm = x_ref.shape[0]

    @pl.when(pl.program_id(0) == 0)
    def _():
        run_ref[...] = jnp.zeros_like(run_ref)

    h = _modulated_norm(x_ref[...], ng_ref[...], sc_ref[...], sh_ref[...])
    h_ref[...] = h
    logits = jnp.dot(h, rw_ref[...], preferred_element_type=F32, precision=HIGHEST) + rb_ref[...]
    lane = lax.broadcasted_iota(jnp.int32, (tm, LANES), 1)
    lane_f = lane.astype(F32)
    lg = jnp.where(lane < N_EXPERTS, logits, -jnp.inf)
    tops, hits = [], []
    for _ in range(TOP_K):
        m = jnp.max(lg, axis=-1, keepdims=True)
        idx = jnp.min(jnp.where(lg == m, lane_f, float(LANES)), axis=-1, keepdims=True)
        hit = lane_f == idx
        tops.append((m, idx))
        hits.append(hit)
        lg = jnp.where(hit, -jnp.inf, lg)
    exps = [jnp.exp(m - tops[0][0]) for m, _ in tops]
    denom = exps[0]
    for ex in exps[1:]:
        denom = denom + ex

    member = jnp.zeros((tm, LANES), F32)
    for hit in hits:
        member = member + jnp.where(hit, 1.0, 0.0)
    rr = lax.broadcasted_iota(jnp.int32, (tm, tm), 0)
    cc = lax.broadcasted_iota(jnp.int32, (tm, tm), 1)
    earlier = jnp.where(cc < rr, 1.0, 0.0).astype(BF16)
    before = jnp.dot(earlier, member.astype(BF16), preferred_element_type=F32) + run_ref[...]

    e_out = jnp.zeros((tm, LANES), F32)
    w_out = jnp.zeros((tm, LANES), F32)
    r_out = jnp.zeros((tm, LANES), F32)
    for k in range(TOP_K):
        rank = jnp.sum(jnp.where(hits[k], before, 0.0), axis=-1, keepdims=True)
        e_out = jnp.where(lane == k, tops[k][1], e_out)
        w_out = jnp.where(lane == k, exps[k] / denom, w_out)
        r_out = jnp.where(lane == k, rank, r_out)
    e_ref[...] = e_out.astype(jnp.int32)
    w_ref[...] = w_out
    r_ref[...] = r_out.astype(jnp.int32)
    run = run_ref[...] + jnp.sum(member, axis=0, keepdims=True)
    run_ref[...] = run
    cnt_ref[...] = run.astype(jnp.int32)


def _dispatch_kernel(n_items, tile_rows, start_ref, e_ref, r_ref, h_ref, xs_in_hbm, xs_hbm,
                     item_ref, sem):
    del xs_in_hbm
    tm = h_ref.shape[0]
    unroll = 32
    assert item_ref.shape[0] % unroll == 0 and tile_rows % unroll == 0
    groups = tile_rows // unroll
    assert groups & (groups - 1) == 0

    @pl.when(pl.program_id(0) == 0)
    def _():
        def clear(c, _):
            tile = lax.shift_right_logical(c, groups.bit_length() - 1)
            dummy = n_items + (tile & 1) * tile_rows + (c & (groups - 1)) * unroll
            for u in range(unroll):
                item_ref[c * unroll + u] = dummy + u
            return 0
        lax.fori_loop(0, item_ref.shape[0] // unroll, clear, 0)

    base = pl.program_id(0) * tm * TOP_K
    for t in range(tm):
        for k in range(TOP_K):
            j = t * TOP_K + k
            row = start_ref[e_ref[0, j]] + r_ref[0, j]
            item_ref[row] = base + j
            pltpu.make_async_copy(h_ref.at[pl.ds(t, 1), :], xs_hbm.at[pl.ds(row, 1), :],
                                  sem).start()
    for _ in range(TOP_K):
        pltpu.make_async_copy(h_ref, xs_hbm.at[pl.ds(0, tm), :], sem).wait()


def _expert_kernel(layer, be_ref, first_ref, nu_ref, nv_ref, wslot_ref, next_ref, item_ref,
                   xs_ref, wgu_hbm, bgu_ref, wd_hbm, bd_ref, yt_hbm,
                   wgu_f32, wd_f32, wgu_bf, wd_bf, ybuf, sems, wsems):
    i = pl.program_id(0)
    de = wd_bf.shape[0]
    tm = xs_ref.shape[0]
    n_items = yt_hbm.shape[0] - 2 * tm
    slot = i % 2
    nv = nv_ref[i]

    @pl.when(i == 0)
    def _():
        ybuf[...] = jnp.zeros_like(ybuf)
        fills = [pltpu.make_async_copy(ybuf.at[par], yt_hbm.at[pl.ds(n_items + par * tm, tm), :],
                                       sems.at[par]) for par in range(2)]
        for cp in fills:
            cp.start()
        for cp in fills:
            cp.wait()

    def weight_copies(expert, ws):
        return (pltpu.make_async_copy(wgu_hbm.at[layer, expert], wgu_f32.at[ws], wsems.at[ws, 0]),
                pltpu.make_async_copy(wd_hbm.at[layer, expert], wd_f32.at[ws], wsems.at[ws, 1]))

    @pl.when(nv > 0)
    def _():
        @pl.when(first_ref[i] == 1)
        def _():
            ws = wslot_ref[i]

            @pl.when(i == 0)
            def _():
                for cp in weight_copies(be_ref[0], 0):
                    cp.start()

            for cp in weight_copies(be_ref[i], ws):
                cp.wait()

            @pl.when(next_ref[i] >= 0)
            def _():
                for cp in weight_copies(next_ref[i], 1 - ws):
                    cp.start()

            wgu_bf[...] = wgu_f32[ws].astype(BF16)
            wd_bf[...] = wd_f32[ws].astype(BF16)

        gu = jnp.dot(xs_ref[...].astype(BF16), wgu_bf[...], preferred_element_type=F32) + bgu_ref[...]
        gate = jnp.minimum(gu[:, :de], SWIGLU_LIMIT)
        up = jnp.clip(gu[:, de:], -SWIGLU_LIMIT, SWIGLU_LIMIT)
        act = gate * _sigmoid(SWIGLU_ALPHA * gate) * (up + 1.0)
        y = jnp.dot(act.astype(BF16), wd_bf[...], preferred_element_type=F32) + bd_ref[...]
        for par in range(2):
            @pl.when(slot == par)
            def _(par=par):
                ybuf[par] = y
                for r in range(tm):
                    pltpu.make_async_copy(ybuf.at[par, pl.ds(r, 1), :],
                                          yt_hbm.at[pl.ds(item_ref[0, r], 1), :],
                                          sems.at[par]).start()

    def wait_tile(par):
        pltpu.make_async_copy(ybuf.at[par], yt_hbm.at[pl.ds(0, tm), :], sems.at[par]).wait()

    for par in range(2):
        @pl.when((i > 0) & (slot == 1 - par) & (nv_ref[jnp.maximum(i - 1, 0)] > 0))
        def _(par=par):
            wait_tile(par)

        @pl.when((i == pl.num_programs(0) - 1) & (slot == par) & (nv > 0))
        def _(par=par):
            wait_tile(par)


def _combine_kernel(x_ref, yt_ref, w_ref, g2_ref, o_ref):
    o_ref[...] = x_ref[...] + g2_ref[...] * (w_ref[...] * yt_ref[...])


def _tok_mod_spec(j, tm, s, d):
    return pl.BlockSpec((None, None, 1, d), lambda i: (j, (i * tm) // s, 0, 0))


def _moe_route(xt, t, s, mod, ng, router_w, router_b, tm=256):
    d = xt.shape[1]
    assert t % tm == 0
    rw = jnp.pad(router_w, ((0, 0), (0, LANES - N_EXPERTS)))
    rb = jnp.pad(router_b, (0, LANES - N_EXPERTS)).reshape(1, LANES)
    return pl.pallas_call(
        _route_kernel,
        grid=(t // tm,),
        in_specs=[
            pl.BlockSpec((tm, d), lambda i: (i, 0)),
            _tok_mod_spec(3, tm, s, d), _tok_mod_spec(4, tm, s, d),
            pl.BlockSpec((1, d), lambda i: (0, 0)),
            pl.BlockSpec((d, LANES), lambda i: (0, 0)),
            pl.BlockSpec((1, LANES), lambda i: (0, 0)),
        ],
        out_specs=[
            pl.BlockSpec((tm, d), lambda i: (i, 0)),
            pl.BlockSpec((tm, LANES), lambda i: (i, 0)),
            pl.BlockSpec((tm, LANES), lambda i: (i, 0)),
            pl.BlockSpec((tm, LANES), lambda i: (i, 0)),
            pl.BlockSpec((1, LANES), lambda i: (0, 0)),
        ],
        out_shape=[
            jax.ShapeDtypeStruct((t, d), F32),
            jax.ShapeDtypeStruct((t, LANES), jnp.int32),
            jax.ShapeDtypeStruct((t, LANES), F32),
            jax.ShapeDtypeStruct((t, LANES), jnp.int32),
            jax.ShapeDtypeStruct((1, LANES), jnp.int32),
        ],
        scratch_shapes=[pltpu.VMEM((1, LANES), F32)],
        compiler_params=_cparams(("arbitrary",)),
    )(xt, mod, mod, ng.reshape(1, d), rw, rb)


def _moe_plan(t, counts, tm):
    n_e = N_EXPERTS
    counts = counts[0, :n_e]
    padded = (counts + tm - 1) // tm * tm
    pad_end = jnp.cumsum(padded)
    pad_start = pad_end - padded
    n_tiles = (t * TOP_K) // tm + n_e
    tile_start = jnp.arange(n_tiles, dtype=jnp.int32) * tm
    n_used = (pad_end[-1] // tm).astype(jnp.int32)
    tile_expert = jnp.minimum(jnp.searchsorted(pad_end, tile_start, side='right'), n_e - 1)
    tile_expert = jnp.where(jnp.arange(n_tiles) < n_used, tile_expert,
                            tile_expert[jnp.maximum(n_used - 1, 0)]).astype(jnp.int32)
    first = jnp.concatenate([jnp.ones((1,), jnp.int32),
                             (tile_expert[1:] != tile_expert[:-1]).astype(jnp.int32)])
    seg_end = (pad_start + counts)[tile_expert]
    n_valid = jnp.where(jnp.arange(n_tiles) < n_used,
                        jnp.clip(seg_end - tile_start, 0, tm), 0).astype(jnp.int32)
    used = jnp.arange(n_tiles) < n_used
    w_slot = ((jnp.cumsum(first) - 1) % 2).astype(jnp.int32)
    first_pos = jnp.where((first == 1) & used, jnp.arange(n_tiles), n_tiles)
    next_first = lax.cummin(first_pos[::-1])[::-1]
    next_first = jnp.concatenate([next_first[1:], jnp.full((1,), n_tiles)])
    next_expert = jnp.where(next_first < n_tiles,
                            tile_expert[jnp.minimum(next_first, n_tiles - 1)], -1).astype(jnp.int32)
    return (pad_start.astype(jnp.int32), tile_expert, first, n_used.reshape(1), n_valid,
            w_slot, next_expert, n_tiles)


def _moe_dispatch(h2, pad_start, top_e, rank, xs_buf, tm_exp, tm=128):
    t, d = h2.shape
    n_rows = xs_buf.shape[0]
    flat = lambda a: a[:, :TOP_K].reshape(t // tm, 1, tm * TOP_K)
    item_spec = pl.BlockSpec((None, 1, tm * TOP_K), lambda i: (i, 0, 0), memory_space=pltpu.SMEM)
    return pl.pallas_call(
        functools.partial(_dispatch_kernel, t * TOP_K, tm_exp),
        grid=(t // tm,),
        in_specs=[
            pl.BlockSpec(memory_space=pltpu.SMEM),
            item_spec, item_spec,
            pl.BlockSpec((tm, d), lambda i: (i, 0)),
            pl.BlockSpec(memory_space=pl.ANY),
        ],
        out_specs=[pl.BlockSpec(memory_space=pl.ANY), pl.BlockSpec(memory_space=pltpu.SMEM)],
        out_shape=[jax.ShapeDtypeStruct((n_rows, d), F32),
                   jax.ShapeDtypeStruct((n_rows,), jnp.int32)],
        scratch_shapes=[pltpu.SemaphoreType.DMA],
        input_output_aliases={4: 0},
        compiler_params=_cparams(("arbitrary",)),
    )(pad_start, flat(top_e), flat(rank), h2, xs_buf)


def _moe_experts(xs, n_items, tile_expert, first, n_used, n_valid, w_slot, next_expert, row_item,
                 w_gu, b_gu, w_down, b_down, layer, tm):
    n_rows, d = xs.shape
    n_e = N_EXPERTS
    de = w_down.shape[2]
    n_tiles = n_rows // tm
    tile_idx = lambda i, be, fi, nu, *_: (jnp.maximum(jnp.minimum(i, nu[0] - 1), 0), 0)
    b_idx = lambda i, be, *_: (layer, be[i], 0, 0)
    return pl.pallas_call(
        functools.partial(_expert_kernel, layer),
        grid_spec=pltpu.PrefetchScalarGridSpec(
            num_scalar_prefetch=6,
            grid=(n_tiles,),
            in_specs=[
                pl.BlockSpec((None, 1, tm), lambda i, *_: (i, 0, 0), memory_space=pltpu.SMEM),
                pl.BlockSpec((tm, d), tile_idx),
                pl.BlockSpec(memory_space=pl.ANY),
                pl.BlockSpec((None, None, 1, 2 * de), b_idx),
                pl.BlockSpec(memory_space=pl.ANY),
                pl.BlockSpec((None, None, 1, d), b_idx),
            ],
            out_specs=pl.BlockSpec(memory_space=pl.ANY),
            scratch_shapes=[pltpu.VMEM((2, d, 2 * de), F32), pltpu.VMEM((2, de, d), F32),
                            pltpu.VMEM((d, 2 * de), BF16), pltpu.VMEM((de, d), BF16),
                            pltpu.VMEM((2, tm, d), F32), pltpu.SemaphoreType.DMA((2,)),
                            pltpu.SemaphoreType.DMA((2, 2))],
        ),
        out_shape=jax.ShapeDtypeStruct((n_items + 2 * tm, d), F32),
        compiler_params=_cparams(("arbitrary",)),
    )(tile_expert, first, n_used, n_valid, w_slot, next_expert, row_item.reshape(n_tiles, 1, tm),
      xs, w_gu, b_gu.reshape(b_gu.shape[0], n_e, 1, 2 * de), w_down,
      b_down.reshape(b_down.shape[0], n_e, 1, d))


def _moe_combine(xt, s, mod, w_item, yt, tm=512):
    t, d = xt.shape
    return pl.pallas_call(
        _combine_kernel,
        grid=(t // tm,),
        in_specs=[
            pl.BlockSpec((tm, d), lambda i: (i, 0)),
            pl.BlockSpec((tm, d), lambda i: (i, 0)),
            pl.BlockSpec((tm, 1), lambda i: (i, 0)),
            _tok_mod_spec(5, tm, s, d),
        ],
        out_specs=pl.BlockSpec((tm, d), lambda i: (i, 0)),
        out_shape=jax.ShapeDtypeStruct((t, d), F32),
        compiler_params=_cparams(("arbitrary",)),
    )(xt, yt, w_item, mod)


def _moe_rows(n_tokens, tm_exp=256):
    return (n_tokens // tm_exp + N_EXPERTS) * tm_exp


def _moe_layer(x, xs_buf, mod, ng, router_w, router_b, w_gu, b_gu, w_down, b_down, layer,
               tm_exp=256):
    b, s, d = x.shape
    t = b * s
    xt = x.reshape(t, d)
    assert t % TOP_K == 0
    n_tok = t // TOP_K
    h2, top_e, top_w, rank, counts = _moe_route(xt, n_tok, s, mod, ng, router_w, router_b)
    (pad_start, tile_expert, first, n_used, n_valid, w_slot, next_expert,
     n_tiles) = _moe_plan(n_tok, counts, tm_exp)
    assert xs_buf.shape[0] == n_tiles * tm_exp
    xs, row_item = _moe_dispatch(h2, pad_start, top_e, rank, xs_buf, tm_exp)
    yt = _moe_experts(xs, t, tile_expert, first, n_used, n_valid, w_slot, next_expert, row_item,
                      w_gu, b_gu, w_down, b_down, layer, tm_exp)
    w_item = top_w[:, :TOP_K].reshape(t, 1)
    return _moe_combine(xt, s, mod, w_item, yt).reshape(b, s, d), xs


def kernel(x, c, norm_mix_g, norm_ffn_g, w_mod, b_mod, lru_w_in, lru_conv_w, lru_conv_b, lru_w_a, lru_b_a, lru_w_x, lru_b_x, lru_lambda, lru_w_out, attn_w_qkv, attn_q_norm_g, attn_k_norm_g, attn_w_out, router_w, router_b, expert_w_gu, expert_b_gu, expert_w_down, expert_b_down):
    depth = w_mod.shape[0]
    mods = _modulation(c, w_mod, b_mod)
    n_tokens = x.shape[0] * x.shape[1]
    xs_buf = jnp.zeros((_moe_rows(n_tokens), x.shape[2]), F32)
    for i in range(depth):
        mod = mods[i]
        j = i // 2
        if i % 2 == 0:
            x = _rglru_layer(x, mod, norm_mix_g[i], lru_w_in[j], lru_conv_w[j], lru_conv_b[j],
                             lru_w_a[j], lru_b_a[j], lru_w_x[j], lru_b_x[j], lru_lambda[j],
                             lru_w_out[j])
        else:
            x = _moba_layer(x, mod, norm_mix_g[i], attn_w_qkv[j], attn_q_norm_g[j],
                            attn_k_norm_g[j], attn_w_out[j])
        x, xs_buf = _moe_layer(x, xs_buf, mod, norm_ffn_g[i], router_w[i], router_b[i],
                               expert_w_gu, expert_b_gu, expert_w_down, expert_b_down, i)
    return x
```

```python
import functools

import jax
import jax.numpy as jnp
from jax import lax
from jax.experimental import pallas as pl
from jax.experimental.pallas import tpu as pltpu

NORM_EPS = 1e-6
N_MOD = 6
LRU_BLOCKS = 4
CONV_WIDTH = 4
LRU_C = 8.0
ATTN_HEADS = 8
MOBA_BLOCK = 256
MOBA_TOPK = 3
N_EXPERTS = 32
TOP_K = 4
SWIGLU_LIMIT = 7.0
SWIGLU_ALPHA = 1.702

LANES = 128
SUBLANES = 8
MASK_BIAS = -(2.0 ** 100)
LOG2_E = 1.4426950408889634
ATTN_PARTS = 1
VMEM_LIMIT = 52 * 1024 * 1024

F32 = jnp.float32
BF16 = jnp.bfloat16
HIGHEST = lax.Precision.HIGHEST


def _cparams(sem):
    return pltpu.CompilerParams(dimension_semantics=sem, vmem_limit_bytes=VMEM_LIMIT)


def _sigmoid(z):
    return 1.0 / (1.0 + jnp.exp(-z))


def _modulated_norm(x, g, sc, sh):
    ms = jnp.mean(x * x, axis=-1, keepdims=True)
    return x * lax.rsqrt(ms + NORM_EPS) * (g * (1.0 + sc)) + sh


def _mod_kernel(c_ref, w_ref, b_ref, o_ref):
    c = c_ref[...]
    cond = c * _sigmoid(c)
    o_ref[...] = jnp.dot(cond, w_ref[...], preferred_element_type=F32,
                         precision=HIGHEST) + b_ref[...]


def _modulation(c, w_mod, b_mod):
    depth, d, _ = w_mod.shape
    b = c.shape[0]
    rows = -(-b // SUBLANES) * SUBLANES
    c_pad = jnp.pad(c, ((0, rows - b), (0, 0)))
    out = pl.pallas_call(
        _mod_kernel,
        grid=(depth, N_MOD),
        in_specs=[
            pl.BlockSpec((rows, d), lambda i, j: (0, 0)),
            pl.BlockSpec((None, d, d), lambda i, j: (i, 0, j)),
            pl.BlockSpec((None, None, 1, d), lambda i, j: (i, j, 0, 0)),
        ],
        out_specs=pl.BlockSpec((None, None, rows, d), lambda i, j: (i, j, 0, 0)),
        out_shape=jax.ShapeDtypeStruct((depth, N_MOD, rows, d), F32),
        compiler_params=_cparams(("arbitrary", "arbitrary")),
    )(c_pad, w_mod, b_mod.reshape(depth, N_MOD, 1, d))
    return out[:, :, :b].reshape(depth, N_MOD, b, 1, d)


def _mod_spec(j, d):
    return pl.BlockSpec((None, None, 1, d), lambda b, s: (j, b, 0, 0))


def _rglru_kernel(x_ref, sh_ref, sc_ref, g1_ref, ng_ref, win_ref, cw_ref, cb_ref,
                  wa_ref, ba_ref, wx_ref, bx_ref, lam_ref, wout_ref, o_ref,
                  ext_ref, a_ref, u_ref, h_ref):
    ts, d = x_ref.shape
    bw = d // LRU_BLOCKS

    @pl.when(pl.program_id(1) == 0)
    def _():
        ext_ref[...] = jnp.zeros_like(ext_ref)
        h_ref[...] = jnp.zeros_like(h_ref)

    x = x_ref[...]
    h = _modulated_norm(x, ng_ref[...], sc_ref[...], sh_ref[...])
    gr = jnp.dot(h.astype(BF16), win_ref[...], preferred_element_type=F32)
    gate_branch = gr[:, :d]
    rec = gr[:, d:]

    groups = ts // SUBLANES
    rec3 = rec.reshape(groups, SUBLANES, d)
    tail3 = ext_ref[...].reshape(1, SUBLANES, d)
    grow = lax.broadcasted_iota(jnp.int32, rec3.shape, 1)
    xc3 = cb_ref[...] + cw_ref[CONV_WIDTH - 1:CONV_WIDTH, :] * rec3
    for s in range(1, CONV_WIDTH):
        rot = pltpu.roll(rec3, s, 1)
        rot_before = jnp.concatenate([pltpu.roll(tail3, s, 1), rot[:groups - 1]], axis=0)
        tap = cw_ref[CONV_WIDTH - 1 - s:CONV_WIDTH - s, :]
        xc3 = xc3 + tap * jnp.where(grow >= s, rot, rot_before)
    xc = xc3.reshape(ts, d)
    ext_ref[...] = rec[ts - SUBLANES:, :]

    xcb = xc.astype(BF16)
    ra = jnp.concatenate(
        [jnp.dot(xcb[:, g * bw:(g + 1) * bw], wa_ref[g], preferred_element_type=F32)
         for g in range(LRU_BLOCKS)], axis=1) + ba_ref[...]
    rx = jnp.concatenate(
        [jnp.dot(xcb[:, g * bw:(g + 1) * bw], wx_ref[g], preferred_element_type=F32)
         for g in range(LRU_BLOCKS)], axis=1) + bx_ref[...]
    r = _sigmoid(ra)
    ig = _sigmoid(rx)
    z = -lam_ref[...]
    softplus = jnp.maximum(z, 0.0) + jnp.log(1.0 + jnp.exp(-jnp.abs(z)))
    a = jnp.exp2(r * ((-LRU_C * LOG2_E) * softplus))
    v = 1.0 - a * a
    u = jnp.where(v > 0.0, v * lax.rsqrt(v), 0.0) * (ig * xc)

    a = a.reshape(ts // SUBLANES, SUBLANES, d)
    u = u.reshape(ts // SUBLANES, SUBLANES, d)
    row = lax.broadcasted_iota(jnp.int32, a.shape, 1)
    k = 1
    while k < SUBLANES:
        a_prev = pltpu.roll(a, k, 1)
        u_prev = pltpu.roll(u, k, 1)
        m = row >= k
        u = jnp.where(m, a * u_prev + u, u)
        a = jnp.where(m, a * a_prev, a)
        k *= 2
    a_ref[...] = a.reshape(ts, d)
    u_ref[...] = u.reshape(ts, d)

    def group(j, hc):
        r0 = pl.multiple_of(j * SUBLANES, SUBLANES)
        hs = u_ref[pl.ds(r0, SUBLANES), :] + a_ref[pl.ds(r0, SUBLANES), :] * hc
        u_ref[pl.ds(r0, SUBLANES), :] = hs
        return hs[SUBLANES - 1:SUBLANES, :]

    h_ref[0:1, :] = lax.fori_loop(0, ts // SUBLANES, group, h_ref[0:1, :])

    y = jax.nn.gelu(gate_branch, approximate=True) * u_ref[...]
    out = jnp.dot(y.astype(BF16), wout_ref[...], preferred_element_type=F32)
    o_ref[...] = x + g1_ref[...] * out


def _rglru_layer(x, mod, ng, w_in, conv_w, conv_b, w_a, b_a, w_x, b_x, lam, w_out, ts=256):
    b, s, d = x.shape
    full = lambda shape: pl.BlockSpec(shape, lambda bi, si: (0,) * len(shape))
    row = lambda v: v.reshape(1, d)
    return pl.pallas_call(
        _rglru_kernel,
        grid=(b, s // ts),
        in_specs=[
            pl.BlockSpec((None, ts, d), lambda bi, si: (bi, si, 0)),
            _mod_spec(0, d), _mod_spec(1, d), _mod_spec(2, d),
            full((1, d)), full((d, 2 * d)), full((CONV_WIDTH, d)), full((1, d)),
            full(w_a.shape), full((1, d)), full(w_x.shape), full((1, d)), full((1, d)),
            full((d, d)),
        ],
        out_specs=pl.BlockSpec((None, ts, d), lambda bi, si: (bi, si, 0)),
        out_shape=jax.ShapeDtypeStruct(x.shape, F32),
        scratch_shapes=[
            pltpu.VMEM((SUBLANES, d), F32),
            pltpu.VMEM((ts, d), F32),
            pltpu.VMEM((ts, d), F32),
            pltpu.VMEM((SUBLANES, d), F32),
        ],
        compiler_params=_cparams(("arbitrary", "arbitrary")),
    )(x, mod, mod, mod, row(ng), w_in.astype(BF16), conv_w, row(conv_b),
      w_a.astype(BF16), row(b_a), w_x.astype(BF16), row(b_x), row(lam), w_out.astype(BF16))


def _qkv_kernel(x_ref, sh_ref, sc_ref, ng_ref, w_ref, qg_ref, kg_ref,
                q_ref, kext_ref, v_ref, km_ref):
    ts, d = x_ref.shape
    nh, _, hd = q_ref.shape
    blk = pl.program_id(1)
    h = _modulated_norm(x_ref[...], ng_ref[...], sc_ref[...], sh_ref[...])
    qkv = jnp.dot(h.astype(BF16), w_ref[...], preferred_element_type=F32)
    lane = lax.broadcasted_iota(jnp.int32, (ts, hd), 1)
    onehot = jnp.where(lane == blk, 1.0, 0.0).astype(BF16)
    for hh in range(nh):
        qh = qkv[:, hh * hd:(hh + 1) * hd]
        qn = qh * lax.rsqrt(jnp.mean(qh * qh, axis=-1, keepdims=True) + NORM_EPS)
        q_ref[hh] = (qn * qg_ref[...] * (hd ** -0.5 * LOG2_E)).astype(BF16)
        kh = qkv[:, d + hh * hd:d + (hh + 1) * hd]
        kn = kh * lax.rsqrt(jnp.mean(kh * kh, axis=-1, keepdims=True) + NORM_EPS) * kg_ref[...]
        kext_ref[hh, :, 0:hd] = kn.astype(BF16)
        kext_ref[hh, :, hd:2 * hd] = onehot
        km_ref[:, hh * hd:(hh + 1) * hd] = jnp.mean(kn, axis=0, keepdims=True)
        v_ref[hh] = qkv[:, 2 * d + hh * hd:2 * d + (hh + 1) * hd].astype(BF16)


def _attn_select_blocks(q_ref, km_ref, bias_ref, tq):
    nhs, s, hd = q_ref.shape
    nt = (((1,), (1,)), ((), ()))
    nbp = -(-(s // tq) // SUBLANES) * SUBLANES
    blk = lax.broadcasted_iota(jnp.int32, (nbp, s), 0)
    blk_f = blk.astype(F32)
    own = lax.broadcasted_iota(jnp.int32, (nbp, s), 1) // tq
    past = blk < own
    for hh in range(nhs):
        pieces, rest = [], km_ref[hh, 0:nbp, :]
        for _ in range(3):
            piece = rest.astype(BF16)
            pieces.append(piece)
            rest = rest - piece.astype(F32)
        gate3 = lax.dot_general(jnp.concatenate(pieces, axis=0), q_ref[hh], nt,
                                preferred_element_type=F32)
        gate = gate3[0:nbp] + gate3[nbp:2 * nbp] + gate3[2 * nbp:3 * nbp]
        g = jnp.where(past, gate, -jnp.inf)
        sel = blk == own
        for _ in range(MOBA_TOPK):
            m = jnp.max(g, axis=0, keepdims=True)
            idx = jnp.min(jnp.where(g == m, blk_f, float(nbp)), axis=0, keepdims=True)
            hit = blk_f == idx
            sel = sel | (hit & past)
            g = jnp.where(hit, -jnp.inf, g)
        bias_t = jnp.where(sel, 0.0, MASK_BIAS)
        pad = jnp.zeros((LANES - nbp, tq), F32)
        for c in range(s // tq):
            tile = jnp.concatenate([bias_t[:, c * tq:(c + 1) * tq], pad], axis=0).T
            bias_ref[hh, c * tq:(c + 1) * tq, :] = tile.astype(BF16)


def _attn_query_block_steps(qi, hh, q_ref, bias_ref, kext_ref, v_ref, o_ref, s_ref):
    tq = s_ref.shape[3]
    buf = qi % s_ref.shape[0]
    nt = (((1,), (1,)), ((), ()))
    rows = slice(qi * tq, (qi + 1) * tq)
    st = {}

    def score(j):
        if j == 0:
            st['q'] = jnp.concatenate([q_ref[hh, rows, :], bias_ref[hh, rows, :]], axis=1)
        sj = lax.dot_general(st['q'], kext_ref[hh, j * tq:(j + 1) * tq, :], nt,
                             preferred_element_type=F32)
        if j == qi:
            rr = lax.broadcasted_iota(jnp.int32, (tq, tq), 0)
            cc = lax.broadcasted_iota(jnp.int32, (tq, tq), 1)
            sj = jnp.where(cc <= rr, sj, MASK_BIAS)
        s_ref[buf, hh, j * tq:(j + 1) * tq, :] = sj
        st['mx'] = sj if j == 0 else jnp.maximum(st['mx'], sj)
        if j == qi:
            st['m'] = jnp.broadcast_to(jnp.max(st['mx'], axis=-1, keepdims=True), (tq, tq))

    def weigh(j):
        p = jnp.exp2(s_ref[buf, hh, j * tq:(j + 1) * tq, :] - st['m'])
        pv = jnp.dot(p.astype(BF16), v_ref[hh, j * tq:(j + 1) * tq, :],
                     preferred_element_type=F32)
        lj = jnp.sum(p, axis=-1, keepdims=True)
        st['l'] = lj if j == 0 else st['l'] + lj
        st['acc'] = pv if j == 0 else st['acc'] + pv
        if j == qi:
            o_ref[hh, rows, :] = (st['acc'] / st['l']).astype(BF16)

    blocks = range(qi + 1)
    return ([functools.partial(score, j) for j in blocks],
            [functools.partial(weigh, j) for j in blocks])


def _attn_query_blocks(qis, q_ref, bias_ref, kext_ref, v_ref, o_ref, s_ref):
    todo = [_attn_query_block_steps(qi, hh, q_ref, bias_ref, kext_ref, v_ref, o_ref, s_ref)
            for hh in range(s_ref.shape[1]) for qi in qis]
    pending = []
    for scores, weighs in todo + [([], [])]:
        for k in range(max(len(scores), len(pending))):
            if k < len(scores):
                scores[k]()
            if k < len(pending):
                pending[k]()
        pending = weighs


def _attn_kernel(q_ref, kext_ref, v_ref, km_ref, o_ref, s_ref, bias_ref):
    tq = s_ref.shape[3]
    nb = kext_ref.shape[1] // tq
    total = nb * (nb + 1) // 2
    bounds = [0]
    for part in range(1, ATTN_PARTS):
        qi = bounds[-1]
        while qi < nb and qi * (qi + 1) // 2 < total * part // ATTN_PARTS:
            qi += 1
        bounds.append(qi)
    bounds.append(nb)
    for part in range(ATTN_PARTS):
        @pl.when(pl.program_id(2) == part)
        def _(part=part):
            if part == 0:
                _attn_select_blocks(q_ref, km_ref, bias_ref, tq)
            _attn_query_blocks(range(bounds[part], bounds[part + 1]),
                               q_ref, bias_ref, kext_ref, v_ref, o_ref, s_ref)


def _attn_out_kernel(o_ref, x_ref, g1_ref, w_ref, y_ref):
    nh = o_ref.shape[0]
    o = jnp.concatenate([o_ref[hh] for hh in range(nh)], axis=1)
    out = jnp.dot(o, w_ref[...], preferred_element_type=F32)
    y_ref[...] = x_ref[...] + g1_ref[...] * out


def _moba_layer(x, mod, ng, w_qkv, q_g, k_g, w_out, ts_out=512, heads_per_step=1):
    b, s, d = x.shape
    nh = ATTN_HEADS
    hd = d // nh
    tq = MOBA_BLOCK
    nb = s // tq
    assert s % tq == 0 and nb <= LANES and hd == LANES
    full = lambda shape: pl.BlockSpec(shape, lambda bi, si: (0,) * len(shape))

    q, kext, v, km = pl.pallas_call(
        _qkv_kernel,
        grid=(b, nb),
        in_specs=[
            pl.BlockSpec((None, tq, d), lambda bi, si: (bi, si, 0)),
            _mod_spec(0, d), _mod_spec(1, d),
            full((1, d)), full((d, 3 * d)), full((1, hd)), full((1, hd)),
        ],
        out_specs=[
            pl.BlockSpec((None, nh, tq, hd), lambda bi, si: (bi, 0, si, 0)),
            pl.BlockSpec((None, nh, tq, 2 * hd), lambda bi, si: (bi, 0, si, 0)),
            pl.BlockSpec((None, nh, tq, hd), lambda bi, si: (bi, 0, si, 0)),
            pl.BlockSpec((None, None, 1, d), lambda bi, si: (bi, si, 0, 0)),
        ],
        out_shape=[
            jax.ShapeDtypeStruct((b, nh, s, hd), BF16),
            jax.ShapeDtypeStruct((b, nh, s, 2 * hd), BF16),
            jax.ShapeDtypeStruct((b, nh, s, hd), BF16),
            jax.ShapeDtypeStruct((b, nb, 1, d), F32),
        ],
        compiler_params=_cparams(("arbitrary", "arbitrary")),
    )(x, mod, mod, ng.reshape(1, d), w_qkv.astype(BF16), q_g.reshape(1, hd), k_g.reshape(1, hd))

    km = km.reshape(b, nb, nh, hd).transpose(0, 2, 1, 3)
    km = jnp.pad(km, ((0, 0), (0, 0), (0, LANES - nb), (0, 0)))

    o = pl.pallas_call(
        _attn_kernel,
        grid=(b, nh // heads_per_step, ATTN_PARTS),
        in_specs=[
            pl.BlockSpec((None, heads_per_step, s, hd), lambda bi, hi, pi: (bi, hi, 0, 0)),
            pl.BlockSpec((None, heads_per_step, s, 2 * hd), lambda bi, hi, pi: (bi, hi, 0, 0)),
            pl.BlockSpec((None, heads_per_step, s, hd), lambda bi, hi, pi: (bi, hi, 0, 0)),
            pl.BlockSpec((None, heads_per_step, LANES, hd), lambda bi, hi, pi: (bi, hi, 0, 0)),
        ],
        out_specs=pl.BlockSpec((None, heads_per_step, s, hd), lambda bi, hi, pi: (bi, hi, 0, 0)),
        out_shape=jax.ShapeDtypeStruct((b, nh, s, hd), BF16),
        scratch_shapes=[pltpu.VMEM((2, heads_per_step, s, tq), F32),
                        pltpu.VMEM((heads_per_step, s, LANES), BF16)],
        compiler_params=_cparams(("arbitrary", "arbitrary", "arbitrary")),
    )(q, kext, v, km)

    return pl.pallas_call(
        _attn_out_kernel,
        grid=(b, s // ts_out),
        in_specs=[
            pl.BlockSpec((None, nh, ts_out, hd), lambda bi, si: (bi, 0, si, 0)),
            pl.BlockSpec((None, ts_out, d), lambda bi, si: (bi, si, 0)),
            _mod_spec(2, d),
            full((d, d)),
        ],
        out_specs=pl.BlockSpec((None, ts_out, d), lambda bi, si: (bi, si, 0)),
        out_shape=jax.ShapeDtypeStruct(x.shape, F32),
        compiler_params=_cparams(("arbitrary", "arbitrary")),
    )(o, x, mod, w_out.astype(BF16))


def _route_kernel(x_ref, sh_ref, sc_ref, ng_ref, rw_ref, rb_ref,
                  h_ref, e_ref, w_ref, r_ref, cnt_ref, run_ref):
    tm = x_ref.shape[0]

    @pl.when(pl.program_id(0) == 0)
    def _():
        run_ref[...] = jnp.zeros_like(run_ref)

    h = _modulated_norm(x_ref[...], ng_ref[...], sc_ref[...], sh_ref[...])
    h_ref[...] = h
    logits = jnp.dot(h, rw_ref[...], preferred_element_type=F32, precision=HIGHEST) + rb_ref[...]
    lane = lax.broadcasted_iota(jnp.int32, (tm, LANES), 1)
    lane_f = lane.astype(F32)
    lg = jnp.where(lane < N_EXPERTS, logits, -jnp.inf)
    tops, hits = [], []
    for _ in range(TOP_K):
        m = jnp.max(lg, axis=-1, keepdims=True)
        idx = jnp.min(jnp.where(lg == m, lane_f, float(LANES)), axis=-1, keepdims=True)
        hit = lane_f == idx
        tops.append((m, idx))
        hits.append(hit)
        lg = jnp.where(hit, -jnp.inf, lg)
    exps = [jnp.exp(m - tops[0][0]) for m, _ in tops]
    denom = exps[0]
    for ex in exps[1:]:
        denom = denom + ex

    member = jnp.zeros((tm, LANES), F32)
    for hit in hits:
        member = member + jnp.where(hit, 1.0, 0.0)
    rr = lax.broadcasted_iota(jnp.int32, (tm, tm), 0)
    cc = lax.broadcasted_iota(jnp.int32, (tm, tm), 1)
    earlier = jnp.where(cc < rr, 1.0, 0.0).astype(BF16)
    before = jnp.dot(earlier, member.astype(BF16), preferred_element_type=F32) + run_ref[...]

    e_out = jnp.zeros((tm, LANES), F32)
    w_out = jnp.zeros((tm, LANES), F32)
    r_out = jnp.zeros((tm, LANES), F32)
    for k in range(TOP_K):
        rank = jnp.sum(jnp.where(hits[k], before, 0.0), axis=-1, keepdims=True)
        e_out = jnp.where(lane == k, tops[k][1], e_out)
        w_out = jnp.where(lane == k, exps[k] / denom, w_out)
        r_out = jnp.where(lane == k, rank, r_out)
    e_ref[...] = e_out.astype(jnp.int32)
    w_ref[...] = w_out
    r_ref[...] = r_out.astype(jnp.int32)
    run = run_ref[...] + jnp.sum(member, axis=0, keepdims=True)
    run_ref[...] = run
    cnt_ref[...] = run.astype(jnp.int32)


def _dispatch_kernel(n_items, tile_rows, start_ref, e_ref, r_ref, h_ref, xs_in_hbm, xs_hbm,
                     item_ref, sem):
    del xs_in_hbm
    tm = h_ref.shape[0]
    unroll = 32
    assert item_ref.shape[0] % unroll == 0 and tile_rows % unroll == 0
    groups = tile_rows // unroll
    assert groups & (groups - 1) == 0

    @pl.when(pl.program_id(0) == 0)
    def _():
        def clear(c, _):
            tile = lax.shift_right_logical(c, groups.bit_length() - 1)
            dummy = n_items + (tile & 1) * tile_rows + (c & (groups - 1)) * unroll
            for u in range(unroll):
                item_ref[c * unroll + u] = dummy + u
            return 0
        lax.fori_loop(0, item_ref.shape[0] // unroll, clear, 0)

    base = pl.program_id(0) * tm * TOP_K
    for t in range(tm):
        for k in range(TOP_K):
            j = t * TOP_K + k
            row = start_ref[e_ref[0, j]] + r_ref[0, j]
            item_ref[row] = base + j
            pltpu.make_async_copy(h_ref.at[pl.ds(t, 1), :], xs_hbm.at[pl.ds(row, 1), :],
                                  sem).start()
    for _ in range(TOP_K):
        pltpu.make_async_copy(h_ref, xs_hbm.at[pl.ds(0, tm), :], sem).wait()


def _expert_kernel(layer, be_ref, first_ref, nu_ref, nv_ref, wslot_ref, next_ref, item_ref,
                   xs_ref, wgu_hbm, bgu_ref, wd_hbm, bd_ref, yt_hbm,
                   wgu_f32, wd_f32, wgu_bf, wd_bf, ybuf, sems, wsems):
    i = pl.program_id(0)
    de = wd_bf.shape[0]
    tm = xs_ref.shape[0]
    n_items = yt_hbm.shape[0] - 2 * tm
    slot = i % 2
    nv = nv_ref[i]

    @pl.when(i == 0)
    def _():
        ybuf[...] = jnp.zeros_like(ybuf)
        fills = [pltpu.make_async_copy(ybuf.at[par], yt_hbm.at[pl.ds(n_items + par * tm, tm), :],
                                       sems.at[par]) for par in range(2)]
        for cp in fills:
            cp.start()
        for cp in fills:
            cp.wait()

    def weight_copies(expert, ws):
        return (pltpu.make_async_copy(wgu_hbm.at[layer, expert], wgu_f32.at[ws], wsems.at[ws, 0]),
                pltpu.make_async_copy(wd_hbm.at[layer, expert], wd_f32.at[ws], wsems.at[ws, 1]))

    @pl.when(nv > 0)
    def _():
        @pl.when(first_ref[i] == 1)
        def _():
            ws = wslot_ref[i]

            @pl.when(i == 0)
            def _():
                for cp in weight_copies(be_ref[0], 0):
                    cp.start()

            for cp in weight_copies(be_ref[i], ws):
                cp.wait()

            @pl.when(next_ref[i] >= 0)
            def _():
                for cp in weight_copies(next_ref[i], 1 - ws):
                    cp.start()

            wgu_bf[...] = wgu_f32[ws].astype(BF16)
            wd_bf[...] = wd_f32[ws].astype(BF16)

        gu = jnp.dot(xs_ref[...].astype(BF16), wgu_bf[...], preferred_element_type=F32) + bgu_ref[...]
        gate = jnp.minimum(gu[:, :de], SWIGLU_LIMIT)
        up = jnp.clip(gu[:, de:], -SWIGLU_LIMIT, SWIGLU_LIMIT)
        act = gate * _sigmoid(SWIGLU_ALPHA * gate) * (up + 1.0)
        y = jnp.dot(act.astype(BF16), wd_bf[...], preferred_element_type=F32) + bd_ref[...]
        for par in range(2):
            @pl.when(slot == par)
            def _(par=par):
                ybuf[par] = y
                for r in range(tm):
                    pltpu.make_async_copy(ybuf.at[par, pl.ds(r, 1), :],
                                          yt_hbm.at[pl.ds(item_ref[0, r], 1), :],
                                          sems.at[par]).start()

    def wait_tile(par):
        pltpu.make_async_copy(ybuf.at[par], yt_hbm.at[pl.ds(0, tm), :], sems.at[par]).wait()

    for par in range(2):
        @pl.when((i > 0) & (slot == 1 - par) & (nv_ref[jnp.maximum(i - 1, 0)] > 0))
        def _(par=par):
            wait_tile(par)

        @pl.when((i == pl.num_programs(0) - 1) & (slot == par) & (nv > 0))
        def _(par=par):
            wait_tile(par)


def _combine_kernel(x_ref, yt_ref, w_ref, g2_ref, o_ref):
    o_ref[...] = x_ref[...] + g2_ref[...] * (w_ref[...] * yt_ref[...])


def _tok_mod_spec(j, tm, s, d):
    return pl.BlockSpec((None, None, 1, d), lambda i: (j, (i * tm) // s, 0, 0))


def _moe_route(xt, t, s, mod, ng, router_w, router_b, tm=256):
    d = xt.shape[1]
    assert t % tm == 0
    rw = jnp.pad(router_w, ((0, 0), (0, LANES - N_EXPERTS)))
    rb = jnp.pad(router_b, (0, LANES - N_EXPERTS)).reshape(1, LANES)
    return pl.pallas_call(
        _route_kernel,
        grid=(t // tm,),
        in_specs=[
            pl.BlockSpec((tm, d), lambda i: (i, 0)),
            _tok_mod_spec(3, tm, s, d), _tok_mod_spec(4, tm, s, d),
            pl.BlockSpec((1, d), lambda i: (0, 0)),
            pl.BlockSpec((d, LANES), lambda i: (0, 0)),
            pl.BlockSpec((1, LANES), lambda i: (0, 0)),
        ],
        out_specs=[
            pl.BlockSpec((tm, d), lambda i: (i, 0)),
            pl.BlockSpec((tm, LANES), lambda i: (i, 0)),
            pl.BlockSpec((tm, LANES), lambda i: (i, 0)),
            pl.BlockSpec((tm, LANES), lambda i: (i, 0)),
            pl.BlockSpec((1, LANES), lambda i: (0, 0)),
        ],
        out_shape=[
            jax.ShapeDtypeStruct((t, d), F32),
            jax.ShapeDtypeStruct((t, LANES), jnp.int32),
            jax.ShapeDtypeStruct((t, LANES), F32),
            jax.ShapeDtypeStruct((t, LANES), jnp.int32),
            jax.ShapeDtypeStruct((1, LANES), jnp.int32),
        ],
        scratch_shapes=[pltpu.VMEM((1, LANES), F32)],
        compiler_params=_cparams(("arbitrary",)),
    )(xt, mod, mod, ng.reshape(1, d), rw, rb)


def _moe_plan(t, counts, tm):
    n_e = N_EXPERTS
    counts = counts[0, :n_e]
    padded = (counts + tm - 1) // tm * tm
    pad_end = jnp.cumsum(padded)
    pad_start = pad_end - padded
    n_tiles = (t * TOP_K) // tm + n_e
    tile_start = jnp.arange(n_tiles, dtype=jnp.int32) * tm
    n_used = (pad_end[-1] // tm).astype(jnp.int32)
    tile_expert = jnp.minimum(jnp.searchsorted(pad_end, tile_start, side='right'), n_e - 1)
    tile_expert = jnp.where(jnp.arange(n_tiles) < n_used, tile_expert,
                            tile_expert[jnp.maximum(n_used - 1, 0)]).astype(jnp.int32)
    first = jnp.concatenate([jnp.ones((1,), jnp.int32),
                             (tile_expert[1:] != tile_expert[:-1]).astype(jnp.int32)])
    seg_end = (pad_start + counts)[tile_expert]
    n_valid = jnp.where(jnp.arange(n_tiles) < n_used,
                        jnp.clip(seg_end - tile_start, 0, tm), 0).astype(jnp.int32)
    used = jnp.arange(n_tiles) < n_used
    w_slot = ((jnp.cumsum(first) - 1) % 2).astype(jnp.int32)
    first_pos = jnp.where((first == 1) & used, jnp.arange(n_tiles), n_tiles)
    next_first = lax.cummin(first_pos[::-1])[::-1]
    next_first = jnp.concatenate([next_first[1:], jnp.full((1,), n_tiles)])
    next_expert = jnp.where(next_first < n_tiles,
                            tile_expert[jnp.minimum(next_first, n_tiles - 1)], -1).astype(jnp.int32)
    return (pad_start.astype(jnp.int32), tile_expert, first, n_used.reshape(1), n_valid,
            w_slot, next_expert, n_tiles)


def _moe_dispatch(h2, pad_start, top_e, rank, xs_buf, tm_exp, tm=128):
    t, d = h2.shape
    n_rows = xs_buf.shape[0]
    flat = lambda a: a[:, :TOP_K].reshape(t // tm, 1, tm * TOP_K)
    item_spec = pl.BlockSpec((None, 1, tm * TOP_K), lambda i: (i, 0, 0), memory_space=pltpu.SMEM)
    return pl.pallas_call(
        functools.partial(_dispatch_kernel, t * TOP_K, tm_exp),
        grid=(t // tm,),
        in_specs=[
            pl.BlockSpec(memory_space=pltpu.SMEM),
            item_spec, item_spec,
            pl.BlockSpec((tm, d), lambda i: (i, 0)),
            pl.BlockSpec(memory_space=pl.ANY),
        ],
        out_specs=[pl.BlockSpec(memory_space=pl.ANY), pl.BlockSpec(memory_space=pltpu.SMEM)],
        out_shape=[jax.ShapeDtypeStruct((n_rows, d), F32),
                   jax.ShapeDtypeStruct((n_rows,), jnp.int32)],
        scratch_shapes=[pltpu.SemaphoreType.DMA],
        input_output_aliases={4: 0},
        compiler_params=_cparams(("arbitrary",)),
    )(pad_start, flat(top_e), flat(rank), h2, xs_buf)


def _moe_experts(xs, n_items, tile_expert, first, n_used, n_valid, w_slot, next_expert, row_item,
                 w_gu, b_gu, w_down, b_down, layer, tm):
    n_rows, d = xs.shape
    n_e = N_EXPERTS
    de = w_down.shape[2]
    n_tiles = n_rows // tm
    tile_idx = lambda i, be, fi, nu, *_: (jnp.maximum(jnp.minimum(i, nu[0] - 1), 0), 0)
    b_idx = lambda i, be, *_: (layer, be[i], 0, 0)
    return pl.pallas_call(
        functools.partial(_expert_kernel, layer),
        grid_spec=pltpu.PrefetchScalarGridSpec(
            num_scalar_prefetch=6,
            grid=(n_tiles,),
            in_specs=[
                pl.BlockSpec((None, 1, tm), lambda i, *_: (i, 0, 0), memory_space=pltpu.SMEM),
                pl.BlockSpec((tm, d), tile_idx),
                pl.BlockSpec(memory_space=pl.ANY),
                pl.BlockSpec((None, None, 1, 2 * de), b_idx),
                pl.BlockSpec(memory_space=pl.ANY),
                pl.BlockSpec((None, None, 1, d), b_idx),
            ],
            out_specs=pl.BlockSpec(memory_space=pl.ANY),
            scratch_shapes=[pltpu.VMEM((2, d, 2 * de), F32), pltpu.VMEM((2, de, d), F32),
                            pltpu.VMEM((d, 2 * de), BF16), pltpu.VMEM((de, d), BF16),
                            pltpu.VMEM((2, tm, d), F32), pltpu.SemaphoreType.DMA((2,)),
                            pltpu.SemaphoreType.DMA((2, 2))],
        ),
        out_shape=jax.ShapeDtypeStruct((n_items + 2 * tm, d), F32),
        compiler_params=_cparams(("arbitrary",)),
    )(tile_expert, first, n_used, n_valid, w_slot, next_expert, row_item.reshape(n_tiles, 1, tm),
      xs, w_gu, b_gu.reshape(b_gu.shape[0], n_e, 1, 2 * de), w_down,
      b_down.reshape(b_down.shape[0], n_e, 1, d))


def _moe_combine(xt, s, mod, w_item, yt, tm=512):
    t, d = xt.shape
    return pl.pallas_call(
        _combine_kernel,
        grid=(t // tm,),
        in_specs=[
            pl.BlockSpec((tm, d), lambda i: (i, 0)),
            pl.BlockSpec((tm, d), lambda i: (i, 0)),
            pl.BlockSpec((tm, 1), lambda i: (i, 0)),
            _tok_mod_spec(5, tm, s, d),
        ],
        out_specs=pl.BlockSpec((tm, d), lambda i: (i, 0)),
        out_shape=jax.ShapeDtypeStruct((t, d), F32),
        compiler_params=_cparams(("arbitrary",)),
    )(xt, yt, w_item, mod)


def _moe_rows(n_tokens, tm_exp=256):
    return (n_tokens // tm_exp + N_EXPERTS) * tm_exp


def _moe_layer(x, xs_buf, mod, ng, router_w, router_b, w_gu, b_gu, w_down, b_down, layer,
               tm_exp=256):
    b, s, d = x.shape
    t = b * s
    xt = x.reshape(t, d)
    assert t % TOP_K == 0
    n_tok = t // TOP_K
    h2, top_e, top_w, rank, counts = _moe_route(xt, n_tok, s, mod, ng, router_w, router_b)
    (pad_start, tile_expert, first, n_used, n_valid, w_slot, next_expert,
     n_tiles) = _moe_plan(n_tok, counts, tm_exp)
    assert xs_buf.shape[0] == n_tiles * tm_exp
    xs, row_item = _moe_dispatch(h2, pad_start, top_e, rank, xs_buf, tm_exp)
    yt = _moe_experts(xs, t, tile_expert, first, n_used, n_valid, w_slot, next_expert, row_item,
                      w_gu, b_gu, w_down, b_down, layer, tm_exp)
    w_item = top_w[:, :TOP_K].reshape(t, 1)
    return _moe_combine(xt, s, mod, w_item, yt).reshape(b, s, d), xs


def kernel(x, c, norm_mix_g, norm_ffn_g, w_mod, b_mod, lru_w_in, lru_conv_w, lru_conv_b, lru_w_a, lru_b_a, lru_w_x, lru_b_x, lru_lambda, lru_w_out, attn_w_qkv, attn_q_norm_g, attn_k_norm_g, attn_w_out, router_w, router_b, expert_w_gu, expert_b_gu, expert_w_down, expert_b_down):
    depth = w_mod.shape[0]
    mods = _modulation(c, w_mod, b_mod)
    n_tokens = x.shape[0] * x.shape[1]
    xs_buf = jnp.zeros((_moe_rows(n_tokens), x.shape[2]), F32)
    for i in range(depth):
        mod = mods[i]
        j = i // 2
        if i % 2 == 0:
            x = _rglru_layer(x, mod, norm_mix_g[i], lru_w_in[j], lru_conv_w[j], lru_conv_b[j],
                             lru_w_a[j], lru_b_a[j], lru_w_x[j], lru_b_x[j], lru_lambda[j],
                             lru_w_out[j])
        else:
            x = _moba_layer(x, mod, norm_mix_g[i], attn_w_qkv[j], attn_q_norm_g[j],
                            attn_k_norm_g[j], attn_w_out[j])
        x, xs_buf = _moe_layer(x, xs_buf, mod, norm_ffn_g[i], router_w[i], router_b[i],
                               expert_w_gu, expert_b_gu, expert_w_down, expert_b_down, i)
    return x
```

```python
import functools

import jax
import jax.numpy as jnp
from jax import lax
from jax.experimental import pallas as pl
from jax.experimental.pallas import tpu as pltpu

NORM_EPS = 1e-6
N_MOD = 6
LRU_BLOCKS = 4
CONV_WIDTH = 4
LRU_C = 8.0
ATTN_HEADS = 8
MOBA_BLOCK = 256
MOBA_TOPK = 3
N_EXPERTS = 32
TOP_K = 4
SWIGLU_LIMIT = 7.0
SWIGLU_ALPHA = 1.702

LANES = 128
SUBLANES = 8
MASK_BIAS = -(2.0 ** 100)
LOG2_E = 1.4426950408889634
ATTN_PARTS = 1
VMEM_LIMIT = 52 * 1024 * 1024

F32 = jnp.float32
BF16 = jnp.bfloat16
HIGHEST = lax.Precision.HIGHEST


def _cparams(sem):
    return pltpu.CompilerParams(dimension_semantics=sem, vmem_limit_bytes=VMEM_LIMIT)


def _sigmoid(z):
    return 1.0 / (1.0 + jnp.exp(-z))


def _modulated_norm(x, g, sc, sh):
    ms = jnp.mean(x * x, axis=-1, keepdims=True)
    return x * lax.rsqrt(ms + NORM_EPS) * (g * (1.0 + sc)) + sh


def _mod_kernel(c_ref, w_ref, b_ref, o_ref):
    c = c_ref[...]
    cond = c * _sigmoid(c)
    o_ref[...] = jnp.dot(cond, w_ref[...], preferred_element_type=F32,
                         precision=HIGHEST) + b_ref[...]


def _modulation(c, w_mod, b_mod):
    depth, d, _ = w_mod.shape
    b = c.shape[0]
    rows = -(-b // SUBLANES) * SUBLANES
    c_pad = jnp.pad(c, ((0, rows - b), (0, 0)))
    out = pl.pallas_call(
        _mod_kernel,
        grid=(depth, N_MOD),
        in_specs=[
            pl.BlockSpec((rows, d), lambda i, j: (0, 0)),
            pl.BlockSpec((None, d, d), lambda i, j: (i, 0, j)),
            pl.BlockSpec((None, None, 1, d), lambda i, j: (i, j, 0, 0)),
        ],
        out_specs=pl.BlockSpec((None, None, rows, d), lambda i, j: (i, j, 0, 0)),
        out_shape=jax.ShapeDtypeStruct((depth, N_MOD, rows, d), F32),
        compiler_params=_cparams(("arbitrary", "arbitrary")),
    )(c_pad, w_mod, b_mod.reshape(depth, N_MOD, 1, d))
    return out[:, :, :b].reshape(depth, N_MOD, b, 1, d)


def _mod_spec(j, d):
    return pl.BlockSpec((None, None, 1, d), lambda b, s: (j, b, 0, 0))


def _rglru_kernel(x_ref, sh_ref, sc_ref, g1_ref, ng_ref, win_ref, cw_ref, cb_ref,
                  wa_ref, ba_ref, wx_ref, bx_ref, lam_ref, wout_ref, o_ref,
                  ext_ref, a_ref, u_ref, h_ref):
    ts, d = x_ref.shape
    bw = d // LRU_BLOCKS

    @pl.when(pl.program_id(1) == 0)
    def _():
        ext_ref[...] = jnp.zeros_like(ext_ref)
        h_ref[...] = jnp.zeros_like(h_ref)

    x = x_ref[...]
    h = _modulated_norm(x, ng_ref[...], sc_ref[...], sh_ref[...])
    gr = jnp.dot(h.astype(BF16), win_ref[...], preferred_element_type=F32)
    gate_branch = gr[:, :d]
    rec = gr[:, d:]

    groups = ts // SUBLANES
    rec3 = rec.reshape(groups, SUBLANES, d)
    tail3 = ext_ref[...].reshape(1, SUBLANES, d)
    grow = lax.broadcasted_iota(jnp.int32, rec3.shape, 1)
    xc3 = cb_ref[...] + cw_ref[CONV_WIDTH - 1:CONV_WIDTH, :] * rec3
    for s in range(1, CONV_WIDTH):
        rot = pltpu.roll(rec3, s, 1)
        rot_before = jnp.concatenate([pltpu.roll(tail3, s, 1), rot[:groups - 1]], axis=0)
        tap = cw_ref[CONV_WIDTH - 1 - s:CONV_WIDTH - s, :]
        xc3 = xc3 + tap * jnp.where(grow >= s, rot, rot_before)
    xc = xc3.reshape(ts, d)
    ext_ref[...] = rec[ts - SUBLANES:, :]

    xcb = xc.astype(BF16)
    ra = jnp.concatenate(
        [jnp.dot(xcb[:, g * bw:(g + 1) * bw], wa_ref[g], preferred_element_type=F32)
         for g in range(LRU_BLOCKS)], axis=1) + ba_ref[...]
    rx = jnp.concatenate(
        [jnp.dot(xcb[:, g * bw:(g + 1) * bw], wx_ref[g], preferred_element_type=F32)
         for g in range(LRU_BLOCKS)], axis=1) + bx_ref[...]
    r = _sigmoid(ra)
    ig = _sigmoid(rx)
    z = -lam_ref[...]
    softplus = jnp.maximum(z, 0.0) + jnp.log(1.0 + jnp.exp(-jnp.abs(z)))
    a = jnp.exp2(r * ((-LRU_C * LOG2_E) * softplus))
    v = 1.0 - a * a
    u = jnp.where(v > 0.0, v * lax.rsqrt(v), 0.0) * (ig * xc)

    a = a.reshape(ts // SUBLANES, SUBLANES, d)
    u = u.reshape(ts // SUBLANES, SUBLANES, d)
    row = lax.broadcasted_iota(jnp.int32, a.shape, 1)
    k = 1
    while k < SUBLANES:
        a_prev = pltpu.roll(a, k, 1)
        u_prev = pltpu.roll(u, k, 1)
        m = row >= k
        u = jnp.where(m, a * u_prev + u, u)
        a = jnp.where(m, a * a_prev, a)
        k *= 2
    a_ref[...] = a.reshape(ts, d)
    u_ref[...] = u.reshape(ts, d)

    def group(j, hc):
        r0 = pl.multiple_of(j * SUBLANES, SUBLANES)
        hs = u_ref[pl.ds(r0, SUBLANES), :] + a_ref[pl.ds(r0, SUBLANES), :] * hc
        u_ref[pl.ds(r0, SUBLANES), :] = hs
        return hs[SUBLANES - 1:SUBLANES, :]

    h_ref[0:1, :] = lax.fori_loop(0, ts // SUBLANES, group, h_ref[0:1, :])

    y = jax.nn.gelu(gate_branch, approximate=True) * u_ref[...]
    out = jnp.dot(y.astype(BF16), wout_ref[...], preferred_element_type=F32)
    o_ref[...] = x + g1_ref[...] * out


def _rglru_layer(x, mod, ng, w_in, conv_w, conv_b, w_a, b_a, w_x, b_x, lam, w_out, ts=256):
    b, s, d = x.shape
    full = lambda shape: pl.BlockSpec(shape, lambda bi, si: (0,) * len(shape))
    row = lambda v: v.reshape(1, d)
    return pl.pallas_call(
        _rglru_kernel,
        grid=(b, s // ts),
        in_specs=[
            pl.BlockSpec((None, ts, d), lambda bi, si: (bi, si, 0)),
            _mod_spec(0, d), _mod_spec(1, d), _mod_spec(2, d),
            full((1, d)), full((d, 2 * d)), full((CONV_WIDTH, d)), full((1, d)),
            full(w_a.shape), full((1, d)), full(w_x.shape), full((1, d)), full((1, d)),
            full((d, d)),
        ],
        out_specs=pl.BlockSpec((None, ts, d), lambda bi, si: (bi, si, 0)),
        out_shape=jax.ShapeDtypeStruct(x.shape, F32),
        scratch_shapes=[
            pltpu.VMEM((SUBLANES, d), F32),
            pltpu.VMEM((ts, d), F32),
            pltpu.VMEM((ts, d), F32),
            pltpu.VMEM((SUBLANES, d), F32),
        ],
        compiler_params=_cparams(("arbitrary", "arbitrary")),
    )(x, mod, mod, mod, row(ng), w_in.astype(BF16), conv_w, row(conv_b),
      w_a.astype(BF16), row(b_a), w_x.astype(BF16), row(b_x), row(lam), w_out.astype(BF16))


def _qkv_kernel(x_ref, sh_ref, sc_ref, ng_ref, w_ref, qg_ref, kg_ref,
                q_ref, kext_ref, v_ref, km_ref):
    ts, d = x_ref.shape
    nh, _, hd = q_ref.shape
    blk = pl.program_id(1)
    h = _modulated_norm(x_ref[...], ng_ref[...], sc_ref[...], sh_ref[...])
    qkv = jnp.dot(h.astype(BF16), w_ref[...], preferred_element_type=F32)
    lane = lax.broadcasted_iota(jnp.int32, (ts, hd), 1)
    onehot = jnp.where(lane == blk, 1.0, 0.0).astype(BF16)
    for hh in range(nh):
        qh = qkv[:, hh * hd:(hh + 1) * hd]
        qn = qh * lax.rsqrt(jnp.mean(qh * qh, axis=-1, keepdims=True) + NORM_EPS)
        q_ref[hh] = (qn * qg_ref[...] * (hd ** -0.5 * LOG2_E)).astype(BF16)
        kh = qkv[:, d + hh * hd:d + (hh + 1) * hd]
        kn = kh * lax.rsqrt(jnp.mean(kh * kh, axis=-1, keepdims=True) + NORM_EPS) * kg_ref[...]
        kext_ref[hh, :, 0:hd] = kn.astype(BF16)
        kext_ref[hh, :, hd:2 * hd] = onehot
        km_ref[:, hh * hd:(hh + 1) * hd] = jnp.mean(kn, axis=0, keepdims=True)
        v_ref[hh] = qkv[:, 2 * d + hh * hd:2 * d + (hh + 1) * hd].astype(BF16)


def _attn_select_blocks(q_ref, km_ref, bias_ref, tq):
    nhs, s, hd = q_ref.shape
    nt = (((1,), (1,)), ((), ()))
    nbp = -(-(s // tq) // SUBLANES) * SUBLANES
    blk = lax.broadcasted_iota(jnp.int32, (nbp, s), 0)
    blk_f = blk.astype(F32)
    own = lax.broadcasted_iota(jnp.int32, (nbp, s), 1) // tq
    past = blk < own
    for hh in range(nhs):
        pieces, rest = [], km_ref[hh, 0:nbp, :]
        for _ in range(3):
            piece = rest.astype(BF16)
            pieces.append(piece)
            rest = rest - piece.astype(F32)
        gate3 = lax.dot_general(jnp.concatenate(pieces, axis=0), q_ref[hh], nt,
                                preferred_element_type=F32)
        gate = gate3[0:nbp] + gate3[nbp:2 * nbp] + gate3[2 * nbp:3 * nbp]
        g = jnp.where(past, gate, -jnp.inf)
        sel = blk == own
        for _ in range(MOBA_TOPK):
            m = jnp.max(g, axis=0, keepdims=True)
            idx = jnp.min(jnp.where(g == m, blk_f, float(nbp)), axis=0, keepdims=True)
            hit = blk_f == idx
            sel = sel | (hit & past)
            g = jnp.where(hit, -jnp.inf, g)
        bias_t = jnp.where(sel, 0.0, MASK_BIAS)
        pad = jnp.zeros((LANES - nbp, tq), F32)
        for c in range(s // tq):
            tile = jnp.concatenate([bias_t[:, c * tq:(c + 1) * tq], pad], axis=0).T
            bias_ref[hh, c * tq:(c + 1) * tq, :] = tile.astype(BF16)


def _attn_query_block_steps(qi, hh, q_ref, bias_ref, kext_ref, v_ref, o_ref, s_ref):
    tq = s_ref.shape[3]
    buf = qi % s_ref.shape[0]
    nt = (((1,), (1,)), ((), ()))
    rows = slice(qi * tq, (qi + 1) * tq)
    st = {}

    def score(j):
        if j == 0:
            st['q'] = jnp.concatenate([q_ref[hh, rows, :], bias_ref[hh, rows, :]], axis=1)
        sj = lax.dot_general(st['q'], kext_ref[hh, j * tq:(j + 1) * tq, :], nt,
                             preferred_element_type=F32)
        if j == qi:
            rr = lax.broadcasted_iota(jnp.int32, (tq, tq), 0)
            cc = lax.broadcasted_iota(jnp.int32, (tq, tq), 1)
            sj = jnp.where(cc <= rr, sj, MASK_BIAS)
        s_ref[buf, hh, j * tq:(j + 1) * tq, :] = sj
        st['mx'] = sj if j == 0 else jnp.maximum(st['mx'], sj)
        if j == qi:
            st['m'] = jnp.broadcast_to(jnp.max(st['mx'], axis=-1, keepdims=True), (tq, tq))

    def weigh(j):
        p = jnp.exp2(s_ref[buf, hh, j * tq:(j + 1) * tq, :] - st['m'])
        pv = jnp.dot(p.astype(BF16), v_ref[hh, j * tq:(j + 1) * tq, :],
                     preferred_element_type=F32)
        lj = jnp.sum(p, axis=-1, keepdims=True)
        st['l'] = lj if j == 0 else st['l'] + lj
        st['acc'] = pv if j == 0 else st['acc'] + pv
        if j == qi:
            o_ref[hh, rows, :] = (st['acc'] / st['l']).astype(BF16)

    blocks = range(qi + 1)
    return ([functools.partial(score, j) for j in blocks],
            [functools.partial(weigh, j) for j in blocks])


def _attn_query_blocks(qis, q_ref, bias_ref, kext_ref, v_ref, o_ref, s_ref):
    todo = [_attn_query_block_steps(qi, hh, q_ref, bias_ref, kext_ref, v_ref, o_ref, s_ref)
            for hh in range(s_ref.shape[1]) for qi in qis]
    pending = []
    for scores, weighs in todo + [([], [])]:
        for k in range(max(len(scores), len(pending))):
            if k < len(scores):
                scores[k]()
            if k < len(pending):
                pending[k]()
        pending = weighs


def _attn_kernel(q_ref, kext_ref, v_ref, km_ref, o_ref, s_ref, bias_ref):
    tq = s_ref.shape[3]
    nb = kext_ref.shape[1] // tq
    total = nb * (nb + 1) // 2
    bounds = [0]
    for part in range(1, ATTN_PARTS):
        qi = bounds[-1]
        while qi < nb and qi * (qi + 1) // 2 < total * part // ATTN_PARTS:
            qi += 1
        bounds.append(qi)
    bounds.append(nb)
    for part in range(ATTN_PARTS):
        @pl.when(pl.program_id(2) == part)
        def _(part=part):
            if part == 0:
                _attn_select_blocks(q_ref, km_ref, bias_ref, tq)
            _attn_query_blocks(range(bounds[part], bounds[part + 1]),
                               q_ref, bias_ref, kext_ref, v_ref, o_ref, s_ref)


def _attn_out_kernel(o_ref, x_ref, g1_ref, w_ref, y_ref):
    nh = o_ref.shape[0]
    o = jnp.concatenate([o_ref[hh] for hh in range(nh)], axis=1)
    out = jnp.dot(o, w_ref[...], preferred_element_type=F32)
    y_ref[...] = x_ref[...] + g1_ref[...] * out


def _moba_layer(x, mod, ng, w_qkv, q_g, k_g, w_out, ts_out=512, heads_per_step=1):
    b, s, d = x.shape
    nh = ATTN_HEADS
    hd = d // nh
    tq = MOBA_BLOCK
    nb = s // tq
    assert s % tq == 0 and nb <= LANES and hd == LANES
    full = lambda shape: pl.BlockSpec(shape, lambda bi, si: (0,) * len(shape))

    q, kext, v, km = pl.pallas_call(
        _qkv_kernel,
        grid=(b, nb),
        in_specs=[
            pl.BlockSpec((None, tq, d), lambda bi, si: (bi, si, 0)),
            _mod_spec(0, d), _mod_spec(1, d),
            full((1, d)), full((d, 3 * d)), full((1, hd)), full((1, hd)),
        ],
        out_specs=[
            pl.BlockSpec((None, nh, tq, hd), lambda bi, si: (bi, 0, si, 0)),
            pl.BlockSpec((None, nh, tq, 2 * hd), lambda bi, si: (bi, 0, si, 0)),
            pl.BlockSpec((None, nh, tq, hd), lambda bi, si: (bi, 0, si, 0)),
            pl.BlockSpec((None, None, 1, d), lambda bi, si: (bi, si, 0, 0)),
        ],
        out_shape=[
            jax.ShapeDtypeStruct((b, nh, s, hd), BF16),
            jax.ShapeDtypeStruct((b, nh, s, 2 * hd), BF16),
            jax.ShapeDtypeStruct((b, nh, s, hd), BF16),
            jax.ShapeDtypeStruct((b, nb, 1, d), F32),
        ],
        compiler_params=_cparams(("arbitrary", "arbitrary")),
    )(x, mod, mod, ng.reshape(1, d), w_qkv.astype(BF16), q_g.reshape(1, hd), k_g.reshape(1, hd))

    km = km.reshape(b, nb, nh, hd).transpose(0, 2, 1, 3)
    km = jnp.pad(km, ((0, 0), (0, 0), (0, LANES - nb), (0, 0)))

    o = pl.pallas_call(
        _attn_kernel,
        grid=(b, nh // heads_per_step, ATTN_PARTS),
        in_specs=[
            pl.BlockSpec((None, heads_per_step, s, hd), lambda bi, hi, pi: (bi, hi, 0, 0)),
            pl.BlockSpec((None, heads_per_step, s, 2 * hd), lambda bi, hi, pi: (bi, hi, 0, 0)),
            pl.BlockSpec((None, heads_per_step, s, hd), lambda bi, hi, pi: (bi, hi, 0, 0)),
            pl.BlockSpec((None, heads_per_step, LANES, hd), lambda bi, hi, pi: (bi, hi, 0, 0)),
        ],
        out_specs=pl.BlockSpec((None, heads_per_step, s, hd), lambda bi, hi, pi: (bi, hi, 0, 0)),
        out_shape=jax.ShapeDtypeStruct((b, nh, s, hd), BF16),
        scratch_shapes=[pltpu.VMEM((2, heads_per_step, s, tq), F32),
                        pltpu.VMEM((heads_per_step, s, LANES), BF16)],
        compiler_params=_cparams(("arbitrary", "arbitrary", "arbitrary")),
    )(q, kext, v, km)

    return pl.pallas_call(
        _attn_out_kernel,
        grid=(b, s // ts_out),
        in_specs=[
            pl.BlockSpec((None, nh, ts_out, hd), lambda bi, si: (bi, 0, si, 0)),
            pl.BlockSpec((None, ts_out, d), lambda bi, si: (bi, si, 0)),
            _mod_spec(2, d),
            full((d, d)),
        ],
        out_specs=pl.BlockSpec((None, ts_out, d), lambda bi, si: (bi, si, 0)),
        out_shape=jax.ShapeDtypeStruct(x.shape, F32),
        compiler_params=_cparams(("arbitrary", "arbitrary")),
    )(o, x, mod, w_out.astype(BF16))


def _route_kernel(x_ref, sh_ref, sc_ref, ng_ref, rw_ref, rb_ref,
                  h_ref, e_ref, w_ref, r_ref, cnt_ref, run_ref):
    tm = x_ref.shape[0]

    @pl.when(pl.program_id(0) == 0)
    def _():
        run_ref[...] = jnp.zeros_like(run_ref)

    h = _modulated_norm(x_ref[...], ng_ref[...], sc_ref[...], sh_ref[...])
    h_ref[...] = h
    logits = jnp.dot(h, rw_ref[...], preferred_element_type=F32, precision=HIGHEST) + rb_ref[...]
    lane = lax.broadcasted_iota(jnp.int32, (tm, LANES), 1)
    lane_f = lane.astype(F32)
    lg = jnp.where(lane < N_EXPERTS, logits, -jnp.inf)
    tops, hits = [], []
    for _ in range(TOP_K):
        m = jnp.max(lg, axis=-1, keepdims=True)
        idx = jnp.min(jnp.where(lg == m, lane_f, float(LANES)), axis=-1, keepdims=True)
        hit = lane_f == idx
        tops.append((m, idx))
        hits.append(hit)
        lg = jnp.where(hit, -jnp.inf, lg)
    exps = [jnp.exp(m - tops[0][0]) for m, _ in tops]
    denom = exps[0]
    for ex in exps[1:]:
        denom = denom + ex

    member = jnp.zeros((tm, LANES), F32)
    for hit in hits:
        member = member + jnp.where(hit, 1.0, 0.0)
    rr = lax.broadcasted_iota(jnp.int32, (tm, tm), 0)
    cc = lax.broadcasted_iota(jnp.int32, (tm, tm), 1)
    earlier = jnp.where(cc < rr, 1.0, 0.0).astype(BF16)
    before = jnp.dot(earlier, member.astype(BF16), preferred_element_type=F32) + run_ref[...]

    e_out = jnp.zeros((tm, LANES), F32)
    w_out = jnp.zeros((tm, LANES), F32)
    r_out = jnp.zeros((tm, LANES), F32)
    for k in range(TOP_K):
        rank = jnp.sum(jnp.where(hits[k], before, 0.0), axis=-1, keepdims=True)
        e_out = jnp.where(lane == k, tops[k][1], e_out)
        w_out = jnp.where(lane == k, exps[k] / denom, w_out)
        r_out = jnp.where(lane == k, rank, r_out)
    e_ref[...] = e_out.astype(jnp.int32)
    w_ref[...] = w_out
    r_ref[...] = r_out.astype(jnp.int32)
    run = run_ref[...] + jnp.sum(member, axis=0, keepdims=True)
    run_ref[...] = run
    cnt_ref[...] = run.astype(jnp.int32)


def _dispatch_kernel(n_items, tile_rows, start_ref, e_ref, r_ref, h_ref, xs_in_hbm, xs_hbm,
                     item_ref, sem):
    del xs_in_hbm
    tm = h_ref.shape[0]
    unroll = 32
    assert item_ref.shape[0] % unroll == 0 and tile_rows % unroll == 0
    groups = tile_rows // unroll
    assert groups & (groups - 1) == 0

    @pl.when(pl.program_id(0) == 0)
    def _():
        def clear(c, _):
            tile = lax.shift_right_logical(c, groups.bit_length() - 1)
            dummy = n_items + (tile & 1) * tile_rows + (c & (groups - 1)) * unroll
            for u in range(unroll):
                item_ref[c * unroll + u] = dummy + u
            return 0
        lax.fori_loop(0, item_ref.shape[0] // unroll, clear, 0)

    base = pl.program_id(0) * tm * TOP_K
    for t in range(tm):
        for k in range(TOP_K):
            j = t * TOP_K + k
            row = start_ref[e_ref[0, j]] + r_ref[0, j]
            item_ref[row] = base + j
            pltpu.make_async_copy(h_ref.at[pl.ds(t, 1), :], xs_hbm.at[pl.ds(row, 1), :],
                                  sem).start()
    for _ in range(TOP_K):
        pltpu.make_async_copy(h_ref, xs_hbm.at[pl.ds(0, tm), :], sem).wait()


def _expert_kernel(layer, be_ref, first_ref, nu_ref, nv_ref, wslot_ref, next_ref, item_ref,
                   xs_ref, wgu_hbm, bgu_ref, wd_hbm, bd_ref, yt_hbm,
                   wgu_f32, wd_f32, wgu_bf, wd_bf, ybuf, sems, wsems):
    i = pl.program_id(0)
    last = pl.num_programs(0) - 1
    de = wd_bf.shape[0]
    tm = xs_ref.shape[0]
    n_items = yt_hbm.shape[0] - 2 * tm
    slot = i % 2
    nv = nv_ref[i]
    has_prev = (i >= 1) & (nv_ref[jnp.maximum(i - 1, 0)] > 0)

    @pl.when(i == 0)
    def _():
        ybuf[...] = jnp.zeros_like(ybuf)
        fills = [pltpu.make_async_copy(ybuf.at[par], yt_hbm.at[pl.ds(n_items + par * tm, tm), :],
                                       sems.at[par]) for par in range(2)]
        for cp in fills:
            cp.start()
        for cp in fills:
            cp.wait()

    def weight_copies(expert, ws):
        return (pltpu.make_async_copy(wgu_hbm.at[layer, expert], wgu_f32.at[ws], wsems.at[ws, 0]),
                pltpu.make_async_copy(wd_hbm.at[layer, expert], wd_f32.at[ws], wsems.at[ws, 1]))

    @pl.when(nv > 0)
    def _():
        @pl.when(first_ref[i] == 1)
        def _():
            ws = wslot_ref[i]

            @pl.when(i == 0)
            def _():
                for cp in weight_copies(be_ref[0], 0):
                    cp.start()

            for cp in weight_copies(be_ref[i], ws):
                cp.wait()

            @pl.when(next_ref[i] >= 0)
            def _():
                for cp in weight_copies(next_ref[i], 1 - ws):
                    cp.start()

            wgu_bf[...] = wgu_f32[ws].astype(BF16)
            wd_bf[...] = wd_f32[ws].astype(BF16)

    def wait_tile(par):
        pltpu.make_async_copy(ybuf.at[par], yt_hbm.at[pl.ds(0, tm), :], sems.at[par]).wait()

    def send_tile(par):
        for r in range(tm):
            pltpu.make_async_copy(ybuf.at[par, pl.ds(r, 1), :],
                                  yt_hbm.at[pl.ds(item_ref[0, r], 1), :], sems.at[par]).start()

    def compute_tile(par):
        gu = jnp.dot(xs_ref[...].astype(BF16), wgu_bf[...], preferred_element_type=F32) + bgu_ref[...]
        gate = jnp.minimum(gu[:, :de], SWIGLU_LIMIT)
        up = jnp.clip(gu[:, de:], -SWIGLU_LIMIT, SWIGLU_LIMIT)
        act = gate * _sigmoid(SWIGLU_ALPHA * gate) * (up + 1.0)
        ybuf[par] = jnp.dot(act.astype(BF16), wd_bf[...], preferred_element_type=F32) + bd_ref[...]

    for par in range(2):
        @pl.when((i >= 2) & (nv_ref[jnp.maximum(i - 2, 0)] > 0) & (slot == par))
        def _(par=par):
            wait_tile(par)

        @pl.when((nv > 0) & has_prev & (slot == par))
        def _(par=par):
            send_tile(1 - par)
            compute_tile(par)

        @pl.when((nv == 0) & has_prev & (slot == par))
        def _(par=par):
            send_tile(1 - par)

        @pl.when((i == last) & has_prev & (slot == par))
        def _(par=par):
            wait_tile(1 - par)

    @pl.when((nv > 0) & jnp.logical_not(has_prev))
    def _():
        compute_tile(0)


def _combine_kernel(x_ref, yt_ref, w_ref, g2_ref, o_ref):
    o_ref[...] = x_ref[...] + g2_ref[...] * (w_ref[...] * yt_ref[...])


def _tok_mod_spec(j, tm, s, d):
    return pl.BlockSpec((None, None, 1, d), lambda i: (j, (i * tm) // s, 0, 0))


def _moe_route(xt, t, s, mod, ng, router_w, router_b, tm=256):
    d = xt.shape[1]
    assert t % tm == 0
    rw = jnp.pad(router_w, ((0, 0), (0, LANES - N_EXPERTS)))
    rb = jnp.pad(router_b, (0, LANES - N_EXPERTS)).reshape(1, LANES)
    return pl.pallas_call(
        _route_kernel,
        grid=(t // tm,),
        in_specs=[
            pl.BlockSpec((tm, d), lambda i: (i, 0)),
            _tok_mod_spec(3, tm, s, d), _tok_mod_spec(4, tm, s, d),
            pl.BlockSpec((1, d), lambda i: (0, 0)),
            pl.BlockSpec((d, LANES), lambda i: (0, 0)),
            pl.BlockSpec((1, LANES), lambda i: (0, 0)),
        ],
        out_specs=[
            pl.BlockSpec((tm, d), lambda i: (i, 0)),
            pl.BlockSpec((tm, LANES), lambda i: (i, 0)),
            pl.BlockSpec((tm, LANES), lambda i: (i, 0)),
            pl.BlockSpec((tm, LANES), lambda i: (i, 0)),
            pl.BlockSpec((1, LANES), lambda i: (0, 0)),
        ],
        out_shape=[
            jax.ShapeDtypeStruct((t, d), F32),
            jax.ShapeDtypeStruct((t, LANES), jnp.int32),
            jax.ShapeDtypeStruct((t, LANES), F32),
            jax.ShapeDtypeStruct((t, LANES), jnp.int32),
            jax.ShapeDtypeStruct((1, LANES), jnp.int32),
        ],
        scratch_shapes=[pltpu.VMEM((1, LANES), F32)],
        compiler_params=_cparams(("arbitrary",)),
    )(xt, mod, mod, ng.reshape(1, d), rw, rb)


def _moe_plan(t, counts, tm):
    n_e = N_EXPERTS
    counts = counts[0, :n_e]
    padded = (counts + tm - 1) // tm * tm
    pad_end = jnp.cumsum(padded)
    pad_start = pad_end - padded
    n_tiles = (t * TOP_K) // tm + n_e
    tile_start = jnp.arange(n_tiles, dtype=jnp.int32) * tm
    n_used = (pad_end[-1] // tm).astype(jnp.int32)
    tile_expert = jnp.minimum(jnp.searchsorted(pad_end, tile_start, side='right'), n_e - 1)
    tile_expert = jnp.where(jnp.arange(n_tiles) < n_used, tile_expert,
                            tile_expert[jnp.maximum(n_used - 1, 0)]).astype(jnp.int32)
    first = jnp.concatenate([jnp.ones((1,), jnp.int32),
                             (tile_expert[1:] != tile_expert[:-1]).astype(jnp.int32)])
    seg_end = (pad_start + counts)[tile_expert]
    n_valid = jnp.where(jnp.arange(n_tiles) < n_used,
                        jnp.clip(seg_end - tile_start, 0, tm), 0).astype(jnp.int32)
    used = jnp.arange(n_tiles) < n_used
    w_slot = ((jnp.cumsum(first) - 1) % 2).astype(jnp.int32)
    first_pos = jnp.where((first == 1) & used, jnp.arange(n_tiles), n_tiles)
    next_first = lax.cummin(first_pos[::-1])[::-1]
    next_first = jnp.concatenate([next_first[1:], jnp.full((1,), n_tiles)])
    next_expert = jnp.where(next_first < n_tiles,
                            tile_expert[jnp.minimum(next_first, n_tiles - 1)], -1).astype(jnp.int32)
    return (pad_start.astype(jnp.int32), tile_expert, first, n_used.reshape(1), n_valid,
            w_slot, next_expert, n_tiles)


def _moe_dispatch(h2, pad_start, top_e, rank, xs_buf, tm_exp, tm=128):
    t, d = h2.shape
    n_rows = xs_buf.shape[0]
    flat = lambda a: a[:, :TOP_K].reshape(t // tm, 1, tm * TOP_K)
    item_spec = pl.BlockSpec((None, 1, tm * TOP_K), lambda i: (i, 0, 0), memory_space=pltpu.SMEM)
    return pl.pallas_call(
        functools.partial(_dispatch_kernel, t * TOP_K, tm_exp),
        grid=(t // tm,),
        in_specs=[
            pl.BlockSpec(memory_space=pltpu.SMEM),
            item_spec, item_spec,
            pl.BlockSpec((tm, d), lambda i: (i, 0)),
            pl.BlockSpec(memory_space=pl.ANY),
        ],
        out_specs=[pl.BlockSpec(memory_space=pl.ANY), pl.BlockSpec(memory_space=pltpu.SMEM)],
        out_shape=[jax.ShapeDtypeStruct((n_rows, d), F32),
                   jax.ShapeDtypeStruct((n_rows,), jnp.int32)],
        scratch_shapes=[pltpu.SemaphoreType.DMA],
        input_output_aliases={4: 0},
        compiler_params=_cparams(("arbitrary",)),
    )(pad_start, flat(top_e), flat(rank), h2, xs_buf)


def _moe_experts(xs, n_items, tile_expert, first, n_used, n_valid, w_slot, next_expert, row_item,
                 w_gu, b_gu, w_down, b_down, layer, tm):
    n_rows, d = xs.shape
    n_e = N_EXPERTS
    de = w_down.shape[2]
    n_tiles = n_rows // tm
    tile_idx = lambda i, be, fi, nu, *_: (jnp.maximum(jnp.minimum(i, nu[0] - 1), 0), 0)
    b_idx = lambda i, be, *_: (layer, be[i], 0, 0)
    extend = lambda a, fill: jnp.concatenate([a, jnp.full((1,), fill, a.dtype)])
    tile_expert = jnp.concatenate([tile_expert, tile_expert[-1:]])
    first, n_valid, w_slot = extend(first, 0), extend(n_valid, 0), extend(w_slot, 0)
    next_expert = extend(next_expert, -1)
    return pl.pallas_call(
        functools.partial(_expert_kernel, layer),
        grid_spec=pltpu.PrefetchScalarGridSpec(
            num_scalar_prefetch=6,
            grid=(n_tiles + 1,),
            in_specs=[
                pl.BlockSpec((None, 1, tm), lambda i, *_: (jnp.maximum(i - 1, 0), 0, 0),
                             memory_space=pltpu.SMEM),
                pl.BlockSpec((tm, d), tile_idx),
                pl.BlockSpec(memory_space=pl.ANY),
                pl.BlockSpec((None, None, 1, 2 * de), b_idx),
                pl.BlockSpec(memory_space=pl.ANY),
                pl.BlockSpec((None, None, 1, d), b_idx),
            ],
            out_specs=pl.BlockSpec(memory_space=pl.ANY),
            scratch_shapes=[pltpu.VMEM((2, d, 2 * de), F32), pltpu.VMEM((2, de, d), F32),
                            pltpu.VMEM((d, 2 * de), BF16), pltpu.VMEM((de, d), BF16),
                            pltpu.VMEM((2, tm, d), F32), pltpu.SemaphoreType.DMA((2,)),
                            pltpu.SemaphoreType.DMA((2, 2))],
        ),
        out_shape=jax.ShapeDtypeStruct((n_items + 2 * tm, d), F32),
        compiler_params=_cparams(("arbitrary",)),
    )(tile_expert, first, n_used, n_valid, w_slot, next_expert, row_item.reshape(n_tiles, 1, tm),
      xs, w_gu, b_gu.reshape(b_gu.shape[0], n_e, 1, 2 * de), w_down,
      b_down.reshape(b_down.shape[0], n_e, 1, d))


def _moe_combine(xt, s, mod, w_item, yt, tm=512):
    t, d = xt.shape
    return pl.pallas_call(
        _combine_kernel,
        grid=(t // tm,),
        in_specs=[
            pl.BlockSpec((tm, d), lambda i: (i, 0)),
            pl.BlockSpec((tm, d), lambda i: (i, 0)),
            pl.BlockSpec((tm, 1), lambda i: (i, 0)),
            _tok_mod_spec(5, tm, s, d),
        ],
        out_specs=pl.BlockSpec((tm, d), lambda i: (i, 0)),
        out_shape=jax.ShapeDtypeStruct((t, d), F32),
        compiler_params=_cparams(("arbitrary",)),
    )(xt, yt, w_item, mod)


def _moe_rows(n_tokens, tm_exp=256):
    return (n_tokens // tm_exp + N_EXPERTS) * tm_exp


def _moe_layer(x, xs_buf, mod, ng, router_w, router_b, w_gu, b_gu, w_down, b_down, layer,
               tm_exp=256):
    b, s, d = x.shape
    t = b * s
    xt = x.reshape(t, d)
    assert t % TOP_K == 0
    n_tok = t // TOP_K
    h2, top_e, top_w, rank, counts = _moe_route(xt, n_tok, s, mod, ng, router_w, router_b)
    (pad_start, tile_expert, first, n_used, n_valid, w_slot, next_expert,
     n_tiles) = _moe_plan(n_tok, counts, tm_exp)
    assert xs_buf.shape[0] == n_tiles * tm_exp
    xs, row_item = _moe_dispatch(h2, pad_start, top_e, rank, xs_buf, tm_exp)
    yt = _moe_experts(xs, t, tile_expert, first, n_used, n_valid, w_slot, next_expert, row_item,
                      w_gu, b_gu, w_down, b_down, layer, tm_exp)
    w_item = top_w[:, :TOP_K].reshape(t, 1)
    return _moe_combine(xt, s, mod, w_item, yt).reshape(b, s, d), xs


def kernel(x, c, norm_mix_g, norm_ffn_g, w_mod, b_mod, lru_w_in, lru_conv_w, lru_conv_b, lru_w_a, lru_b_a, lru_w_x, lru_b_x, lru_lambda, lru_w_out, attn_w_qkv, attn_q_norm_g, attn_k_norm_g, attn_w_out, router_w, router_b, expert_w_gu, expert_b_gu, expert_w_down, expert_b_down):
    depth = w_mod.shape[0]
    mods = _modulation(c, w_mod, b_mod)
    n_tokens = x.shape[0] * x.shape[1]
    xs_buf = jnp.zeros((_moe_rows(n_tokens), x.shape[2]), F32)
    for i in range(depth):
        mod = mods[i]
        j = i // 2
        if i % 2 == 0:
            x = _rglru_layer(x, mod, norm_mix_g[i], lru_w_in[j], lru_conv_w[j], lru_conv_b[j],
                             lru_w_a[j], lru_b_a[j], lru_w_x[j], lru_b_x[j], lru_lambda[j],
                             lru_w_out[j])
        else:
            x = _moba_layer(x, mod, norm_mix_g[i], attn_w_qkv[j], attn_q_norm_g[j],
                            attn_k_norm_g[j], attn_w_out[j])
        x, xs_buf = _moe_layer(x, xs_buf, mod, norm_ffn_g[i], router_w[i], router_b[i],
                               expert_w_gu, expert_b_gu, expert_w_down, expert_b_down, i)
    return x
```

```python
import functools

import jax
import jax.numpy as jnp
from jax import lax
from jax.experimental import pallas as pl
from jax.experimental.pallas import tpu as pltpu

NORM_EPS = 1e-6
N_MOD = 6
LRU_BLOCKS = 4
CONV_WIDTH = 4
LRU_C = 8.0
ATTN_HEADS = 8
MOBA_BLOCK = 256
MOBA_TOPK = 3
N_EXPERTS = 32
TOP_K = 4
SWIGLU_LIMIT = 7.0
SWIGLU_ALPHA = 1.702

LANES = 128
SUBLANES = 8
MASK_BIAS = -(2.0 ** 100)
LOG2_E = 1.4426950408889634
ATTN_PARTS = 1
VMEM_LIMIT = 52 * 1024 * 1024

F32 = jnp.float32
BF16 = jnp.bfloat16
HIGHEST = lax.Precision.HIGHEST


def _cparams(sem):
    return pltpu.CompilerParams(dimension_semantics=sem, vmem_limit_bytes=VMEM_LIMIT)


def _sigmoid(z):
    return 1.0 / (1.0 + jnp.exp(-z))


def _modulated_norm(x, g, sc, sh):
    ms = jnp.mean(x * x, axis=-1, keepdims=True)
    return x * lax.rsqrt(ms + NORM_EPS) * (g * (1.0 + sc)) + sh


def _mod_kernel(c_ref, w_ref, b_ref, o_ref):
    c = c_ref[...]
    cond = c * _sigmoid(c)
    o_ref[...] = jnp.dot(cond, w_ref[...], preferred_element_type=F32,
                         precision=HIGHEST) + b_ref[...]


def _modulation(c, w_mod, b_mod):
    depth, d, _ = w_mod.shape
    b = c.shape[0]
    rows = -(-b // SUBLANES) * SUBLANES
    c_pad = jnp.pad(c, ((0, rows - b), (0, 0)))
    out = pl.pallas_call(
        _mod_kernel,
        grid=(depth, N_MOD),
        in_specs=[
            pl.BlockSpec((rows, d), lambda i, j: (0, 0)),
            pl.BlockSpec((None, d, d), lambda i, j: (i, 0, j)),
            pl.BlockSpec((None, None, 1, d), lambda i, j: (i, j, 0, 0)),
        ],
        out_specs=pl.BlockSpec((None, None, rows, d), lambda i, j: (i, j, 0, 0)),
        out_shape=jax.ShapeDtypeStruct((depth, N_MOD, rows, d), F32),
        compiler_params=_cparams(("arbitrary", "arbitrary")),
    )(c_pad, w_mod, b_mod.reshape(depth, N_MOD, 1, d))
    return out[:, :, :b].reshape(depth, N_MOD, b, 1, d)


def _mod_spec(j, d):
    return pl.BlockSpec((None, None, 1, d), lambda b, s: (j, b, 0, 0))


def _rglru_kernel(x_ref, sh_ref, sc_ref, g1_ref, ng_ref, win_ref, cw_ref, cb_ref,
                  wa_ref, ba_ref, wx_ref, bx_ref, lam_ref, wout_ref, o_ref,
                  ext_ref, a_ref, u_ref, h_ref):
    ts, d = x_ref.shape
    bw = d // LRU_BLOCKS

    @pl.when(pl.program_id(1) == 0)
    def _():
        ext_ref[...] = jnp.zeros_like(ext_ref)
        h_ref[...] = jnp.zeros_like(h_ref)

    x = x_ref[...]
    h = _modulated_norm(x, ng_ref[...], sc_ref[...], sh_ref[...])
    gr = jnp.dot(h.astype(BF16), win_ref[...], preferred_element_type=F32)
    gate_branch = gr[:, :d]
    rec = gr[:, d:]

    groups = ts // SUBLANES
    rec3 = rec.reshape(groups, SUBLANES, d)
    tail3 = ext_ref[...].reshape(1, SUBLANES, d)
    grow = lax.broadcasted_iota(jnp.int32, rec3.shape, 1)
    xc3 = cb_ref[...] + cw_ref[CONV_WIDTH - 1:CONV_WIDTH, :] * rec3
    for s in range(1, CONV_WIDTH):
        rot = pltpu.roll(rec3, s, 1)
        rot_before = jnp.concatenate([pltpu.roll(tail3, s, 1), rot[:groups - 1]], axis=0)
        tap = cw_ref[CONV_WIDTH - 1 - s:CONV_WIDTH - s, :]
        xc3 = xc3 + tap * jnp.where(grow >= s, rot, rot_before)
    xc = xc3.reshape(ts, d)
    ext_ref[...] = rec[ts - SUBLANES:, :]

    xcb = xc.astype(BF16)
    ra = jnp.concatenate(
        [jnp.dot(xcb[:, g * bw:(g + 1) * bw], wa_ref[g], preferred_element_type=F32)
         for g in range(LRU_BLOCKS)], axis=1) + ba_ref[...]
    rx = jnp.concatenate(
        [jnp.dot(xcb[:, g * bw:(g + 1) * bw], wx_ref[g], preferred_element_type=F32)
         for g in range(LRU_BLOCKS)], axis=1) + bx_ref[...]
    r = _sigmoid(ra)
    ig = _sigmoid(rx)
    z = -lam_ref[...]
    softplus = jnp.maximum(z, 0.0) + jnp.log(1.0 + jnp.exp(-jnp.abs(z)))
    a = jnp.exp2(r * ((-LRU_C * LOG2_E) * softplus))
    v = 1.0 - a * a
    u = jnp.where(v > 0.0, v * lax.rsqrt(v), 0.0) * (ig * xc)

    a = a.reshape(ts // SUBLANES, SUBLANES, d)
    u = u.reshape(ts // SUBLANES, SUBLANES, d)
    row = lax.broadcasted_iota(jnp.int32, a.shape, 1)
    k = 1
    while k < SUBLANES:
        a_prev = pltpu.roll(a, k, 1)
        u_prev = pltpu.roll(u, k, 1)
        m = row >= k
        u = jnp.where(m, a * u_prev + u, u)
        a = jnp.where(m, a * a_prev, a)
        k *= 2
    a_ref[...] = a.reshape(ts, d)
    u_ref[...] = u.reshape(ts, d)

    def group(j, hc):
        r0 = pl.multiple_of(j * SUBLANES, SUBLANES)
        hs = u_ref[pl.ds(r0, SUBLANES), :] + a_ref[pl.ds(r0, SUBLANES), :] * hc
        u_ref[pl.ds(r0, SUBLANES), :] = hs
        return hs[SUBLANES - 1:SUBLANES, :]

    h_ref[0:1, :] = lax.fori_loop(0, ts // SUBLANES, group, h_ref[0:1, :])

    y = jax.nn.gelu(gate_branch, approximate=True) * u_ref[...]
    out = jnp.dot(y.astype(BF16), wout_ref[...], preferred_element_type=F32)
    o_ref[...] = x + g1_ref[...] * out


def _rglru_layer(x, mod, ng, w_in, conv_w, conv_b, w_a, b_a, w_x, b_x, lam, w_out, ts=256):
    b, s, d = x.shape
    full = lambda shape: pl.BlockSpec(shape, lambda bi, si: (0,) * len(shape))
    row = lambda v: v.reshape(1, d)
    return pl.pallas_call(
        _rglru_kernel,
        grid=(b, s // ts),
        in_specs=[
            pl.BlockSpec((None, ts, d), lambda bi, si: (bi, si, 0)),
            _mod_spec(0, d), _mod_spec(1, d), _mod_spec(2, d),
            full((1, d)), full((d, 2 * d)), full((CONV_WIDTH, d)), full((1, d)),
            full(w_a.shape), full((1, d)), full(w_x.shape), full((1, d)), full((1, d)),
            full((d, d)),
        ],
        out_specs=pl.BlockSpec((None, ts, d), lambda bi, si: (bi, si, 0)),
        out_shape=jax.ShapeDtypeStruct(x.shape, F32),
        scratch_shapes=[
            pltpu.VMEM((SUBLANES, d), F32),
            pltpu.VMEM((ts, d), F32),
            pltpu.VMEM((ts, d), F32),
            pltpu.VMEM((SUBLANES, d), F32),
        ],
        compiler_params=_cparams(("arbitrary", "arbitrary")),
    )(x, mod, mod, mod, row(ng), w_in.astype(BF16), conv_w, row(conv_b),
      w_a.astype(BF16), row(b_a), w_x.astype(BF16), row(b_x), row(lam), w_out.astype(BF16))


def _qkv_kernel(x_ref, sh_ref, sc_ref, ng_ref, w_ref, qg_ref, kg_ref,
                q_ref, kext_ref, v_ref, km_ref):
    ts, d = x_ref.shape
    nh, _, hd = q_ref.shape
    blk = pl.program_id(1)
    h = _modulated_norm(x_ref[...], ng_ref[...], sc_ref[...], sh_ref[...])
    qkv = jnp.dot(h.astype(BF16), w_ref[...], preferred_element_type=F32)
    lane = lax.broadcasted_iota(jnp.int32, (ts, hd), 1)
    onehot = jnp.where(lane == blk, 1.0, 0.0).astype(BF16)
    for hh in range(nh):
        qh = qkv[:, hh * hd:(hh + 1) * hd]
        qn = qh * lax.rsqrt(jnp.mean(qh * qh, axis=-1, keepdims=True) + NORM_EPS)
        q_ref[hh] = (qn * qg_ref[...] * (hd ** -0.5 * LOG2_E)).astype(BF16)
        kh = qkv[:, d + hh * hd:d + (hh + 1) * hd]
        kn = kh * lax.rsqrt(jnp.mean(kh * kh, axis=-1, keepdims=True) + NORM_EPS) * kg_ref[...]
        kext_ref[hh, :, 0:hd] = kn.astype(BF16)
        kext_ref[hh, :, hd:2 * hd] = onehot
        km_ref[:, hh * hd:(hh + 1) * hd] = jnp.mean(kn, axis=0, keepdims=True)
        v_ref[hh] = qkv[:, 2 * d + hh * hd:2 * d + (hh + 1) * hd].astype(BF16)


def _attn_select_blocks(q_ref, km_ref, bias_ref, tq):
    nhs, s, hd = q_ref.shape
    nt = (((1,), (1,)), ((), ()))
    nbp = -(-(s // tq) // SUBLANES) * SUBLANES
    blk = lax.broadcasted_iota(jnp.int32, (nbp, s), 0)
    blk_f = blk.astype(F32)
    own = lax.broadcasted_iota(jnp.int32, (nbp, s), 1) // tq
    past = blk < own
    for hh in range(nhs):
        pieces, rest = [], km_ref[hh, 0:nbp, :]
        for _ in range(3):
            piece = rest.astype(BF16)
            pieces.append(piece)
            rest = rest - piece.astype(F32)
        gate3 = lax.dot_general(jnp.concatenate(pieces, axis=0), q_ref[hh], nt,
                                preferred_element_type=F32)
        gate = gate3[0:nbp] + gate3[nbp:2 * nbp] + gate3[2 * nbp:3 * nbp]
        g = jnp.where(past, gate, -jnp.inf)
        sel = blk == own
        for _ in range(MOBA_TOPK):
            m = jnp.max(g, axis=0, keepdims=True)
            idx = jnp.min(jnp.where(g == m, blk_f, float(nbp)), axis=0, keepdims=True)
            hit = blk_f == idx
            sel = sel | (hit & past)
            g = jnp.where(hit, -jnp.inf, g)
        bias_t = jnp.where(sel, 0.0, MASK_BIAS)
        pad = jnp.zeros((LANES - nbp, tq), F32)
        for c in range(s // tq):
            tile = jnp.concatenate([bias_t[:, c * tq:(c + 1) * tq], pad], axis=0).T
            bias_ref[hh, c * tq:(c + 1) * tq, :] = tile.astype(BF16)


def _attn_query_block_steps(qi, hh, q_ref, bias_ref, kext_ref, v_ref, o_ref, s_ref):
    tq = s_ref.shape[3]
    buf = qi % s_ref.shape[0]
    nt = (((1,), (1,)), ((), ()))
    rows = slice(qi * tq, (qi + 1) * tq)
    st = {}

    def score(j):
        if j == 0:
            st['q'] = jnp.concatenate([q_ref[hh, rows, :], bias_ref[hh, rows, :]], axis=1)
        sj = lax.dot_general(st['q'], kext_ref[hh, j * tq:(j + 1) * tq, :], nt,
                             preferred_element_type=F32)
        if j == qi:
            rr = lax.broadcasted_iota(jnp.int32, (tq, tq), 0)
            cc = lax.broadcasted_iota(jnp.int32, (tq, tq), 1)
            sj = jnp.where(cc <= rr, sj, MASK_BIAS)
        s_ref[buf, hh, j * tq:(j + 1) * tq, :] = sj
        st['mx'] = sj if j == 0 else jnp.maximum(st['mx'], sj)
        if j == qi:
            st['m'] = jnp.broadcast_to(jnp.max(st['mx'], axis=-1, keepdims=True), (tq, tq))

    def weigh(j):
        p = jnp.exp2(s_ref[buf, hh, j * tq:(j + 1) * tq, :] - st['m'])
        pv = jnp.dot(p.astype(BF16), v_ref[hh, j * tq:(j + 1) * tq, :],
                     preferred_element_type=F32)
        lj = jnp.sum(p, axis=-1, keepdims=True)
        st['l'] = lj if j == 0 else st['l'] + lj
        st['acc'] = pv if j == 0 else st['acc'] + pv
        if j == qi:
            o_ref[hh, rows, :] = (st['acc'] / st['l']).astype(BF16)

    blocks = range(qi + 1)
    return ([functools.partial(score, j) for j in blocks],
            [functools.partial(weigh, j) for j in blocks])


def _attn_query_blocks(qis, q_ref, bias_ref, kext_ref, v_ref, o_ref, s_ref):
    todo = [_attn_query_block_steps(qi, hh, q_ref, bias_ref, kext_ref, v_ref, o_ref, s_ref)
            for hh in range(s_ref.shape[1]) for qi in qis]
    pending = []
    for scores, weighs in todo + [([], [])]:
        for k in range(max(len(scores), len(pending))):
            if k < len(scores):
                scores[k]()
            if k < len(pending):
                pending[k]()
        pending = weighs


def _attn_kernel(q_ref, kext_ref, v_ref, km_ref, o_ref, s_ref, bias_ref):
    tq = s_ref.shape[3]
    nb = kext_ref.shape[1] // tq
    total = nb * (nb + 1) // 2
    bounds = [0]
    for part in range(1, ATTN_PARTS):
        qi = bounds[-1]
        while qi < nb and qi * (qi + 1) // 2 < total * part // ATTN_PARTS:
            qi += 1
        bounds.append(qi)
    bounds.append(nb)
    for part in range(ATTN_PARTS):
        @pl.when(pl.program_id(2) == part)
        def _(part=part):
            if part == 0:
                _attn_select_blocks(q_ref, km_ref, bias_ref, tq)
            _attn_query_blocks(range(bounds[part], bounds[part + 1]),
                               q_ref, bias_ref, kext_ref, v_ref, o_ref, s_ref)


def _attn_out_kernel(o_ref, x_ref, g1_ref, w_ref, y_ref):
    nh = o_ref.shape[0]
    o = jnp.concatenate([o_ref[hh] for hh in range(nh)], axis=1)
    out = jnp.dot(o, w_ref[...], preferred_element_type=F32)
    y_ref[...] = x_ref[...] + g1_ref[...] * out


def _moba_layer(x, mod, ng, w_qkv, q_g, k_g, w_out, ts_out=1024, heads_per_step=1):
    b, s, d = x.shape
    nh = ATTN_HEADS
    hd = d // nh
    tq = MOBA_BLOCK
    nb = s // tq
    assert s % tq == 0 and nb <= LANES and hd == LANES
    full = lambda shape: pl.BlockSpec(shape, lambda bi, si: (0,) * len(shape))

    q, kext, v, km = pl.pallas_call(
        _qkv_kernel,
        grid=(b, nb),
        in_specs=[
            pl.BlockSpec((None, tq, d), lambda bi, si: (bi, si, 0)),
            _mod_spec(0, d), _mod_spec(1, d),
            full((1, d)), full((d, 3 * d)), full((1, hd)), full((1, hd)),
        ],
        out_specs=[
            pl.BlockSpec((None, nh, tq, hd), lambda bi, si: (bi, 0, si, 0)),
            pl.BlockSpec((None, nh, tq, 2 * hd), lambda bi, si: (bi, 0, si, 0)),
            pl.BlockSpec((None, nh, tq, hd), lambda bi, si: (bi, 0, si, 0)),
            pl.BlockSpec((None, None, 1, d), lambda bi, si: (bi, si, 0, 0)),
        ],
        out_shape=[
            jax.ShapeDtypeStruct((b, nh, s, hd), BF16),
            jax.ShapeDtypeStruct((b, nh, s, 2 * hd), BF16),
            jax.ShapeDtypeStruct((b, nh, s, hd), BF16),
            jax.ShapeDtypeStruct((b, nb, 1, d), F32),
        ],
        compiler_params=_cparams(("arbitrary", "arbitrary")),
    )(x, mod, mod, ng.reshape(1, d), w_qkv.astype(BF16), q_g.reshape(1, hd), k_g.reshape(1, hd))

    km = km.reshape(b, nb, nh, hd).transpose(0, 2, 1, 3)
    km = jnp.pad(km, ((0, 0), (0, 0), (0, LANES - nb), (0, 0)))

    o = pl.pallas_call(
        _attn_kernel,
        grid=(b, nh // heads_per_step, ATTN_PARTS),
        in_specs=[
            pl.BlockSpec((None, heads_per_step, s, hd), lambda bi, hi, pi: (bi, hi, 0, 0)),
            pl.BlockSpec((None, heads_per_step, s, 2 * hd), lambda bi, hi, pi: (bi, hi, 0, 0)),
            pl.BlockSpec((None, heads_per_step, s, hd), lambda bi, hi, pi: (bi, hi, 0, 0)),
            pl.BlockSpec((None, heads_per_step, LANES, hd), lambda bi, hi, pi: (bi, hi, 0, 0)),
        ],
        out_specs=pl.BlockSpec((None, heads_per_step, s, hd), lambda bi, hi, pi: (bi, hi, 0, 0)),
        out_shape=jax.ShapeDtypeStruct((b, nh, s, hd), BF16),
        scratch_shapes=[pltpu.VMEM((2, heads_per_step, s, tq), F32),
                        pltpu.VMEM((heads_per_step, s, LANES), BF16)],
        compiler_params=_cparams(("arbitrary", "arbitrary", "arbitrary")),
    )(q, kext, v, km)

    return pl.pallas_call(
        _attn_out_kernel,
        grid=(b, s // ts_out),
        in_specs=[
            pl.BlockSpec((None, nh, ts_out, hd), lambda bi, si: (bi, 0, si, 0)),
            pl.BlockSpec((None, ts_out, d), lambda bi, si: (bi, si, 0)),
            _mod_spec(2, d),
            full((d, d)),
        ],
        out_specs=pl.BlockSpec((None, ts_out, d), lambda bi, si: (bi, si, 0)),
        out_shape=jax.ShapeDtypeStruct(x.shape, F32),
        compiler_params=_cparams(("arbitrary", "arbitrary")),
    )(o, x, mod, w_out.astype(BF16))


def _route_kernel(x_ref, sh_ref, sc_ref, ng_ref, rw_ref, rb_ref,
                  h_ref, e_ref, w_ref, r_ref, cnt_ref, run_ref):
    tm = x_ref.shape[0]

    @pl.when(pl.program_id(0) == 0)
    def _():
        run_ref[...] = jnp.zeros_like(run_ref)

    h = _modulated_norm(x_ref[...], ng_ref[...], sc_ref[...], sh_ref[...])
    h_ref[...] = h
    logits = jnp.dot(h, rw_ref[...], preferred_element_type=F32, precision=HIGHEST) + rb_ref[...]
    lane = lax.broadcasted_iota(jnp.int32, (tm, LANES), 1)
    lane_f = lane.astype(F32)
    lg = jnp.where(lane < N_EXPERTS, logits, -jnp.inf)
    tops, hits = [], []
    for _ in range(TOP_K):
        m = jnp.max(lg, axis=-1, keepdims=True)
        idx = jnp.min(jnp.where(lg == m, lane_f, float(LANES)), axis=-1, keepdims=True)
        hit = lane_f == idx
        tops.append((m, idx))
        hits.append(hit)
        lg = jnp.where(hit, -jnp.inf, lg)
    exps = [jnp.exp(m - tops[0][0]) for m, _ in tops]
    denom = exps[0]
    for ex in exps[1:]:
        denom = denom + ex

    member = jnp.zeros((tm, LANES), F32)
    for hit in hits:
        member = member + jnp.where(hit, 1.0, 0.0)
    rr = lax.broadcasted_iota(jnp.int32, (tm, tm), 0)
    cc = lax.broadcasted_iota(jnp.int32, (tm, tm), 1)
    earlier = jnp.where(cc < rr, 1.0, 0.0).astype(BF16)
    before = jnp.dot(earlier, member.astype(BF16), preferred_element_type=F32) + run_ref[...]

    e_out = jnp.zeros((tm, LANES), F32)
    w_out = jnp.zeros((tm, LANES), F32)
    r_out = jnp.zeros((tm, LANES), F32)
    for k in range(TOP_K):
        rank = jnp.sum(jnp.where(hits[k], before, 0.0), axis=-1, keepdims=True)
        e_out = jnp.where(lane == k, tops[k][1], e_out)
        w_out = jnp.where(lane == k, exps[k] / denom, w_out)
        r_out = jnp.where(lane == k, rank, r_out)
    e_ref[...] = e_out.astype(jnp.int32)
    w_ref[...] = w_out
    r_ref[...] = r_out.astype(jnp.int32)
    run = run_ref[...] + jnp.sum(member, axis=0, keepdims=True)
    run_ref[...] = run
    cnt_ref[...] = run.astype(jnp.int32)


def _dispatch_kernel(n_items, tile_rows, start_ref, e_ref, r_ref, h_ref, xs_in_hbm, xs_hbm,
                     item_ref, sem):
    del xs_in_hbm
    tm = h_ref.shape[0]
    unroll = 32
    assert item_ref.shape[0] % unroll == 0 and tile_rows % unroll == 0
    groups = tile_rows // unroll
    assert groups & (groups - 1) == 0

    @pl.when(pl.program_id(0) == 0)
    def _():
        def clear(c, _):
            tile = lax.shift_right_logical(c, groups.bit_length() - 1)
            dummy = n_items + (tile & 1) * tile_rows + (c & (groups - 1)) * unroll
            for u in range(unroll):
                item_ref[c * unroll + u] = dummy + u
            return 0
        lax.fori_loop(0, item_ref.shape[0] // unroll, clear, 0)

    base = pl.program_id(0) * tm * TOP_K
    for t in range(tm):
        for k in range(TOP_K):
            j = t * TOP_K + k
            row = start_ref[e_ref[0, j]] + r_ref[0, j]
            item_ref[row] = base + j
            pltpu.make_async_copy(h_ref.at[pl.ds(t, 1), :], xs_hbm.at[pl.ds(row, 1), :],
                                  sem).start()
    for _ in range(TOP_K):
        pltpu.make_async_copy(h_ref, xs_hbm.at[pl.ds(0, tm), :], sem).wait()


def _expert_kernel(layer, be_ref, first_ref, nu_ref, nv_ref, wslot_ref, next_ref, item_ref,
                   xs_ref, wgu_hbm, bgu_ref, wd_hbm, bd_ref, yt_hbm,
                   wgu_f32, wd_f32, wgu_bf, wd_bf, ybuf, sems, wsems):
    i = pl.program_id(0)
    last = pl.num_programs(0) - 1
    de = wd_bf.shape[0]
    tm = xs_ref.shape[0]
    n_items = yt_hbm.shape[0] - 2 * tm
    slot = i % 2
    nv = nv_ref[i]
    has_prev = (i >= 1) & (nv_ref[jnp.maximum(i - 1, 0)] > 0)

    @pl.when(i == 0)
    def _():
        ybuf[...] = jnp.zeros_like(ybuf)
        fills = [pltpu.make_async_copy(ybuf.at[par], yt_hbm.at[pl.ds(n_items + par * tm, tm), :],
                                       sems.at[par]) for par in range(2)]
        for cp in fills:
            cp.start()
        for cp in fills:
            cp.wait()

    def weight_copies(expert, ws):
        return (pltpu.make_async_copy(wgu_hbm.at[layer, expert], wgu_f32.at[ws], wsems.at[ws, 0]),
                pltpu.make_async_copy(wd_hbm.at[layer, expert], wd_f32.at[ws], wsems.at[ws, 1]))

    @pl.when(nv > 0)
    def _():
        @pl.when(first_ref[i] == 1)
        def _():
            ws = wslot_ref[i]

            @pl.when(i == 0)
            def _():
                for cp in weight_copies(be_ref[0], 0):
                    cp.start()

            for cp in weight_copies(be_ref[i], ws):
                cp.wait()

            @pl.when(next_ref[i] >= 0)
            def _():
                for cp in weight_copies(next_ref[i], 1 - ws):
                    cp.start()

            wgu_bf[...] = wgu_f32[ws].astype(BF16)
            wd_bf[...] = wd_f32[ws].astype(BF16)

    def wait_tile(par):
        pltpu.make_async_copy(ybuf.at[par], yt_hbm.at[pl.ds(0, tm), :], sems.at[par]).wait()

    def send_tile(par):
        for r in range(tm):
            pltpu.make_async_copy(ybuf.at[par, pl.ds(r, 1), :],
                                  yt_hbm.at[pl.ds(item_ref[0, r], 1), :], sems.at[par]).start()

    def compute_tile(par):
        gu = jnp.dot(xs_ref[...].astype(BF16), wgu_bf[...], preferred_element_type=F32) + bgu_ref[...]
        gate = jnp.minimum(gu[:, :de], SWIGLU_LIMIT)
        up = jnp.clip(gu[:, de:], -SWIGLU_LIMIT, SWIGLU_LIMIT)
        act = gate * _sigmoid(SWIGLU_ALPHA * gate) * (up + 1.0)
        ybuf[par] = jnp.dot(act.astype(BF16), wd_bf[...], preferred_element_type=F32) + bd_ref[...]

    for par in range(2):
        @pl.when((i >= 2) & (nv_ref[jnp.maximum(i - 2, 0)] > 0) & (slot == par))
        def _(par=par):
            wait_tile(par)

        @pl.when((nv > 0) & has_prev & (slot == par))
        def _(par=par):
            send_tile(1 - par)
            compute_tile(par)

        @pl.when((nv == 0) & has_prev & (slot == par))
        def _(par=par):
            send_tile(1 - par)

        @pl.when((i == last) & has_prev & (slot == par))
        def _(par=par):
            wait_tile(1 - par)

    @pl.when((nv > 0) & jnp.logical_not(has_prev))
    def _():
        compute_tile(0)


def _combine_kernel(x_ref, yt_ref, w_ref, g2_ref, o_ref):
    o_ref[...] = x_ref[...] + g2_ref[...] * (w_ref[...] * yt_ref[...])


def _tok_mod_spec(j, tm, s, d):
    return pl.BlockSpec((None, None, 1, d), lambda i: (j, (i * tm) // s, 0, 0))


def _moe_route(xt, t, s, mod, ng, router_w, router_b, tm=256):
    d = xt.shape[1]
    assert t % tm == 0
    rw = jnp.pad(router_w, ((0, 0), (0, LANES - N_EXPERTS)))
    rb = jnp.pad(router_b, (0, LANES - N_EXPERTS)).reshape(1, LANES)
    return pl.pallas_call(
        _route_kernel,
        grid=(t // tm,),
        in_specs=[
            pl.BlockSpec((tm, d), lambda i: (i, 0)),
            _tok_mod_spec(3, tm, s, d), _tok_mod_spec(4, tm, s, d),
            pl.BlockSpec((1, d), lambda i: (0, 0)),
            pl.BlockSpec((d, LANES), lambda i: (0, 0)),
            pl.BlockSpec((1, LANES), lambda i: (0, 0)),
        ],
        out_specs=[
            pl.BlockSpec((tm, d), lambda i: (i, 0)),
            pl.BlockSpec((tm, LANES), lambda i: (i, 0)),
            pl.BlockSpec((tm, LANES), lambda i: (i, 0)),
            pl.BlockSpec((tm, LANES), lambda i: (i, 0)),
            pl.BlockSpec((1, LANES), lambda i: (0, 0)),
        ],
        out_shape=[
            jax.ShapeDtypeStruct((t, d), F32),
            jax.ShapeDtypeStruct((t, LANES), jnp.int32),
            jax.ShapeDtypeStruct((t, LANES), F32),
            jax.ShapeDtypeStruct((t, LANES), jnp.int32),
            jax.ShapeDtypeStruct((1, LANES), jnp.int32),
        ],
        scratch_shapes=[pltpu.VMEM((1, LANES), F32)],
        compiler_params=_cparams(("arbitrary",)),
    )(xt, mod, mod, ng.reshape(1, d), rw, rb)


def _moe_plan(t, counts, tm):
    n_e = N_EXPERTS
    counts = counts[0, :n_e]
    padded = (counts + tm - 1) // tm * tm
    pad_end = jnp.cumsum(padded)
    pad_start = pad_end - padded
    n_tiles = (t * TOP_K) // tm + n_e
    tile_start = jnp.arange(n_tiles, dtype=jnp.int32) * tm
    n_used = (pad_end[-1] // tm).astype(jnp.int32)
    tile_expert = jnp.minimum(jnp.searchsorted(pad_end, tile_start, side='right'), n_e - 1)
    tile_expert = jnp.where(jnp.arange(n_tiles) < n_used, tile_expert,
                            tile_expert[jnp.maximum(n_used - 1, 0)]).astype(jnp.int32)
    first = jnp.concatenate([jnp.ones((1,), jnp.int32),
                             (tile_expert[1:] != tile_expert[:-1]).astype(jnp.int32)])
    seg_end = (pad_start + counts)[tile_expert]
    n_valid = jnp.where(jnp.arange(n_tiles) < n_used,
                        jnp.clip(seg_end - tile_start, 0, tm), 0).astype(jnp.int32)
    used = jnp.arange(n_tiles) < n_used
    w_slot = ((jnp.cumsum(first) - 1) % 2).astype(jnp.int32)
    first_pos = jnp.where((first == 1) & used, jnp.arange(n_tiles), n_tiles)
    next_first = lax.cummin(first_pos[::-1])[::-1]
    next_first = jnp.concatenate([next_first[1:], jnp.full((1,), n_tiles)])
    next_expert = jnp.where(next_first < n_tiles,
                            tile_expert[jnp.minimum(next_first, n_tiles - 1)], -1).astype(jnp.int32)
    return (pad_start.astype(jnp.int32), tile_expert, first, n_used.reshape(1), n_valid,
            w_slot, next_expert, n_tiles)


def _moe_dispatch(h2, pad_start, top_e, rank, xs_buf, tm_exp, tm=256):
    t, d = h2.shape
    n_rows = xs_buf.shape[0]
    flat = lambda a: a[:, :TOP_K].reshape(t // tm, 1, tm * TOP_K)
    item_spec = pl.BlockSpec((None, 1, tm * TOP_K), lambda i: (i, 0, 0), memory_space=pltpu.SMEM)
    return pl.pallas_call(
        functools.partial(_dispatch_kernel, t * TOP_K, tm_exp),
        grid=(t // tm,),
        in_specs=[
            pl.BlockSpec(memory_space=pltpu.SMEM),
            item_spec, item_spec,
            pl.BlockSpec((tm, d), lambda i: (i, 0)),
            pl.BlockSpec(memory_space=pl.ANY),
        ],
        out_specs=[pl.BlockSpec(memory_space=pl.ANY), pl.BlockSpec(memory_space=pltpu.SMEM)],
        out_shape=[jax.ShapeDtypeStruct((n_rows, d), F32),
                   jax.ShapeDtypeStruct((n_rows,), jnp.int32)],
        scratch_shapes=[pltpu.SemaphoreType.DMA],
        input_output_aliases={4: 0},
        compiler_params=_cparams(("arbitrary",)),
    )(pad_start, flat(top_e), flat(rank), h2, xs_buf)


def _moe_experts(xs, n_items, tile_expert, first, n_used, n_valid, w_slot, next_expert, row_item,
                 w_gu, b_gu, w_down, b_down, layer, tm):
    n_rows, d = xs.shape
    n_e = N_EXPERTS
    de = w_down.shape[2]
    n_tiles = n_rows // tm
    tile_idx = lambda i, be, fi, nu, *_: (jnp.maximum(jnp.minimum(i, nu[0] - 1), 0), 0)
    b_idx = lambda i, be, *_: (layer, be[i], 0, 0)
    extend = lambda a, fill: jnp.concatenate([a, jnp.full((1,), fill, a.dtype)])
    tile_expert = jnp.concatenate([tile_expert, tile_expert[-1:]])
    first, n_valid, w_slot = extend(first, 0), extend(n_valid, 0), extend(w_slot, 0)
    next_expert = extend(next_expert, -1)
    return pl.pallas_call(
        functools.partial(_expert_kernel, layer),
        grid_spec=pltpu.PrefetchScalarGridSpec(
            num_scalar_prefetch=6,
            grid=(n_tiles + 1,),
            in_specs=[
                pl.BlockSpec((None, 1, tm), lambda i, *_: (jnp.maximum(i - 1, 0), 0, 0),
                             memory_space=pltpu.SMEM),
                pl.BlockSpec((tm, d), tile_idx),
                pl.BlockSpec(memory_space=pl.ANY),
                pl.BlockSpec((None, None, 1, 2 * de), b_idx),
                pl.BlockSpec(memory_space=pl.ANY),
                pl.BlockSpec((None, None, 1, d), b_idx),
            ],
            out_specs=pl.BlockSpec(memory_space=pl.ANY),
            scratch_shapes=[pltpu.VMEM((2, d, 2 * de), F32), pltpu.VMEM((2, de, d), F32),
                            pltpu.VMEM((d, 2 * de), BF16), pltpu.VMEM((de, d), BF16),
                            pltpu.VMEM((2, tm, d), F32), pltpu.SemaphoreType.DMA((2,)),
                            pltpu.SemaphoreType.DMA((2, 2))],
        ),
        out_shape=jax.ShapeDtypeStruct((n_items + 2 * tm, d), F32),
        compiler_params=_cparams(("arbitrary",)),
    )(tile_expert, first, n_used, n_valid, w_slot, next_expert, row_item.reshape(n_tiles, 1, tm),
      xs, w_gu, b_gu.reshape(b_gu.shape[0], n_e, 1, 2 * de), w_down,
      b_down.reshape(b_down.shape[0], n_e, 1, d))


def _moe_combine(xt, s, mod, w_item, yt, tm=1024):
    t, d = xt.shape
    return pl.pallas_call(
        _combine_kernel,
        grid=(t // tm,),
        in_specs=[
            pl.BlockSpec((tm, d), lambda i: (i, 0)),
            pl.BlockSpec((tm, d), lambda i: (i, 0)),
            pl.BlockSpec((tm, 1), lambda i: (i, 0)),
            _tok_mod_spec(5, tm, s, d),
        ],
        out_specs=pl.BlockSpec((tm, d), lambda i: (i, 0)),
        out_shape=jax.ShapeDtypeStruct((t, d), F32),
        compiler_params=_cparams(("arbitrary",)),
    )(xt, yt, w_item, mod)


def _moe_rows(n_tokens, tm_exp=256):
    return (n_tokens // tm_exp + N_EXPERTS) * tm_exp


def _moe_layer(x, xs_buf, mod, ng, router_w, router_b, w_gu, b_gu, w_down, b_down, layer,
               tm_exp=256):
    b, s, d = x.shape
    t = b * s
    xt = x.reshape(t, d)
    assert t % TOP_K == 0
    n_tok = t // TOP_K
    h2, top_e, top_w, rank, counts = _moe_route(xt, n_tok, s, mod, ng, router_w, router_b)
    (pad_start, tile_expert, first, n_used, n_valid, w_slot, next_expert,
     n_tiles) = _moe_plan(n_tok, counts, tm_exp)
    assert xs_buf.shape[0] == n_tiles * tm_exp
    xs, row_item = _moe_dispatch(h2, pad_start, top_e, rank, xs_buf, tm_exp)
    yt = _moe_experts(xs, t, tile_expert, first, n_used, n_valid, w_slot, next_expert, row_item,
                      w_gu, b_gu, w_down, b_down, layer, tm_exp)
    w_item = top_w[:, :TOP_K].reshape(t, 1)
    return _moe_combine(xt, s, mod, w_item, yt).reshape(b, s, d), xs


def kernel(x, c, norm_mix_g, norm_ffn_g, w_mod, b_mod, lru_w_in, lru_conv_w, lru_conv_b, lru_w_a, lru_b_a, lru_w_x, lru_b_x, lru_lambda, lru_w_out, attn_w_qkv, attn_q_norm_g, attn_k_norm_g, attn_w_out, router_w, router_b, expert_w_gu, expert_b_gu, expert_w_down, expert_b_down):
    depth = w_mod.shape[0]
    mods = _modulation(c, w_mod, b_mod)
    n_tokens = x.shape[0] * x.shape[1]
    xs_buf = jnp.zeros((_moe_rows(n_tokens), x.shape[2]), F32)
    for i in range(depth):
        mod = mods[i]
        j = i // 2
        if i % 2 == 0:
            x = _rglru_layer(x, mod, norm_mix_g[i], lru_w_in[j], lru_conv_w[j], lru_conv_b[j],
                             lru_w_a[j], lru_b_a[j], lru_w_x[j], lru_b_x[j], lru_lambda[j],
                             lru_w_out[j])
        else:
            x = _moba_layer(x, mod, norm_mix_g[i], attn_w_qkv[j], attn_q_norm_g[j],
                            attn_k_norm_g[j], attn_w_out[j])
        x, xs_buf = _moe_layer(x, xs_buf, mod, norm_ffn_g[i], router_w[i], router_b[i],
                               expert_w_gu, expert_b_gu, expert_w_down, expert_b_down, i)
    return x
```

```python
import functools

import jax
import jax.numpy as jnp
from jax import lax
from jax.experimental import pallas as pl
from jax.experimental.pallas import tpu as pltpu

NORM_EPS = 1e-6
N_MOD = 6
LRU_BLOCKS = 4
CONV_WIDTH = 4
LRU_C = 8.0
ATTN_HEADS = 8
MOBA_BLOCK = 256
MOBA_TOPK = 3
N_EXPERTS = 32
TOP_K = 4
SWIGLU_LIMIT = 7.0
SWIGLU_ALPHA = 1.702

LANES = 128
SUBLANES = 8
MASK_BIAS = -(2.0 ** 100)
LOG2_E = 1.4426950408889634
ATTN_PARTS = 1
VMEM_LIMIT = 52 * 1024 * 1024

F32 = jnp.float32
BF16 = jnp.bfloat16
HIGHEST = lax.Precision.HIGHEST


def _cparams(sem):
    return pltpu.CompilerParams(dimension_semantics=sem, vmem_limit_bytes=VMEM_LIMIT)


def _sigmoid(z):
    return 1.0 / (1.0 + jnp.exp(-z))


def _modulated_norm(x, g, sc, sh):
    ms = jnp.mean(x * x, axis=-1, keepdims=True)
    return x * lax.rsqrt(ms + NORM_EPS) * (g * (1.0 + sc)) + sh


def _mod_kernel(c_ref, w_ref, b_ref, o_ref):
    c = c_ref[...]
    cond = c * _sigmoid(c)
    o_ref[...] = jnp.dot(cond, w_ref[...], preferred_element_type=F32,
                         precision=HIGHEST) + b_ref[...]


def _modulation(c, w_mod, b_mod):
    depth, d, _ = w_mod.shape
    b = c.shape[0]
    rows = -(-b // SUBLANES) * SUBLANES
    c_pad = jnp.pad(c, ((0, rows - b), (0, 0)))
    out = pl.pallas_call(
        _mod_kernel,
        grid=(depth, N_MOD),
        in_specs=[
            pl.BlockSpec((rows, d), lambda i, j: (0, 0)),
            pl.BlockSpec((None, d, d), lambda i, j: (i, 0, j)),
            pl.BlockSpec((None, None, 1, d), lambda i, j: (i, j, 0, 0)),
        ],
        out_specs=pl.BlockSpec((None, None, rows, d), lambda i, j: (i, j, 0, 0)),
        out_shape=jax.ShapeDtypeStruct((depth, N_MOD, rows, d), F32),
        compiler_params=_cparams(("arbitrary", "arbitrary")),
    )(c_pad, w_mod, b_mod.reshape(depth, N_MOD, 1, d))
    return out[:, :, :b].reshape(depth, N_MOD, b, 1, d)


def _mod_spec(j, d):
    return pl.BlockSpec((None, None, 1, d), lambda b, s: (j, b, 0, 0))


def _rglru_kernel(x_ref, sh_ref, sc_ref, g1_ref, ng_ref, win_ref, cw_ref, cb_ref,
                  wa_ref, ba_ref, wx_ref, bx_ref, lam_ref, wout_ref, o_ref,
                  ext_ref, a_ref, u_ref, h_ref):
    ts, d = x_ref.shape
    bw = d // LRU_BLOCKS

    @pl.when(pl.program_id(1) == 0)
    def _():
        ext_ref[...] = jnp.zeros_like(ext_ref)
        h_ref[...] = jnp.zeros_like(h_ref)

    x = x_ref[...]
    h = _modulated_norm(x, ng_ref[...], sc_ref[...], sh_ref[...])
    gr = jnp.dot(h.astype(BF16), win_ref[...], preferred_element_type=F32)
    gate_branch = gr[:, :d]
    rec = gr[:, d:]

    groups = ts // SUBLANES
    rec3 = rec.reshape(groups, SUBLANES, d)
    tail3 = ext_ref[...].reshape(1, SUBLANES, d)
    grow = lax.broadcasted_iota(jnp.int32, rec3.shape, 1)
    xc3 = cb_ref[...] + cw_ref[CONV_WIDTH - 1:CONV_WIDTH, :] * rec3
    for s in range(1, CONV_WIDTH):
        rot = pltpu.roll(rec3, s, 1)
        rot_before = jnp.concatenate([pltpu.roll(tail3, s, 1), rot[:groups - 1]], axis=0)
        tap = cw_ref[CONV_WIDTH - 1 - s:CONV_WIDTH - s, :]
        xc3 = xc3 + tap * jnp.where(grow >= s, rot, rot_before)
    xc = xc3.reshape(ts, d)
    ext_ref[...] = rec[ts - SUBLANES:, :]

    xcb = xc.astype(BF16)
    ra = jnp.concatenate(
        [jnp.dot(xcb[:, g * bw:(g + 1) * bw], wa_ref[g], preferred_element_type=F32)
         for g in range(LRU_BLOCKS)], axis=1) + ba_ref[...]
    rx = jnp.concatenate(
        [jnp.dot(xcb[:, g * bw:(g + 1) * bw], wx_ref[g], preferred_element_type=F32)
         for g in range(LRU_BLOCKS)], axis=1) + bx_ref[...]
    r = _sigmoid(ra)
    ig = _sigmoid(rx)
    z = -lam_ref[...]
    softplus = jnp.maximum(z, 0.0) + jnp.log(1.0 + jnp.exp(-jnp.abs(z)))
    a = jnp.exp2(r * ((-LRU_C * LOG2_E) * softplus))
    v = 1.0 - a * a
    u = jnp.where(v > 0.0, v * lax.rsqrt(v), 0.0) * (ig * xc)

    a = a.reshape(ts // SUBLANES, SUBLANES, d)
    u = u.reshape(ts // SUBLANES, SUBLANES, d)
    row = lax.broadcasted_iota(jnp.int32, a.shape, 1)
    k = 1
    while k < SUBLANES:
        a_prev = pltpu.roll(a, k, 1)
        u_prev = pltpu.roll(u, k, 1)
        m = row >= k
        u = jnp.where(m, a * u_prev + u, u)
        a = jnp.where(m, a * a_prev, a)
        k *= 2
    a_ref[...] = a.reshape(ts, d)
    u_ref[...] = u.reshape(ts, d)

    def group(j, hc):
        r0 = pl.multiple_of(j * SUBLANES, SUBLANES)
        hs = u_ref[pl.ds(r0, SUBLANES), :] + a_ref[pl.ds(r0, SUBLANES), :] * hc
        u_ref[pl.ds(r0, SUBLANES), :] = hs
        return hs[SUBLANES - 1:SUBLANES, :]

    h_ref[0:1, :] = lax.fori_loop(0, ts // SUBLANES, group, h_ref[0:1, :])

    y = jax.nn.gelu(gate_branch, approximate=True) * u_ref[...]
    out = jnp.dot(y.astype(BF16), wout_ref[...], preferred_element_type=F32)
    o_ref[...] = x + g1_ref[...] * out


def _rglru_layer(x, mod, ng, w_in, conv_w, conv_b, w_a, b_a, w_x, b_x, lam, w_out, ts=512):
    b, s, d = x.shape
    full = lambda shape: pl.BlockSpec(shape, lambda bi, si: (0,) * len(shape))
    row = lambda v: v.reshape(1, d)
    return pl.pallas_call(
        _rglru_kernel,
        grid=(b, s // ts),
        in_specs=[
            pl.BlockSpec((None, ts, d), lambda bi, si: (bi, si, 0)),
            _mod_spec(0, d), _mod_spec(1, d), _mod_spec(2, d),
            full((1, d)), full((d, 2 * d)), full((CONV_WIDTH, d)), full((1, d)),
            full(w_a.shape), full((1, d)), full(w_x.shape), full((1, d)), full((1, d)),
            full((d, d)),
        ],
        out_specs=pl.BlockSpec((None, ts, d), lambda bi, si: (bi, si, 0)),
        out_shape=jax.ShapeDtypeStruct(x.shape, F32),
        scratch_shapes=[
            pltpu.VMEM((SUBLANES, d), F32),
            pltpu.VMEM((ts, d), F32),
            pltpu.VMEM((ts, d), F32),
            pltpu.VMEM((SUBLANES, d), F32),
        ],
        compiler_params=_cparams(("arbitrary", "arbitrary")),
    )(x, mod, mod, mod, row(ng), w_in.astype(BF16), conv_w, row(conv_b),
      w_a.astype(BF16), row(b_a), w_x.astype(BF16), row(b_x), row(lam), w_out.astype(BF16))


def _qkv_kernel(x_ref, sh_ref, sc_ref, ng_ref, w_ref, qg_ref, kg_ref,
                q_ref, kext_ref, v_ref, km_ref):
    ts, d = x_ref.shape
    nh, _, hd = q_ref.shape
    blk = pl.program_id(1)
    h = _modulated_norm(x_ref[...], ng_ref[...], sc_ref[...], sh_ref[...])
    qkv = jnp.dot(h.astype(BF16), w_ref[...], preferred_element_type=F32)
    lane = lax.broadcasted_iota(jnp.int32, (ts, hd), 1)
    onehot = jnp.where(lane == blk, 1.0, 0.0).astype(BF16)
    for hh in range(nh):
        qh = qkv[:, hh * hd:(hh + 1) * hd]
        qn = qh * lax.rsqrt(jnp.mean(qh * qh, axis=-1, keepdims=True) + NORM_EPS)
        q_ref[hh] = (qn * qg_ref[...] * (hd ** -0.5 * LOG2_E)).astype(BF16)
        kh = qkv[:, d + hh * hd:d + (hh + 1) * hd]
        kn = kh * lax.rsqrt(jnp.mean(kh * kh, axis=-1, keepdims=True) + NORM_EPS) * kg_ref[...]
        kext_ref[hh, :, 0:hd] = kn.astype(BF16)
        kext_ref[hh, :, hd:2 * hd] = onehot
        km_ref[:, hh * hd:(hh + 1) * hd] = jnp.mean(kn, axis=0, keepdims=True)
        v_ref[hh] = qkv[:, 2 * d + hh * hd:2 * d + (hh + 1) * hd].astype(BF16)


def _attn_select_blocks(q_ref, km_ref, bias_ref, tq):
    nhs, s, hd = q_ref.shape
    nt = (((1,), (1,)), ((), ()))
    nbp = -(-(s // tq) // SUBLANES) * SUBLANES
    blk = lax.broadcasted_iota(jnp.int32, (nbp, s), 0)
    blk_f = blk.astype(F32)
    own = lax.broadcasted_iota(jnp.int32, (nbp, s), 1) // tq
    past = blk < own
    for hh in range(nhs):
        pieces, rest = [], km_ref[hh, 0:nbp, :]
        for _ in range(3):
            piece = rest.astype(BF16)
            pieces.append(piece)
            rest = rest - piece.astype(F32)
        gate3 = lax.dot_general(jnp.concatenate(pieces, axis=0), q_ref[hh], nt,
                                preferred_element_type=F32)
        gate = gate3[0:nbp] + gate3[nbp:2 * nbp] + gate3[2 * nbp:3 * nbp]
        g = jnp.where(past, gate, -jnp.inf)
        sel = blk == own
        for _ in range(MOBA_TOPK):
            m = jnp.max(g, axis=0, keepdims=True)
            idx = jnp.min(jnp.where(g == m, blk_f, float(nbp)), axis=0, keepdims=True)
            hit = blk_f == idx
            sel = sel | (hit & past)
            g = jnp.where(hit, -jnp.inf, g)
        bias_t = jnp.where(sel, 0.0, MASK_BIAS)
        pad = jnp.zeros((LANES - nbp, tq), F32)
        for c in range(s // tq):
            tile = jnp.concatenate([bias_t[:, c * tq:(c + 1) * tq], pad], axis=0).T
            bias_ref[hh, c * tq:(c + 1) * tq, :] = tile.astype(BF16)


def _attn_query_block_steps(qi, hh, q_ref, bias_ref, kext_ref, v_ref, o_ref, s_ref):
    tq = s_ref.shape[3]
    buf = qi % s_ref.shape[0]
    nt = (((1,), (1,)), ((), ()))
    rows = slice(qi * tq, (qi + 1) * tq)
    st = {}

    def score(j):
        if j == 0:
            st['q'] = jnp.concatenate([q_ref[hh, rows, :], bias_ref[hh, rows, :]], axis=1)
        sj = lax.dot_general(st['q'], kext_ref[hh, j * tq:(j + 1) * tq, :], nt,
                             preferred_element_type=F32)
        if j == qi:
            rr = lax.broadcasted_iota(jnp.int32, (tq, tq), 0)
            cc = lax.broadcasted_iota(jnp.int32, (tq, tq), 1)
            sj = jnp.where(cc <= rr, sj, MASK_BIAS)
        s_ref[buf, hh, j * tq:(j + 1) * tq, :] = sj
        st['mx'] = sj if j == 0 else jnp.maximum(st['mx'], sj)
        if j == qi:
            st['m'] = jnp.broadcast_to(jnp.max(st['mx'], axis=-1, keepdims=True), (tq, tq))

    def weigh(j):
        p = jnp.exp2(s_ref[buf, hh, j * tq:(j + 1) * tq, :] - st['m'])
        pv = jnp.dot(p.astype(BF16), v_ref[hh, j * tq:(j + 1) * tq, :],
                     preferred_element_type=F32)
        lj = jnp.sum(p, axis=-1, keepdims=True)
        st['l'] = lj if j == 0 else st['l'] + lj
        st['acc'] = pv if j == 0 else st['acc'] + pv
        if j == qi:
            o_ref[hh, rows, :] = (st['acc'] / st['l']).astype(BF16)

    blocks = range(qi + 1)
    return ([functools.partial(score, j) for j in blocks],
            [functools.partial(weigh, j) for j in blocks])


def _attn_query_blocks(qis, q_ref, bias_ref, kext_ref, v_ref, o_ref, s_ref):
    todo = [_attn_query_block_steps(qi, hh, q_ref, bias_ref, kext_ref, v_ref, o_ref, s_ref)
            for hh in range(s_ref.shape[1]) for qi in qis]
    pending = []
    for scores, weighs in todo + [([], [])]:
        for k in range(max(len(scores), len(pending))):
            if k < len(scores):
                scores[k]()
            if k < len(pending):
                pending[k]()
        pending = weighs


def _attn_kernel(q_ref, kext_ref, v_ref, km_ref, o_ref, s_ref, bias_ref):
    tq = s_ref.shape[3]
    nb = kext_ref.shape[1] // tq
    total = nb * (nb + 1) // 2
    bounds = [0]
    for part in range(1, ATTN_PARTS):
        qi = bounds[-1]
        while qi < nb and qi * (qi + 1) // 2 < total * part // ATTN_PARTS:
            qi += 1
        bounds.append(qi)
    bounds.append(nb)
    for part in range(ATTN_PARTS):
        @pl.when(pl.program_id(2) == part)
        def _(part=part):
            if part == 0:
                _attn_select_blocks(q_ref, km_ref, bias_ref, tq)
            _attn_query_blocks(range(bounds[part], bounds[part + 1]),
                               q_ref, bias_ref, kext_ref, v_ref, o_ref, s_ref)


def _attn_out_kernel(o_ref, x_ref, g1_ref, w_ref, y_ref):
    nh = o_ref.shape[0]
    o = jnp.concatenate([o_ref[hh] for hh in range(nh)], axis=1)
    out = jnp.dot(o, w_ref[...], preferred_element_type=F32)
    y_ref[...] = x_ref[...] + g1_ref[...] * out


def _moba_layer(x, mod, ng, w_qkv, q_g, k_g, w_out, ts_out=1024, heads_per_step=1):
    b, s, d = x.shape
    nh = ATTN_HEADS
    hd = d // nh
    tq = MOBA_BLOCK
    nb = s // tq
    assert s % tq == 0 and nb <= LANES and hd == LANES
    full = lambda shape: pl.BlockSpec(shape, lambda bi, si: (0,) * len(shape))

    q, kext, v, km = pl.pallas_call(
        _qkv_kernel,
        grid=(b, nb),
        in_specs=[
            pl.BlockSpec((None, tq, d), lambda bi, si: (bi, si, 0)),
            _mod_spec(0, d), _mod_spec(1, d),
            full((1, d)), full((d, 3 * d)), full((1, hd)), full((1, hd)),
        ],
        out_specs=[
            pl.BlockSpec((None, nh, tq, hd), lambda bi, si: (bi, 0, si, 0)),
            pl.BlockSpec((None, nh, tq, 2 * hd), lambda bi, si: (bi, 0, si, 0)),
            pl.BlockSpec((None, nh, tq, hd), lambda bi, si: (bi, 0, si, 0)),
            pl.BlockSpec((None, None, 1, d), lambda bi, si: (bi, si, 0, 0)),
        ],
        out_shape=[
            jax.ShapeDtypeStruct((b, nh, s, hd), BF16),
            jax.ShapeDtypeStruct((b, nh, s, 2 * hd), BF16),
            jax.ShapeDtypeStruct((b, nh, s, hd), BF16),
            jax.ShapeDtypeStruct((b, nb, 1, d), F32),
        ],
        compiler_params=_cparams(("arbitrary", "arbitrary")),
    )(x, mod, mod, ng.reshape(1, d), w_qkv.astype(BF16), q_g.reshape(1, hd), k_g.reshape(1, hd))

    km = km.reshape(b, nb, nh, hd).transpose(0, 2, 1, 3)
    km = jnp.pad(km, ((0, 0), (0, 0), (0, LANES - nb), (0, 0)))

    o = pl.pallas_call(
        _attn_kernel,
        grid=(b, nh // heads_per_step, ATTN_PARTS),
        in_specs=[
            pl.BlockSpec((None, heads_per_step, s, hd), lambda bi, hi, pi: (bi, hi, 0, 0)),
            pl.BlockSpec((None, heads_per_step, s, 2 * hd), lambda bi, hi, pi: (bi, hi, 0, 0)),
            pl.BlockSpec((None, heads_per_step, s, hd), lambda bi, hi, pi: (bi, hi, 0, 0)),
            pl.BlockSpec((None, heads_per_step, LANES, hd), lambda bi, hi, pi: (bi, hi, 0, 0)),
        ],
        out_specs=pl.BlockSpec((None, heads_per_step, s, hd), lambda bi, hi, pi: (bi, hi, 0, 0)),
        out_shape=jax.ShapeDtypeStruct((b, nh, s, hd), BF16),
        scratch_shapes=[pltpu.VMEM((2, heads_per_step, s, tq), F32),
                        pltpu.VMEM((heads_per_step, s, LANES), BF16)],
        compiler_params=_cparams(("arbitrary", "arbitrary", "arbitrary")),
    )(q, kext, v, km)

    return pl.pallas_call(
        _attn_out_kernel,
        grid=(b, s // ts_out),
        in_specs=[
            pl.BlockSpec((None, nh, ts_out, hd), lambda bi, si: (bi, 0, si, 0)),
            pl.BlockSpec((None, ts_out, d), lambda bi, si: (bi, si, 0)),
            _mod_spec(2, d),
            full((d, d)),
        ],
        out_specs=pl.BlockSpec((None, ts_out, d), lambda bi, si: (bi, si, 0)),
        out_shape=jax.ShapeDtypeStruct(x.shape, F32),
        compiler_params=_cparams(("arbitrary", "arbitrary")),
    )(o, x, mod, w_out.astype(BF16))


def _route_kernel(x_ref, sh_ref, sc_ref, ng_ref, rw_ref, rb_ref,
                  h_ref, e_ref, w_ref, r_ref, cnt_ref, run_ref):
    tm = x_ref.shape[0]

    @pl.when(pl.program_id(0) == 0)
    def _():
        run_ref[...] = jnp.zeros_like(run_ref)

    h = _modulated_norm(x_ref[...], ng_ref[...], sc_ref[...], sh_ref[...])
    h_ref[...] = h
    logits = jnp.dot(h, rw_ref[...], preferred_element_type=F32, precision=HIGHEST) + rb_ref[...]
    lane = lax.broadcasted_iota(jnp.int32, (tm, LANES), 1)
    lane_f = lane.astype(F32)
    lg = jnp.where(lane < N_EXPERTS, logits, -jnp.inf)
    tops, hits = [], []
    for _ in range(TOP_K):
        m = jnp.max(lg, axis=-1, keepdims=True)
        idx = jnp.min(jnp.where(lg == m, lane_f, float(LANES)), axis=-1, keepdims=True)
        hit = lane_f == idx
        tops.append((m, idx))
        hits.append(hit)
        lg = jnp.where(hit, -jnp.inf, lg)
    exps = [jnp.exp(m - tops[0][0]) for m, _ in tops]
    denom = exps[0]
    for ex in exps[1:]:
        denom = denom + ex

    member = jnp.zeros((tm, LANES), F32)
    for hit in hits:
        member = member + jnp.where(hit, 1.0, 0.0)
    rr = lax.broadcasted_iota(jnp.int32, (tm, tm), 0)
    cc = lax.broadcasted_iota(jnp.int32, (tm, tm), 1)
    earlier = jnp.where(cc < rr, 1.0, 0.0).astype(BF16)
    before = jnp.dot(earlier, member.astype(BF16), preferred_element_type=F32) + run_ref[...]

    e_out = jnp.zeros((tm, LANES), F32)
    w_out = jnp.zeros((tm, LANES), F32)
    r_out = jnp.zeros((tm, LANES), F32)
    for k in range(TOP_K):
        rank = jnp.sum(jnp.where(hits[k], before, 0.0), axis=-1, keepdims=True)
        e_out = jnp.where(lane == k, tops[k][1], e_out)
        w_out = jnp.where(lane == k, exps[k] / denom, w_out)
        r_out = jnp.where(lane == k, rank, r_out)
    e_ref[...] = e_out.astype(jnp.int32)
    w_ref[...] = w_out
    r_ref[...] = r_out.astype(jnp.int32)
    run = run_ref[...] + jnp.sum(member, axis=0, keepdims=True)
    run_ref[...] = run
    cnt_ref[...] = run.astype(jnp.int32)


def _dispatch_kernel(n_items, tile_rows, start_ref, e_ref, r_ref, h_ref, xs_in_hbm, xs_hbm,
                     item_ref, sem):
    del xs_in_hbm
    tm = h_ref.shape[0]
    unroll = 32
    assert item_ref.shape[0] % unroll == 0 and tile_rows % unroll == 0
    groups = tile_rows // unroll
    assert groups & (groups - 1) == 0

    @pl.when(pl.program_id(0) == 0)
    def _():
        def clear(c, _):
            tile = lax.shift_right_logical(c, groups.bit_length() - 1)
            dummy = n_items + (tile & 1) * tile_rows + (c & (groups - 1)) * unroll
            for u in range(unroll):
                item_ref[c * unroll + u] = dummy + u
            return 0
        lax.fori_loop(0, item_ref.shape[0] // unroll, clear, 0)

    base = pl.program_id(0) * tm * TOP_K
    for t in range(tm):
        for k in range(TOP_K):
            j = t * TOP_K + k
            row = start_ref[e_ref[0, j]] + r_ref[0, j]
            item_ref[row] = base + j
            pltpu.make_async_copy(h_ref.at[pl.ds(t, 1), :], xs_hbm.at[pl.ds(row, 1), :],
                                  sem).start()
    for _ in range(TOP_K):
        pltpu.make_async_copy(h_ref, xs_hbm.at[pl.ds(0, tm), :], sem).wait()


def _expert_kernel(layer, be_ref, first_ref, nu_ref, nv_ref, wslot_ref, next_ref, item_ref,
                   xs_ref, wgu_hbm, bgu_ref, wd_hbm, bd_ref, yt_hbm,
                   wgu_f32, wd_f32, wgu_bf, wd_bf, ybuf, sems, wsems):
    i = pl.program_id(0)
    last = pl.num_programs(0) - 1
    de = wd_bf.shape[0]
    tm = xs_ref.shape[0]
    n_items = yt_hbm.shape[0] - 2 * tm
    slot = i % 2
    nv = nv_ref[i]
    has_prev = (i >= 1) & (nv_ref[jnp.maximum(i - 1, 0)] > 0)

    @pl.when(i == 0)
    def _():
        ybuf[...] = jnp.zeros_like(ybuf)
        fills = [pltpu.make_async_copy(ybuf.at[par], yt_hbm.at[pl.ds(n_items + par * tm, tm), :],
                                       sems.at[par]) for par in range(2)]
        for cp in fills:
            cp.start()
        for cp in fills:
            cp.wait()

    def weight_copies(expert, ws):
        return (pltpu.make_async_copy(wgu_hbm.at[layer, expert], wgu_f32.at[ws], wsems.at[ws, 0]),
                pltpu.make_async_copy(wd_hbm.at[layer, expert], wd_f32.at[ws], wsems.at[ws, 1]))

    @pl.when(nv > 0)
    def _():
        @pl.when(first_ref[i] == 1)
        def _():
            ws = wslot_ref[i]

            @pl.when(i == 0)
            def _():
                for cp in weight_copies(be_ref[0], 0):
                    cp.start()

            for cp in weight_copies(be_ref[i], ws):
                cp.wait()

            @pl.when(next_ref[i] >= 0)
            def _():
                for cp in weight_copies(next_ref[i], 1 - ws):
                    cp.start()

            wgu_bf[...] = wgu_f32[ws].astype(BF16)
            wd_bf[...] = wd_f32[ws].astype(BF16)

    def wait_tile(par):
        pltpu.make_async_copy(ybuf.at[par], yt_hbm.at[pl.ds(0, tm), :], sems.at[par]).wait()

    def send_tile(par):
        for r in range(tm):
            pltpu.make_async_copy(ybuf.at[par, pl.ds(r, 1), :],
                                  yt_hbm.at[pl.ds(item_ref[0, r], 1), :], sems.at[par]).start()

    def compute_tile(par):
        gu = jnp.dot(xs_ref[...].astype(BF16), wgu_bf[...], preferred_element_type=F32) + bgu_ref[...]
        gate = jnp.minimum(gu[:, :de], SWIGLU_LIMIT)
        up = jnp.clip(gu[:, de:], -SWIGLU_LIMIT, SWIGLU_LIMIT)
        act = gate * _sigmoid(SWIGLU_ALPHA * gate) * (up + 1.0)
        ybuf[par] = jnp.dot(act.astype(BF16), wd_bf[...], preferred_element_type=F32) + bd_ref[...]

    for par in range(2):
        @pl.when((i >= 2) & (nv_ref[jnp.maximum(i - 2, 0)] > 0) & (slot == par))
        def _(par=par):
            wait_tile(par)

        @pl.when((nv > 0) & has_prev & (slot == par))
        def _(par=par):
            send_tile(1 - par)
            compute_tile(par)

        @pl.when((nv == 0) & has_prev & (slot == par))
        def _(par=par):
            send_tile(1 - par)

        @pl.when((i == last) & has_prev & (slot == par))
        def _(par=par):
            wait_tile(1 - par)

    @pl.when((nv > 0) & jnp.logical_not(has_prev))
    def _():
        compute_tile(0)


def _combine_kernel(x_ref, yt_ref, w_ref, g2_ref, o_ref):
    o_ref[...] = x_ref[...] + g2_ref[...] * (w_ref[...] * yt_ref[...])


def _tok_mod_spec(j, tm, s, d):
    return pl.BlockSpec((None, None, 1, d), lambda i: (j, (i * tm) // s, 0, 0))


def _moe_route(xt, t, s, mod, ng, router_w, router_b, tm=256):
    d = xt.shape[1]
    assert t % tm == 0
    rw = jnp.pad(router_w, ((0, 0), (0, LANES - N_EXPERTS)))
    rb = jnp.pad(router_b, (0, LANES - N_EXPERTS)).reshape(1, LANES)
    return pl.pallas_call(
        _route_kernel,
        grid=(t // tm,),
        in_specs=[
            pl.BlockSpec((tm, d), lambda i: (i, 0)),
            _tok_mod_spec(3, tm, s, d), _tok_mod_spec(4, tm, s, d),
            pl.BlockSpec((1, d), lambda i: (0, 0)),
            pl.BlockSpec((d, LANES), lambda i: (0, 0)),
            pl.BlockSpec((1, LANES), lambda i: (0, 0)),
        ],
        out_specs=[
            pl.BlockSpec((tm, d), lambda i: (i, 0)),
            pl.BlockSpec((tm, LANES), lambda i: (i, 0)),
            pl.BlockSpec((tm, LANES), lambda i: (i, 0)),
            pl.BlockSpec((tm, LANES), lambda i: (i, 0)),
            pl.BlockSpec((1, LANES), lambda i: (0, 0)),
        ],
        out_shape=[
            jax.ShapeDtypeStruct((t, d), F32),
            jax.ShapeDtypeStruct((t, LANES), jnp.int32),
            jax.ShapeDtypeStruct((t, LANES), F32),
            jax.ShapeDtypeStruct((t, LANES), jnp.int32),
            jax.ShapeDtypeStruct((1, LANES), jnp.int32),
        ],
        scratch_shapes=[pltpu.VMEM((1, LANES), F32)],
        compiler_params=_cparams(("arbitrary",)),
    )(xt, mod, mod, ng.reshape(1, d), rw, rb)


def _moe_plan(t, counts, tm):
    n_e = N_EXPERTS
    counts = counts[0, :n_e]
    padded = (counts + tm - 1) // tm * tm
    pad_end = jnp.cumsum(padded)
    pad_start = pad_end - padded
    n_tiles = (t * TOP_K) // tm + n_e
    tile_start = jnp.arange(n_tiles, dtype=jnp.int32) * tm
    n_used = (pad_end[-1] // tm).astype(jnp.int32)
    tile_expert = jnp.minimum(jnp.searchsorted(pad_end, tile_start, side='right'), n_e - 1)
    tile_expert = jnp.where(jnp.arange(n_tiles) < n_used, tile_expert,
                            tile_expert[jnp.maximum(n_used - 1, 0)]).astype(jnp.int32)
    first = jnp.concatenate([jnp.ones((1,), jnp.int32),
                             (tile_expert[1:] != tile_expert[:-1]).astype(jnp.int32)])
    seg_end = (pad_start + counts)[tile_expert]
    n_valid = jnp.where(jnp.arange(n_tiles) < n_used,
                        jnp.clip(seg_end - tile_start, 0, tm), 0).astype(jnp.int32)
    used = jnp.arange(n_tiles) < n_used
    w_slot = ((jnp.cumsum(first) - 1) % 2).astype(jnp.int32)
    first_pos = jnp.where((first == 1) & used, jnp.arange(n_tiles), n_tiles)
    next_first = lax.cummin(first_pos[::-1])[::-1]
    next_first = jnp.concatenate([next_first[1:], jnp.full((1,), n_tiles)])
    next_expert = jnp.where(next_first < n_tiles,
                            tile_expert[jnp.minimum(next_first, n_tiles - 1)], -1).astype(jnp.int32)
    return (pad_start.astype(jnp.int32), tile_expert, first, n_used.reshape(1), n_valid,
            w_slot, next_expert, n_tiles)


def _moe_dispatch(h2, pad_start, top_e, rank, xs_buf, tm_exp, tm=256):
    t, d = h2.shape
    n_rows = xs_buf.shape[0]
    flat = lambda a: a[:, :TOP_K].reshape(t // tm, 1, tm * TOP_K)
    item_spec = pl.BlockSpec((None, 1, tm * TOP_K), lambda i: (i, 0, 0), memory_space=pltpu.SMEM)
    return pl.pallas_call(
        functools.partial(_dispatch_kernel, t * TOP_K, tm_exp),
        grid=(t // tm,),
        in_specs=[
            pl.BlockSpec(memory_space=pltpu.SMEM),
            item_spec, item_spec,
            pl.BlockSpec((tm, d), lambda i: (i, 0)),
            pl.BlockSpec(memory_space=pl.ANY),
        ],
        out_specs=[pl.BlockSpec(memory_space=pl.ANY), pl.BlockSpec(memory_space=pltpu.SMEM)],
        out_shape=[jax.ShapeDtypeStruct((n_rows, d), F32),
                   jax.ShapeDtypeStruct((n_rows,), jnp.int32)],
        scratch_shapes=[pltpu.SemaphoreType.DMA],
        input_output_aliases={4: 0},
        compiler_params=_cparams(("arbitrary",)),
    )(pad_start, flat(top_e), flat(rank), h2, xs_buf)


def _moe_experts(xs, n_items, tile_expert, first, n_used, n_valid, w_slot, next_expert, row_item,
                 w_gu, b_gu, w_down, b_down, layer, tm):
    n_rows, d = xs.shape
    n_e = N_EXPERTS
    de = w_down.shape[2]
    n_tiles = n_rows // tm
    tile_idx = lambda i, be, fi, nu, *_: (jnp.maximum(jnp.minimum(i, nu[0] - 1), 0), 0)
    b_idx = lambda i, be, *_: (layer, be[i], 0, 0)
    extend = lambda a, fill: jnp.concatenate([a, jnp.full((1,), fill, a.dtype)])
    tile_expert = jnp.concatenate([tile_expert, tile_expert[-1:]])
    first, n_valid, w_slot = extend(first, 0), extend(n_valid, 0), extend(w_slot, 0)
    next_expert = extend(next_expert, -1)
    return pl.pallas_call(
        functools.partial(_expert_kernel, layer),
        grid_spec=pltpu.PrefetchScalarGridSpec(
            num_scalar_prefetch=6,
            grid=(n_tiles + 1,),
            in_specs=[
                pl.BlockSpec((None, 1, tm), lambda i, *_: (jnp.maximum(i - 1, 0), 0, 0),
                             memory_space=pltpu.SMEM),
                pl.BlockSpec((tm, d), tile_idx),
                pl.BlockSpec(memory_space=pl.ANY),
                pl.BlockSpec((None, None, 1, 2 * de), b_idx),
                pl.BlockSpec(memory_space=pl.ANY),
                pl.BlockSpec((None, None, 1, d), b_idx),
            ],
            out_specs=pl.BlockSpec(memory_space=pl.ANY),
            scratch_shapes=[pltpu.VMEM((2, d, 2 * de), F32), pltpu.VMEM((2, de, d), F32),
                            pltpu.VMEM((d, 2 * de), BF16), pltpu.VMEM((de, d), BF16),
                            pltpu.VMEM((2, tm, d), F32), pltpu.SemaphoreType.DMA((2,)),
                            pltpu.SemaphoreType.DMA((2, 2))],
        ),
        out_shape=jax.ShapeDtypeStruct((n_items + 2 * tm, d), F32),
        compiler_params=_cparams(("arbitrary",)),
    )(tile_expert, first, n_used, n_valid, w_slot, next_expert, row_item.reshape(n_tiles, 1, tm),
      xs, w_gu, b_gu.reshape(b_gu.shape[0], n_e, 1, 2 * de), w_down,
      b_down.reshape(b_down.shape[0], n_e, 1, d))


def _moe_combine(xt, s, mod, w_item, yt, tm=1024):
    t, d = xt.shape
    return pl.pallas_call(
        _combine_kernel,
        grid=(t // tm,),
        in_specs=[
            pl.BlockSpec((tm, d), lambda i: (i, 0)),
            pl.BlockSpec((tm, d), lambda i: (i, 0)),
            pl.BlockSpec((tm, 1), lambda i: (i, 0)),
            _tok_mod_spec(5, tm, s, d),
        ],
        out_specs=pl.BlockSpec((tm, d), lambda i: (i, 0)),
        out_shape=jax.ShapeDtypeStruct((t, d), F32),
        compiler_params=_cparams(("arbitrary",)),
    )(xt, yt, w_item, mod)


def _moe_rows(n_tokens, tm_exp=256):
    return (n_tokens // tm_exp + N_EXPERTS) * tm_exp


def _moe_layer(x, xs_buf, mod, ng, router_w, router_b, w_gu, b_gu, w_down, b_down, layer,
               tm_exp=256):
    b, s, d = x.shape
    t = b * s
    xt = x.reshape(t, d)
    assert t % TOP_K == 0
    n_tok = t // TOP_K
    h2, top_e, top_w, rank, counts = _moe_route(xt, n_tok, s, mod, ng, router_w, router_b)
    (pad_start, tile_expert, first, n_used, n_valid, w_slot, next_expert,
     n_tiles) = _moe_plan(n_tok, counts, tm_exp)
    assert xs_buf.shape[0] == n_tiles * tm_exp
    xs, row_item = _moe_dispatch(h2, pad_start, top_e, rank, xs_buf, tm_exp)
    yt = _moe_experts(xs, t, tile_expert, first, n_used, n_valid, w_slot, next_expert, row_item,
                      w_gu, b_gu, w_down, b_down, layer, tm_exp)
    w_item = top_w[:, :TOP_K].reshape(t, 1)
    return _moe_combine(xt, s, mod, w_item, yt).reshape(b, s, d), xs


def kernel(x, c, norm_mix_g, norm_ffn_g, w_mod, b_mod, lru_w_in, lru_conv_w, lru_conv_b, lru_w_a, lru_b_a, lru_w_x, lru_b_x, lru_lambda, lru_w_out, attn_w_qkv, attn_q_norm_g, attn_k_norm_g, attn_w_out, router_w, router_b, expert_w_gu, expert_b_gu, expert_w_down, expert_b_down):
    depth = w_mod.shape[0]
    mods = _modulation(c, w_mod, b_mod)
    n_tokens = x.shape[0] * x.shape[1]
    xs_buf = jnp.zeros((_moe_rows(n_tokens), x.shape[2]), F32)
    for i in range(depth):
        mod = mods[i]
        j = i // 2
        if i % 2 == 0:
            x = _rglru_layer(x, mod, norm_mix_g[i], lru_w_in[j], lru_conv_w[j], lru_conv_b[j],
                             lru_w_a[j], lru_b_a[j], lru_w_x[j], lru_b_x[j], lru_lambda[j],
                             lru_w_out[j])
        else:
            x = _moba_layer(x, mod, norm_mix_g[i], attn_w_qkv[j], attn_q_norm_g[j],
                            attn_k_norm_g[j], attn_w_out[j])
        x, xs_buf = _moe_layer(x, xs_buf, mod, norm_ffn_g[i], router_w[i], router_b[i],
                               expert_w_gu, expert_b_gu, expert_w_down, expert_b_down, i)
    return x
```

```python
import functools

import jax
import jax.numpy as jnp
from jax import lax
from jax.experimental import pallas as pl
from jax.experimental.pallas import tpu as pltpu

NORM_EPS = 1e-6
N_MOD = 6
LRU_BLOCKS = 4
CONV_WIDTH = 4
LRU_C = 8.0
ATTN_HEADS = 8
MOBA_BLOCK = 256
MOBA_TOPK = 3
N_EXPERTS = 32
TOP_K = 4
SWIGLU_LIMIT = 7.0
SWIGLU_ALPHA = 1.702

LANES = 128
SUBLANES = 8
MASK_BIAS = -(2.0 ** 100)
LOG2_E = 1.4426950408889634
ATTN_PARTS = 1
VMEM_LIMIT = 52 * 1024 * 1024

F32 = jnp.float32
BF16 = jnp.bfloat16
HIGHEST = lax.Precision.HIGHEST


def _cparams(sem):
    return pltpu.CompilerParams(dimension_semantics=sem, vmem_limit_bytes=VMEM_LIMIT)


def _sigmoid(z):
    return 1.0 / (1.0 + jnp.exp(-z))


def _modulated_norm(x, g, sc, sh):
    ms = jnp.mean(x * x, axis=-1, keepdims=True)
    return x * lax.rsqrt(ms + NORM_EPS) * (g * (1.0 + sc)) + sh


def _mod_kernel(c_ref, w_ref, b_ref, o_ref):
    c = c_ref[...]
    cond = c * _sigmoid(c)
    o_ref[...] = jnp.dot(cond, w_ref[...], preferred_element_type=F32,
                         precision=HIGHEST) + b_ref[...]


def _modulation(c, w_mod, b_mod):
    depth, d, _ = w_mod.shape
    b = c.shape[0]
    rows = -(-b // SUBLANES) * SUBLANES
    c_pad = jnp.pad(c, ((0, rows - b), (0, 0)))
    out = pl.pallas_call(
        _mod_kernel,
        grid=(depth, N_MOD),
        in_specs=[
            pl.BlockSpec((rows, d), lambda i, j: (0, 0)),
            pl.BlockSpec((None, d, d), lambda i, j: (i, 0, j)),
            pl.BlockSpec((None, None, 1, d), lambda i, j: (i, j, 0, 0)),
        ],
        out_specs=pl.BlockSpec((None, None, rows, d), lambda i, j: (i, j, 0, 0)),
        out_shape=jax.ShapeDtypeStruct((depth, N_MOD, rows, d), F32),
        compiler_params=_cparams(("arbitrary", "arbitrary")),
    )(c_pad, w_mod, b_mod.reshape(depth, N_MOD, 1, d))
    return out[:, :, :b].reshape(depth, N_MOD, b, 1, d)


def _mod_spec(j, d):
    return pl.BlockSpec((None, None, 1, d), lambda b, s: (j, b, 0, 0))


def _rglru_kernel(x_ref, sh_ref, sc_ref, g1_ref, ng_ref, win_ref, cw_ref, cb_ref,
                  wa_ref, ba_ref, wx_ref, bx_ref, lam_ref, wout_ref, o_ref,
                  ext_ref, a_ref, u_ref, h_ref):
    ts, d = x_ref.shape
    bw = d // LRU_BLOCKS

    @pl.when(pl.program_id(1) == 0)
    def _():
        ext_ref[...] = jnp.zeros_like(ext_ref)
        h_ref[...] = jnp.zeros_like(h_ref)

    x = x_ref[...]
    h = _modulated_norm(x, ng_ref[...], sc_ref[...], sh_ref[...])
    gr = jnp.dot(h.astype(BF16), win_ref[...], preferred_element_type=F32)
    gate_branch = gr[:, :d]
    rec = gr[:, d:]

    groups = ts // SUBLANES
    rec3 = rec.reshape(groups, SUBLANES, d)
    tail3 = ext_ref[...].reshape(1, SUBLANES, d)
    grow = lax.broadcasted_iota(jnp.int32, rec3.shape, 1)
    xc3 = cb_ref[...] + cw_ref[CONV_WIDTH - 1:CONV_WIDTH, :] * rec3
    for s in range(1, CONV_WIDTH):
        rot = pltpu.roll(rec3, s, 1)
        rot_before = jnp.concatenate([pltpu.roll(tail3, s, 1), rot[:groups - 1]], axis=0)
        tap = cw_ref[CONV_WIDTH - 1 - s:CONV_WIDTH - s, :]
        xc3 = xc3 + tap * jnp.where(grow >= s, rot, rot_before)
    xc = xc3.reshape(ts, d)
    ext_ref[...] = rec[ts - SUBLANES:, :]

    xcb = xc.astype(BF16)
    ra = jnp.concatenate(
        [jnp.dot(xcb[:, g * bw:(g + 1) * bw], wa_ref[g], preferred_element_type=F32)
         for g in range(LRU_BLOCKS)], axis=1) + ba_ref[...]
    rx = jnp.concatenate(
        [jnp.dot(xcb[:, g * bw:(g + 1) * bw], wx_ref[g], preferred_element_type=F32)
         for g in range(LRU_BLOCKS)], axis=1) + bx_ref[...]
    r = _sigmoid(ra)
    ig = _sigmoid(rx)
    z = -lam_ref[...]
    softplus = jnp.maximum(z, 0.0) + jnp.log(1.0 + jnp.exp(-jnp.abs(z)))
    a = jnp.exp2(r * ((-LRU_C * LOG2_E) * softplus))
    v = 1.0 - a * a
    u = jnp.where(v > 0.0, v * lax.rsqrt(v), 0.0) * (ig * xc)

    a = a.reshape(ts // SUBLANES, SUBLANES, d)
    u = u.reshape(ts // SUBLANES, SUBLANES, d)
    row = lax.broadcasted_iota(jnp.int32, a.shape, 1)
    k = 1
    while k < SUBLANES:
        a_prev = pltpu.roll(a, k, 1)
        u_prev = pltpu.roll(u, k, 1)
        m = row >= k
        u = jnp.where(m, a * u_prev + u, u)
        a = jnp.where(m, a * a_prev, a)
        k *= 2
    a_ref[...] = a.reshape(ts, d)
    u_ref[...] = u.reshape(ts, d)

    def group(j, hc):
        r0 = pl.multiple_of(j * SUBLANES, SUBLANES)
        hs = u_ref[pl.ds(r0, SUBLANES), :] + a_ref[pl.ds(r0, SUBLANES), :] * hc
        u_ref[pl.ds(r0, SUBLANES), :] = hs
        return hs[SUBLANES - 1:SUBLANES, :]

    h_ref[0:1, :] = lax.fori_loop(0, ts // SUBLANES, group, h_ref[0:1, :])

    y = jax.nn.gelu(gate_branch, approximate=True) * u_ref[...]
    out = jnp.dot(y.astype(BF16), wout_ref[...], preferred_element_type=F32)
    o_ref[...] = x + g1_ref[...] * out


def _rglru_layer(x, mod, ng, w_in, conv_w, conv_b, w_a, b_a, w_x, b_x, lam, w_out, ts=512):
    b, s, d = x.shape
    full = lambda shape: pl.BlockSpec(shape, lambda bi, si: (0,) * len(shape))
    row = lambda v: v.reshape(1, d)
    return pl.pallas_call(
        _rglru_kernel,
        grid=(b, s // ts),
        in_specs=[
            pl.BlockSpec((None, ts, d), lambda bi, si: (bi, si, 0)),
            _mod_spec(0, d), _mod_spec(1, d), _mod_spec(2, d),
            full((1, d)), full((d, 2 * d)), full((CONV_WIDTH, d)), full((1, d)),
            full(w_a.shape), full((1, d)), full(w_x.shape), full((1, d)), full((1, d)),
            full((d, d)),
        ],
        out_specs=pl.BlockSpec((None, ts, d), lambda bi, si: (bi, si, 0)),
        out_shape=jax.ShapeDtypeStruct(x.shape, F32),
        scratch_shapes=[
            pltpu.VMEM((SUBLANES, d), F32),
            pltpu.VMEM((ts, d), F32),
            pltpu.VMEM((ts, d), F32),
            pltpu.VMEM((SUBLANES, d), F32),
        ],
        compiler_params=_cparams(("arbitrary", "arbitrary")),
    )(x, mod, mod, mod, row(ng), w_in.astype(BF16), conv_w, row(conv_b),
      w_a.astype(BF16), row(b_a), w_x.astype(BF16), row(b_x), row(lam), w_out.astype(BF16))


def _qkv_kernel(x_ref, sh_ref, sc_ref, ng_ref, w_ref, qg_ref, kg_ref,
                q_ref, kext_ref, v_ref, km_ref):
    ts, d = x_ref.shape
    nh, _, hd = q_ref.shape
    blk = pl.program_id(1)
    h = _modulated_norm(x_ref[...], ng_ref[...], sc_ref[...], sh_ref[...])
    qkv = jnp.dot(h.astype(BF16), w_ref[...], preferred_element_type=F32)
    lane = lax.broadcasted_iota(jnp.int32, (ts, hd), 1)
    onehot = jnp.where(lane == blk, 1.0, 0.0).astype(BF16)
    for hh in range(nh):
        qh = qkv[:, hh * hd:(hh + 1) * hd]
        qn = qh * lax.rsqrt(jnp.mean(qh * qh, axis=-1, keepdims=True) + NORM_EPS)
        q_ref[hh] = (qn * qg_ref[...] * (hd ** -0.5 * LOG2_E)).astype(BF16)
        kh = qkv[:, d + hh * hd:d + (hh + 1) * hd]
        kn = kh * lax.rsqrt(jnp.mean(kh * kh, axis=-1, keepdims=True) + NORM_EPS) * kg_ref[...]
        kext_ref[hh, :, 0:hd] = kn.astype(BF16)
        kext_ref[hh, :, hd:2 * hd] = onehot
        km_ref[:, hh * hd:(hh + 1) * hd] = jnp.mean(kn, axis=0, keepdims=True)
        v_ref[hh] = qkv[:, 2 * d + hh * hd:2 * d + (hh + 1) * hd].astype(BF16)


def _attn_select_blocks(q_ref, km_ref, bias_ref, tq):
    nhs, s, hd = q_ref.shape
    nt = (((1,), (1,)), ((), ()))
    nbp = -(-(s // tq) // SUBLANES) * SUBLANES
    blk = lax.broadcasted_iota(jnp.int32, (nbp, s), 0)
    blk_f = blk.astype(F32)
    own = lax.broadcasted_iota(jnp.int32, (nbp, s), 1) // tq
    past = blk < own
    for hh in range(nhs):
        pieces, rest = [], km_ref[hh, 0:nbp, :]
        for _ in range(3):
            piece = rest.astype(BF16)
            pieces.append(piece)
            rest = rest - piece.astype(F32)
        gate3 = lax.dot_general(jnp.concatenate(pieces, axis=0), q_ref[hh], nt,
                                preferred_element_type=F32)
        gate = gate3[0:nbp] + gate3[nbp:2 * nbp] + gate3[2 * nbp:3 * nbp]
        g = jnp.where(past, gate, -jnp.inf)
        sel = blk == own
        for _ in range(MOBA_TOPK):
            m = jnp.max(g, axis=0, keepdims=True)
            idx = jnp.min(jnp.where(g == m, blk_f, float(nbp)), axis=0, keepdims=True)
            hit = blk_f == idx
            sel = sel | (hit & past)
            g = jnp.where(hit, -jnp.inf, g)
        bias_t = jnp.where(sel, 0.0, MASK_BIAS)
        pad = jnp.zeros((LANES - nbp, tq), F32)
        for c in range(s // tq):
            tile = jnp.concatenate([bias_t[:, c * tq:(c + 1) * tq], pad], axis=0).T
            bias_ref[hh, c * tq:(c + 1) * tq, :] = tile.astype(BF16)


def _attn_query_block_steps(qi, hh, q_ref, bias_ref, kext_ref, v_ref, o_ref, s_ref):
    tq = s_ref.shape[3]
    buf = qi % s_ref.shape[0]
    nt = (((1,), (1,)), ((), ()))
    rows = slice(qi * tq, (qi + 1) * tq)
    st = {}

    def score(j):
        if j == 0:
            st['q'] = jnp.concatenate([q_ref[hh, rows, :], bias_ref[hh, rows, :]], axis=1)
        sj = lax.dot_general(st['q'], kext_ref[hh, j * tq:(j + 1) * tq, :], nt,
                             preferred_element_type=F32)
        if j == qi:
            rr = lax.broadcasted_iota(jnp.int32, (tq, tq), 0)
            cc = lax.broadcasted_iota(jnp.int32, (tq, tq), 1)
            sj = jnp.where(cc <= rr, sj, MASK_BIAS)
        s_ref[buf, hh, j * tq:(j + 1) * tq, :] = sj
        st['mx'] = sj if j == 0 else jnp.maximum(st['mx'], sj)
        if j == qi:
            st['m'] = jnp.broadcast_to(jnp.max(st['mx'], axis=-1, keepdims=True), (tq, tq))

    def weigh(j):
        p = jnp.exp2(s_ref[buf, hh, j * tq:(j + 1) * tq, :] - st['m'])
        pv = jnp.dot(p.astype(BF16), v_ref[hh, j * tq:(j + 1) * tq, :],
                     preferred_element_type=F32)
        lj = jnp.sum(p, axis=-1, keepdims=True)
        st['l'] = lj if j == 0 else st['l'] + lj
        st['acc'] = pv if j == 0 else st['acc'] + pv
        if j == qi:
            o_ref[hh, rows, :] = (st['acc'] / st['l']).astype(BF16)

    blocks = range(qi + 1)
    return ([functools.partial(score, j) for j in blocks],
            [functools.partial(weigh, j) for j in blocks])


def _attn_query_blocks(qis, q_ref, bias_ref, kext_ref, v_ref, o_ref, s_ref):
    todo = [_attn_query_block_steps(qi, hh, q_ref, bias_ref, kext_ref, v_ref, o_ref, s_ref)
            for hh in range(s_ref.shape[1]) for qi in qis]
    pending = []
    for scores, weighs in todo + [([], [])]:
        for k in range(max(len(scores), len(pending))):
            if k < len(scores):
                scores[k]()
            if k < len(pending):
                pending[k]()
        pending = weighs


def _attn_kernel(q_ref, kext_ref, v_ref, km_ref, o_ref, s_ref, bias_ref):
    tq = s_ref.shape[3]
    nb = kext_ref.shape[1] // tq
    total = nb * (nb + 1) // 2
    bounds = [0]
    for part in range(1, ATTN_PARTS):
        qi = bounds[-1]
        while qi < nb and qi * (qi + 1) // 2 < total * part // ATTN_PARTS:
            qi += 1
        bounds.append(qi)
    bounds.append(nb)
    for part in range(ATTN_PARTS):
        @pl.when(pl.program_id(2) == part)
        def _(part=part):
            if part == 0:
                _attn_select_blocks(q_ref, km_ref, bias_ref, tq)
            _attn_query_blocks(range(bounds[part], bounds[part + 1]),
                               q_ref, bias_ref, kext_ref, v_ref, o_ref, s_ref)


def _attn_out_kernel(o_ref, x_ref, g1_ref, w_ref, y_ref):
    nh = o_ref.shape[0]
    o = jnp.concatenate([o_ref[hh] for hh in range(nh)], axis=1)
    out = jnp.dot(o, w_ref[...], preferred_element_type=F32)
    y_ref[...] = x_ref[...] + g1_ref[...] * out


def _moba_layer(x, mod, ng, w_qkv, q_g, k_g, w_out, ts_out=1024, heads_per_step=1):
    b, s, d = x.shape
    nh = ATTN_HEADS
    hd = d // nh
    tq = MOBA_BLOCK
    nb = s // tq
    assert s % tq == 0 and nb <= LANES and hd == LANES
    full = lambda shape: pl.BlockSpec(shape, lambda bi, si: (0,) * len(shape))

    q, kext, v, km = pl.pallas_call(
        _qkv_kernel,
        grid=(b, nb),
        in_specs=[
            pl.BlockSpec((None, tq, d), lambda bi, si: (bi, si, 0)),
            _mod_spec(0, d), _mod_spec(1, d),
            full((1, d)), full((d, 3 * d)), full((1, hd)), full((1, hd)),
        ],
        out_specs=[
            pl.BlockSpec((None, nh, tq, hd), lambda bi, si: (bi, 0, si, 0)),
            pl.BlockSpec((None, nh, tq, 2 * hd), lambda bi, si: (bi, 0, si, 0)),
            pl.BlockSpec((None, nh, tq, hd), lambda bi, si: (bi, 0, si, 0)),
            pl.BlockSpec((None, None, 1, d), lambda bi, si: (bi, si, 0, 0)),
        ],
        out_shape=[
            jax.ShapeDtypeStruct((b, nh, s, hd), BF16),
            jax.ShapeDtypeStruct((b, nh, s, 2 * hd), BF16),
            jax.ShapeDtypeStruct((b, nh, s, hd), BF16),
            jax.ShapeDtypeStruct((b, nb, 1, d), F32),
        ],
        compiler_params=_cparams(("arbitrary", "arbitrary")),
    )(x, mod, mod, ng.reshape(1, d), w_qkv.astype(BF16), q_g.reshape(1, hd), k_g.reshape(1, hd))

    km = km.reshape(b, nb, nh, hd).transpose(0, 2, 1, 3)
    km = jnp.pad(km, ((0, 0), (0, 0), (0, LANES - nb), (0, 0)))

    o = pl.pallas_call(
        _attn_kernel,
        grid=(b, nh // heads_per_step, ATTN_PARTS),
        in_specs=[
            pl.BlockSpec((None, heads_per_step, s, hd), lambda bi, hi, pi: (bi, hi, 0, 0)),
            pl.BlockSpec((None, heads_per_step, s, 2 * hd), lambda bi, hi, pi: (bi, hi, 0, 0)),
            pl.BlockSpec((None, heads_per_step, s, hd), lambda bi, hi, pi: (bi, hi, 0, 0)),
            pl.BlockSpec((None, heads_per_step, LANES, hd), lambda bi, hi, pi: (bi, hi, 0, 0)),
        ],
        out_specs=pl.BlockSpec((None, heads_per_step, s, hd), lambda bi, hi, pi: (bi, hi, 0, 0)),
        out_shape=jax.ShapeDtypeStruct((b, nh, s, hd), BF16),
        scratch_shapes=[pltpu.VMEM((2, heads_per_step, s, tq), F32),
                        pltpu.VMEM((heads_per_step, s, LANES), BF16)],
        compiler_params=_cparams(("arbitrary", "arbitrary", "arbitrary")),
    )(q, kext, v, km)

    return pl.pallas_call(
        _attn_out_kernel,
        grid=(b, s // ts_out),
        in_specs=[
            pl.BlockSpec((None, nh, ts_out, hd), lambda bi, si: (bi, 0, si, 0)),
            pl.BlockSpec((None, ts_out, d), lambda bi, si: (bi, si, 0)),
            _mod_spec(2, d),
            full((d, d)),
        ],
        out_specs=pl.BlockSpec((None, ts_out, d), lambda bi, si: (bi, si, 0)),
        out_shape=jax.ShapeDtypeStruct(x.shape, F32),
        compiler_params=_cparams(("arbitrary", "arbitrary")),
    )(o, x, mod, w_out.astype(BF16))


def _route_kernel(x_ref, sh_ref, sc_ref, ng_ref, rw_ref, rb_ref,
                  h_ref, e_ref, w_ref, r_ref, cnt_ref, run_ref):
    tm = x_ref.shape[0]

    @pl.when(pl.program_id(0) == 0)
    def _():
        run_ref[...] = jnp.zeros_like(run_ref)

    h = _modulated_norm(x_ref[...], ng_ref[...], sc_ref[...], sh_ref[...])
    h_ref[...] = h
    logits = jnp.dot(h, rw_ref[...], preferred_element_type=F32, precision=HIGHEST) + rb_ref[...]
    lane = lax.broadcasted_iota(jnp.int32, (tm, LANES), 1)
    lane_f = lane.astype(F32)
    lg = jnp.where(lane < N_EXPERTS, logits, -jnp.inf)
    tops, hits = [], []
    for _ in range(TOP_K):
        m = jnp.max(lg, axis=-1, keepdims=True)
        idx = jnp.min(jnp.where(lg == m, lane_f, float(LANES)), axis=-1, keepdims=True)
        hit = lane_f == idx
        tops.append((m, idx))
        hits.append(hit)
        lg = jnp.where(hit, -jnp.inf, lg)
    exps = [jnp.exp(m - tops[0][0]) for m, _ in tops]
    denom = exps[0]
    for ex in exps[1:]:
        denom = denom + ex

    member = jnp.zeros((tm, LANES), F32)
    for hit in hits:
        member = member + jnp.where(hit, 1.0, 0.0)
    rr = lax.broadcasted_iota(jnp.int32, (tm, tm), 0)
    cc = lax.broadcasted_iota(jnp.int32, (tm, tm), 1)
    earlier = jnp.where(cc < rr, 1.0, 0.0).astype(BF16)
    before = jnp.dot(earlier, member.astype(BF16), preferred_element_type=F32) + run_ref[...]

    e_out = jnp.zeros((tm, LANES), F32)
    w_out = jnp.zeros((tm, LANES), F32)
    r_out = jnp.zeros((tm, LANES), F32)
    for k in range(TOP_K):
        rank = jnp.sum(jnp.where(hits[k], before, 0.0), axis=-1, keepdims=True)
        e_out = jnp.where(lane == k, tops[k][1], e_out)
        w_out = jnp.where(lane == k, exps[k] / denom, w_out)
        r_out = jnp.where(lane == k, rank, r_out)
    e_ref[...] = e_out.astype(jnp.int32)
    w_ref[...] = w_out
    r_ref[...] = r_out.astype(jnp.int32)
    run = run_ref[...] + jnp.sum(member, axis=0, keepdims=True)
    run_ref[...] = run
    cnt_ref[...] = run.astype(jnp.int32)


def _dispatch_kernel(n_items, tile_rows, start_ref, e_ref, r_ref, h_ref, xs_in_hbm, xs_hbm,
                     item_ref, sem):
    del xs_in_hbm
    tm = h_ref.shape[0]
    unroll = 32
    assert item_ref.shape[0] % unroll == 0 and tile_rows % unroll == 0
    groups = tile_rows // unroll
    assert groups & (groups - 1) == 0

    @pl.when(pl.program_id(0) == 0)
    def _():
        def clear(c, _):
            tile = lax.shift_right_logical(c, groups.bit_length() - 1)
            dummy = n_items + (tile & 1) * tile_rows + (c & (groups - 1)) * unroll
            for u in range(unroll):
                item_ref[c * unroll + u] = dummy + u
            return 0
        lax.fori_loop(0, item_ref.shape[0] // unroll, clear, 0)

    base = pl.program_id(0) * tm * TOP_K
    for t in range(tm):
        for k in range(TOP_K):
            j = t * TOP_K + k
            row = start_ref[e_ref[0, j]] + r_ref[0, j]
            item_ref[row] = base + j
            pltpu.make_async_copy(h_ref.at[pl.ds(t, 1), :], xs_hbm.at[pl.ds(row, 1), :],
                                  sem).start(priority=j % 2)
    for _ in range(TOP_K):
        pltpu.make_async_copy(h_ref, xs_hbm.at[pl.ds(0, tm), :], sem).wait()


def _expert_kernel(layer, be_ref, first_ref, nu_ref, nv_ref, wslot_ref, next_ref, item_ref,
                   xs_ref, wgu_hbm, bgu_ref, wd_hbm, bd_ref, yt_hbm,
                   wgu_f32, wd_f32, wgu_bf, wd_bf, ybuf, sems, wsems):
    i = pl.program_id(0)
    last = pl.num_programs(0) - 1
    de = wd_bf.shape[0]
    tm = xs_ref.shape[0]
    n_items = yt_hbm.shape[0] - 2 * tm
    slot = i % 2
    nv = nv_ref[i]
    has_prev = (i >= 1) & (nv_ref[jnp.maximum(i - 1, 0)] > 0)

    @pl.when(i == 0)
    def _():
        ybuf[...] = jnp.zeros_like(ybuf)
        fills = [pltpu.make_async_copy(ybuf.at[par], yt_hbm.at[pl.ds(n_items + par * tm, tm), :],
                                       sems.at[par]) for par in range(2)]
        for cp in fills:
            cp.start()
        for cp in fills:
            cp.wait()

    def weight_copies(expert, ws):
        return (pltpu.make_async_copy(wgu_hbm.at[layer, expert], wgu_f32.at[ws], wsems.at[ws, 0]),
                pltpu.make_async_copy(wd_hbm.at[layer, expert], wd_f32.at[ws], wsems.at[ws, 1]))

    @pl.when(nv > 0)
    def _():
        @pl.when(first_ref[i] == 1)
        def _():
            ws = wslot_ref[i]

            @pl.when(i == 0)
            def _():
                for cp in weight_copies(be_ref[0], 0):
                    cp.start()

            for cp in weight_copies(be_ref[i], ws):
                cp.wait()

            @pl.when(next_ref[i] >= 0)
            def _():
                for cp in weight_copies(next_ref[i], 1 - ws):
                    cp.start()

            wgu_bf[...] = wgu_f32[ws].astype(BF16)
            wd_bf[...] = wd_f32[ws].astype(BF16)

    def wait_tile(par):
        pltpu.make_async_copy(ybuf.at[par], yt_hbm.at[pl.ds(0, tm), :], sems.at[par]).wait()

    def send_tile(par):
        for r in range(tm):
            pltpu.make_async_copy(ybuf.at[par, pl.ds(r, 1), :],
                                  yt_hbm.at[pl.ds(item_ref[0, r], 1), :],
                                  sems.at[par]).start(priority=r % 2)

    def compute_tile(par):
        gu = jnp.dot(xs_ref[...].astype(BF16), wgu_bf[...], preferred_element_type=F32) + bgu_ref[...]
        gate = jnp.minimum(gu[:, :de], SWIGLU_LIMIT)
        up = jnp.clip(gu[:, de:], -SWIGLU_LIMIT, SWIGLU_LIMIT)
        act = gate * _sigmoid(SWIGLU_ALPHA * gate) * (up + 1.0)
        ybuf[par] = jnp.dot(act.astype(BF16), wd_bf[...], preferred_element_type=F32) + bd_ref[...]

    for par in range(2):
        @pl.when((i >= 2) & (nv_ref[jnp.maximum(i - 2, 0)] > 0) & (slot == par))
        def _(par=par):
            wait_tile(par)

        @pl.when((nv > 0) & has_prev & (slot == par))
        def _(par=par):
            send_tile(1 - par)
            compute_tile(par)

        @pl.when((nv == 0) & has_prev & (slot == par))
        def _(par=par):
            send_tile(1 - par)

        @pl.when((i == last) & has_prev & (slot == par))
        def _(par=par):
            wait_tile(1 - par)

    @pl.when((nv > 0) & jnp.logical_not(has_prev))
    def _():
        compute_tile(0)


def _combine_kernel(x_ref, yt_ref, w_ref, g2_ref, o_ref):
    o_ref[...] = x_ref[...] + g2_ref[...] * (w_ref[...] * yt_ref[...])


def _tok_mod_spec(j, tm, s, d):
    return pl.BlockSpec((None, None, 1, d), lambda i: (j, (i * tm) // s, 0, 0))


def _moe_route(xt, t, s, mod, ng, router_w, router_b, tm=256):
    d = xt.shape[1]
    assert t % tm == 0
    rw = jnp.pad(router_w, ((0, 0), (0, LANES - N_EXPERTS)))
    rb = jnp.pad(router_b, (0, LANES - N_EXPERTS)).reshape(1, LANES)
    return pl.pallas_call(
        _route_kernel,
        grid=(t // tm,),
        in_specs=[
            pl.BlockSpec((tm, d), lambda i: (i, 0)),
            _tok_mod_spec(3, tm, s, d), _tok_mod_spec(4, tm, s, d),
            pl.BlockSpec((1, d), lambda i: (0, 0)),
            pl.BlockSpec((d, LANES), lambda i: (0, 0)),
            pl.BlockSpec((1, LANES), lambda i: (0, 0)),
        ],
        out_specs=[
            pl.BlockSpec((tm, d), lambda i: (i, 0)),
            pl.BlockSpec((tm, LANES), lambda i: (i, 0)),
            pl.BlockSpec((tm, LANES), lambda i: (i, 0)),
            pl.BlockSpec((tm, LANES), lambda i: (i, 0)),
            pl.BlockSpec((1, LANES), lambda i: (0, 0)),
        ],
        out_shape=[
            jax.ShapeDtypeStruct((t, d), F32),
            jax.ShapeDtypeStruct((t, LANES), jnp.int32),
            jax.ShapeDtypeStruct((t, LANES), F32),
            jax.ShapeDtypeStruct((t, LANES), jnp.int32),
            jax.ShapeDtypeStruct((1, LANES), jnp.int32),
        ],
        scratch_shapes=[pltpu.VMEM((1, LANES), F32)],
        compiler_params=_cparams(("arbitrary",)),
    )(xt, mod, mod, ng.reshape(1, d), rw, rb)


def _moe_plan(t, counts, tm):
    n_e = N_EXPERTS
    counts = counts[0, :n_e]
    padded = (counts + tm - 1) // tm * tm
    pad_end = jnp.cumsum(padded)
    pad_start = pad_end - padded
    n_tiles = (t * TOP_K) // tm + n_e
    tile_start = jnp.arange(n_tiles, dtype=jnp.int32) * tm
    n_used = (pad_end[-1] // tm).astype(jnp.int32)
    tile_expert = jnp.minimum(jnp.searchsorted(pad_end, tile_start, side='right'), n_e - 1)
    tile_expert = jnp.where(jnp.arange(n_tiles) < n_used, tile_expert,
                            tile_expert[jnp.maximum(n_used - 1, 0)]).astype(jnp.int32)
    first = jnp.concatenate([jnp.ones((1,), jnp.int32),
                             (tile_expert[1:] != tile_expert[:-1]).astype(jnp.int32)])
    seg_end = (pad_start + counts)[tile_expert]
    n_valid = jnp.where(jnp.arange(n_tiles) < n_used,
                        jnp.clip(seg_end - tile_start, 0, tm), 0).astype(jnp.int32)
    used = jnp.arange(n_tiles) < n_used
    w_slot = ((jnp.cumsum(first) - 1) % 2).astype(jnp.int32)
    first_pos = jnp.where((first == 1) & used, jnp.arange(n_tiles), n_tiles)
    next_first = lax.cummin(first_pos[::-1])[::-1]
    next_first = jnp.concatenate([next_first[1:], jnp.full((1,), n_tiles)])
    next_expert = jnp.where(next_first < n_tiles,
                            tile_expert[jnp.minimum(next_first, n_tiles - 1)], -1).astype(jnp.int32)
    return (pad_start.astype(jnp.int32), tile_expert, first, n_used.reshape(1), n_valid,
            w_slot, next_expert, n_tiles)


def _moe_dispatch(h2, pad_start, top_e, rank, xs_buf, tm_exp, tm=256):
    t, d = h2.shape
    n_rows = xs_buf.shape[0]
    flat = lambda a: a[:, :TOP_K].reshape(t // tm, 1, tm * TOP_K)
    item_spec = pl.BlockSpec((None, 1, tm * TOP_K), lambda i: (i, 0, 0), memory_space=pltpu.SMEM)
    return pl.pallas_call(
        functools.partial(_dispatch_kernel, t * TOP_K, tm_exp),
        grid=(t // tm,),
        in_specs=[
            pl.BlockSpec(memory_space=pltpu.SMEM),
            item_spec, item_spec,
            pl.BlockSpec((tm, d), lambda i: (i, 0)),
            pl.BlockSpec(memory_space=pl.ANY),
        ],
        out_specs=[pl.BlockSpec(memory_space=pl.ANY), pl.BlockSpec(memory_space=pltpu.SMEM)],
        out_shape=[jax.ShapeDtypeStruct((n_rows, d), F32),
                   jax.ShapeDtypeStruct((n_rows,), jnp.int32)],
        scratch_shapes=[pltpu.SemaphoreType.DMA],
        input_output_aliases={4: 0},
        compiler_params=_cparams(("arbitrary",)),
    )(pad_start, flat(top_e), flat(rank), h2, xs_buf)


def _moe_experts(xs, n_items, tile_expert, first, n_used, n_valid, w_slot, next_expert, row_item,
                 w_gu, b_gu, w_down, b_down, layer, tm):
    n_rows, d = xs.shape
    n_e = N_EXPERTS
    de = w_down.shape[2]
    n_tiles = n_rows // tm
    tile_idx = lambda i, be, fi, nu, *_: (jnp.maximum(jnp.minimum(i, nu[0] - 1), 0), 0)
    b_idx = lambda i, be, *_: (layer, be[i], 0, 0)
    extend = lambda a, fill: jnp.concatenate([a, jnp.full((1,), fill, a.dtype)])
    tile_expert = jnp.concatenate([tile_expert, tile_expert[-1:]])
    first, n_valid, w_slot = extend(first, 0), extend(n_valid, 0), extend(w_slot, 0)
    next_expert = extend(next_expert, -1)
    return pl.pallas_call(
        functools.partial(_expert_kernel, layer),
        grid_spec=pltpu.PrefetchScalarGridSpec(
            num_scalar_prefetch=6,
            grid=(n_tiles + 1,),
            in_specs=[
                pl.BlockSpec((None, 1, tm), lambda i, *_: (jnp.maximum(i - 1, 0), 0, 0),
                             memory_space=pltpu.SMEM),
                pl.BlockSpec((tm, d), tile_idx),
                pl.BlockSpec(memory_space=pl.ANY),
                pl.BlockSpec((None, None, 1, 2 * de), b_idx),
                pl.BlockSpec(memory_space=pl.ANY),
                pl.BlockSpec((None, None, 1, d), b_idx),
            ],
            out_specs=pl.BlockSpec(memory_space=pl.ANY),
            scratch_shapes=[pltpu.VMEM((2, d, 2 * de), F32), pltpu.VMEM((2, de, d), F32),
                            pltpu.VMEM((d, 2 * de), BF16), pltpu.VMEM((de, d), BF16),
                            pltpu.VMEM((2, tm, d), F32), pltpu.SemaphoreType.DMA((2,)),
                            pltpu.SemaphoreType.DMA((2, 2))],
        ),
        out_shape=jax.ShapeDtypeStruct((n_items + 2 * tm, d), F32),
        compiler_params=_cparams(("arbitrary",)),
    )(tile_expert, first, n_used, n_valid, w_slot, next_expert, row_item.reshape(n_tiles, 1, tm),
      xs, w_gu, b_gu.reshape(b_gu.shape[0], n_e, 1, 2 * de), w_down,
      b_down.reshape(b_down.shape[0], n_e, 1, d))


def _moe_combine(xt, s, mod, w_item, yt, tm=1024):
    t, d = xt.shape
    return pl.pallas_call(
        _combine_kernel,
        grid=(t // tm,),
        in_specs=[
            pl.BlockSpec((tm, d), lambda i: (i, 0)),
            pl.BlockSpec((tm, d), lambda i: (i, 0)),
            pl.BlockSpec((tm, 1), lambda i: (i, 0)),
            _tok_mod_spec(5, tm, s, d),
        ],
        out_specs=pl.BlockSpec((tm, d), lambda i: (i, 0)),
        out_shape=jax.ShapeDtypeStruct((t, d), F32),
        compiler_params=_cparams(("arbitrary",)),
    )(xt, yt, w_item, mod)


def _moe_rows(n_tokens, tm_exp=256):
    return (n_tokens // tm_exp + N_EXPERTS) * tm_exp


def _moe_layer(x, xs_buf, mod, ng, router_w, router_b, w_gu, b_gu, w_down, b_down, layer,
               tm_exp=256):
    b, s, d = x.shape
    t = b * s
    xt = x.reshape(t, d)
    assert t % TOP_K == 0
    n_tok = t // TOP_K
    h2, top_e, top_w, rank, counts = _moe_route(xt, n_tok, s, mod, ng, router_w, router_b)
    (pad_start, tile_expert, first, n_used, n_valid, w_slot, next_expert,
     n_tiles) = _moe_plan(n_tok, counts, tm_exp)
    assert xs_buf.shape[0] == n_tiles * tm_exp
    xs, row_item = _moe_dispatch(h2, pad_start, top_e, rank, xs_buf, tm_exp)
    yt = _moe_experts(xs, t, tile_expert, first, n_used, n_valid, w_slot, next_expert, row_item,
                      w_gu, b_gu, w_down, b_down, layer, tm_exp)
    w_item = top_w[:, :TOP_K].reshape(t, 1)
    return _moe_combine(xt, s, mod, w_item, yt).reshape(b, s, d), xs


def kernel(x, c, norm_mix_g, norm_ffn_g, w_mod, b_mod, lru_w_in, lru_conv_w, lru_conv_b, lru_w_a, lru_b_a, lru_w_x, lru_b_x, lru_lambda, lru_w_out, attn_w_qkv, attn_q_norm_g, attn_k_norm_g, attn_w_out, router_w, router_b, expert_w_gu, expert_b_gu, expert_w_down, expert_b_down):
    depth = w_mod.shape[0]
    mods = _modulation(c, w_mod, b_mod)
    n_tokens = x.shape[0] * x.shape[1]
    xs_buf = jnp.zeros((_moe_rows(n_tokens), x.shape[2]), F32)
    for i in range(depth):
        mod = mods[i]
        j = i // 2
        if i % 2 == 0:
            x = _rglru_layer(x, mod, norm_mix_g[i], lru_w_in[j], lru_conv_w[j], lru_conv_b[j],
                             lru_w_a[j], lru_b_a[j], lru_w_x[j], lru_b_x[j], lru_lambda[j],
                             lru_w_out[j])
        else:
            x = _moba_layer(x, mod, norm_mix_g[i], attn_w_qkv[j], attn_q_norm_g[j],
                            attn_k_norm_g[j], attn_w_out[j])
        x, xs_buf = _moe_layer(x, xs_buf, mod, norm_ffn_g[i], router_w[i], router_b[i],
                               expert_w_gu, expert_b_gu, expert_w_down, expert_b_down, i)
    return x
```
